```python
import math
import jax, jax.numpy as jnp
from jax import lax
import numpy as np

D_MODEL = 2048
BATCH = 2
SEQ = 4096
DEPTH = 2

GRID_W = 64
D_MIX = D_MODEL
N_MIXERS = 4
D_GROUP = D_MIX // N_MIXERS
HEAD_DIM = 64
N_NA_HEADS = D_GROUP // HEAD_DIM
N_RET_HEADS = D_GROUP // HEAD_DIM
N_FFT_GROUPS = 4
FFT_GROUP_DIM = D_GROUP // N_FFT_GROUPS
NA_KH_MAX = 8
NA_KW = 16
RET_CHUNK = 128
CONV_WIDTH = 31
ROPE_BASE = 10000.0
EPS = 1e-6
N_IN_PIECES = 13
D_IN = N_IN_PIECES * D_GROUP

kernel_name = "hybrid_parallel_fourier_natten_retnet_conformer"


def _rmsnorm(x, g):
    xf = x.astype(jnp.float32)
    ms = jnp.mean(xf * xf, axis=-1, keepdims=True)
    return (xf * lax.rsqrt(ms + EPS)).astype(x.dtype) * g


def _fourier_mix(u, w_fft):
    B, S, _ = u.shape
    uf = u.astype(jnp.float32).reshape(B, S, N_FFT_GROUPS, FFT_GROUP_DIM)
    y = jnp.real(jnp.fft.fft2(uf, axes=(1, 3), norm="ortho"))
    return y.reshape(B, S, D_GROUP).astype(u.dtype) @ w_fft


def _neighbourhood_attention(q, k, v, rel_bias):
    B, S, _ = q.shape
    rows = S // GRID_W
    kh = min(NA_KH_MAX, rows)

    def grid(t):
        return t.reshape(B, rows, GRID_W, N_NA_HEADS, HEAD_DIM)

    qg = grid(q) * (HEAD_DIM ** -0.5)
    kg, vg = grid(k), grid(v)
    r = jnp.arange(rows)
    row_start = jnp.clip(r - kh // 2, 0, rows - kh)
    key_rows = row_start[:, None] + jnp.arange(kh)[None, :]
    kb = kg[:, key_rows]
    vb = vg[:, key_rows]
    col = jnp.arange(GRID_W)
    col_start = jnp.clip(col - NA_KW // 2, 0, GRID_W - NA_KW)
    rel_c = col[None, :] - col_start[:, None]
    col_in = (rel_c >= 0) & (rel_c < NA_KW)
    dc = col[None, :] - col[:, None]
    dr = key_rows - r[:, None]
    bias = rel_bias[:, (dr + NA_KH_MAX - 1)[:, :, None, None],
                    jnp.clip(dc + NA_KW - 1, 0, 2 * NA_KW - 2)[None, None]]
    bias = bias.transpose(0, 1, 3, 2, 4)
    s = jnp.einsum('brqhd,brakhd->bhrqak', qg, kb).astype(jnp.float32)
    s = s + bias[None].astype(jnp.float32)
    s = jnp.where(col_in[:, None, :], s, -jnp.inf)
    p = jax.nn.softmax(s.reshape(B, N_NA_HEADS, rows, GRID_W, kh * GRID_W), axis=-1)
    p = p.reshape(B, N_NA_HEADS, rows, GRID_W, kh, GRID_W).astype(v.dtype)
    o = jnp.einsum('bhrqak,brakhd->brqhd', p, vb)
    return o.reshape(B, S, D_GROUP)


def _rotary(t):
    S = t.shape[2]
    half = HEAD_DIM // 2
    inv = ROPE_BASE ** (-jnp.arange(half, dtype=jnp.float32) / half)
    ang = jnp.arange(S, dtype=jnp.float32)[:, None] * inv[None, :]
    cos, sin = jnp.cos(ang), jnp.sin(ang)
    t1, t2 = t[..., :half], t[..., half:]
    return jnp.concatenate([t1 * cos - t2 * sin, t1 * sin + t2 * cos], axis=-1)


def _retention_direction(q, k, v, log_gamma, include_diag):
    B, H, S, Dh = q.shape
    C = RET_CHUNK
    n = S // C
    qc = q.reshape(B, H, n, C, Dh)
    kc = k.reshape(B, H, n, C, Dh)
    vc = v.reshape(B, H, n, C, Dh)
    idx = jnp.arange(C, dtype=jnp.float32)
    diff = idx[:, None] - idx[None, :]
    mask = (diff >= 0) if include_diag else (diff > 0)
    lg = log_gamma[:, None, None]
    d_intra = jnp.where(mask, jnp.exp(lg * jnp.where(mask, diff, 0.0)), 0.0)
    scores = jnp.einsum('bhnid,bhnjd->bhnij', qc, kc) * d_intra[None, :, None]
    o_intra = jnp.einsum('bhnij,bhnjd->bhnid', scores, vc)
    k_decay = jnp.exp(log_gamma[:, None] * (C - 1 - idx)[None, :])
    kv = jnp.einsum('bhnjd,bhnje->nbhde', kc * k_decay[None, :, None, :, None], vc)
    chunk_decay = jnp.exp(log_gamma * C)[None, :, None, None]

    def step(state, kv_n):
        return chunk_decay * state + kv_n, state

    _, prev = lax.scan(step, jnp.zeros((B, H, Dh, Dh), jnp.float32), kv)
    q_decay = jnp.exp(log_gamma[:, None] * (idx + 1.0)[None, :])
    o_cross = jnp.einsum('bhnid,nbhde->bhnie', qc * q_decay[None, :, None, :, None], prev)
    return (o_intra + o_cross).reshape(B, H, S, Dh)


def _retention(q, k, v, logit_fwd, logit_bwd):
    B, S, _ = q.shape

    def heads(t):
        return t.astype(jnp.float32).reshape(B, S, N_RET_HEADS, HEAD_DIM).transpose(0, 2, 1, 3)

    qh = _rotary(heads(q)) * (HEAD_DIM ** -0.5)
    kh = _rotary(heads(k))
    vh = heads(v)
    lf = jax.nn.log_sigmoid(logit_fwd.astype(jnp.float32))
    lb = jax.nn.log_sigmoid(logit_bwd.astype(jnp.float32))
    fwd = _retention_direction(qh, kh, vh, lf, True)
    bwd = _retention_direction(qh[:, :, ::-1], kh[:, :, ::-1], vh[:, :, ::-1], lb, False)[:, :, ::-1]
    o = fwd + bwd
    o = o * lax.rsqrt(jnp.mean(o * o, axis=-1, keepdims=True) + EPS)
    return o.transpose(0, 2, 1, 3).reshape(B, S, D_GROUP).astype(q.dtype)


def _conformer_conv(a, b, conv_w, conv_b, ln_g, ln_b, w_pw):
    u = a * jax.nn.sigmoid(b)
    y = lax.conv_general_dilated(
        u, conv_w[:, None, :].astype(u.dtype), window_strides=(1,),
        padding=[(CONV_WIDTH // 2, CONV_WIDTH // 2)],
        dimension_numbers=('NWC', 'WIO', 'NWC'), feature_group_count=D_GROUP) + conv_b
    yf = y.astype(jnp.float32)
    mu = jnp.mean(yf, axis=-1, keepdims=True)
    var = jnp.mean((yf - mu) ** 2, axis=-1, keepdims=True)
    y = ((yf - mu) * lax.rsqrt(var + EPS)).astype(u.dtype) * ln_g + ln_b
    return jax.nn.silu(y) @ w_pw


def setup_inputs(seed: int = 0) -> dict:
    key = jax.random.key(seed)
    ks = jax.random.split(key, 17)
    f32 = jnp.float32
    nrm = lambda k, shape, s: jax.random.normal(k, shape, f32) * s
    gam = 1.0 - 2.0 ** (-5.0 - jnp.arange(N_RET_HEADS, dtype=f32))
    ret_logit0 = jnp.log(gam) - jnp.log1p(-gam)
    return {
        "x": nrm(ks[0], (BATCH, SEQ, D_MODEL), 1.0),
        "c": nrm(ks[1], (BATCH, D_MODEL), 1.0),
        "norm_g": 1.0 + nrm(ks[2], (DEPTH, D_MODEL), 0.01),
        "w_ada": nrm(ks[3], (DEPTH, D_MODEL, 3 * D_MODEL), 0.5 * D_MODEL ** -0.5),
        "b_ada": nrm(ks[4], (DEPTH, 3 * D_MODEL), 0.01),
        "w_in": nrm(ks[5], (DEPTH, D_MODEL, D_IN), D_MODEL ** -0.5),
        "w_fft": nrm(ks[6], (DEPTH, D_GROUP, D_GROUP), D_GROUP ** -0.5),
        "na_rel_bias": nrm(ks[7], (DEPTH, N_NA_HEADS, 2 * NA_KH_MAX - 1, 2 * NA_KW - 1), 0.02),
        "ret_logit_fwd": ret_logit0[None] + nrm(ks[8], (DEPTH, N_RET_HEADS), 0.01),
        "ret_logit_bwd": ret_logit0[None] + nrm(ks[9], (DEPTH, N_RET_HEADS), 0.01),
        "conv_w": nrm(ks[10], (DEPTH, CONV_WIDTH, D_GROUP), CONV_WIDTH ** -0.5),
        "conv_b": nrm(ks[11], (DEPTH, D_GROUP), 0.01),
        "conv_ln_g": 1.0 + nrm(ks[12], (DEPTH, D_GROUP), 0.01),
        "conv_ln_b": nrm(ks[13], (DEPTH, D_GROUP), 0.01),
        "conv_w_pw": nrm(ks[14], (DEPTH, D_GROUP, D_GROUP), D_GROUP ** -0.5),
        "w_out": nrm(ks[15], (DEPTH, D_MIX, D_MODEL), D_MIX ** -0.5),
        "final_g": 1.0 + nrm(ks[16], (D_MODEL,), 0.01),
    }


def reference(x, c, norm_g, w_ada, b_ada, w_in, w_fft, na_rel_bias, ret_logit_fwd, ret_logit_bwd,
              conv_w, conv_b, conv_ln_g, conv_ln_b, conv_w_pw, w_out, final_g):
    c_act = jax.nn.silu(c)
    for l in range(DEPTH):
        mod = c_act @ w_ada[l] + b_ada[l]
        shift, scale, gate = jnp.split(mod, 3, axis=-1)
        h = _rmsnorm(x, norm_g[l]) * (1.0 + scale[:, None, :]) + shift[:, None, :]
        z = h @ w_in[l]
        (f_x, f_g, na_q, na_k, na_v, na_g, r_q, r_k, r_v, r_g,
         cv_a, cv_b, cv_g) = jnp.split(z, N_IN_PIECES, axis=-1)
        o_fft = _fourier_mix(f_x, w_fft[l]) * jax.nn.silu(f_g)
        o_na = _neighbourhood_attention(na_q, na_k, na_v, na_rel_bias[l]) * jax.nn.silu(na_g)
        o_ret = _retention(r_q, r_k, r_v, ret_logit_fwd[l], ret_logit_bwd[l]) * jax.nn.silu(r_g)
        o_cv = _conformer_conv(cv_a, cv_b, conv_w[l], conv_b[l], conv_ln_g[l], conv_ln_b[l],
                               conv_w_pw[l]) * jax.nn.silu(cv_g)
        y = jnp.concatenate([o_fft, o_na, o_ret, o_cv], axis=-1) @ w_out[l]
        x = x + gate[:, None, :] * y
    return _rmsnorm(x, final_g)
```

```python
import functools
import math

import numpy as np
import jax
import jax.numpy as jnp
from jax import lax
from jax.experimental import pallas as pl
from jax.experimental.pallas import tpu as pltpu

F32 = jnp.float32
BF16 = jnp.bfloat16

D_MODEL = 2048
DEPTH = 2
GRID_W = 64
D_GROUP = 512
HEAD_DIM = 64
N_HEADS = D_GROUP // HEAD_DIM
N_FFT_GROUPS = 4
FFT_GROUP_DIM = D_GROUP // N_FFT_GROUPS
NA_KH = 8
NA_KW = 16
CONV_WIDTH = 31
CONV_HALF = CONV_WIDTH // 2
ROPE_BASE = 10000.0
EPS = 1e-6
N_PIECES = 13
(P_FX, P_FG, P_NQ, P_NK, P_NV, P_NG, P_RQ, P_RK, P_RV, P_RG, P_CA, P_CB, P_CG) = range(N_PIECES)

MXU_DIM = 256
HEADS_PER_TILE = MXU_DIM // HEAD_DIM
VMEM_LIMIT = 56 * 1024 * 1024
SUBLANES_BF16 = 16

NEG_BIG = -1e30


def _params(sem, vmem=VMEM_LIMIT):
    return pltpu.CompilerParams(dimension_semantics=sem, vmem_limit_bytes=vmem)


def _silu(t):
    return t * jax.nn.sigmoid(t)


ADA_TN = 768
ADA_ROWS = 8


def _ada_kernel(c_ref, w_ref, b_ref, o_ref):
    ca = _silu(c_ref[...]).astype(BF16)
    w = w_ref[...].astype(BF16)
    o_ref[...] = jnp.dot(ca, w, preferred_element_type=F32) + b_ref[...]


def _ada(c, w_ada, b_ada):
    batch = c.shape[0]
    c_pad = jnp.zeros((ADA_ROWS, D_MODEL), F32).at[:batch].set(c)
    n3 = 3 * D_MODEL
    out = pl.pallas_call(
        _ada_kernel,
        name="ada",
        grid=(DEPTH, n3 // ADA_TN),
        in_specs=[
            pl.BlockSpec((ADA_ROWS, D_MODEL), lambda l, j: (0, 0)),
            pl.BlockSpec((None, D_MODEL, ADA_TN), lambda l, j: (l, 0, j)),
            pl.BlockSpec((None, 1, ADA_TN), lambda l, j: (l, 0, j)),
        ],
        out_specs=pl.BlockSpec((None, ADA_ROWS, ADA_TN), lambda l, j: (l, 0, j)),
        out_shape=jax.ShapeDtypeStruct((DEPTH, ADA_ROWS, n3), F32),
        compiler_params=_params(("arbitrary", "arbitrary")),
    )(c_pad, w_ada, b_ada.reshape(DEPTH, 1, n3))
    return out[:, :batch]


INPROJ_TM = 1024


def _inproj_kernel(x_ref, g_ref, sc_ref, sh_ref, w_ref, o_ref, h_ref):
    @pl.when(pl.program_id(2) == 0)
    def _():
        x = x_ref[...]
        ms = jnp.mean(x * x, axis=-1, keepdims=True)
        h = (x * lax.rsqrt(ms + EPS)) * g_ref[...]
        h = h * (1.0 + sc_ref[...]) + sh_ref[...]
        h_ref[...] = h.astype(BF16)

    o_ref[...] = jnp.dot(h_ref[...], w_ref[...], preferred_element_type=F32).astype(BF16)


def _inproj(x, g, scale, shift, w_bf16):
    batch, seq, _ = x.shape
    tm = INPROJ_TM
    return pl.pallas_call(
        _inproj_kernel,
        name="inproj",
        grid=(batch, seq // tm, N_PIECES),
        in_specs=[
            pl.BlockSpec((None, tm, D_MODEL), lambda b, i, j: (b, i, 0)),
            pl.BlockSpec((1, D_MODEL), lambda b, i, j: (0, 0)),
            pl.BlockSpec((None, 1, D_MODEL), lambda b, i, j: (b, 0, 0)),
            pl.BlockSpec((None, 1, D_MODEL), lambda b, i, j: (b, 0, 0)),
            pl.BlockSpec((D_MODEL, D_GROUP), lambda b, i, j: (0, j)),
        ],
        out_specs=pl.BlockSpec((None, None, tm, D_GROUP), lambda b, i, j: (j, b, i, 0)),
        out_shape=jax.ShapeDtypeStruct((N_PIECES, batch, seq, D_GROUP), BF16),
        scratch_shapes=[pltpu.VMEM((tm, D_MODEL), BF16)],
        compiler_params=_params(("arbitrary", "arbitrary", "arbitrary")),
    )(x, g.reshape(1, D_MODEL), scale[:, None, :], shift[:, None, :], w_bf16)


OUTPROJ_TM = 512


def _outproj_kernel(a0_ref, a1_ref, a2_ref, a3_ref, w_ref, x_ref, gate_ref, fg_ref, o_ref, *, final):
    a = jnp.concatenate([a0_ref[...], a1_ref[...], a2_ref[...], a3_ref[...]], axis=-1)
    y = jnp.dot(a, w_ref[...], preferred_element_type=F32)
    xn = x_ref[...] + gate_ref[...] * y
    if final:
        ms = jnp.mean(xn * xn, axis=-1, keepdims=True)
        xn = (xn * lax.rsqrt(ms + EPS)) * fg_ref[...]
    o_ref[...] = xn


def _outproj(mixed, w_bf16, x, gate, final_g, final):
    batch, seq, _ = x.shape
    tm = OUTPROJ_TM
    a_spec = pl.BlockSpec((None, tm, D_GROUP), lambda b, i: (b, i, 0))
    return pl.pallas_call(
        functools.partial(_outproj_kernel, final=final),
        name="outproj",
        grid=(batch, seq // tm),
        in_specs=[
            a_spec, a_spec, a_spec, a_spec,
            pl.BlockSpec((D_MODEL, D_MODEL), lambda b, i: (0, 0)),
            pl.BlockSpec((None, tm, D_MODEL), lambda b, i: (b, i, 0)),
            pl.BlockSpec((None, 1, D_MODEL), lambda b, i: (b, 0, 0)),
            pl.BlockSpec((1, D_MODEL), lambda b, i: (0, 0)),
        ],
        out_specs=pl.BlockSpec((None, tm, D_MODEL), lambda b, i: (b, i, 0)),
        out_shape=jax.ShapeDtypeStruct((batch, seq, D_MODEL), F32),
        compiler_params=_params(("arbitrary", "arbitrary")),
    )(*mixed, w_bf16, x, gate[:, None, :], final_g.reshape(1, D_MODEL))


FFT_LANES = 4096
FFT_K1_BLOCK = 8


def _fft_constants(seq):
    rows = seq // GRID_W
    assert rows == GRID_W
    n = np.arange(GRID_W)
    ang1 = 2.0 * np.pi * ((n[:, None] * n[None, :]) % GRID_W) / GRID_W
    f1 = np.concatenate([np.cos(ang1), -np.sin(ang1)], axis=0)
    k1 = n[:, None, None]
    k2 = n[None, :, None]
    s2 = n[None, None, :]
    ang2 = 2.0 * np.pi * ((s2 * (k1 + GRID_W * k2)) % seq) / seq
    mr, mi = np.cos(ang2), -np.sin(ang2)
    m2 = np.concatenate([np.concatenate([mr, -mi], axis=2),
                         np.concatenate([mi, mr], axis=2)], axis=1)
    c = np.arange(FFT_GROUP_DIM)
    angc = 2.0 * np.pi * ((c[:, None] * c[None, :]) % FFT_GROUP_DIM) / FFT_GROUP_DIM
    fc = np.concatenate([np.cos(angc), np.sin(angc)], axis=0)
    return tuple(jnp.asarray(t, F32).astype(BF16) for t in (f1, m2, fc))


def _fft1_kernel(u_ref, f1_ref, g_ref):
    g_ref[...] = jnp.dot(f1_ref[...], u_ref[...], preferred_element_type=F32).astype(BF16)


def _fft2_kernel(g_ref, m2_ref, fc_ref, w_ref, gate_ref, o_ref, *, norm):
    kb = FFT_K1_BLOCK
    xr, xi = [], []
    for kk in range(kb):
        gk = jnp.concatenate([g_ref[0, kk], g_ref[1, kk]], axis=0)
        xk = jnp.dot(m2_ref[kk], gk, preferred_element_type=F32)
        xr.append(xk[:GRID_W])
        xi.append(xk[GRID_W:])
    xr = jnp.concatenate(xr, axis=0).astype(BF16)
    xi = jnp.concatenate(xi, axis=0).astype(BF16)
    ys = []
    for g in range(N_FFT_GROUPS):
        sl = slice(g * FFT_GROUP_DIM, (g + 1) * FFT_GROUP_DIM)
        lhs = jnp.concatenate([xr[:, sl], xi[:, sl]], axis=-1)
        ys.append(jnp.dot(lhs, fc_ref[...], preferred_element_type=F32))
    y = (jnp.concatenate(ys, axis=-1) * norm).astype(BF16)
    o = jnp.dot(y, w_ref[...], preferred_element_type=F32)
    for kk in range(kb):
        gate = gate_ref[:, kk * D_GROUP:(kk + 1) * D_GROUP].astype(F32)
        o_ref[:, kk * D_GROUP:(kk + 1) * D_GROUP] = (
            o[kk * GRID_W:(kk + 1) * GRID_W] * _silu(gate)).astype(BF16)


def _fourier(zp, w_fft_bf16, consts):
    _, batch, seq, _ = zp.shape
    f1, m2, fc = consts
    rows = seq // GRID_W
    lanes = GRID_W * D_GROUP
    zrow = zp.reshape(N_PIECES, batch, rows, lanes)
    g = pl.pallas_call(
        _fft1_kernel,
        name="fft_rows",
        grid=(batch, lanes // FFT_LANES),
        in_specs=[
            pl.BlockSpec((None, None, rows, FFT_LANES), lambda b, j: (P_FX, b, 0, j)),
            pl.BlockSpec((2 * GRID_W, rows), lambda b, j: (0, 0)),
        ],
        out_specs=pl.BlockSpec((None, 2 * GRID_W, FFT_LANES), lambda b, j: (b, 0, j)),
        out_shape=jax.ShapeDtypeStruct((batch, 2 * GRID_W, lanes), BF16),
        compiler_params=_params(("arbitrary", "arbitrary")),
    )(zrow, f1)
    g5 = g.reshape(batch, 2, GRID_W, GRID_W, D_GROUP)
    kb = FFT_K1_BLOCK
    out = pl.pallas_call(
        functools.partial(_fft2_kernel, norm=1.0 / math.sqrt(seq * FFT_GROUP_DIM)),
        name="fft_cols",
        grid=(batch, GRID_W // kb),
        in_specs=[
            pl.BlockSpec((None, 2, kb, GRID_W, D_GROUP), lambda b, j: (b, 0, j, 0, 0)),
            pl.BlockSpec((kb, 2 * GRID_W, 2 * GRID_W), lambda b, j: (j, 0, 0)),
            pl.BlockSpec((2 * FFT_GROUP_DIM, FFT_GROUP_DIM), lambda b, j: (0, 0)),
            pl.BlockSpec((D_GROUP, D_GROUP), lambda b, j: (0, 0)),
            pl.BlockSpec((None, None, rows, kb * D_GROUP), lambda b, j: (P_FG, b, 0, j)),
        ],
        out_specs=pl.BlockSpec((None, rows, kb * D_GROUP), lambda b, j: (b, 0, j)),
        out_shape=jax.ShapeDtypeStruct((batch, rows, lanes), BF16),
        compiler_params=_params(("arbitrary", "arbitrary")),
    )(g5, m2, fc, w_fft_bf16, zrow)
    return out.reshape(batch, seq, D_GROUP)


NA_ROWS_PER_STEP = 4


def _na_bias_table(rel_bias, rows):
    kh = min(NA_KH, rows)
    col = np.arange(GRID_W)
    col_start = np.clip(col - NA_KW // 2, 0, GRID_W - NA_KW)
    rel_c = col[None, :] - col_start[:, None]
    col_in = (rel_c >= 0) & (rel_c < NA_KW)
    dc = np.clip(col[None, :] - col[:, None] + NA_KW - 1, 0, 2 * NA_KW - 2)
    off = np.arange(kh)[:, None]
    a = np.arange(kh)[None, :]
    dr = a - off + NA_KH - 1
    tbl = rel_bias[:, dr[:, :, None, None], dc[None, None, :, :]]
    tbl = jnp.where(col_in[None, None, None], tbl.astype(F32), NEG_BIG)
    tbl = tbl.transpose(1, 0, 3, 2, 4)
    return tbl.reshape(kh, N_HEADS, GRID_W, kh * GRID_W)


def _head_select_mask():
    r = lax.broadcasted_iota(jnp.int32, (MXU_DIM, MXU_DIM), 0) // HEAD_DIM
    c = lax.broadcasted_iota(jnp.int32, (MXU_DIM, MXU_DIM), 1) // HEAD_DIM
    return r == c


def _stack_heads(t):
    n = t.shape[0]
    reps = jnp.concatenate([t] * HEADS_PER_TILE, axis=0)
    row_head = lax.broadcasted_iota(jnp.int32, reps.shape, 0) // n
    lane_head = lax.broadcasted_iota(jnp.int32, reps.shape, 1) // HEAD_DIM
    return jnp.where(row_head == lane_head, reps, jnp.zeros_like(reps))


def _unstack_heads(t4, n):
    lane_head = lax.broadcasted_iota(jnp.int32, (n, MXU_DIM), 1) // HEAD_DIM
    out = jnp.zeros((n, MXU_DIM), t4.dtype)
    for h in range(HEADS_PER_TILE):
        out = jnp.where(lane_head == h, t4[h * n:(h + 1) * n], out)
    return out


def _na_kernel(q_ref, k_ref, v_ref, g_ref, bias_ref, o_ref, *, rows):
    kh = min(NA_KH, rows)
    step = pl.program_id(1)

    def row_body(rr, carry):
        r = step * NA_ROWS_PER_STEP + rr
        r_start = jnp.clip(r - kh // 2, 0, rows - kh)
        variant = r - r_start
        q_off = pl.multiple_of(rr * GRID_W, GRID_W)
        k_off = pl.multiple_of(r_start * GRID_W, GRID_W)
        for cg in range(D_GROUP // MXU_DIM):
            lanes = slice(cg * MXU_DIM, (cg + 1) * MXU_DIM)
            q = q_ref[pl.ds(q_off, GRID_W), lanes]
            kw = k_ref[pl.ds(k_off, kh * GRID_W), lanes]
            vw = v_ref[pl.ds(k_off, kh * GRID_W), lanes]
            q4 = _stack_heads(q)
            s = lax.dot_general(q4, kw, (((1,), (1,)), ((), ())), preferred_element_type=F32)
            bias = bias_ref[variant, cg * HEADS_PER_TILE:(cg + 1) * HEADS_PER_TILE]
            s = s * (HEAD_DIM ** -0.5) + bias.reshape(HEADS_PER_TILE * GRID_W, kh * GRID_W)
            m = jnp.max(s, axis=-1, keepdims=True)
            e = jnp.exp(s - m)
            l = jnp.sum(e, axis=-1, keepdims=True)
            o4 = jnp.dot(e.astype(BF16), vw, preferred_element_type=F32) / l
            o = _unstack_heads(o4, GRID_W)
            gate = g_ref[pl.ds(q_off, GRID_W), lanes].astype(F32)
            o_ref[pl.ds(q_off, GRID_W), lanes] = (o * _silu(gate)).astype(BF16)
        return carry

    lax.fori_loop(0, NA_ROWS_PER_STEP, row_body, 0)


def _neighbourhood(zp, bias_tbl):
    _, batch, seq, _ = zp.shape
    rows = seq // GRID_W
    tq = NA_ROWS_PER_STEP * GRID_W
    q_spec = lambda p: pl.BlockSpec((None, None, tq, D_GROUP), lambda b, i: (p, b, i, 0))
    kv_spec = lambda p: pl.BlockSpec((None, None, seq, D_GROUP), lambda b, i: (p, b, 0, 0))
    return pl.pallas_call(
        functools.partial(_na_kernel, rows=rows),
        name="natten",
        grid=(batch, rows // NA_ROWS_PER_STEP),
        in_specs=[
            q_spec(P_NQ), kv_spec(P_NK), kv_spec(P_NV), q_spec(P_NG),
            pl.BlockSpec(bias_tbl.shape, lambda b, i: (0, 0, 0, 0), pipeline_mode=pl.Buffered(1)),
        ],
        out_specs=pl.BlockSpec((None, tq, D_GROUP), lambda b, i: (b, i, 0)),
        out_shape=jax.ShapeDtypeStruct((batch, seq, D_GROUP), BF16),
        compiler_params=_params(("arbitrary", "arbitrary")),
    )(zp, zp, zp, zp, bias_tbl)


RET_CHUNK = MXU_DIM


def _log_sigmoid(t):
    return jnp.minimum(t, 0.0) - jnp.log1p(jnp.exp(-jnp.abs(t)))


def _rope_tables(seq):
    half = HEAD_DIM // 2
    inv = ROPE_BASE ** (-jnp.arange(half, dtype=F32) / half)
    ang = jnp.arange(seq, dtype=F32)[:, None] * inv[None, :]
    cos, sin = jnp.cos(ang), jnp.sin(ang)
    cos2 = jnp.concatenate([cos, cos], axis=-1)
    sin2 = jnp.concatenate([-sin, sin], axis=-1)
    return jnp.tile(cos2, (1, 2)), jnp.tile(sin2, (1, 2))


def _ret_kernel(lf_s_ref, lb_s_ref, q_ref, k_ref, v_ref, g_ref, cos_ref, sin_ref, lfl_ref, lbl_ref,
                o_ref, qr_ref, kr_ref, ob_ref, dmat_ref, dec_ref, state_ref, *, seq):
    C = RET_CHUNK
    n_chunks = seq // C
    cg = pl.program_id(1)

    ri = lax.broadcasted_iota(jnp.int32, (C, C), 0)
    ci = lax.broadcasted_iota(jnp.int32, (C, C), 1)
    diff = (ri - ci).astype(F32)
    for hh in range(HEADS_PER_TILE):
        h = cg * HEADS_PER_TILE + hh
        lf = _log_sigmoid(jnp.full((C, C), lf_s_ref[h], F32))
        lb = _log_sigmoid(jnp.full((C, C), lb_s_ref[h], F32))
        dmat_ref[hh] = jnp.where(diff >= 0, jnp.exp(lf * diff), jnp.exp(lb * (-diff)))
    lfl = _log_sigmoid(lfl_ref[...])
    lbl = _log_sigmoid(lbl_ref[...])
    idx = lax.broadcasted_iota(jnp.int32, (C, MXU_DIM), 0).astype(F32)
    dec_ref[0] = jnp.exp(lfl * (idx + 1.0))
    dec_ref[1] = jnp.exp(lfl * (C - 1.0 - idx))
    dec_ref[2] = jnp.exp(lbl * (C - idx))
    dec_ref[3] = jnp.exp(lbl * idx)
    cd_f = jnp.exp(lfl * float(C))
    cd_b = jnp.exp(lbl * float(C))

    def rope(t, rows):
        lane = lax.broadcasted_iota(jnp.int32, t.shape, 1)
        first_half = (lane % HEAD_DIM) < (HEAD_DIM // 2)
        cs = cos_ref[rows, :]
        sn = sin_ref[rows, :]
        cs = jnp.concatenate([cs, cs], axis=-1)
        sn = jnp.concatenate([sn, sn], axis=-1)
        swapped = jnp.where(first_half, pltpu.roll(t, MXU_DIM - HEAD_DIM // 2, 1),
                            pltpu.roll(t, HEAD_DIM // 2, 1))
        return t * cs + swapped * sn

    def kv_update(k, v, kd, cd):
        kv = lax.dot_general((k * kd).astype(BF16), v, (((0,), (0,)), ((), ())),
                             preferred_element_type=F32)
        state_ref[...] = cd * state_ref[...] + jnp.where(_head_select_mask(), kv, 0.0)

    state_ref[...] = jnp.zeros_like(state_ref)

    def bwd_body(i, carry):
        n = n_chunks - 1 - i
        rows = pl.ds(pl.multiple_of(n * C, C), C)
        q = rope(q_ref[rows, :].astype(F32), rows) * (HEAD_DIM ** -0.5)
        k = rope(k_ref[rows, :].astype(F32), rows)
        qr_ref[rows, :] = q.astype(BF16)
        kr_ref[rows, :] = k.astype(BF16)
        ob_ref[rows, :] = jnp.dot((q * dec_ref[2]).astype(BF16), state_ref[...].astype(BF16),
                                  preferred_element_type=F32)
        kv_update(k, v_ref[rows, :], dec_ref[3], cd_b)
        return carry

    lax.fori_loop(0, n_chunks, bwd_body, 0)

    state_ref[...] = jnp.zeros_like(state_ref)

    def fwd_body(n, carry):
        rows = pl.ds(pl.multiple_of(n * C, C), C)
        qb = qr_ref[rows, :]
        kb = kr_ref[rows, :]
        v = v_ref[rows, :]
        o_cross = jnp.dot((qb.astype(F32) * dec_ref[0]).astype(BF16), state_ref[...].astype(BF16),
                          preferred_element_type=F32)
        q4 = _stack_heads(qb)
        sc = lax.dot_general(q4, kb, (((1,), (1,)), ((), ())), preferred_element_type=F32)
        sc = sc * dmat_ref[...].reshape(HEADS_PER_TILE * C, C)
        o4 = jnp.dot(sc.astype(BF16), v, preferred_element_type=F32)
        o = _unstack_heads(o4, C) + o_cross + ob_ref[rows, :]
        sq = o * o
        lane_head = lax.broadcasted_iota(jnp.int32, (C, MXU_DIM), 1) // HEAD_DIM
        inv = jnp.zeros((C, MXU_DIM), F32)
        for hh in range(HEADS_PER_TILE):
            sel = lane_head == hh
            ms = jnp.sum(jnp.where(sel, sq, 0.0), axis=-1, keepdims=True) * (1.0 / HEAD_DIM)
            inv = jnp.where(sel, lax.rsqrt(ms + EPS), inv)
        gate = g_ref[rows, :].astype(F32)
        o_ref[rows, :] = (o * inv * _silu(gate)).astype(BF16)
        kv_update(kb.astype(F32), v, dec_ref[1], cd_f)
        return carry

    lax.fori_loop(0, n_chunks, fwd_body, 0)


def _retention(zp, logit_f, logit_b, rope_tbl):
    _, batch, seq, _ = zp.shape
    cos_t, sin_t = rope_tbl
    n_cg = D_GROUP // MXU_DIM
    spec = lambda p: pl.BlockSpec((None, None, seq, MXU_DIM), lambda b, c, *_: (p, b, 0, c))
    tbl_spec = pl.BlockSpec((seq, 2 * HEAD_DIM), lambda b, c, *_: (0, 0))
    lane_spec = pl.BlockSpec((1, MXU_DIM), lambda b, c, *_: (0, c))
    lane_f = jnp.repeat(logit_f.astype(F32), HEAD_DIM)[None, :]
    lane_b = jnp.repeat(logit_b.astype(F32), HEAD_DIM)[None, :]
    return pl.pallas_call(
        functools.partial(_ret_kernel, seq=seq),
        name="retention",
        grid_spec=pltpu.PrefetchScalarGridSpec(
            num_scalar_prefetch=2,
            grid=(batch, n_cg),
            in_specs=[spec(P_RQ), spec(P_RK), spec(P_RV), spec(P_RG), tbl_spec, tbl_spec,
                      lane_spec, lane_spec],
            out_specs=pl.BlockSpec((None, seq, MXU_DIM), lambda b, c, *_: (b, 0, c)),
            scratch_shapes=[
                pltpu.VMEM((seq, MXU_DIM), BF16),
                pltpu.VMEM((seq, MXU_DIM), BF16),
                pltpu.VMEM((seq, MXU_DIM), F32),
                pltpu.VMEM((HEADS_PER_TILE, RET_CHUNK, RET_CHUNK), F32),
                pltpu.VMEM((4, RET_CHUNK, MXU_DIM), F32),
                pltpu.VMEM((MXU_DIM, MXU_DIM), F32),
            ],
        ),
        out_shape=jax.ShapeDtypeStruct((batch, seq, D_GROUP), BF16),
        compiler_params=_params(("arbitrary", "arbitrary")),
    )(logit_f.astype(F32), logit_b.astype(F32), zp, zp, zp, zp, cos_t, sin_t, lane_f, lane_b)


CONV_TS = 512
CONV_HALO = SUBLANES_BF16


def _conv_kernel(a_ref, b_ref, g_ref, cw_ref, cb_ref, lg_ref, lb_ref, w_ref, o_ref, u_ref, *, seq):
    ts, halo = CONV_TS, CONV_HALO
    j = pl.program_id(1)
    n_tiles = seq // ts

    def glu(start, size):
        rows = pl.ds(pl.multiple_of(start, halo), size)
        a = a_ref[rows, :].astype(F32)
        return a * jax.nn.sigmoid(b_ref[rows, :].astype(F32))

    base = j * ts
    u_ref[halo:halo + ts, :] = glu(base, ts)
    lo = glu(jnp.maximum(base - halo, 0), halo)
    u_ref[0:halo, :] = jnp.where(j > 0, lo, 0.0)
    hi = glu(jnp.minimum(base + ts, seq - halo), halo)
    u_ref[halo + ts:, :] = jnp.where(j < n_tiles - 1, hi, 0.0)

    acc = jnp.zeros((ts, D_GROUP), F32)
    for w in range(CONV_WIDTH):
        off = halo - CONV_HALF + w
        acc = acc + u_ref[off:off + ts, :] * cw_ref[w:w + 1, :]
    y = acc + cb_ref[...]
    mu = jnp.mean(y, axis=-1, keepdims=True)
    yc = y - mu
    var = jnp.mean(yc * yc, axis=-1, keepdims=True)
    y = (yc * lax.rsqrt(var + EPS)) * lg_ref[...] + lb_ref[...]
    o = jnp.dot(_silu(y).astype(BF16), w_ref[...], preferred_element_type=F32)
    gate = g_ref[...].astype(F32)
    o_ref[...] = (o * _silu(gate)).astype(BF16)


def _conformer(zp, conv_w, conv_b, ln_g, ln_b, w_pw_bf16):
    _, batch, seq, _ = zp.shape
    ts = CONV_TS
    full = lambda p: pl.BlockSpec((None, None, seq, D_GROUP), lambda b, j: (p, b, 0, 0))
    vec = pl.BlockSpec((1, D_GROUP), lambda b, j: (0, 0))
    return pl.pallas_call(
        functools.partial(_conv_kernel, seq=seq),
        name="conformer",
        grid=(batch, seq // ts),
        in_specs=[
            full(P_CA), full(P_CB),
            pl.BlockSpec((None, None, ts, D_GROUP), lambda b, j: (P_CG, b, j, 0)),
            pl.BlockSpec((CONV_WIDTH, D_GROUP), lambda b, j: (0, 0)),
            vec, vec, vec,
            pl.BlockSpec((D_GROUP, D_GROUP), lambda b, j: (0, 0)),
        ],
        out_specs=pl.BlockSpec((None, ts, D_GROUP), lambda b, j: (b, j, 0)),
        out_shape=jax.ShapeDtypeStruct((batch, seq, D_GROUP), BF16),
        scratch_shapes=[pltpu.VMEM((ts + 2 * CONV_HALO, D_GROUP), F32)],
        compiler_params=_params(("arbitrary", "arbitrary")),
    )(zp, zp, zp, conv_w, conv_b.reshape(1, D_GROUP), ln_g.reshape(1, D_GROUP),
      ln_b.reshape(1, D_GROUP), w_pw_bf16)


def kernel(x, c, norm_g, w_ada, b_ada, w_in, w_fft, na_rel_bias, ret_logit_fwd, ret_logit_bwd,
           conv_w, conv_b, conv_ln_g, conv_ln_b, conv_w_pw, w_out, final_g):
    batch, seq, _ = x.shape
    rows = seq // GRID_W
    mod = _ada(c, w_ada, b_ada)
    fft_consts = _fft_constants(seq)
    rope_tbl = _rope_tables(seq)
    for l in range(DEPTH):
        shift, scale, gate = jnp.split(mod[l], 3, axis=-1)
        zp = _inproj(x, norm_g[l], scale, shift, w_in[l].astype(BF16))
        o_fft = _fourier(zp, w_fft[l].astype(BF16), fft_consts)
        o_na = _neighbourhood(zp, _na_bias_table(na_rel_bias[l], rows))
        o_ret = _retention(zp, ret_logit_fwd[l], ret_logit_bwd[l], rope_tbl)
        o_cv = _conformer(zp, conv_w[l], conv_b[l], conv_ln_g[l], conv_ln_b[l],
                          conv_w_pw[l].astype(BF16))
        x = _outproj((o_fft, o_na, o_ret, o_cv), w_out[l].astype(BF16), x, gate, final_g,
                     final=(l == DEPTH - 1))
    return x
```

```python
import functools
import math

import numpy as np
import jax
import jax.numpy as jnp
from jax import lax
from jax.experimental import pallas as pl
from jax.experimental.pallas import tpu as pltpu

F32 = jnp.float32
BF16 = jnp.bfloat16

D_MODEL = 2048
DEPTH = 2
GRID_W = 64
D_GROUP = 512
HEAD_DIM = 64
N_HEADS = D_GROUP // HEAD_DIM
N_FFT_GROUPS = 4
FFT_GROUP_DIM = D_GROUP // N_FFT_GROUPS
NA_KH = 8
NA_KW = 16
CONV_WIDTH = 31
CONV_HALF = CONV_WIDTH // 2
ROPE_BASE = 10000.0
EPS = 1e-6
N_PIECES = 13
(P_FX, P_FG, P_NQ, P_NK, P_NV, P_NG, P_RQ, P_RK, P_RV, P_RG, P_CA, P_CB, P_CG) = range(N_PIECES)

MXU_DIM = 256
HEADS_PER_TILE = MXU_DIM // HEAD_DIM
VMEM_LIMIT = 56 * 1024 * 1024
SUBLANES_BF16 = 16

NEG_BIG = -1e30


def _params(sem, vmem=VMEM_LIMIT):
    return pltpu.CompilerParams(dimension_semantics=sem, vmem_limit_bytes=vmem)


def _silu(t):
    return t * jax.nn.sigmoid(t)


ADA_TN = 768
ADA_ROWS = 8


def _ada_kernel(c_ref, w_ref, b_ref, o_ref):
    ca = _silu(c_ref[...]).astype(BF16)
    w = w_ref[...].astype(BF16)
    o_ref[...] = jnp.dot(ca, w, preferred_element_type=F32) + b_ref[...]


def _ada(c, w_ada, b_ada):
    batch = c.shape[0]
    c_pad = jnp.zeros((ADA_ROWS, D_MODEL), F32).at[:batch].set(c)
    n3 = 3 * D_MODEL
    out = pl.pallas_call(
        _ada_kernel,
        name="ada",
        grid=(DEPTH, n3 // ADA_TN),
        in_specs=[
            pl.BlockSpec((ADA_ROWS, D_MODEL), lambda l, j: (0, 0)),
            pl.BlockSpec((None, D_MODEL, ADA_TN), lambda l, j: (l, 0, j)),
            pl.BlockSpec((None, 1, ADA_TN), lambda l, j: (l, 0, j)),
        ],
        out_specs=pl.BlockSpec((None, ADA_ROWS, ADA_TN), lambda l, j: (l, 0, j)),
        out_shape=jax.ShapeDtypeStruct((DEPTH, ADA_ROWS, n3), F32),
        compiler_params=_params(("arbitrary", "arbitrary")),
    )(c_pad, w_ada, b_ada.reshape(DEPTH, 1, n3))
    return out[:, :batch]


INPROJ_TM = 1024


def _inproj_kernel(x_ref, g_ref, sc_ref, sh_ref, w_ref, o_ref, h_ref):
    @pl.when(pl.program_id(2) == 0)
    def _():
        x = x_ref[...]
        ms = jnp.mean(x * x, axis=-1, keepdims=True)
        h = (x * lax.rsqrt(ms + EPS)) * g_ref[...]
        h = h * (1.0 + sc_ref[...]) + sh_ref[...]
        h_ref[...] = h.astype(BF16)

    o_ref[...] = jnp.dot(h_ref[...], w_ref[...], preferred_element_type=F32).astype(BF16)


def _inproj(x, g, scale, shift, w_bf16):
    batch, seq, _ = x.shape
    tm = INPROJ_TM
    return pl.pallas_call(
        _inproj_kernel,
        name="inproj",
        grid=(batch, seq // tm, N_PIECES),
        in_specs=[
            pl.BlockSpec((None, tm, D_MODEL), lambda b, i, j: (b, i, 0)),
            pl.BlockSpec((1, D_MODEL), lambda b, i, j: (0, 0)),
            pl.BlockSpec((None, 1, D_MODEL), lambda b, i, j: (b, 0, 0)),
            pl.BlockSpec((None, 1, D_MODEL), lambda b, i, j: (b, 0, 0)),
            pl.BlockSpec((D_MODEL, D_GROUP), lambda b, i, j: (0, j)),
        ],
        out_specs=pl.BlockSpec((None, None, tm, D_GROUP), lambda b, i, j: (j, b, i, 0)),
        out_shape=jax.ShapeDtypeStruct((N_PIECES, batch, seq, D_GROUP), BF16),
        scratch_shapes=[pltpu.VMEM((tm, D_MODEL), BF16)],
        compiler_params=_params(("arbitrary", "arbitrary", "arbitrary")),
    )(x, g.reshape(1, D_MODEL), scale[:, None, :], shift[:, None, :], w_bf16)


OUTPROJ_TM = 512


def _outproj_kernel(a0_ref, a1_ref, a2_ref, a3_ref, w_ref, x_ref, gate_ref, fg_ref, o_ref, *, final):
    a = jnp.concatenate([a0_ref[...], a1_ref[...], a2_ref[...], a3_ref[...]], axis=-1)
    y = jnp.dot(a, w_ref[...], preferred_element_type=F32)
    xn = x_ref[...] + gate_ref[...] * y
    if final:
        ms = jnp.mean(xn * xn, axis=-1, keepdims=True)
        xn = (xn * lax.rsqrt(ms + EPS)) * fg_ref[...]
    o_ref[...] = xn


def _outproj(mixed, w_bf16, x, gate, final_g, final):
    batch, seq, _ = x.shape
    tm = OUTPROJ_TM
    a_spec = pl.BlockSpec((None, tm, D_GROUP), lambda b, i: (b, i, 0))
    return pl.pallas_call(
        functools.partial(_outproj_kernel, final=final),
        name="outproj",
        grid=(batch, seq // tm),
        in_specs=[
            a_spec, a_spec, a_spec, a_spec,
            pl.BlockSpec((D_MODEL, D_MODEL), lambda b, i: (0, 0)),
            pl.BlockSpec((None, tm, D_MODEL), lambda b, i: (b, i, 0)),
            pl.BlockSpec((None, 1, D_MODEL), lambda b, i: (b, 0, 0)),
            pl.BlockSpec((1, D_MODEL), lambda b, i: (0, 0)),
        ],
        out_specs=pl.BlockSpec((None, tm, D_MODEL), lambda b, i: (b, i, 0)),
        out_shape=jax.ShapeDtypeStruct((batch, seq, D_MODEL), F32),
        compiler_params=_params(("arbitrary", "arbitrary")),
    )(*mixed, w_bf16, x, gate[:, None, :], final_g.reshape(1, D_MODEL))


FFT_LANES = 4096
FFT_K1_BLOCK = 8


def _fft_constants(seq):
    rows = seq // GRID_W
    assert rows == GRID_W
    n = np.arange(GRID_W)
    ang1 = 2.0 * np.pi * ((n[:, None] * n[None, :]) % GRID_W) / GRID_W
    f1 = np.concatenate([np.cos(ang1), -np.sin(ang1)], axis=0)
    k1 = n[:, None, None]
    k2 = n[None, :, None]
    s2 = n[None, None, :]
    ang2 = 2.0 * np.pi * ((s2 * (k1 + GRID_W * k2)) % seq) / seq
    mr, mi = np.cos(ang2), -np.sin(ang2)
    m2 = np.concatenate([np.concatenate([mr, -mi], axis=2),
                         np.concatenate([mi, mr], axis=2)], axis=1)
    c = np.arange(FFT_GROUP_DIM)
    angc = 2.0 * np.pi * ((c[:, None] * c[None, :]) % FFT_GROUP_DIM) / FFT_GROUP_DIM
    fc = np.concatenate([np.cos(angc), np.sin(angc)], axis=0)
    return tuple(jnp.asarray(t, F32).astype(BF16) for t in (f1, m2, fc))


def _fft1_kernel(u_ref, f1_ref, g_ref):
    g_ref[...] = jnp.dot(f1_ref[...], u_ref[...], preferred_element_type=F32).astype(BF16)


def _fft2_kernel(g_ref, m2_ref, fc_ref, w_ref, gate_ref, o_ref, *, norm):
    kb = FFT_K1_BLOCK
    xr, xi = [], []
    for kk in range(kb):
        gk = jnp.concatenate([g_ref[0, kk], g_ref[1, kk]], axis=0)
        xk = jnp.dot(m2_ref[kk], gk, preferred_element_type=F32)
        xr.append(xk[:GRID_W])
        xi.append(xk[GRID_W:])
    xr = jnp.concatenate(xr, axis=0).astype(BF16)
    xi = jnp.concatenate(xi, axis=0).astype(BF16)
    ys = []
    for g in range(N_FFT_GROUPS):
        sl = slice(g * FFT_GROUP_DIM, (g + 1) * FFT_GROUP_DIM)
        lhs = jnp.concatenate([xr[:, sl], xi[:, sl]], axis=-1)
        ys.append(jnp.dot(lhs, fc_ref[...], preferred_element_type=F32))
    y = (jnp.concatenate(ys, axis=-1) * norm).astype(BF16)
    o = jnp.dot(y, w_ref[...], preferred_element_type=F32)
    for kk in range(kb):
        gate = gate_ref[:, kk * D_GROUP:(kk + 1) * D_GROUP].astype(F32)
        o_ref[:, kk * D_GROUP:(kk + 1) * D_GROUP] = (
            o[kk * GRID_W:(kk + 1) * GRID_W] * _silu(gate)).astype(BF16)


def _fourier(zp, w_fft_bf16, consts):
    _, batch, seq, _ = zp.shape
    f1, m2, fc = consts
    rows = seq // GRID_W
    lanes = GRID_W * D_GROUP
    assert (P_FX, P_FG) == (0, 1)
    zrow = zp[:2].reshape(2, batch, rows, lanes)
    g = pl.pallas_call(
        _fft1_kernel,
        name="fft_rows",
        grid=(batch, lanes // FFT_LANES),
        in_specs=[
            pl.BlockSpec((None, None, rows, FFT_LANES), lambda b, j: (P_FX, b, 0, j)),
            pl.BlockSpec((2 * GRID_W, rows), lambda b, j: (0, 0)),
        ],
        out_specs=pl.BlockSpec((None, 2 * GRID_W, FFT_LANES), lambda b, j: (b, 0, j)),
        out_shape=jax.ShapeDtypeStruct((batch, 2 * GRID_W, lanes), BF16),
        compiler_params=_params(("arbitrary", "arbitrary")),
    )(zrow, f1)
    g5 = g.reshape(batch, 2, GRID_W, GRID_W, D_GROUP)
    kb = FFT_K1_BLOCK
    out = pl.pallas_call(
        functools.partial(_fft2_kernel, norm=1.0 / math.sqrt(seq * FFT_GROUP_DIM)),
        name="fft_cols",
        grid=(batch, GRID_W // kb),
        in_specs=[
            pl.BlockSpec((None, 2, kb, GRID_W, D_GROUP), lambda b, j: (b, 0, j, 0, 0)),
            pl.BlockSpec((kb, 2 * GRID_W, 2 * GRID_W), lambda b, j: (j, 0, 0)),
            pl.BlockSpec((2 * FFT_GROUP_DIM, FFT_GROUP_DIM), lambda b, j: (0, 0)),
            pl.BlockSpec((D_GROUP, D_GROUP), lambda b, j: (0, 0)),
            pl.BlockSpec((None, None, rows, kb * D_GROUP), lambda b, j: (P_FG, b, 0, j)),
        ],
        out_specs=pl.BlockSpec((None, rows, kb * D_GROUP), lambda b, j: (b, 0, j)),
        out_shape=jax.ShapeDtypeStruct((batch, rows, lanes), BF16),
        compiler_params=_params(("arbitrary", "arbitrary")),
    )(g5, m2, fc, w_fft_bf16, zrow)
    return out.reshape(batch, seq, D_GROUP)


NA_ROWS_PER_STEP = 4


def _na_bias_table(rel_bias, rows):
    kh = min(NA_KH, rows)
    col = np.arange(GRID_W)
    col_start = np.clip(col - NA_KW // 2, 0, GRID_W - NA_KW)
    rel_c = col[None, :] - col_start[:, None]
    col_in = (rel_c >= 0) & (rel_c < NA_KW)
    dc = np.clip(col[None, :] - col[:, None] + NA_KW - 1, 0, 2 * NA_KW - 2)
    onehot = dc[None] == np.arange(2 * NA_KW - 1)[:, None, None]
    by_col = jnp.sum(jnp.where(onehot[None, None], rel_bias.astype(F32)[:, :, :, None, None], 0.0),
                     axis=2)
    by_col = jnp.where(col_in[None, None], by_col, NEG_BIG)
    tbl = jnp.stack([by_col[:, NA_KH - 1 - o:NA_KH - 1 - o + kh] for o in range(kh)])
    tbl = tbl.transpose(0, 1, 3, 2, 4)
    return tbl.reshape(kh, N_HEADS, GRID_W, kh * GRID_W)


def _head_select_mask():
    r = lax.broadcasted_iota(jnp.int32, (MXU_DIM, MXU_DIM), 0) // HEAD_DIM
    c = lax.broadcasted_iota(jnp.int32, (MXU_DIM, MXU_DIM), 1) // HEAD_DIM
    return r == c


def _stack_heads(t):
    n = t.shape[0]
    reps = jnp.concatenate([t] * HEADS_PER_TILE, axis=0)
    row_head = lax.broadcasted_iota(jnp.int32, reps.shape, 0) // n
    lane_head = lax.broadcasted_iota(jnp.int32, reps.shape, 1) // HEAD_DIM
    return jnp.where(row_head == lane_head, reps, jnp.zeros_like(reps))


def _unstack_heads(t4, n):
    lane_head = lax.broadcasted_iota(jnp.int32, (n, MXU_DIM), 1) // HEAD_DIM
    out = jnp.zeros((n, MXU_DIM), t4.dtype)
    for h in range(HEADS_PER_TILE):
        out = jnp.where(lane_head == h, t4[h * n:(h + 1) * n], out)
    return out


def _na_kernel(q_ref, k_ref, v_ref, g_ref, bias_ref, o_ref, *, rows):
    kh = min(NA_KH, rows)
    step = pl.program_id(1)

    def row_body(rr, carry):
        r = step * NA_ROWS_PER_STEP + rr
        r_start = jnp.clip(r - kh // 2, 0, rows - kh)
        variant = r - r_start
        q_off = pl.multiple_of(rr * GRID_W, GRID_W)
        k_off = pl.multiple_of(r_start * GRID_W, GRID_W)
        for cg in range(D_GROUP // MXU_DIM):
            lanes = slice(cg * MXU_DIM, (cg + 1) * MXU_DIM)
            q = q_ref[pl.ds(q_off, GRID_W), lanes]
            kw = k_ref[pl.ds(k_off, kh * GRID_W), lanes]
            vw = v_ref[pl.ds(k_off, kh * GRID_W), lanes]
            q4 = _stack_heads(q)
            s = lax.dot_general(q4, kw, (((1,), (1,)), ((), ())), preferred_element_type=F32)
            bias = bias_ref[variant, cg * HEADS_PER_TILE:(cg + 1) * HEADS_PER_TILE]
            s = s * (HEAD_DIM ** -0.5) + bias.reshape(HEADS_PER_TILE * GRID_W, kh * GRID_W)
            m = jnp.max(s, axis=-1, keepdims=True)
            e = jnp.exp(s - m)
            l = jnp.sum(e, axis=-1, keepdims=True)
            o4 = jnp.dot(e.astype(BF16), vw, preferred_element_type=F32) / l
            o = _unstack_heads(o4, GRID_W)
            gate = g_ref[pl.ds(q_off, GRID_W), lanes].astype(F32)
            o_ref[pl.ds(q_off, GRID_W), lanes] = (o * _silu(gate)).astype(BF16)
        return carry

    lax.fori_loop(0, NA_ROWS_PER_STEP, row_body, 0)


def _neighbourhood(zp, bias_tbl):
    _, batch, seq, _ = zp.shape
    rows = seq // GRID_W
    tq = NA_ROWS_PER_STEP * GRID_W
    q_spec = lambda p: pl.BlockSpec((None, None, tq, D_GROUP), lambda b, i: (p, b, i, 0))
    kv_spec = lambda p: pl.BlockSpec((None, None, seq, D_GROUP), lambda b, i: (p, b, 0, 0))
    return pl.pallas_call(
        functools.partial(_na_kernel, rows=rows),
        name="natten",
        grid=(batch, rows // NA_ROWS_PER_STEP),
        in_specs=[
            q_spec(P_NQ), kv_spec(P_NK), kv_spec(P_NV), q_spec(P_NG),
            pl.BlockSpec(bias_tbl.shape, lambda b, i: (0, 0, 0, 0), pipeline_mode=pl.Buffered(1)),
        ],
        out_specs=pl.BlockSpec((None, tq, D_GROUP), lambda b, i: (b, i, 0)),
        out_shape=jax.ShapeDtypeStruct((batch, seq, D_GROUP), BF16),
        compiler_params=_params(("arbitrary", "arbitrary")),
    )(zp, zp, zp, zp, bias_tbl)


RET_CHUNK = MXU_DIM


def _log_sigmoid(t):
    return jnp.minimum(t, 0.0) - jnp.log1p(jnp.exp(-jnp.abs(t)))


def _rope_tables(seq):
    half = HEAD_DIM // 2
    inv = ROPE_BASE ** (-jnp.arange(half, dtype=F32) / half)
    ang = jnp.arange(seq, dtype=F32)[:, None] * inv[None, :]
    cos, sin = jnp.cos(ang), jnp.sin(ang)
    cos2 = jnp.concatenate([cos, cos], axis=-1)
    sin2 = jnp.concatenate([-sin, sin], axis=-1)
    return jnp.tile(cos2, (1, 2)), jnp.tile(sin2, (1, 2))


def _ret_kernel(lf_s_ref, lb_s_ref, q_ref, k_ref, v_ref, g_ref, cos_ref, sin_ref, lfl_ref, lbl_ref,
                o_ref, qr_ref, kr_ref, ob_ref, dmat_ref, dec_ref, state_ref, *, seq):
    C = RET_CHUNK
    n_chunks = seq // C
    cg = pl.program_id(1)

    ri = lax.broadcasted_iota(jnp.int32, (C, C), 0)
    ci = lax.broadcasted_iota(jnp.int32, (C, C), 1)
    diff = (ri - ci).astype(F32)
    for hh in range(HEADS_PER_TILE):
        h = cg * HEADS_PER_TILE + hh
        lf = _log_sigmoid(jnp.full((C, C), lf_s_ref[h], F32))
        lb = _log_sigmoid(jnp.full((C, C), lb_s_ref[h], F32))
        dmat_ref[hh] = jnp.where(diff >= 0, jnp.exp(lf * diff), jnp.exp(lb * (-diff)))
    lfl = _log_sigmoid(lfl_ref[...])
    lbl = _log_sigmoid(lbl_ref[...])
    idx = lax.broadcasted_iota(jnp.int32, (C, MXU_DIM), 0).astype(F32)
    dec_ref[0] = jnp.exp(lfl * (idx + 1.0))
    dec_ref[1] = jnp.exp(lfl * (C - 1.0 - idx))
    dec_ref[2] = jnp.exp(lbl * (C - idx))
    dec_ref[3] = jnp.exp(lbl * idx)
    cd_f = jnp.exp(lfl * float(C))
    cd_b = jnp.exp(lbl * float(C))

    def rope(t, rows):
        lane = lax.broadcasted_iota(jnp.int32, t.shape, 1)
        first_half = (lane % HEAD_DIM) < (HEAD_DIM // 2)
        cs = cos_ref[rows, :]
        sn = sin_ref[rows, :]
        cs = jnp.concatenate([cs, cs], axis=-1)
        sn = jnp.concatenate([sn, sn], axis=-1)
        swapped = jnp.where(first_half, pltpu.roll(t, MXU_DIM - HEAD_DIM // 2, 1),
                            pltpu.roll(t, HEAD_DIM // 2, 1))
        return t * cs + swapped * sn

    def kv_update(k, v, kd, cd):
        kv = lax.dot_general((k * kd).astype(BF16), v, (((0,), (0,)), ((), ())),
                             preferred_element_type=F32)
        state_ref[...] = cd * state_ref[...] + jnp.where(_head_select_mask(), kv, 0.0)

    state_ref[...] = jnp.zeros_like(state_ref)

    def bwd_body(i, carry):
        n = n_chunks - 1 - i
        rows = pl.ds(pl.multiple_of(n * C, C), C)
        q = rope(q_ref[rows, :].astype(F32), rows) * (HEAD_DIM ** -0.5)
        k = rope(k_ref[rows, :].astype(F32), rows)
        qr_ref[rows, :] = q.astype(BF16)
        kr_ref[rows, :] = k.astype(BF16)
        ob_ref[rows, :] = jnp.dot((q * dec_ref[2]).astype(BF16), state_ref[...].astype(BF16),
                                  preferred_element_type=F32)
        kv_update(k, v_ref[rows, :], dec_ref[3], cd_b)
        return carry

    lax.fori_loop(0, n_chunks, bwd_body, 0)

    state_ref[...] = jnp.zeros_like(state_ref)

    def fwd_body(n, carry):
        rows = pl.ds(pl.multiple_of(n * C, C), C)
        qb = qr_ref[rows, :]
        kb = kr_ref[rows, :]
        v = v_ref[rows, :]
        o_cross = jnp.dot((qb.astype(F32) * dec_ref[0]).astype(BF16), state_ref[...].astype(BF16),
                          preferred_element_type=F32)
        q4 = _stack_heads(qb)
        sc = lax.dot_general(q4, kb, (((1,), (1,)), ((), ())), preferred_element_type=F32)
        sc = sc * dmat_ref[...].reshape(HEADS_PER_TILE * C, C)
        o4 = jnp.dot(sc.astype(BF16), v, preferred_element_type=F32)
        o = _unstack_heads(o4, C) + o_cross + ob_ref[rows, :]
        sq = o * o
        lane_head = lax.broadcasted_iota(jnp.int32, (C, MXU_DIM), 1) // HEAD_DIM
        inv = jnp.zeros((C, MXU_DIM), F32)
        for hh in range(HEADS_PER_TILE):
            sel = lane_head == hh
            ms = jnp.sum(jnp.where(sel, sq, 0.0), axis=-1, keepdims=True) * (1.0 / HEAD_DIM)
            inv = jnp.where(sel, lax.rsqrt(ms + EPS), inv)
        gate = g_ref[rows, :].astype(F32)
        o_ref[rows, :] = (o * inv * _silu(gate)).astype(BF16)
        kv_update(kb.astype(F32), v, dec_ref[1], cd_f)
        return carry

    lax.fori_loop(0, n_chunks, fwd_body, 0)


def _retention(zp, logit_f, logit_b, rope_tbl):
    _, batch, seq, _ = zp.shape
    cos_t, sin_t = rope_tbl
    n_cg = D_GROUP // MXU_DIM
    spec = lambda p: pl.BlockSpec((None, None, seq, MXU_DIM), lambda b, c, *_: (p, b, 0, c))
    tbl_spec = pl.BlockSpec((seq, 2 * HEAD_DIM), lambda b, c, *_: (0, 0))
    lane_spec = pl.BlockSpec((1, MXU_DIM), lambda b, c, *_: (0, c))
    lane_f = jnp.repeat(logit_f.astype(F32), HEAD_DIM)[None, :]
    lane_b = jnp.repeat(logit_b.astype(F32), HEAD_DIM)[None, :]
    return pl.pallas_call(
        functools.partial(_ret_kernel, seq=seq),
        name="retention",
        grid_spec=pltpu.PrefetchScalarGridSpec(
            num_scalar_prefetch=2,
            grid=(batch, n_cg),
            in_specs=[spec(P_RQ), spec(P_RK), spec(P_RV), spec(P_RG), tbl_spec, tbl_spec,
                      lane_spec, lane_spec],
            out_specs=pl.BlockSpec((None, seq, MXU_DIM), lambda b, c, *_: (b, 0, c)),
            scratch_shapes=[
                pltpu.VMEM((seq, MXU_DIM), BF16),
                pltpu.VMEM((seq, MXU_DIM), BF16),
                pltpu.VMEM((seq, MXU_DIM), F32),
                pltpu.VMEM((HEADS_PER_TILE, RET_CHUNK, RET_CHUNK), F32),
                pltpu.VMEM((4, RET_CHUNK, MXU_DIM), F32),
                pltpu.VMEM((MXU_DIM, MXU_DIM), F32),
            ],
        ),
        out_shape=jax.ShapeDtypeStruct((batch, seq, D_GROUP), BF16),
        compiler_params=_params(("arbitrary", "arbitrary")),
    )(logit_f.astype(F32), logit_b.astype(F32), zp, zp, zp, zp, cos_t, sin_t, lane_f, lane_b)


CONV_TS = 512
CONV_HALO = SUBLANES_BF16


def _conv_kernel(a_ref, b_ref, g_ref, cw_ref, cb_ref, lg_ref, lb_ref, w_ref, o_ref, u_ref, *, seq):
    ts, halo = CONV_TS, CONV_HALO
    j = pl.program_id(1)
    n_tiles = seq // ts

    def glu(start, size):
        rows = pl.ds(pl.multiple_of(start, halo), size)
        a = a_ref[rows, :].astype(F32)
        return a * jax.nn.sigmoid(b_ref[rows, :].astype(F32))

    base = j * ts
    u_ref[halo:halo + ts, :] = glu(base, ts)
    lo = glu(jnp.maximum(base - halo, 0), halo)
    u_ref[0:halo, :] = jnp.where(j > 0, lo, 0.0)
    hi = glu(jnp.minimum(base + ts, seq - halo), halo)
    u_ref[halo + ts:, :] = jnp.where(j < n_tiles - 1, hi, 0.0)

    acc = jnp.zeros((ts, D_GROUP), F32)
    for w in range(CONV_WIDTH):
        off = halo - CONV_HALF + w
        acc = acc + u_ref[off:off + ts, :] * cw_ref[w:w + 1, :]
    y = acc + cb_ref[...]
    mu = jnp.mean(y, axis=-1, keepdims=True)
    yc = y - mu
    var = jnp.mean(yc * yc, axis=-1, keepdims=True)
    y = (yc * lax.rsqrt(var + EPS)) * lg_ref[...] + lb_ref[...]
    o = jnp.dot(_silu(y).astype(BF16), w_ref[...], preferred_element_type=F32)
    gate = g_ref[...].astype(F32)
    o_ref[...] = (o * _silu(gate)).astype(BF16)


def _conformer(zp, conv_w, conv_b, ln_g, ln_b, w_pw_bf16):
    _, batch, seq, _ = zp.shape
    ts = CONV_TS
    full = lambda p: pl.BlockSpec((None, None, seq, D_GROUP), lambda b, j: (p, b, 0, 0))
    vec = pl.BlockSpec((1, D_GROUP), lambda b, j: (0, 0))
    return pl.pallas_call(
        functools.partial(_conv_kernel, seq=seq),
        name="conformer",
        grid=(batch, seq // ts),
        in_specs=[
            full(P_CA), full(P_CB),
            pl.BlockSpec((None, None, ts, D_GROUP), lambda b, j: (P_CG, b, j, 0)),
            pl.BlockSpec((CONV_WIDTH, D_GROUP), lambda b, j: (0, 0)),
            vec, vec, vec,
            pl.BlockSpec((D_GROUP, D_GROUP), lambda b, j: (0, 0)),
        ],
        out_specs=pl.BlockSpec((None, ts, D_GROUP), lambda b, j: (b, j, 0)),
        out_shape=jax.ShapeDtypeStruct((batch, seq, D_GROUP), BF16),
        scratch_shapes=[pltpu.VMEM((ts + 2 * CONV_HALO, D_GROUP), F32)],
        compiler_params=_params(("arbitrary", "arbitrary")),
    )(zp, zp, zp, conv_w, conv_b.reshape(1, D_GROUP), ln_g.reshape(1, D_GROUP),
      ln_b.reshape(1, D_GROUP), w_pw_bf16)


def kernel(x, c, norm_g, w_ada, b_ada, w_in, w_fft, na_rel_bias, ret_logit_fwd, ret_logit_bwd,
           conv_w, conv_b, conv_ln_g, conv_ln_b, conv_w_pw, w_out, final_g):
    batch, seq, _ = x.shape
    rows = seq // GRID_W
    mod = _ada(c, w_ada, b_ada)
    fft_consts = _fft_constants(seq)
    rope_tbl = _rope_tables(seq)
    for l in range(DEPTH):
        shift, scale, gate = jnp.split(mod[l], 3, axis=-1)
        zp = _inproj(x, norm_g[l], scale, shift, w_in[l].astype(BF16))
        o_fft = _fourier(zp, w_fft[l].astype(BF16), fft_consts)
        o_na = _neighbourhood(zp, _na_bias_table(na_rel_bias[l], rows))
        o_ret = _retention(zp, ret_logit_fwd[l], ret_logit_bwd[l], rope_tbl)
        o_cv = _conformer(zp, conv_w[l], conv_b[l], conv_ln_g[l], conv_ln_b[l],
                          conv_w_pw[l].astype(BF16))
        x = _outproj((o_fft, o_na, o_ret, o_cv), w_out[l].astype(BF16), x, gate, final_g,
                     final=(l == DEPTH - 1))
    return x
```

```python
import functools
import math

import numpy as np
import jax
import jax.numpy as jnp
from jax import lax
from jax.experimental import pallas as pl
from jax.experimental.pallas import tpu as pltpu

F32 = jnp.float32
BF16 = jnp.bfloat16

D_MODEL = 2048
DEPTH = 2
GRID_W = 64
D_GROUP = 512
HEAD_DIM = 64
N_HEADS = D_GROUP // HEAD_DIM
N_FFT_GROUPS = 4
FFT_GROUP_DIM = D_GROUP // N_FFT_GROUPS
NA_KH = 8
NA_KW = 16
CONV_WIDTH = 31
CONV_HALF = CONV_WIDTH // 2
ROPE_BASE = 10000.0
EPS = 1e-6
N_PIECES = 13
(P_FX, P_FG, P_NQ, P_NK, P_NV, P_NG, P_RQ, P_RK, P_RV, P_RG, P_CA, P_CB, P_CG) = range(N_PIECES)

MXU_DIM = 256
HEADS_PER_TILE = MXU_DIM // HEAD_DIM
VMEM_LIMIT = 56 * 1024 * 1024
SUBLANES_BF16 = 16
SUBLANES_F32 = 8
LANES = 128

NEG_BIG = -1e30


def _params(sem, vmem=VMEM_LIMIT):
    return pltpu.CompilerParams(dimension_semantics=sem, vmem_limit_bytes=vmem)


def _sigmoid(t):
    return 0.5 * jnp.tanh(0.5 * t) + 0.5


def _silu(t):
    h = 0.5 * t
    return h + h * jnp.tanh(h)


ADA_TN = 768
ADA_ROWS = 8


def _ada_kernel(c_ref, w_ref, b_ref, o_ref):
    ca = _silu(c_ref[...]).astype(BF16)
    w = w_ref[...].astype(BF16)
    o_ref[...] = jnp.dot(ca, w, preferred_element_type=F32) + b_ref[...]


def _ada(c, w_ada, b_ada):
    batch = c.shape[0]
    c_pad = jnp.zeros((ADA_ROWS, D_MODEL), F32).at[:batch].set(c)
    n3 = 3 * D_MODEL
    out = pl.pallas_call(
        _ada_kernel,
        name="ada",
        grid=(DEPTH, n3 // ADA_TN),
        in_specs=[
            pl.BlockSpec((ADA_ROWS, D_MODEL), lambda l, j: (0, 0)),
            pl.BlockSpec((None, D_MODEL, ADA_TN), lambda l, j: (l, 0, j)),
            pl.BlockSpec((None, 1, ADA_TN), lambda l, j: (l, 0, j)),
        ],
        out_specs=pl.BlockSpec((None, ADA_ROWS, ADA_TN), lambda l, j: (l, 0, j)),
        out_shape=jax.ShapeDtypeStruct((DEPTH, ADA_ROWS, n3), F32),
        compiler_params=_params(("arbitrary", "arbitrary")),
    )(c_pad, w_ada, b_ada.reshape(DEPTH, 1, n3))
    return out[:, :batch]


INPROJ_TM = 1024


def _inproj_kernel(x_ref, g_ref, sc_ref, sh_ref, w_ref, o_ref, h_ref):
    @pl.when(pl.program_id(2) == 0)
    def _():
        x = x_ref[...]
        ms = jnp.mean(x * x, axis=-1, keepdims=True)
        h = (x * lax.rsqrt(ms + EPS)) * g_ref[...]
        h = h * (1.0 + sc_ref[...]) + sh_ref[...]
        h_ref[...] = h.astype(BF16)

    o_ref[...] = jnp.dot(h_ref[...], w_ref[...], preferred_element_type=F32).astype(BF16)


def _inproj(x, g, scale, shift, w_bf16):
    batch, seq, _ = x.shape
    tm = INPROJ_TM
    return pl.pallas_call(
        _inproj_kernel,
        name="inproj",
        grid=(batch, seq // tm, N_PIECES),
        in_specs=[
            pl.BlockSpec((None, tm, D_MODEL), lambda b, i, j: (b, i, 0)),
            pl.BlockSpec((1, D_MODEL), lambda b, i, j: (0, 0)),
            pl.BlockSpec((None, 1, D_MODEL), lambda b, i, j: (b, 0, 0)),
            pl.BlockSpec((None, 1, D_MODEL), lambda b, i, j: (b, 0, 0)),
            pl.BlockSpec((D_MODEL, D_GROUP), lambda b, i, j: (0, j)),
        ],
        out_specs=pl.BlockSpec((None, None, tm, D_GROUP), lambda b, i, j: (j, b, i, 0)),
        out_shape=jax.ShapeDtypeStruct((N_PIECES, batch, seq, D_GROUP), BF16),
        scratch_shapes=[pltpu.VMEM((tm, D_MODEL), BF16)],
        compiler_params=_params(("arbitrary", "arbitrary", "arbitrary")),
    )(x, g.reshape(1, D_MODEL), scale[:, None, :], shift[:, None, :], w_bf16)


OUTPROJ_TM = 512


def _outproj_kernel(a0_ref, a1_ref, a2_ref, a3_ref, w_ref, x_ref, gate_ref, fg_ref, o_ref, *, final):
    a = jnp.concatenate([a0_ref[...], a1_ref[...], a2_ref[...], a3_ref[...]], axis=-1)
    y = jnp.dot(a, w_ref[...], preferred_element_type=F32)
    xn = x_ref[...] + gate_ref[...] * y
    if final:
        ms = jnp.mean(xn * xn, axis=-1, keepdims=True)
        xn = (xn * lax.rsqrt(ms + EPS)) * fg_ref[...]
    o_ref[...] = xn


def _outproj(mixed, w_bf16, x, gate, final_g, final):
    batch, seq, _ = x.shape
    tm = OUTPROJ_TM
    a_spec = pl.BlockSpec((None, tm, D_GROUP), lambda b, i: (b, i, 0))
    return pl.pallas_call(
        functools.partial(_outproj_kernel, final=final),
        name="outproj",
        grid=(batch, seq // tm),
        in_specs=[
            a_spec, a_spec, a_spec, a_spec,
            pl.BlockSpec((D_MODEL, D_MODEL), lambda b, i: (0, 0)),
            pl.BlockSpec((None, tm, D_MODEL), lambda b, i: (b, i, 0)),
            pl.BlockSpec((None, 1, D_MODEL), lambda b, i: (b, 0, 0)),
            pl.BlockSpec((1, D_MODEL), lambda b, i: (0, 0)),
        ],
        out_specs=pl.BlockSpec((None, tm, D_MODEL), lambda b, i: (b, i, 0)),
        out_shape=jax.ShapeDtypeStruct((batch, seq, D_MODEL), F32),
        compiler_params=_params(("arbitrary", "arbitrary")),
    )(*mixed, w_bf16, x, gate[:, None, :], final_g.reshape(1, D_MODEL))


FFT_LANES = 4096
FFT_K1_BLOCK = 8


def _fft_constants(seq):
    rows = seq // GRID_W
    assert rows == GRID_W
    n = np.arange(GRID_W)
    ang1 = 2.0 * np.pi * ((n[:, None] * n[None, :]) % GRID_W) / GRID_W
    f1 = np.concatenate([np.cos(ang1), -np.sin(ang1)], axis=0)
    k1 = n[:, None, None]
    k2 = n[None, :, None]
    s2 = n[None, None, :]
    ang2 = 2.0 * np.pi * ((s2 * (k1 + GRID_W * k2)) % seq) / seq
    mr, mi = np.cos(ang2), -np.sin(ang2)
    m2 = np.concatenate([np.concatenate([mr, -mi], axis=2),
                         np.concatenate([mi, mr], axis=2)], axis=1)
    c = np.arange(FFT_GROUP_DIM)
    angc = 2.0 * np.pi * ((c[:, None] * c[None, :]) % FFT_GROUP_DIM) / FFT_GROUP_DIM
    fc = np.concatenate([np.cos(angc), np.sin(angc)], axis=0)
    return tuple(jnp.asarray(t, F32).astype(BF16) for t in (f1, m2, fc))


def _fft1_kernel(u_ref, f1_ref, g_ref):
    g_ref[...] = jnp.dot(f1_ref[...], u_ref[...], preferred_element_type=F32).astype(BF16)


def _fft2_kernel(g_ref, m2_ref, fc_ref, w_ref, gate_ref, o_ref, *, norm):
    kb = FFT_K1_BLOCK
    xr, xi = [], []
    for kk in range(kb):
        gk = jnp.concatenate([g_ref[0, kk], g_ref[1, kk]], axis=0)
        xk = jnp.dot(m2_ref[kk], gk, preferred_element_type=F32)
        xr.append(xk[:GRID_W])
        xi.append(xk[GRID_W:])
    xr = jnp.concatenate(xr, axis=0).astype(BF16)
    xi = jnp.concatenate(xi, axis=0).astype(BF16)
    ys = []
    for g in range(N_FFT_GROUPS):
        sl = slice(g * FFT_GROUP_DIM, (g + 1) * FFT_GROUP_DIM)
        lhs = jnp.concatenate([xr[:, sl], xi[:, sl]], axis=-1)
        ys.append(jnp.dot(lhs, fc_ref[...], preferred_element_type=F32))
    y = (jnp.concatenate(ys, axis=-1) * norm).astype(BF16)
    o = jnp.dot(y, w_ref[...], preferred_element_type=F32)
    for kk in range(kb):
        gate = gate_ref[:, kk * D_GROUP:(kk + 1) * D_GROUP].astype(F32)
        o_ref[:, kk * D_GROUP:(kk + 1) * D_GROUP] = (
            o[kk * GRID_W:(kk + 1) * GRID_W] * _silu(gate)).astype(BF16)


def _fourier(zp, w_fft_bf16, consts):
    _, batch, seq, _ = zp.shape
    f1, m2, fc = consts
    rows = seq // GRID_W
    lanes = GRID_W * D_GROUP
    assert (P_FX, P_FG) == (0, 1)
    zrow = zp[:2].reshape(2, batch, rows, lanes)
    g = pl.pallas_call(
        _fft1_kernel,
        name="fft_rows",
        grid=(batch, lanes // FFT_LANES),
        in_specs=[
            pl.BlockSpec((None, None, rows, FFT_LANES), lambda b, j: (P_FX, b, 0, j)),
            pl.BlockSpec((2 * GRID_W, rows), lambda b, j: (0, 0)),
        ],
        out_specs=pl.BlockSpec((None, 2 * GRID_W, FFT_LANES), lambda b, j: (b, 0, j)),
        out_shape=jax.ShapeDtypeStruct((batch, 2 * GRID_W, lanes), BF16),
        compiler_params=_params(("arbitrary", "arbitrary")),
    )(zrow, f1)
    g5 = g.reshape(batch, 2, GRID_W, GRID_W, D_GROUP)
    kb = FFT_K1_BLOCK
    out = pl.pallas_call(
        functools.partial(_fft2_kernel, norm=1.0 / math.sqrt(seq * FFT_GROUP_DIM)),
        name="fft_cols",
        grid=(batch, GRID_W // kb),
        in_specs=[
            pl.BlockSpec((None, 2, kb, GRID_W, D_GROUP), lambda b, j: (b, 0, j, 0, 0)),
            pl.BlockSpec((kb, 2 * GRID_W, 2 * GRID_W), lambda b, j: (j, 0, 0)),
            pl.BlockSpec((2 * FFT_GROUP_DIM, FFT_GROUP_DIM), lambda b, j: (0, 0)),
            pl.BlockSpec((D_GROUP, D_GROUP), lambda b, j: (0, 0)),
            pl.BlockSpec((None, None, rows, kb * D_GROUP), lambda b, j: (P_FG, b, 0, j)),
        ],
        out_specs=pl.BlockSpec((None, rows, kb * D_GROUP), lambda b, j: (b, 0, j)),
        out_shape=jax.ShapeDtypeStruct((batch, rows, lanes), BF16),
        compiler_params=_params(("arbitrary", "arbitrary")),
    )(g5, m2, fc, w_fft_bf16, zrow)
    return out.reshape(batch, seq, D_GROUP)


NA_ROWS_PER_STEP = 4


def _na_bias_table(rel_bias, rows):
    kh = min(NA_KH, rows)
    col = np.arange(GRID_W)
    col_start = np.clip(col - NA_KW // 2, 0, GRID_W - NA_KW)
    rel_c = col[None, :] - col_start[:, None]
    col_in = (rel_c >= 0) & (rel_c < NA_KW)
    dc = np.clip(col[None, :] - col[:, None] + NA_KW - 1, 0, 2 * NA_KW - 2)
    onehot = dc[None] == np.arange(2 * NA_KW - 1)[:, None, None]
    by_col = jnp.sum(jnp.where(onehot[None, None], rel_bias.astype(F32)[:, :, :, None, None], 0.0),
                     axis=2)
    by_col = jnp.where(col_in[None, None], by_col, NEG_BIG)
    tbl = jnp.stack([by_col[:, NA_KH - 1 - o:NA_KH - 1 - o + kh] for o in range(kh)])
    tbl = tbl.transpose(0, 1, 3, 2, 4)
    return tbl.reshape(kh, N_HEADS, GRID_W, kh * GRID_W)


def _head_select_mask():
    r = lax.broadcasted_iota(jnp.int32, (MXU_DIM, MXU_DIM), 0) // HEAD_DIM
    c = lax.broadcasted_iota(jnp.int32, (MXU_DIM, MXU_DIM), 1) // HEAD_DIM
    return r == c


def _stack_heads(t):
    n = t.shape[0]
    reps = jnp.concatenate([t] * HEADS_PER_TILE, axis=0)
    row_head = lax.broadcasted_iota(jnp.int32, reps.shape, 0) // n
    lane_head = lax.broadcasted_iota(jnp.int32, reps.shape, 1) // HEAD_DIM
    return jnp.where(row_head == lane_head, reps, jnp.zeros_like(reps))


def _unstack_heads(t4, n):
    lane_head = lax.broadcasted_iota(jnp.int32, (n, MXU_DIM), 1) // HEAD_DIM
    out = jnp.zeros((n, MXU_DIM), t4.dtype)
    for h in range(HEADS_PER_TILE):
        out = jnp.where(lane_head == h, t4[h * n:(h + 1) * n], out)
    return out


def _na_kernel(q_ref, k_ref, v_ref, g_ref, bias_ref, o_ref, *, rows):
    kh = min(NA_KH, rows)
    step = pl.program_id(1)

    def row_body(rr, carry):
        r = step * NA_ROWS_PER_STEP + rr
        r_start = jnp.clip(r - kh // 2, 0, rows - kh)
        variant = r - r_start
        q_off = pl.multiple_of(rr * GRID_W, GRID_W)
        k_off = pl.multiple_of(r_start * GRID_W, GRID_W)
        for cg in range(D_GROUP // MXU_DIM):
            lanes = slice(cg * MXU_DIM, (cg + 1) * MXU_DIM)
            q = q_ref[pl.ds(q_off, GRID_W), lanes] * (HEAD_DIM ** -0.5)
            kw = k_ref[pl.ds(k_off, kh * GRID_W), lanes]
            vw = v_ref[pl.ds(k_off, kh * GRID_W), lanes]
            q4 = _stack_heads(q)
            s = lax.dot_general(q4, kw, (((1,), (1,)), ((), ())), preferred_element_type=F32)
            bias = bias_ref[variant, cg * HEADS_PER_TILE:(cg + 1) * HEADS_PER_TILE]
            s = s + bias.reshape(HEADS_PER_TILE * GRID_W, kh * GRID_W)
            m = jnp.max(s, axis=-1, keepdims=True)
            e = jnp.exp(s - m)
            l = jnp.sum(e, axis=-1, keepdims=True)
            o4 = jnp.dot(e.astype(BF16), vw, preferred_element_type=F32) / l
            o = _unstack_heads(o4, GRID_W)
            gate = g_ref[pl.ds(q_off, GRID_W), lanes].astype(F32)
            o_ref[pl.ds(q_off, GRID_W), lanes] = (o * _silu(gate)).astype(BF16)
        return carry

    lax.fori_loop(0, NA_ROWS_PER_STEP, row_body, 0, unroll=True)


def _neighbourhood(zp, bias_tbl):
    _, batch, seq, _ = zp.shape
    rows = seq // GRID_W
    tq = NA_ROWS_PER_STEP * GRID_W
    q_spec = lambda p: pl.BlockSpec((None, None, tq, D_GROUP), lambda b, i: (p, b, i, 0))
    kv_spec = lambda p: pl.BlockSpec((None, None, seq, D_GROUP), lambda b, i: (p, b, 0, 0))
    return pl.pallas_call(
        functools.partial(_na_kernel, rows=rows),
        name="natten",
        grid=(batch, rows // NA_ROWS_PER_STEP),
        in_specs=[
            q_spec(P_NQ), kv_spec(P_NK), kv_spec(P_NV), q_spec(P_NG),
            pl.BlockSpec(bias_tbl.shape, lambda b, i: (0, 0, 0, 0), pipeline_mode=pl.Buffered(1)),
        ],
        out_specs=pl.BlockSpec((None, tq, D_GROUP), lambda b, i: (b, i, 0)),
        out_shape=jax.ShapeDtypeStruct((batch, seq, D_GROUP), BF16),
        compiler_params=_params(("arbitrary", "arbitrary")),
    )(zp, zp, zp, zp, bias_tbl)


RET_CHUNK = MXU_DIM


def _log_sigmoid(t):
    return jnp.minimum(t, 0.0) - jnp.log1p(jnp.exp(-jnp.abs(t)))


def _rope_tables(seq):
    half = HEAD_DIM // 2
    inv = ROPE_BASE ** (-jnp.arange(half, dtype=F32) / half)
    ang = jnp.arange(seq, dtype=F32)[:, None] * inv[None, :]
    cos, sin = jnp.cos(ang), jnp.sin(ang)
    cos2 = jnp.concatenate([cos, cos], axis=-1)
    sin2 = jnp.concatenate([-sin, sin], axis=-1)
    return jnp.tile(cos2, (1, 2)), jnp.tile(sin2, (1, 2))


def _ret_kernel(lf_s_ref, lb_s_ref, q_ref, k_ref, v_ref, g_ref, cos_ref, sin_ref, lfl_ref, lbl_ref,
                o_ref, qr_ref, kr_ref, ob_ref, dmat_ref, dec_ref, state_ref, ones_ref, *, seq):
    C = RET_CHUNK
    n_chunks = seq // C
    cg = pl.program_id(1)
    ones_bd = jnp.where(_head_select_mask(), 1.0, 0.0).astype(BF16)
    ones_ref[...] = jnp.concatenate([ones_bd, ones_bd], axis=0)

    ri = lax.broadcasted_iota(jnp.int32, (C, C), 0)
    ci = lax.broadcasted_iota(jnp.int32, (C, C), 1)
    diff = (ri - ci).astype(F32)
    for hh in range(HEADS_PER_TILE):
        h = cg * HEADS_PER_TILE + hh
        lf = _log_sigmoid(jnp.full((C, C), lf_s_ref[h], F32))
        lb = _log_sigmoid(jnp.full((C, C), lb_s_ref[h], F32))
        dmat_ref[hh] = jnp.where(diff >= 0, jnp.exp(lf * diff), jnp.exp(lb * (-diff)))
    lfl = _log_sigmoid(lfl_ref[...])
    lbl = _log_sigmoid(lbl_ref[...])
    idx = lax.broadcasted_iota(jnp.int32, (C, MXU_DIM), 0).astype(F32)
    dec_ref[0] = jnp.exp(lfl * (idx + 1.0))
    dec_ref[1] = jnp.exp(lfl * (C - 1.0 - idx))
    dec_ref[2] = jnp.exp(lbl * (C - idx))
    dec_ref[3] = jnp.exp(lbl * idx)
    cd_f = jnp.exp(lfl * float(C))
    cd_b = jnp.exp(lbl * float(C))

    def rope(t, rows):
        lane = lax.broadcasted_iota(jnp.int32, t.shape, 1)
        first_half = (lane % HEAD_DIM) < (HEAD_DIM // 2)
        cs = cos_ref[rows, :]
        sn = sin_ref[rows, :]
        cs = jnp.concatenate([cs, cs], axis=-1)
        sn = jnp.concatenate([sn, sn], axis=-1)
        swapped = jnp.where(first_half, pltpu.roll(t, MXU_DIM - HEAD_DIM // 2, 1),
                            pltpu.roll(t, HEAD_DIM // 2, 1))
        return t * cs + swapped * sn

    def kv_update(k, v, kd, cd):
        kv = lax.dot_general((k * kd).astype(BF16), v, (((0,), (0,)), ((), ())),
                             preferred_element_type=F32)
        state_ref[...] = cd * state_ref[...] + jnp.where(_head_select_mask(), kv, 0.0)

    state_ref[...] = jnp.zeros_like(state_ref)

    def bwd_body(i, carry):
        n = n_chunks - 1 - i
        rows = pl.ds(pl.multiple_of(n * C, C), C)
        q = rope(q_ref[rows, :].astype(F32), rows) * (HEAD_DIM ** -0.5)
        k = rope(k_ref[rows, :].astype(F32), rows)
        qr_ref[rows, :] = q.astype(BF16)
        kr_ref[rows, :] = k.astype(BF16)
        ob_ref[rows, :] = jnp.dot((q * dec_ref[2]).astype(BF16), state_ref[...].astype(BF16),
                                  preferred_element_type=F32)
        kv_update(k, v_ref[rows, :], dec_ref[3], cd_b)
        return carry

    lax.fori_loop(0, n_chunks, bwd_body, 0, unroll=2)

    state_ref[...] = jnp.zeros_like(state_ref)

    def fwd_body(n, carry):
        rows = pl.ds(pl.multiple_of(n * C, C), C)
        qb = qr_ref[rows, :]
        kb = kr_ref[rows, :]
        v = v_ref[rows, :]
        o_cross = jnp.dot((qb.astype(F32) * dec_ref[0]).astype(BF16), state_ref[...].astype(BF16),
                          preferred_element_type=F32)
        q4 = _stack_heads(qb)
        sc = lax.dot_general(q4, kb, (((1,), (1,)), ((), ())), preferred_element_type=F32)
        sc = sc * dmat_ref[...].reshape(HEADS_PER_TILE * C, C)
        o4 = jnp.dot(sc.astype(BF16), v, preferred_element_type=F32)
        o = _unstack_heads(o4, C) + o_cross + ob_ref[rows, :]
        sq = o * o
        hi = sq.astype(BF16)
        lo = (sq - hi.astype(F32)).astype(BF16)
        ss = jnp.dot(jnp.concatenate([hi, lo], axis=-1), ones_ref[...], preferred_element_type=F32)
        inv = lax.rsqrt(ss * (1.0 / HEAD_DIM) + EPS)
        gate = g_ref[rows, :].astype(F32)
        o_ref[rows, :] = (o * inv * _silu(gate)).astype(BF16)
        kv_update(kb.astype(F32), v, dec_ref[1], cd_f)
        return carry

    lax.fori_loop(0, n_chunks, fwd_body, 0, unroll=2)


def _retention(zp, logit_f, logit_b, rope_tbl):
    _, batch, seq, _ = zp.shape
    cos_t, sin_t = rope_tbl
    n_cg = D_GROUP // MXU_DIM
    spec = lambda p: pl.BlockSpec((None, None, seq, MXU_DIM), lambda b, c, *_: (p, b, 0, c))
    tbl_spec = pl.BlockSpec((seq, 2 * HEAD_DIM), lambda b, c, *_: (0, 0))
    lane_spec = pl.BlockSpec((1, MXU_DIM), lambda b, c, *_: (0, c))
    lane_f = jnp.repeat(logit_f.astype(F32), HEAD_DIM)[None, :]
    lane_b = jnp.repeat(logit_b.astype(F32), HEAD_DIM)[None, :]
    return pl.pallas_call(
        functools.partial(_ret_kernel, seq=seq),
        name="retention",
        grid_spec=pltpu.PrefetchScalarGridSpec(
            num_scalar_prefetch=2,
            grid=(batch, n_cg),
            in_specs=[spec(P_RQ), spec(P_RK), spec(P_RV), spec(P_RG), tbl_spec, tbl_spec,
                      lane_spec, lane_spec],
            out_specs=pl.BlockSpec((None, seq, MXU_DIM), lambda b, c, *_: (b, 0, c)),
            scratch_shapes=[
                pltpu.VMEM((seq, MXU_DIM), BF16),
                pltpu.VMEM((seq, MXU_DIM), BF16),
                pltpu.VMEM((seq, MXU_DIM), F32),
                pltpu.VMEM((HEADS_PER_TILE, RET_CHUNK, RET_CHUNK), F32),
                pltpu.VMEM((4, RET_CHUNK, MXU_DIM), F32),
                pltpu.VMEM((MXU_DIM, MXU_DIM), F32),
                pltpu.VMEM((2 * MXU_DIM, MXU_DIM), BF16),
            ],
        ),
        out_shape=jax.ShapeDtypeStruct((batch, seq, D_GROUP), BF16),
        compiler_params=_params(("arbitrary", "arbitrary")),
    )(logit_f.astype(F32), logit_b.astype(F32), zp, zp, zp, zp, cos_t, sin_t, lane_f, lane_b)


CONV_TS = 512
CONV_HALO = SUBLANES_BF16
CONV_ROWS = 64


def _conv_kernel(a_ref, b_ref, g_ref, cw_ref, cb_ref, lg_ref, lb_ref, w_ref, o_ref, u_ref, h_ref, *, seq):
    ts, halo = CONV_TS, CONV_HALO
    j = pl.program_id(1)
    n_tiles = seq // ts

    def glu(start, size):
        rows = pl.ds(pl.multiple_of(start, halo), size)
        a = a_ref[rows, :].astype(F32)
        return a * _sigmoid(b_ref[rows, :].astype(F32))

    base = j * ts
    u_ref[halo:halo + ts, :] = glu(base, ts)
    lo = glu(jnp.maximum(base - halo, 0), halo)
    u_ref[0:halo, :] = jnp.where(j > 0, lo, 0.0)
    hi = glu(jnp.minimum(base + ts, seq - halo), halo)
    u_ref[halo + ts:, :] = jnp.where(j < n_tiles - 1, hi, 0.0)

    rb = CONV_ROWS
    first = halo - CONV_HALF

    def rows_body(i, carry):
        s0 = pl.multiple_of(i * rb, rb)
        parts = []
        for lt in range(D_GROUP // LANES):
            lanes = slice(lt * LANES, (lt + 1) * LANES)
            y = None
            for r in range(SUBLANES_F32):
                acc = None
                for o in range(r, first + CONV_WIDTH, SUBLANES_F32):
                    if o < first:
                        continue
                    w = o - first
                    term = u_ref[pl.ds(s0 + (o - r), rb + SUBLANES_F32), lanes] * cw_ref[w:w + 1, lanes]
                    acc = term if acc is None else acc + term
                shifted = acc[r:r + rb]
                y = shifted if y is None else y + shifted
            parts.append(y)
        y = jnp.concatenate(parts, axis=-1) + cb_ref[...]
        mu = jnp.mean(y, axis=-1, keepdims=True)
        yc = y - mu
        var = jnp.mean(yc * yc, axis=-1, keepdims=True)
        y = (yc * lax.rsqrt(var + EPS)) * lg_ref[...] + lb_ref[...]
        h_ref[pl.ds(s0, rb), :] = _silu(y).astype(BF16)
        return carry

    lax.fori_loop(0, ts // rb, rows_body, 0)
    o = jnp.dot(h_ref[...], w_ref[...], preferred_element_type=F32)
    gate = g_ref[...].astype(F32)
    o_ref[...] = (o * _silu(gate)).astype(BF16)


def _conformer(zp, conv_w, conv_b, ln_g, ln_b, w_pw_bf16):
    _, batch, seq, _ = zp.shape
    ts = CONV_TS
    full = lambda p: pl.BlockSpec((None, None, seq, D_GROUP), lambda b, j: (p, b, 0, 0))
    vec = pl.BlockSpec((1, D_GROUP), lambda b, j: (0, 0))
    return pl.pallas_call(
        functools.partial(_conv_kernel, seq=seq),
        name="conformer",
        grid=(batch, seq // ts),
        in_specs=[
            full(P_CA), full(P_CB),
            pl.BlockSpec((None, None, ts, D_GROUP), lambda b, j: (P_CG, b, j, 0)),
            pl.BlockSpec((CONV_WIDTH, D_GROUP), lambda b, j: (0, 0)),
            vec, vec, vec,
            pl.BlockSpec((D_GROUP, D_GROUP), lambda b, j: (0, 0)),
        ],
        out_specs=pl.BlockSpec((None, ts, D_GROUP), lambda b, j: (b, j, 0)),
        out_shape=jax.ShapeDtypeStruct((batch, seq, D_GROUP), BF16),
        scratch_shapes=[pltpu.VMEM((ts + 2 * CONV_HALO, D_GROUP), F32),
                        pltpu.VMEM((ts, D_GROUP), BF16)],
        compiler_params=_params(("arbitrary", "arbitrary")),
    )(zp, zp, zp, conv_w, conv_b.reshape(1, D_GROUP), ln_g.reshape(1, D_GROUP),
      ln_b.reshape(1, D_GROUP), w_pw_bf16)


def kernel(x, c, norm_g, w_ada, b_ada, w_in, w_fft, na_rel_bias, ret_logit_fwd, ret_logit_bwd,
           conv_w, conv_b, conv_ln_g, conv_ln_b, conv_w_pw, w_out, final_g):
    batch, seq, _ = x.shape
    rows = seq // GRID_W
    mod = _ada(c, w_ada, b_ada)
    fft_consts = _fft_constants(seq)
    rope_tbl = _rope_tables(seq)
    for l in range(DEPTH):
        shift, scale, gate = jnp.split(mod[l], 3, axis=-1)
        zp = _inproj(x, norm_g[l], scale, shift, w_in[l].astype(BF16))
        o_fft = _fourier(zp, w_fft[l].astype(BF16), fft_consts)
        o_na = _neighbourhood(zp, _na_bias_table(na_rel_bias[l], rows))
        o_ret = _retention(zp, ret_logit_fwd[l], ret_logit_bwd[l], rope_tbl)
        o_cv = _conformer(zp, conv_w[l], conv_b[l], conv_ln_g[l], conv_ln_b[l],
                          conv_w_pw[l].astype(BF16))
        x = _outproj((o_fft, o_na, o_ret, o_cv), w_out[l].astype(BF16), x, gate, final_g,
                     final=(l == DEPTH - 1))
    return x
```

```python
import functools
import math

import numpy as np
import jax
import jax.numpy as jnp
from jax import lax
from jax.experimental import pallas as pl
from jax.experimental.pallas import tpu as pltpu

F32 = jnp.float32
BF16 = jnp.bfloat16

D_MODEL = 2048
DEPTH = 2
GRID_W = 64
D_GROUP = 512
HEAD_DIM = 64
N_HEADS = D_GROUP // HEAD_DIM
N_FFT_GROUPS = 4
FFT_GROUP_DIM = D_GROUP // N_FFT_GROUPS
NA_KH = 8
NA_KW = 16
CONV_WIDTH = 31
CONV_HALF = CONV_WIDTH // 2
ROPE_BASE = 10000.0
EPS = 1e-6
N_PIECES = 13
(P_FX, P_FG, P_NQ, P_NK, P_NV, P_NG, P_RQ, P_RK, P_RV, P_RG, P_CA, P_CB, P_CG) = range(N_PIECES)

MXU_DIM = 256
HEADS_PER_TILE = MXU_DIM // HEAD_DIM
VMEM_LIMIT = 56 * 1024 * 1024
SUBLANES_BF16 = 16
SUBLANES_F32 = 8
LANES = 128

NEG_BIG = -1e30


def _params(sem, vmem=VMEM_LIMIT):
    return pltpu.CompilerParams(dimension_semantics=sem, vmem_limit_bytes=vmem)


def _sigmoid(t):
    return 0.5 * jnp.tanh(0.5 * t) + 0.5


def _silu(t):
    h = 0.5 * t
    return h + h * jnp.tanh(h)


ADA_TN = 768
ADA_ROWS = 8


def _ada_kernel(c_ref, w_ref, b_ref, o_ref):
    ca = _silu(c_ref[...]).astype(BF16)
    w = w_ref[...].astype(BF16)
    o_ref[...] = jnp.dot(ca, w, preferred_element_type=F32) + b_ref[...]


def _ada(c, w_ada, b_ada):
    batch = c.shape[0]
    c_pad = jnp.zeros((ADA_ROWS, D_MODEL), F32).at[:batch].set(c)
    n3 = 3 * D_MODEL
    out = pl.pallas_call(
        _ada_kernel,
        name="ada",
        grid=(DEPTH, n3 // ADA_TN),
        in_specs=[
            pl.BlockSpec((ADA_ROWS, D_MODEL), lambda l, j: (0, 0)),
            pl.BlockSpec((None, D_MODEL, ADA_TN), lambda l, j: (l, 0, j)),
            pl.BlockSpec((None, 1, ADA_TN), lambda l, j: (l, 0, j)),
        ],
        out_specs=pl.BlockSpec((None, ADA_ROWS, ADA_TN), lambda l, j: (l, 0, j)),
        out_shape=jax.ShapeDtypeStruct((DEPTH, ADA_ROWS, n3), F32),
        compiler_params=_params(("arbitrary", "arbitrary")),
    )(c_pad, w_ada, b_ada.reshape(DEPTH, 1, n3))
    return out[:, :batch]


INPROJ_TM = 512
INPROJ_VMEM = 60 * 1024 * 1024


def _inproj_kernel(x_ref, g_ref, sc_ref, sh_ref, w_ref, o_ref, h0_ref, h1_ref, *, n_tiles):
    t = pl.program_id(0)

    def norm(h_ref):
        x = x_ref[...]
        ms = jnp.mean(x * x, axis=-1, keepdims=True)
        h = (x * lax.rsqrt(ms + EPS)) * g_ref[...]
        h = h * (1.0 + sc_ref[...]) + sh_ref[...]
        h_ref[...] = h.astype(BF16)

    def project(h_ref):
        h = h_ref[...]
        for p in range(N_PIECES):
            w = w_ref[:, p * D_GROUP:(p + 1) * D_GROUP]
            o_ref[p] = jnp.dot(h, w, preferred_element_type=F32).astype(BF16)

    odd = t % 2 == 1

    @pl.when(t == 0)
    def _():
        norm(h0_ref)

    @pl.when(odd)
    def _():
        project(h0_ref)
        norm(h1_ref)

    @pl.when(jnp.logical_and(jnp.logical_not(odd), jnp.logical_and(t > 0, t < n_tiles)))
    def _():
        project(h1_ref)
        norm(h0_ref)

    @pl.when(t == n_tiles)
    def _():
        project(h1_ref)


def _inproj(x, g, scale, shift, w_bf16):
    batch, seq, _ = x.shape
    tm = INPROJ_TM
    per_batch = seq // tm
    n_tiles = batch * per_batch
    assert n_tiles % 2 == 0
    norm_tile = lambda t: jnp.minimum(t, n_tiles - 1)
    proj_tile = lambda t: jnp.maximum(t - 1, 0)
    return pl.pallas_call(
        functools.partial(_inproj_kernel, n_tiles=n_tiles),
        name="inproj",
        grid=(n_tiles + 1,),
        in_specs=[
            pl.BlockSpec((None, tm, D_MODEL),
                         lambda t: (norm_tile(t) // per_batch, norm_tile(t) % per_batch, 0)),
            pl.BlockSpec((1, D_MODEL), lambda t: (0, 0)),
            pl.BlockSpec((None, 1, D_MODEL), lambda t: (norm_tile(t) // per_batch, 0, 0)),
            pl.BlockSpec((None, 1, D_MODEL), lambda t: (norm_tile(t) // per_batch, 0, 0)),
            pl.BlockSpec((D_MODEL, N_PIECES * D_GROUP), lambda t: (0, 0),
                         pipeline_mode=pl.Buffered(1)),
        ],
        out_specs=pl.BlockSpec((N_PIECES, None, tm, D_GROUP),
                               lambda t: (0, proj_tile(t) // per_batch, proj_tile(t) % per_batch, 0)),
        out_shape=jax.ShapeDtypeStruct((N_PIECES, batch, seq, D_GROUP), BF16),
        scratch_shapes=[pltpu.VMEM((tm, D_MODEL), BF16), pltpu.VMEM((tm, D_MODEL), BF16)],
        compiler_params=_params(("arbitrary",), INPROJ_VMEM),
    )(x, g.reshape(1, D_MODEL), scale[:, None, :], shift[:, None, :], w_bf16)


OUTPROJ_TM = 512


def _outproj_kernel(a0_ref, a1_ref, a2_ref, a3_ref, w_ref, x_ref, gate_ref, fg_ref, o_ref, *, final):
    a = jnp.concatenate([a0_ref[...], a1_ref[...], a2_ref[...], a3_ref[...]], axis=-1)
    y = jnp.dot(a, w_ref[...], preferred_element_type=F32)
    xn = x_ref[...] + gate_ref[...] * y
    if final:
        ms = jnp.mean(xn * xn, axis=-1, keepdims=True)
        xn = (xn * lax.rsqrt(ms + EPS)) * fg_ref[...]
    o_ref[...] = xn


def _outproj(mixed, w_bf16, x, gate, final_g, final):
    batch, seq, _ = x.shape
    tm = OUTPROJ_TM
    a_spec = pl.BlockSpec((None, tm, D_GROUP), lambda b, i: (b, i, 0))
    return pl.pallas_call(
        functools.partial(_outproj_kernel, final=final),
        name="outproj",
        grid=(batch, seq // tm),
        in_specs=[
            a_spec, a_spec, a_spec, a_spec,
            pl.BlockSpec((D_MODEL, D_MODEL), lambda b, i: (0, 0)),
            pl.BlockSpec((None, tm, D_MODEL), lambda b, i: (b, i, 0)),
            pl.BlockSpec((None, 1, D_MODEL), lambda b, i: (b, 0, 0)),
            pl.BlockSpec((1, D_MODEL), lambda b, i: (0, 0)),
        ],
        out_specs=pl.BlockSpec((None, tm, D_MODEL), lambda b, i: (b, i, 0)),
        out_shape=jax.ShapeDtypeStruct((batch, seq, D_MODEL), F32),
        compiler_params=_params(("arbitrary", "arbitrary")),
    )(*mixed, w_bf16, x, gate[:, None, :], final_g.reshape(1, D_MODEL))


FFT_LANES = 4096
FFT_K1_BLOCK = 8


def _fft_constants(seq):
    rows = seq // GRID_W
    assert rows == GRID_W
    n = np.arange(GRID_W)
    ang1 = 2.0 * np.pi * ((n[:, None] * n[None, :]) % GRID_W) / GRID_W
    f1 = np.concatenate([np.cos(ang1), -np.sin(ang1)], axis=0)
    k1 = n[:, None, None]
    k2 = n[None, :, None]
    s2 = n[None, None, :]
    ang2 = 2.0 * np.pi * ((s2 * (k1 + GRID_W * k2)) % seq) / seq
    mr, mi = np.cos(ang2), -np.sin(ang2)
    m2 = np.concatenate([np.concatenate([mr, -mi], axis=2),
                         np.concatenate([mi, mr], axis=2)], axis=1)
    c = np.arange(FFT_GROUP_DIM)
    angc = 2.0 * np.pi * ((c[:, None] * c[None, :]) % FFT_GROUP_DIM) / FFT_GROUP_DIM
    fc = np.concatenate([np.cos(angc), np.sin(angc)], axis=0)
    return tuple(jnp.asarray(t, F32).astype(BF16) for t in (f1, m2, fc))


def _fft1_kernel(u_ref, f1_ref, g_ref):
    g_ref[...] = jnp.dot(f1_ref[...], u_ref[...], preferred_element_type=F32).astype(BF16)


def _fft2_kernel(g_ref, m2_ref, fc_ref, w_ref, gate_ref, o_ref, *, norm):
    kb = FFT_K1_BLOCK
    xr, xi = [], []
    for kk in range(kb):
        gk = jnp.concatenate([g_ref[0, kk], g_ref[1, kk]], axis=0)
        xk = jnp.dot(m2_ref[kk], gk, preferred_element_type=F32)
        xr.append(xk[:GRID_W])
        xi.append(xk[GRID_W:])
    xr = jnp.concatenate(xr, axis=0).astype(BF16)
    xi = jnp.concatenate(xi, axis=0).astype(BF16)
    ys = []
    for g in range(N_FFT_GROUPS):
        sl = slice(g * FFT_GROUP_DIM, (g + 1) * FFT_GROUP_DIM)
        lhs = jnp.concatenate([xr[:, sl], xi[:, sl]], axis=-1)
        ys.append(jnp.dot(lhs, fc_ref[...], preferred_element_type=F32))
    y = (jnp.concatenate(ys, axis=-1) * norm).astype(BF16)
    o = jnp.dot(y, w_ref[...], preferred_element_type=F32)
    for kk in range(kb):
        gate = gate_ref[:, kk * D_GROUP:(kk + 1) * D_GROUP].astype(F32)
        o_ref[:, kk * D_GROUP:(kk + 1) * D_GROUP] = (
            o[kk * GRID_W:(kk + 1) * GRID_W] * _silu(gate)).astype(BF16)


def _fourier(zp, w_fft_bf16, consts):
    _, batch, seq, _ = zp.shape
    f1, m2, fc = consts
    rows = seq // GRID_W
    lanes = GRID_W * D_GROUP
    assert (P_FX, P_FG) == (0, 1)
    zrow = zp[:2].reshape(2, batch, rows, lanes)
    g = pl.pallas_call(
        _fft1_kernel,
        name="fft_rows",
        grid=(batch, lanes // FFT_LANES),
        in_specs=[
            pl.BlockSpec((None, None, rows, FFT_LANES), lambda b, j: (P_FX, b, 0, j)),
            pl.BlockSpec((2 * GRID_W, rows), lambda b, j: (0, 0)),
        ],
        out_specs=pl.BlockSpec((None, 2 * GRID_W, FFT_LANES), lambda b, j: (b, 0, j)),
        out_shape=jax.ShapeDtypeStruct((batch, 2 * GRID_W, lanes), BF16),
        compiler_params=_params(("arbitrary", "arbitrary")),
    )(zrow, f1)
    g5 = g.reshape(batch, 2, GRID_W, GRID_W, D_GROUP)
    kb = FFT_K1_BLOCK
    out = pl.pallas_call(
        functools.partial(_fft2_kernel, norm=1.0 / math.sqrt(seq * FFT_GROUP_DIM)),
        name="fft_cols",
        grid=(batch, GRID_W // kb),
        in_specs=[
            pl.BlockSpec((None, 2, kb, GRID_W, D_GROUP), lambda b, j: (b, 0, j, 0, 0)),
            pl.BlockSpec((kb, 2 * GRID_W, 2 * GRID_W), lambda b, j: (j, 0, 0)),
            pl.BlockSpec((2 * FFT_GROUP_DIM, FFT_GROUP_DIM), lambda b, j: (0, 0)),
            pl.BlockSpec((D_GROUP, D_GROUP), lambda b, j: (0, 0)),
            pl.BlockSpec((None, None, rows, kb * D_GROUP), lambda b, j: (P_FG, b, 0, j)),
        ],
        out_specs=pl.BlockSpec((None, rows, kb * D_GROUP), lambda b, j: (b, 0, j)),
        out_shape=jax.ShapeDtypeStruct((batch, rows, lanes), BF16),
        compiler_params=_params(("arbitrary", "arbitrary")),
    )(g5, m2, fc, w_fft_bf16, zrow)
    return out.reshape(batch, seq, D_GROUP)


NA_ROWS_PER_STEP = 4


def _na_bias_table(rel_bias, rows):
    kh = min(NA_KH, rows)
    col = np.arange(GRID_W)
    col_start = np.clip(col - NA_KW // 2, 0, GRID_W - NA_KW)
    rel_c = col[None, :] - col_start[:, None]
    col_in = (rel_c >= 0) & (rel_c < NA_KW)
    dc = np.clip(col[None, :] - col[:, None] + NA_KW - 1, 0, 2 * NA_KW - 2)
    onehot = dc[None] == np.arange(2 * NA_KW - 1)[:, None, None]
    by_col = jnp.sum(jnp.where(onehot[None, None], rel_bias.astype(F32)[:, :, :, None, None], 0.0),
                     axis=2)
    by_col = jnp.where(col_in[None, None], by_col, NEG_BIG)
    tbl = jnp.stack([by_col[:, NA_KH - 1 - o:NA_KH - 1 - o + kh] for o in range(kh)])
    tbl = tbl.transpose(0, 1, 3, 2, 4)
    return tbl.reshape(kh, N_HEADS, GRID_W, kh * GRID_W)


def _head_select_mask():
    r = lax.broadcasted_iota(jnp.int32, (MXU_DIM, MXU_DIM), 0) // HEAD_DIM
    c = lax.broadcasted_iota(jnp.int32, (MXU_DIM, MXU_DIM), 1) // HEAD_DIM
    return r == c


def _stack_heads(t):
    n = t.shape[0]
    reps = jnp.concatenate([t] * HEADS_PER_TILE, axis=0)
    row_head = lax.broadcasted_iota(jnp.int32, reps.shape, 0) // n
    lane_head = lax.broadcasted_iota(jnp.int32, reps.shape, 1) // HEAD_DIM
    return jnp.where(row_head == lane_head, reps, jnp.zeros_like(reps))


def _unstack_heads(t4, n):
    lane_head = lax.broadcasted_iota(jnp.int32, (n, MXU_DIM), 1) // HEAD_DIM
    out = jnp.zeros((n, MXU_DIM), t4.dtype)
    for h in range(HEADS_PER_TILE):
        out = jnp.where(lane_head == h, t4[h * n:(h + 1) * n], out)
    return out


def _na_kernel(q_ref, k_ref, v_ref, g_ref, bias_ref, o_ref, *, rows):
    kh = min(NA_KH, rows)
    step = pl.program_id(1)

    def row_body(rr, carry):
        r = step * NA_ROWS_PER_STEP + rr
        r_start = jnp.clip(r - kh // 2, 0, rows - kh)
        variant = r - r_start
        q_off = pl.multiple_of(rr * GRID_W, GRID_W)
        k_off = pl.multiple_of(r_start * GRID_W, GRID_W)
        for cg in range(D_GROUP // MXU_DIM):
            lanes = slice(cg * MXU_DIM, (cg + 1) * MXU_DIM)
            q = q_ref[pl.ds(q_off, GRID_W), lanes] * (HEAD_DIM ** -0.5)
            kw = k_ref[pl.ds(k_off, kh * GRID_W), lanes]
            vw = v_ref[pl.ds(k_off, kh * GRID_W), lanes]
            q4 = _stack_heads(q)
            s = lax.dot_general(q4, kw, (((1,), (1,)), ((), ())), preferred_element_type=F32)
            bias = bias_ref[variant, cg * HEADS_PER_TILE:(cg + 1) * HEADS_PER_TILE]
            s = s + bias.reshape(HEADS_PER_TILE * GRID_W, kh * GRID_W)
            m = jnp.max(s, axis=-1, keepdims=True)
            e = jnp.exp(s - m)
            l = jnp.sum(e, axis=-1, keepdims=True)
            o4 = jnp.dot(e.astype(BF16), vw, preferred_element_type=F32) / l
            o = _unstack_heads(o4, GRID_W)
            gate = g_ref[pl.ds(q_off, GRID_W), lanes].astype(F32)
            o_ref[pl.ds(q_off, GRID_W), lanes] = (o * _silu(gate)).astype(BF16)
        return carry

    lax.fori_loop(0, NA_ROWS_PER_STEP, row_body, 0, unroll=True)


def _neighbourhood(zp, bias_tbl):
    _, batch, seq, _ = zp.shape
    rows = seq // GRID_W
    tq = NA_ROWS_PER_STEP * GRID_W
    q_spec = lambda p: pl.BlockSpec((None, None, tq, D_GROUP), lambda b, i: (p, b, i, 0))
    kv_spec = lambda p: pl.BlockSpec((None, None, seq, D_GROUP), lambda b, i: (p, b, 0, 0))
    return pl.pallas_call(
        functools.partial(_na_kernel, rows=rows),
        name="natten",
        grid=(batch, rows // NA_ROWS_PER_STEP),
        in_specs=[
            q_spec(P_NQ), kv_spec(P_NK), kv_spec(P_NV), q_spec(P_NG),
            pl.BlockSpec(bias_tbl.shape, lambda b, i: (0, 0, 0, 0), pipeline_mode=pl.Buffered(1)),
        ],
        out_specs=pl.BlockSpec((None, tq, D_GROUP), lambda b, i: (b, i, 0)),
        out_shape=jax.ShapeDtypeStruct((batch, seq, D_GROUP), BF16),
        compiler_params=_params(("arbitrary", "arbitrary")),
    )(zp, zp, zp, zp, bias_tbl)


RET_CHUNK = MXU_DIM


def _log_sigmoid(t):
    return jnp.minimum(t, 0.0) - jnp.log1p(jnp.exp(-jnp.abs(t)))


def _rope_tables(seq):
    half = HEAD_DIM // 2
    inv = ROPE_BASE ** (-jnp.arange(half, dtype=F32) / half)
    ang = jnp.arange(seq, dtype=F32)[:, None] * inv[None, :]
    cos, sin = jnp.cos(ang), jnp.sin(ang)
    cos2 = jnp.concatenate([cos, cos], axis=-1)
    sin2 = jnp.concatenate([-sin, sin], axis=-1)
    return jnp.tile(cos2, (1, 2)), jnp.tile(sin2, (1, 2))


def _ret_kernel(lf_s_ref, lb_s_ref, q_ref, k_ref, v_ref, g_ref, cos_ref, sin_ref, lfl_ref, lbl_ref,
                o_ref, qr_ref, kr_ref, ob_ref, dmat_ref, dec_ref, state_ref, ones_ref, *, seq):
    C = RET_CHUNK
    n_chunks = seq // C
    cg = pl.program_id(1)
    ones_bd = jnp.where(_head_select_mask(), 1.0, 0.0).astype(BF16)
    ones_ref[...] = jnp.concatenate([ones_bd, ones_bd], axis=0)

    ri = lax.broadcasted_iota(jnp.int32, (C, C), 0)
    ci = lax.broadcasted_iota(jnp.int32, (C, C), 1)
    diff = (ri - ci).astype(F32)
    for hh in range(HEADS_PER_TILE):
        h = cg * HEADS_PER_TILE + hh
        lf = _log_sigmoid(jnp.full((C, C), lf_s_ref[h], F32))
        lb = _log_sigmoid(jnp.full((C, C), lb_s_ref[h], F32))
        dmat_ref[hh] = jnp.where(diff >= 0, jnp.exp(lf * diff), jnp.exp(lb * (-diff)))
    lfl = _log_sigmoid(lfl_ref[...])
    lbl = _log_sigmoid(lbl_ref[...])
    idx = lax.broadcasted_iota(jnp.int32, (C, MXU_DIM), 0).astype(F32)
    dec_ref[0] = jnp.exp(lfl * (idx + 1.0))
    dec_ref[1] = jnp.exp(lfl * (C - 1.0 - idx))
    dec_ref[2] = jnp.exp(lbl * (C - idx))
    dec_ref[3] = jnp.exp(lbl * idx)
    cd_f = jnp.exp(lfl * float(C))
    cd_b = jnp.exp(lbl * float(C))

    def rope(t, rows):
        lane = lax.broadcasted_iota(jnp.int32, t.shape, 1)
        first_half = (lane % HEAD_DIM) < (HEAD_DIM // 2)
        cs = cos_ref[rows, :]
        sn = sin_ref[rows, :]
        cs = jnp.concatenate([cs, cs], axis=-1)
        sn = jnp.concatenate([sn, sn], axis=-1)
        swapped = jnp.where(first_half, pltpu.roll(t, MXU_DIM - HEAD_DIM // 2, 1),
                            pltpu.roll(t, HEAD_DIM // 2, 1))
        return t * cs + swapped * sn

    def kv_update(k, v, kd, cd):
        kv = lax.dot_general((k * kd).astype(BF16), v, (((0,), (0,)), ((), ())),
                             preferred_element_type=F32)
        state_ref[...] = cd * state_ref[...] + jnp.where(_head_select_mask(), kv, 0.0)

    state_ref[...] = jnp.zeros_like(state_ref)

    def bwd_body(i, carry):
        n = n_chunks - 1 - i
        rows = pl.ds(pl.multiple_of(n * C, C), C)
        q = rope(q_ref[rows, :].astype(F32), rows) * (HEAD_DIM ** -0.5)
        k = rope(k_ref[rows, :].astype(F32), rows)
        qr_ref[rows, :] = q.astype(BF16)
        kr_ref[rows, :] = k.astype(BF16)
        ob_ref[rows, :] = jnp.dot((q * dec_ref[2]).astype(BF16), state_ref[...].astype(BF16),
                                  preferred_element_type=F32)
        kv_update(k, v_ref[rows, :], dec_ref[3], cd_b)
        return carry

    lax.fori_loop(0, n_chunks, bwd_body, 0, unroll=2)

    state_ref[...] = jnp.zeros_like(state_ref)

    def fwd_body(n, carry):
        rows = pl.ds(pl.multiple_of(n * C, C), C)
        qb = qr_ref[rows, :]
        kb = kr_ref[rows, :]
        v = v_ref[rows, :]
        o_cross = jnp.dot((qb.astype(F32) * dec_ref[0]).astype(BF16), state_ref[...].astype(BF16),
                          preferred_element_type=F32)
        q4 = _stack_heads(qb)
        sc = lax.dot_general(q4, kb, (((1,), (1,)), ((), ())), preferred_element_type=F32)
        sc = sc * dmat_ref[...].reshape(HEADS_PER_TILE * C, C)
        o4 = jnp.dot(sc.astype(BF16), v, preferred_element_type=F32)
        o = _unstack_heads(o4, C) + o_cross + ob_ref[rows, :]
        sq = o * o
        hi = sq.astype(BF16)
        lo = (sq - hi.astype(F32)).astype(BF16)
        ss = jnp.dot(jnp.concatenate([hi, lo], axis=-1), ones_ref[...], preferred_element_type=F32)
        inv = lax.rsqrt(ss * (1.0 / HEAD_DIM) + EPS)
        gate = g_ref[rows, :].astype(F32)
        o_ref[rows, :] = (o * inv * _silu(gate)).astype(BF16)
        kv_update(kb.astype(F32), v, dec_ref[1], cd_f)
        return carry

    lax.fori_loop(0, n_chunks, fwd_body, 0, unroll=2)


def _retention(zp, logit_f, logit_b, rope_tbl):
    _, batch, seq, _ = zp.shape
    cos_t, sin_t = rope_tbl
    n_cg = D_GROUP // MXU_DIM
    spec = lambda p: pl.BlockSpec((None, None, seq, MXU_DIM), lambda b, c, *_: (p, b, 0, c))
    tbl_spec = pl.BlockSpec((seq, 2 * HEAD_DIM), lambda b, c, *_: (0, 0))
    lane_spec = pl.BlockSpec((1, MXU_DIM), lambda b, c, *_: (0, c))
    lane_f = jnp.repeat(logit_f.astype(F32), HEAD_DIM)[None, :]
    lane_b = jnp.repeat(logit_b.astype(F32), HEAD_DIM)[None, :]
    return pl.pallas_call(
        functools.partial(_ret_kernel, seq=seq),
        name="retention",
        grid_spec=pltpu.PrefetchScalarGridSpec(
            num_scalar_prefetch=2,
            grid=(batch, n_cg),
            in_specs=[spec(P_RQ), spec(P_RK), spec(P_RV), spec(P_RG), tbl_spec, tbl_spec,
                      lane_spec, lane_spec],
            out_specs=pl.BlockSpec((None, seq, MXU_DIM), lambda b, c, *_: (b, 0, c)),
            scratch_shapes=[
                pltpu.VMEM((seq, MXU_DIM), BF16),
                pltpu.VMEM((seq, MXU_DIM), BF16),
                pltpu.VMEM((seq, MXU_DIM), F32),
                pltpu.VMEM((HEADS_PER_TILE, RET_CHUNK, RET_CHUNK), F32),
                pltpu.VMEM((4, RET_CHUNK, MXU_DIM), F32),
                pltpu.VMEM((MXU_DIM, MXU_DIM), F32),
                pltpu.VMEM((2 * MXU_DIM, MXU_DIM), BF16),
            ],
        ),
        out_shape=jax.ShapeDtypeStruct((batch, seq, D_GROUP), BF16),
        compiler_params=_params(("arbitrary", "arbitrary")),
    )(logit_f.astype(F32), logit_b.astype(F32), zp, zp, zp, zp, cos_t, sin_t, lane_f, lane_b)


CONV_TS = 512
CONV_HALO = SUBLANES_BF16
CONV_ROWS = 64


def _conv_kernel(a_ref, b_ref, g_ref, cw_ref, cb_ref, lg_ref, lb_ref, w_ref, o_ref, u_ref, h_ref, *, seq):
    ts, halo = CONV_TS, CONV_HALO
    j = pl.program_id(1)
    n_tiles = seq // ts

    def glu(start, size):
        rows = pl.ds(pl.multiple_of(start, halo), size)
        a = a_ref[rows, :].astype(F32)
        return a * _sigmoid(b_ref[rows, :].astype(F32))

    base = j * ts
    u_ref[halo:halo + ts, :] = glu(base, ts)
    lo = glu(jnp.maximum(base - halo, 0), halo)
    u_ref[0:halo, :] = jnp.where(j > 0, lo, 0.0)
    hi = glu(jnp.minimum(base + ts, seq - halo), halo)
    u_ref[halo + ts:, :] = jnp.where(j < n_tiles - 1, hi, 0.0)

    rb = CONV_ROWS
    first = halo - CONV_HALF

    def rows_body(i, carry):
        s0 = pl.multiple_of(i * rb, rb)
        parts = []
        for lt in range(D_GROUP // LANES):
            lanes = slice(lt * LANES, (lt + 1) * LANES)
            y = None
            for r in range(SUBLANES_F32):
                acc = None
                for o in range(r, first + CONV_WIDTH, SUBLANES_F32):
                    if o < first:
                        continue
                    w = o - first
                    term = u_ref[pl.ds(s0 + (o - r), rb + SUBLANES_F32), lanes] * cw_ref[w:w + 1, lanes]
                    acc = term if acc is None else acc + term
                shifted = acc[r:r + rb]
                y = shifted if y is None else y + shifted
            parts.append(y)
        y = jnp.concatenate(parts, axis=-1) + cb_ref[...]
        mu = jnp.mean(y, axis=-1, keepdims=True)
        yc = y - mu
        var = jnp.mean(yc * yc, axis=-1, keepdims=True)
        y = (yc * lax.rsqrt(var + EPS)) * lg_ref[...] + lb_ref[...]
        h_ref[pl.ds(s0, rb), :] = _silu(y).astype(BF16)
        return carry

    lax.fori_loop(0, ts // rb, rows_body, 0)
    o = jnp.dot(h_ref[...], w_ref[...], preferred_element_type=F32)
    gate = g_ref[...].astype(F32)
    o_ref[...] = (o * _silu(gate)).astype(BF16)


def _conformer(zp, conv_w, conv_b, ln_g, ln_b, w_pw_bf16):
    _, batch, seq, _ = zp.shape
    ts = CONV_TS
    full = lambda p: pl.BlockSpec((None, None, seq, D_GROUP), lambda b, j: (p, b, 0, 0))
    vec = pl.BlockSpec((1, D_GROUP), lambda b, j: (0, 0))
    return pl.pallas_call(
        functools.partial(_conv_kernel, seq=seq),
        name="conformer",
        grid=(batch, seq // ts),
        in_specs=[
            full(P_CA), full(P_CB),
            pl.BlockSpec((None, None, ts, D_GROUP), lambda b, j: (P_CG, b, j, 0)),
            pl.BlockSpec((CONV_WIDTH, D_GROUP), lambda b, j: (0, 0)),
            vec, vec, vec,
            pl.BlockSpec((D_GROUP, D_GROUP), lambda b, j: (0, 0)),
        ],
        out_specs=pl.BlockSpec((None, ts, D_GROUP), lambda b, j: (b, j, 0)),
        out_shape=jax.ShapeDtypeStruct((batch, seq, D_GROUP), BF16),
        scratch_shapes=[pltpu.VMEM((ts + 2 * CONV_HALO, D_GROUP), F32),
                        pltpu.VMEM((ts, D_GROUP), BF16)],
        compiler_params=_params(("arbitrary", "arbitrary")),
    )(zp, zp, zp, conv_w, conv_b.reshape(1, D_GROUP), ln_g.reshape(1, D_GROUP),
      ln_b.reshape(1, D_GROUP), w_pw_bf16)


def kernel(x, c, norm_g, w_ada, b_ada, w_in, w_fft, na_rel_bias, ret_logit_fwd, ret_logit_bwd,
           conv_w, conv_b, conv_ln_g, conv_ln_b, conv_w_pw, w_out, final_g):
    batch, seq, _ = x.shape
    rows = seq // GRID_W
    mod = _ada(c, w_ada, b_ada)
    fft_consts = _fft_constants(seq)
    rope_tbl = _rope_tables(seq)
    for l in range(DEPTH):
        shift, scale, gate = jnp.split(mod[l], 3, axis=-1)
        zp = _inproj(x, norm_g[l], scale, shift, w_in[l].astype(BF16))
        o_fft = _fourier(zp, w_fft[l].astype(BF16), fft_consts)
        o_na = _neighbourhood(zp, _na_bias_table(na_rel_bias[l], rows))
        o_ret = _retention(zp, ret_logit_fwd[l], ret_logit_bwd[l], rope_tbl)
        o_cv = _conformer(zp, conv_w[l], conv_b[l], conv_ln_g[l], conv_ln_b[l],
                          conv_w_pw[l].astype(BF16))
        x = _outproj((o_fft, o_na, o_ret, o_cv), w_out[l].astype(BF16), x, gate, final_g,
                     final=(l == DEPTH - 1))
    return x
```

```python
import functools
import math

import numpy as np
import jax
import jax.numpy as jnp
from jax import lax
from jax.experimental import pallas as pl
from jax.experimental.pallas import tpu as pltpu

F32 = jnp.float32
BF16 = jnp.bfloat16

D_MODEL = 2048
DEPTH = 2
GRID_W = 64
D_GROUP = 512
HEAD_DIM = 64
N_HEADS = D_GROUP // HEAD_DIM
N_FFT_GROUPS = 4
FFT_GROUP_DIM = D_GROUP // N_FFT_GROUPS
NA_KH = 8
NA_KW = 16
CONV_WIDTH = 31
CONV_HALF = CONV_WIDTH // 2
ROPE_BASE = 10000.0
EPS = 1e-6
N_PIECES = 13
(P_FX, P_FG, P_NQ, P_NK, P_NV, P_NG, P_RQ, P_RK, P_RV, P_RG, P_CA, P_CB, P_CG) = range(N_PIECES)

MXU_DIM = 256
HEADS_PER_TILE = MXU_DIM // HEAD_DIM
VMEM_LIMIT = 56 * 1024 * 1024
SUBLANES_BF16 = 16
SUBLANES_F32 = 8
LANES = 128

NEG_BIG = -1e30


def _params(sem, vmem=VMEM_LIMIT):
    return pltpu.CompilerParams(dimension_semantics=sem, vmem_limit_bytes=vmem)


def _sigmoid(t):
    return 0.5 * jnp.tanh(0.5 * t) + 0.5


def _silu(t):
    h = 0.5 * t
    return h + h * jnp.tanh(h)


ADA_TN = 768
ADA_ROWS = 8


def _ada_kernel(c_ref, w_ref, b_ref, o_ref):
    ca = _silu(c_ref[...]).astype(BF16)
    w = w_ref[...].astype(BF16)
    o_ref[...] = jnp.dot(ca, w, preferred_element_type=F32) + b_ref[...]


def _ada(c, w_ada, b_ada):
    batch = c.shape[0]
    c_pad = jnp.zeros((ADA_ROWS, D_MODEL), F32).at[:batch].set(c)
    n3 = 3 * D_MODEL
    out = pl.pallas_call(
        _ada_kernel,
        name="ada",
        grid=(DEPTH, n3 // ADA_TN),
        in_specs=[
            pl.BlockSpec((ADA_ROWS, D_MODEL), lambda l, j: (0, 0)),
            pl.BlockSpec((None, D_MODEL, ADA_TN), lambda l, j: (l, 0, j)),
            pl.BlockSpec((None, 1, ADA_TN), lambda l, j: (l, 0, j)),
        ],
        out_specs=pl.BlockSpec((None, ADA_ROWS, ADA_TN), lambda l, j: (l, 0, j)),
        out_shape=jax.ShapeDtypeStruct((DEPTH, ADA_ROWS, n3), F32),
        compiler_params=_params(("arbitrary", "arbitrary")),
    )(c_pad, w_ada, b_ada.reshape(DEPTH, 1, n3))
    return out[:, :batch]


INPROJ_TM = 512
INPROJ_VMEM = 60 * 1024 * 1024


def _inproj_kernel(x_ref, g_ref, sc_ref, sh_ref, w_ref, o_ref, h0_ref, h1_ref, *, n_tiles):
    t = pl.program_id(0)

    def norm(h_ref):
        x = x_ref[...]
        ms = jnp.mean(x * x, axis=-1, keepdims=True)
        h = (x * lax.rsqrt(ms + EPS)) * g_ref[...]
        h = h * (1.0 + sc_ref[...]) + sh_ref[...]
        h_ref[...] = h.astype(BF16)

    def project(h_ref):
        h = h_ref[...]
        for p in range(N_PIECES):
            w = w_ref[:, p * D_GROUP:(p + 1) * D_GROUP]
            o_ref[p] = jnp.dot(h, w, preferred_element_type=F32).astype(BF16)

    odd = t % 2 == 1

    @pl.when(t == 0)
    def _():
        norm(h0_ref)

    @pl.when(odd)
    def _():
        project(h0_ref)
        norm(h1_ref)

    @pl.when(jnp.logical_and(jnp.logical_not(odd), jnp.logical_and(t > 0, t < n_tiles)))
    def _():
        project(h1_ref)
        norm(h0_ref)

    @pl.when(t == n_tiles)
    def _():
        project(h1_ref)


def _inproj(x, g, scale, shift, w_bf16):
    batch, seq, _ = x.shape
    tm = INPROJ_TM
    per_batch = seq // tm
    n_tiles = batch * per_batch
    assert n_tiles % 2 == 0
    norm_tile = lambda t: jnp.minimum(t, n_tiles - 1)
    proj_tile = lambda t: jnp.maximum(t - 1, 0)
    return pl.pallas_call(
        functools.partial(_inproj_kernel, n_tiles=n_tiles),
        name="inproj",
        grid=(n_tiles + 1,),
        in_specs=[
            pl.BlockSpec((None, tm, D_MODEL),
                         lambda t: (norm_tile(t) // per_batch, norm_tile(t) % per_batch, 0)),
            pl.BlockSpec((1, D_MODEL), lambda t: (0, 0)),
            pl.BlockSpec((None, 1, D_MODEL), lambda t: (norm_tile(t) // per_batch, 0, 0)),
            pl.BlockSpec((None, 1, D_MODEL), lambda t: (norm_tile(t) // per_batch, 0, 0)),
            pl.BlockSpec((D_MODEL, N_PIECES * D_GROUP), lambda t: (0, 0),
                         pipeline_mode=pl.Buffered(1)),
        ],
        out_specs=pl.BlockSpec((N_PIECES, None, tm, D_GROUP),
                               lambda t: (0, proj_tile(t) // per_batch, proj_tile(t) % per_batch, 0)),
        out_shape=jax.ShapeDtypeStruct((N_PIECES, batch, seq, D_GROUP), BF16),
        scratch_shapes=[pltpu.VMEM((tm, D_MODEL), BF16), pltpu.VMEM((tm, D_MODEL), BF16)],
        compiler_params=_params(("arbitrary",), INPROJ_VMEM),
    )(x, g.reshape(1, D_MODEL), scale[:, None, :], shift[:, None, :], w_bf16)


OUTPROJ_TM = 512


def _outproj_kernel(a0_ref, a1_ref, a2_ref, a3_ref, w_ref, x_ref, gate_ref, fg_ref, o_ref, *, final):
    a = jnp.concatenate([a0_ref[...], a1_ref[...], a2_ref[...], a3_ref[...]], axis=-1)
    y = jnp.dot(a, w_ref[...], preferred_element_type=F32)
    xn = x_ref[...] + gate_ref[...] * y
    if final:
        ms = jnp.mean(xn * xn, axis=-1, keepdims=True)
        xn = (xn * lax.rsqrt(ms + EPS)) * fg_ref[...]
    o_ref[...] = xn


def _outproj(mixed, w_bf16, x, gate, final_g, final):
    batch, seq, _ = x.shape
    tm = OUTPROJ_TM
    a_spec = pl.BlockSpec((None, tm, D_GROUP), lambda b, i: (b, i, 0))
    return pl.pallas_call(
        functools.partial(_outproj_kernel, final=final),
        name="outproj",
        grid=(batch, seq // tm),
        in_specs=[
            a_spec, a_spec, a_spec, a_spec,
            pl.BlockSpec((D_MODEL, D_MODEL), lambda b, i: (0, 0)),
            pl.BlockSpec((None, tm, D_MODEL), lambda b, i: (b, i, 0)),
            pl.BlockSpec((None, 1, D_MODEL), lambda b, i: (b, 0, 0)),
            pl.BlockSpec((1, D_MODEL), lambda b, i: (0, 0)),
        ],
        out_specs=pl.BlockSpec((None, tm, D_MODEL), lambda b, i: (b, i, 0)),
        out_shape=jax.ShapeDtypeStruct((batch, seq, D_MODEL), F32),
        compiler_params=_params(("arbitrary", "arbitrary")),
    )(*mixed, w_bf16, x, gate[:, None, :], final_g.reshape(1, D_MODEL))


FFT_BLOCK = 16


def _fft_constants(seq):
    rows = seq // GRID_W
    assert rows == GRID_W
    n = np.arange(GRID_W)
    ang1 = 2.0 * np.pi * ((n[:, None] * n[None, :]) % GRID_W) / GRID_W
    f1 = np.concatenate([np.cos(ang1), -np.sin(ang1)], axis=0)
    k1 = n[:, None, None]
    k2 = n[None, :, None]
    s2 = n[None, None, :]
    ang2 = 2.0 * np.pi * ((s2 * (k1 + GRID_W * k2)) % seq) / seq
    mr, mi = np.cos(ang2), -np.sin(ang2)
    m2 = np.concatenate([np.concatenate([mr, -mi], axis=2),
                         np.concatenate([mi, mr], axis=2)], axis=1)
    c = np.arange(FFT_GROUP_DIM)
    angc = 2.0 * np.pi * ((c[:, None] * c[None, :]) % FFT_GROUP_DIM) / FFT_GROUP_DIM
    fc = np.concatenate([np.cos(angc), np.sin(angc)], axis=0)
    return tuple(jnp.asarray(t, F32).astype(BF16) for t in (f1, m2, fc))


def _fft_kernel(u_ref, gate_ref, f1_ref, m2_ref, fc_ref, w_ref, o_ref, u32_ref, g_ref, y_ref, *, norm):
    grp = pl.program_id(1)
    n = GRID_W
    blk = FFT_BLOCK

    u32_ref[...] = u_ref[...].astype(F32)

    def stage1(i, carry):
        for j in range(blk):
            s2 = i * blk + j
            us = u32_ref[pl.ds(s2, n, stride=n), :].astype(BF16)
            gs = jnp.dot(f1_ref[...], us, preferred_element_type=F32)
            g_ref[pl.ds(pl.multiple_of(s2 * 2 * n, 2 * n), 2 * n), :] = gs
        return carry

    lax.fori_loop(0, n // blk, stage1, 0)

    def stage2(i, carry):
        lhs = []
        for j in range(blk):
            k1 = i * blk + j
            gr = g_ref[pl.ds(k1, n, stride=2 * n), :]
            gi = g_ref[pl.ds(n + k1, n, stride=2 * n), :]
            gk = jnp.concatenate([gr, gi], axis=0).astype(BF16)
            xk = jnp.dot(m2_ref[k1], gk, preferred_element_type=F32)
            lhs.append(jnp.concatenate([xk[:n], xk[n:]], axis=-1))
        lhs = jnp.concatenate(lhs, axis=0).astype(BF16)
        y = jnp.dot(lhs, fc_ref[...], preferred_element_type=F32) * norm
        y_ref[grp, pl.ds(pl.multiple_of(i * blk * n, blk * n), blk * n), :] = y
        return carry

    lax.fori_loop(0, n // blk, stage2, 0)

    @pl.when(grp == N_FFT_GROUPS - 1)
    def _():
        def stage3(i, carry):
            tiles = []
            for j in range(blk):
                k2 = i * blk + j
                tiles.append(jnp.concatenate(
                    [y_ref[g, pl.ds(k2, n, stride=n), :] for g in range(N_FFT_GROUPS)], axis=-1))
            y = jnp.concatenate(tiles, axis=0).astype(BF16)
            o = jnp.dot(y, w_ref[...], preferred_element_type=F32)
            rows = pl.ds(pl.multiple_of(i * blk * n, blk * n), blk * n)
            o_ref[rows, :] = (o * _silu(gate_ref[rows, :].astype(F32))).astype(BF16)
            return carry

        lax.fori_loop(0, n // blk, stage3, 0)


def _fourier(zp, w_fft_bf16, consts):
    _, batch, seq, _ = zp.shape
    f1, m2, fc = consts
    n = GRID_W
    const = lambda shape: pl.BlockSpec(shape, lambda b, g: (0,) * len(shape))
    return pl.pallas_call(
        functools.partial(_fft_kernel, norm=1.0 / math.sqrt(seq * FFT_GROUP_DIM)),
        name="fourier",
        grid=(batch, N_FFT_GROUPS),
        in_specs=[
            pl.BlockSpec((None, None, seq, FFT_GROUP_DIM), lambda b, g: (P_FX, b, 0, g)),
            pl.BlockSpec((None, None, seq, D_GROUP), lambda b, g: (P_FG, b, 0, 0)),
            const((2 * n, n)), const((n, 2 * n, 2 * n)), const((2 * FFT_GROUP_DIM, FFT_GROUP_DIM)),
            const((D_GROUP, D_GROUP)),
        ],
        out_specs=pl.BlockSpec((None, seq, D_GROUP), lambda b, g: (b, 0, 0)),
        out_shape=jax.ShapeDtypeStruct((batch, seq, D_GROUP), BF16),
        scratch_shapes=[
            pltpu.VMEM((seq, FFT_GROUP_DIM), F32),
            pltpu.VMEM((n * 2 * n, FFT_GROUP_DIM), F32),
            pltpu.VMEM((N_FFT_GROUPS, seq, FFT_GROUP_DIM), F32),
        ],
        compiler_params=_params(("arbitrary", "arbitrary")),
    )(zp, zp, f1, m2, fc, w_fft_bf16)


NA_ROWS_PER_STEP = 4


def _na_bias_table(rel_bias, rows):
    kh = min(NA_KH, rows)
    col = np.arange(GRID_W)
    col_start = np.clip(col - NA_KW // 2, 0, GRID_W - NA_KW)
    rel_c = col[None, :] - col_start[:, None]
    col_in = (rel_c >= 0) & (rel_c < NA_KW)
    dc = np.clip(col[None, :] - col[:, None] + NA_KW - 1, 0, 2 * NA_KW - 2)
    onehot = dc[None] == np.arange(2 * NA_KW - 1)[:, None, None]
    by_col = jnp.sum(jnp.where(onehot[None, None], rel_bias.astype(F32)[:, :, :, None, None], 0.0),
                     axis=2)
    by_col = jnp.where(col_in[None, None], by_col, NEG_BIG)
    tbl = jnp.stack([by_col[:, NA_KH - 1 - o:NA_KH - 1 - o + kh] for o in range(kh)])
    tbl = tbl.transpose(0, 1, 3, 2, 4)
    return tbl.reshape(kh, N_HEADS, GRID_W, kh * GRID_W)


def _head_select_mask():
    r = lax.broadcasted_iota(jnp.int32, (MXU_DIM, MXU_DIM), 0) // HEAD_DIM
    c = lax.broadcasted_iota(jnp.int32, (MXU_DIM, MXU_DIM), 1) // HEAD_DIM
    return r == c


def _stack_heads(t):
    n = t.shape[0]
    reps = jnp.concatenate([t] * HEADS_PER_TILE, axis=0)
    row_head = lax.broadcasted_iota(jnp.int32, reps.shape, 0) // n
    lane_head = lax.broadcasted_iota(jnp.int32, reps.shape, 1) // HEAD_DIM
    return jnp.where(row_head == lane_head, reps, jnp.zeros_like(reps))


def _unstack_heads(t4, n):
    lane_head = lax.broadcasted_iota(jnp.int32, (n, MXU_DIM), 1) // HEAD_DIM
    out = jnp.zeros((n, MXU_DIM), t4.dtype)
    for h in range(HEADS_PER_TILE):
        out = jnp.where(lane_head == h, t4[h * n:(h + 1) * n], out)
    return out


def _na_kernel(q_ref, k_ref, v_ref, g_ref, bias_ref, o_ref, *, rows):
    kh = min(NA_KH, rows)
    step = pl.program_id(1)

    def row_body(rr, carry):
        r = step * NA_ROWS_PER_STEP + rr
        r_start = jnp.clip(r - kh // 2, 0, rows - kh)
        variant = r - r_start
        q_off = pl.multiple_of(rr * GRID_W, GRID_W)
        k_off = pl.multiple_of(r_start * GRID_W, GRID_W)
        for cg in range(D_GROUP // MXU_DIM):
            lanes = slice(cg * MXU_DIM, (cg + 1) * MXU_DIM)
            q = q_ref[pl.ds(q_off, GRID_W), lanes] * (HEAD_DIM ** -0.5)
            kw = k_ref[pl.ds(k_off, kh * GRID_W), lanes]
            vw = v_ref[pl.ds(k_off, kh * GRID_W), lanes]
            q4 = _stack_heads(q)
            s = lax.dot_general(q4, kw, (((1,), (1,)), ((), ())), preferred_element_type=F32)
            bias = bias_ref[variant, cg * HEADS_PER_TILE:(cg + 1) * HEADS_PER_TILE]
            s = s + bias.reshape(HEADS_PER_TILE * GRID_W, kh * GRID_W)
            m = jnp.max(s, axis=-1, keepdims=True)
            e = jnp.exp(s - m)
            l = jnp.sum(e, axis=-1, keepdims=True)
            o4 = jnp.dot(e.astype(BF16), vw, preferred_element_type=F32) / l
            o = _unstack_heads(o4, GRID_W)
            gate = g_ref[pl.ds(q_off, GRID_W), lanes].astype(F32)
            o_ref[pl.ds(q_off, GRID_W), lanes] = (o * _silu(gate)).astype(BF16)
        return carry

    lax.fori_loop(0, NA_ROWS_PER_STEP, row_body, 0, unroll=True)


def _neighbourhood(zp, bias_tbl):
    _, batch, seq, _ = zp.shape
    rows = seq // GRID_W
    tq = NA_ROWS_PER_STEP * GRID_W
    q_spec = lambda p: pl.BlockSpec((None, None, tq, D_GROUP), lambda b, i: (p, b, i, 0))
    kv_spec = lambda p: pl.BlockSpec((None, None, seq, D_GROUP), lambda b, i: (p, b, 0, 0))
    return pl.pallas_call(
        functools.partial(_na_kernel, rows=rows),
        name="natten",
        grid=(batch, rows // NA_ROWS_PER_STEP),
        in_specs=[
            q_spec(P_NQ), kv_spec(P_NK), kv_spec(P_NV), q_spec(P_NG),
            pl.BlockSpec(bias_tbl.shape, lambda b, i: (0, 0, 0, 0), pipeline_mode=pl.Buffered(1)),
        ],
        out_specs=pl.BlockSpec((None, tq, D_GROUP), lambda b, i: (b, i, 0)),
        out_shape=jax.ShapeDtypeStruct((batch, seq, D_GROUP), BF16),
        compiler_params=_params(("arbitrary", "arbitrary")),
    )(zp, zp, zp, zp, bias_tbl)


RET_CHUNK = MXU_DIM


def _log_sigmoid(t):
    return jnp.minimum(t, 0.0) - jnp.log1p(jnp.exp(-jnp.abs(t)))


def _rope_tables(seq):
    half = HEAD_DIM // 2
    inv = ROPE_BASE ** (-jnp.arange(half, dtype=F32) / half)
    ang = jnp.arange(seq, dtype=F32)[:, None] * inv[None, :]
    cos, sin = jnp.cos(ang), jnp.sin(ang)
    cos2 = jnp.concatenate([cos, cos], axis=-1)
    sin2 = jnp.concatenate([-sin, sin], axis=-1)
    return jnp.tile(cos2, (1, 2)), jnp.tile(sin2, (1, 2))


def _ret_kernel(lf_s_ref, lb_s_ref, q_ref, k_ref, v_ref, g_ref, cos_ref, sin_ref, lfl_ref, lbl_ref,
                o_ref, qr_ref, kr_ref, ob_ref, dmat_ref, dec_ref, state_ref, ones_ref, *, seq):
    C = RET_CHUNK
    n_chunks = seq // C
    cg = pl.program_id(1)
    ones_bd = jnp.where(_head_select_mask(), 1.0, 0.0).astype(BF16)
    ones_ref[...] = jnp.concatenate([ones_bd, ones_bd], axis=0)

    ri = lax.broadcasted_iota(jnp.int32, (C, C), 0)
    ci = lax.broadcasted_iota(jnp.int32, (C, C), 1)
    diff = (ri - ci).astype(F32)
    for hh in range(HEADS_PER_TILE):
        h = cg * HEADS_PER_TILE + hh
        lf = _log_sigmoid(jnp.full((C, C), lf_s_ref[h], F32))
        lb = _log_sigmoid(jnp.full((C, C), lb_s_ref[h], F32))
        dmat_ref[hh] = jnp.where(diff >= 0, jnp.exp(lf * diff), jnp.exp(lb * (-diff)))
    lfl = _log_sigmoid(lfl_ref[...])
    lbl = _log_sigmoid(lbl_ref[...])
    idx = lax.broadcasted_iota(jnp.int32, (C, MXU_DIM), 0).astype(F32)
    dec_ref[0] = jnp.exp(lfl * (idx + 1.0))
    dec_ref[1] = jnp.exp(lfl * (C - 1.0 - idx))
    dec_ref[2] = jnp.exp(lbl * (C - idx))
    dec_ref[3] = jnp.exp(lbl * idx)
    cd_f = jnp.exp(lfl * float(C))
    cd_b = jnp.exp(lbl * float(C))

    def rope(t, rows):
        lane = lax.broadcasted_iota(jnp.int32, t.shape, 1)
        first_half = (lane % HEAD_DIM) < (HEAD_DIM // 2)
        cs = cos_ref[rows, :]
        sn = sin_ref[rows, :]
        cs = jnp.concatenate([cs, cs], axis=-1)
        sn = jnp.concatenate([sn, sn], axis=-1)
        swapped = jnp.where(first_half, pltpu.roll(t, MXU_DIM - HEAD_DIM // 2, 1),
                            pltpu.roll(t, HEAD_DIM // 2, 1))
        return t * cs + swapped * sn

    def kv_update(k, v, kd, cd):
        kv = lax.dot_general((k * kd).astype(BF16), v, (((0,), (0,)), ((), ())),
                             preferred_element_type=F32)
        state_ref[...] = cd * state_ref[...] + jnp.where(_head_select_mask(), kv, 0.0)

    state_ref[...] = jnp.zeros_like(state_ref)

    def bwd_body(i, carry):
        n = n_chunks - 1 - i
        rows = pl.ds(pl.multiple_of(n * C, C), C)
        q = rope(q_ref[rows, :].astype(F32), rows) * (HEAD_DIM ** -0.5)
        k = rope(k_ref[rows, :].astype(F32), rows)
        qr_ref[rows, :] = q.astype(BF16)
        kr_ref[rows, :] = k.astype(BF16)
        ob_ref[rows, :] = jnp.dot((q * dec_ref[2]).astype(BF16), state_ref[...].astype(BF16),
                                  preferred_element_type=F32)
        kv_update(k, v_ref[rows, :], dec_ref[3], cd_b)
        return carry

    lax.fori_loop(0, n_chunks, bwd_body, 0, unroll=2)

    state_ref[...] = jnp.zeros_like(state_ref)

    def fwd_body(n, carry):
        rows = pl.ds(pl.multiple_of(n * C, C), C)
        qb = qr_ref[rows, :]
        kb = kr_ref[rows, :]
        v = v_ref[rows, :]
        o_cross = jnp.dot((qb.astype(F32) * dec_ref[0]).astype(BF16), state_ref[...].astype(BF16),
                          preferred_element_type=F32)
        q4 = _stack_heads(qb)
        sc = lax.dot_general(q4, kb, (((1,), (1,)), ((), ())), preferred_element_type=F32)
        sc = sc * dmat_ref[...].reshape(HEADS_PER_TILE * C, C)
        o4 = jnp.dot(sc.astype(BF16), v, preferred_element_type=F32)
        o = _unstack_heads(o4, C) + o_cross + ob_ref[rows, :]
        sq = o * o
        hi = sq.astype(BF16)
        lo = (sq - hi.astype(F32)).astype(BF16)
        ss = jnp.dot(jnp.concatenate([hi, lo], axis=-1), ones_ref[...], preferred_element_type=F32)
        inv = lax.rsqrt(ss * (1.0 / HEAD_DIM) + EPS)
        gate = g_ref[rows, :].astype(F32)
        o_ref[rows, :] = (o * inv * _silu(gate)).astype(BF16)
        kv_update(kb.astype(F32), v, dec_ref[1], cd_f)
        return carry

    lax.fori_loop(0, n_chunks, fwd_body, 0, unroll=2)


def _retention(zp, logit_f, logit_b, rope_tbl):
    _, batch, seq, _ = zp.shape
    cos_t, sin_t = rope_tbl
    n_cg = D_GROUP // MXU_DIM
    spec = lambda p: pl.BlockSpec((None, None, seq, MXU_DIM), lambda b, c, *_: (p, b, 0, c))
    tbl_spec = pl.BlockSpec((seq, 2 * HEAD_DIM), lambda b, c, *_: (0, 0))
    lane_spec = pl.BlockSpec((1, MXU_DIM), lambda b, c, *_: (0, c))
    lane_f = jnp.repeat(logit_f.astype(F32), HEAD_DIM)[None, :]
    lane_b = jnp.repeat(logit_b.astype(F32), HEAD_DIM)[None, :]
    return pl.pallas_call(
        functools.partial(_ret_kernel, seq=seq),
        name="retention",
        grid_spec=pltpu.PrefetchScalarGridSpec(
            num_scalar_prefetch=2,
            grid=(batch, n_cg),
            in_specs=[spec(P_RQ), spec(P_RK), spec(P_RV), spec(P_RG), tbl_spec, tbl_spec,
                      lane_spec, lane_spec],
            out_specs=pl.BlockSpec((None, seq, MXU_DIM), lambda b, c, *_: (b, 0, c)),
            scratch_shapes=[
                pltpu.VMEM((seq, MXU_DIM), BF16),
                pltpu.VMEM((seq, MXU_DIM), BF16),
                pltpu.VMEM((seq, MXU_DIM), F32),
                pltpu.VMEM((HEADS_PER_TILE, RET_CHUNK, RET_CHUNK), F32),
                pltpu.VMEM((4, RET_CHUNK, MXU_DIM), F32),
                pltpu.VMEM((MXU_DIM, MXU_DIM), F32),
                pltpu.VMEM((2 * MXU_DIM, MXU_DIM), BF16),
            ],
        ),
        out_shape=jax.ShapeDtypeStruct((batch, seq, D_GROUP), BF16),
        compiler_params=_params(("arbitrary", "arbitrary")),
    )(logit_f.astype(F32), logit_b.astype(F32), zp, zp, zp, zp, cos_t, sin_t, lane_f, lane_b)


CONV_TS = 512
CONV_HALO = SUBLANES_BF16
CONV_ROWS = 64


def _conv_kernel(a_ref, b_ref, g_ref, cw_ref, cb_ref, lg_ref, lb_ref, w_ref, o_ref, u_ref, h_ref, *, seq):
    ts, halo = CONV_TS, CONV_HALO
    j = pl.program_id(1)
    n_tiles = seq // ts

    def glu(start, size):
        rows = pl.ds(pl.multiple_of(start, halo), size)
        a = a_ref[rows, :].astype(F32)
        return a * _sigmoid(b_ref[rows, :].astype(F32))

    base = j * ts
    u_ref[halo:halo + ts, :] = glu(base, ts)
    lo = glu(jnp.maximum(base - halo, 0), halo)
    u_ref[0:halo, :] = jnp.where(j > 0, lo, 0.0)
    hi = glu(jnp.minimum(base + ts, seq - halo), halo)
    u_ref[halo + ts:, :] = jnp.where(j < n_tiles - 1, hi, 0.0)

    rb = CONV_ROWS
    first = halo - CONV_HALF

    def rows_body(i, carry):
        s0 = pl.multiple_of(i * rb, rb)
        parts = []
        for lt in range(D_GROUP // LANES):
            lanes = slice(lt * LANES, (lt + 1) * LANES)
            y = None
            for r in range(SUBLANES_F32):
                acc = None
                for o in range(r, first + CONV_WIDTH, SUBLANES_F32):
                    if o < first:
                        continue
                    w = o - first
                    term = u_ref[pl.ds(s0 + (o - r), rb + SUBLANES_F32), lanes] * cw_ref[w:w + 1, lanes]
                    acc = term if acc is None else acc + term
                shifted = acc[r:r + rb]
                y = shifted if y is None else y + shifted
            parts.append(y)
        y = jnp.concatenate(parts, axis=-1) + cb_ref[...]
        mu = jnp.mean(y, axis=-1, keepdims=True)
        yc = y - mu
        var = jnp.mean(yc * yc, axis=-1, keepdims=True)
        y = (yc * lax.rsqrt(var + EPS)) * lg_ref[...] + lb_ref[...]
        h_ref[pl.ds(s0, rb), :] = _silu(y).astype(BF16)
        return carry

    lax.fori_loop(0, ts // rb, rows_body, 0)
    o = jnp.dot(h_ref[...], w_ref[...], preferred_element_type=F32)
    gate = g_ref[...].astype(F32)
    o_ref[...] = (o * _silu(gate)).astype(BF16)


def _conformer(zp, conv_w, conv_b, ln_g, ln_b, w_pw_bf16):
    _, batch, seq, _ = zp.shape
    ts = CONV_TS
    full = lambda p: pl.BlockSpec((None, None, seq, D_GROUP), lambda b, j: (p, b, 0, 0))
    vec = pl.BlockSpec((1, D_GROUP), lambda b, j: (0, 0))
    return pl.pallas_call(
        functools.partial(_conv_kernel, seq=seq),
        name="conformer",
        grid=(batch, seq // ts),
        in_specs=[
            full(P_CA), full(P_CB),
            pl.BlockSpec((None, None, ts, D_GROUP), lambda b, j: (P_CG, b, j, 0)),
            pl.BlockSpec((CONV_WIDTH, D_GROUP), lambda b, j: (0, 0)),
            vec, vec, vec,
            pl.BlockSpec((D_GROUP, D_GROUP), lambda b, j: (0, 0)),
        ],
        out_specs=pl.BlockSpec((None, ts, D_GROUP), lambda b, j: (b, j, 0)),
        out_shape=jax.ShapeDtypeStruct((batch, seq, D_GROUP), BF16),
        scratch_shapes=[pltpu.VMEM((ts + 2 * CONV_HALO, D_GROUP), F32),
                        pltpu.VMEM((ts, D_GROUP), BF16)],
        compiler_params=_params(("arbitrary", "arbitrary")),
    )(zp, zp, zp, conv_w, conv_b.reshape(1, D_GROUP), ln_g.reshape(1, D_GROUP),
      ln_b.reshape(1, D_GROUP), w_pw_bf16)


def kernel(x, c, norm_g, w_ada, b_ada, w_in, w_fft, na_rel_bias, ret_logit_fwd, ret_logit_bwd,
           conv_w, conv_b, conv_ln_g, conv_ln_b, conv_w_pw, w_out, final_g):
    batch, seq, _ = x.shape
    rows = seq // GRID_W
    mod = _ada(c, w_ada, b_ada)
    fft_consts = _fft_constants(seq)
    rope_tbl = _rope_tables(seq)
    for l in range(DEPTH):
        shift, scale, gate = jnp.split(mod[l], 3, axis=-1)
        zp = _inproj(x, norm_g[l], scale, shift, w_in[l].astype(BF16))
        o_fft = _fourier(zp, w_fft[l].astype(BF16), fft_consts)
        o_na = _neighbourhood(zp, _na_bias_table(na_rel_bias[l], rows))
        o_ret = _retention(zp, ret_logit_fwd[l], ret_logit_bwd[l], rope_tbl)
        o_cv = _conformer(zp, conv_w[l], conv_b[l], conv_ln_g[l], conv_ln_b[l],
                          conv_w_pw[l].astype(BF16))
        x = _outproj((o_fft, o_na, o_ret, o_cv), w_out[l].astype(BF16), x, gate, final_g,
                     final=(l == DEPTH - 1))
    return x
```

```python
import functools
import math

import numpy as np
import jax
import jax.numpy as jnp
from jax import lax
from jax.experimental import pallas as pl
from jax.experimental.pallas import tpu as pltpu

F32 = jnp.float32
BF16 = jnp.bfloat16

D_MODEL = 2048
DEPTH = 2
GRID_W = 64
D_GROUP = 512
HEAD_DIM = 64
N_HEADS = D_GROUP // HEAD_DIM
N_FFT_GROUPS = 4
FFT_GROUP_DIM = D_GROUP // N_FFT_GROUPS
NA_KH = 8
NA_KW = 16
CONV_WIDTH = 31
CONV_HALF = CONV_WIDTH // 2
ROPE_BASE = 10000.0
EPS = 1e-6
N_PIECES = 13
(P_FX, P_FG, P_NQ, P_NK, P_NV, P_NG, P_RQ, P_RK, P_RV, P_RG, P_CA, P_CB, P_CG) = range(N_PIECES)

MXU_DIM = 256
HEADS_PER_TILE = MXU_DIM // HEAD_DIM
VMEM_LIMIT = 56 * 1024 * 1024
SUBLANES_BF16 = 16
SUBLANES_F32 = 8
LANES = 128

NEG_BIG = -1e30


def _params(sem, vmem=VMEM_LIMIT):
    return pltpu.CompilerParams(dimension_semantics=sem, vmem_limit_bytes=vmem)


def _sigmoid(t):
    return 0.5 * jnp.tanh(0.5 * t) + 0.5


def _silu(t):
    h = 0.5 * t
    return h + h * jnp.tanh(h)


ADA_TN = 768
ADA_ROWS = 8
(MOD_SHIFT, MOD_SCALE, MOD_GATE) = range(3)


def _ada_kernel(c_ref, w_ref, b_ref, o_ref):
    ca = _silu(c_ref[...]).astype(BF16)
    w = w_ref[...].astype(BF16)
    o_ref[...] = jnp.dot(ca, w, preferred_element_type=F32) + b_ref[...]


def _ada(c, w_ada, b_ada):
    batch = c.shape[0]
    c_pad = jnp.zeros((ADA_ROWS, D_MODEL), F32).at[:batch].set(c)
    n3 = 3 * D_MODEL
    out = pl.pallas_call(
        _ada_kernel,
        name="ada",
        grid=(DEPTH, n3 // ADA_TN),
        in_specs=[
            pl.BlockSpec((ADA_ROWS, D_MODEL), lambda l, j: (0, 0)),
            pl.BlockSpec((None, D_MODEL, ADA_TN), lambda l, j: (l, 0, j)),
            pl.BlockSpec((None, 1, ADA_TN), lambda l, j: (l, 0, j)),
        ],
        out_specs=pl.BlockSpec((None, ADA_ROWS, ADA_TN), lambda l, j: (l, 0, j)),
        out_shape=jax.ShapeDtypeStruct((DEPTH, ADA_ROWS, n3), F32),
        compiler_params=_params(("arbitrary", "arbitrary")),
    )(c_pad, w_ada, b_ada.reshape(DEPTH, 1, n3))
    return out.reshape(DEPTH, ADA_ROWS, 1, n3)


INPROJ_TM = 512
INPROJ_VMEM = 60 * 1024 * 1024


def _inproj_kernel(x_ref, g_ref, sc_ref, sh_ref, w_ref, o_ref, h0_ref, h1_ref, *, n_tiles):
    t = pl.program_id(0)

    def norm(h_ref):
        x = x_ref[...]
        ms = jnp.mean(x * x, axis=-1, keepdims=True)
        h = (x * lax.rsqrt(ms + EPS)) * g_ref[...]
        h = h * (1.0 + sc_ref[...]) + sh_ref[...]
        h_ref[...] = h.astype(BF16)

    def project(h_ref):
        h = h_ref[...]
        for p in range(N_PIECES):
            w = w_ref[:, p * D_GROUP:(p + 1) * D_GROUP]
            o_ref[p] = jnp.dot(h, w, preferred_element_type=F32).astype(BF16)

    odd = t % 2 == 1

    @pl.when(t == 0)
    def _():
        norm(h0_ref)

    @pl.when(odd)
    def _():
        project(h0_ref)
        norm(h1_ref)

    @pl.when(jnp.logical_and(jnp.logical_not(odd), jnp.logical_and(t > 0, t < n_tiles)))
    def _():
        project(h1_ref)
        norm(h0_ref)

    @pl.when(t == n_tiles)
    def _():
        project(h1_ref)


def _inproj(x, g, mod, w_bf16, layer):
    batch, seq, _ = x.shape
    tm = INPROJ_TM
    per_batch = seq // tm
    n_tiles = batch * per_batch
    assert n_tiles % 2 == 0
    norm_tile = lambda t: jnp.minimum(t, n_tiles - 1)
    proj_tile = lambda t: jnp.maximum(t - 1, 0)
    return pl.pallas_call(
        functools.partial(_inproj_kernel, n_tiles=n_tiles),
        name="inproj",
        grid=(n_tiles + 1,),
        in_specs=[
            pl.BlockSpec((None, tm, D_MODEL),
                         lambda t: (norm_tile(t) // per_batch, norm_tile(t) % per_batch, 0)),
            pl.BlockSpec((1, D_MODEL), lambda t: (0, 0)),
            pl.BlockSpec((None, None, 1, D_MODEL),
                         lambda t: (layer, norm_tile(t) // per_batch, 0, MOD_SCALE)),
            pl.BlockSpec((None, None, 1, D_MODEL),
                         lambda t: (layer, norm_tile(t) // per_batch, 0, MOD_SHIFT)),
            pl.BlockSpec((None, D_MODEL, N_PIECES * D_GROUP), lambda t: (layer, 0, 0),
                         pipeline_mode=pl.Buffered(1)),
        ],
        out_specs=pl.BlockSpec((N_PIECES, None, tm, D_GROUP),
                               lambda t: (0, proj_tile(t) // per_batch, proj_tile(t) % per_batch, 0)),
        out_shape=jax.ShapeDtypeStruct((N_PIECES, batch, seq, D_GROUP), BF16),
        scratch_shapes=[pltpu.VMEM((tm, D_MODEL), BF16), pltpu.VMEM((tm, D_MODEL), BF16)],
        compiler_params=_params(("arbitrary",), INPROJ_VMEM),
    )(x, g.reshape(1, D_MODEL), mod, mod, w_bf16)


OUTPROJ_TM = 512


def _outproj_kernel(a0_ref, a1_ref, a2_ref, a3_ref, w_ref, x_ref, gate_ref, fg_ref, o_ref, *, final):
    a = jnp.concatenate([a0_ref[...], a1_ref[...], a2_ref[...], a3_ref[...]], axis=-1)
    y = jnp.dot(a, w_ref[...], preferred_element_type=F32)
    xn = x_ref[...] + gate_ref[...] * y
    if final:
        ms = jnp.mean(xn * xn, axis=-1, keepdims=True)
        xn = (xn * lax.rsqrt(ms + EPS)) * fg_ref[...]
    o_ref[...] = xn


def _outproj(mixed, w_bf16, x, mod, final_g, layer, final):
    batch, seq, _ = x.shape
    tm = OUTPROJ_TM
    a_spec = pl.BlockSpec((None, tm, D_GROUP), lambda b, i: (b, i, 0))
    return pl.pallas_call(
        functools.partial(_outproj_kernel, final=final),
        name="outproj",
        grid=(batch, seq // tm),
        in_specs=[
            a_spec, a_spec, a_spec, a_spec,
            pl.BlockSpec((None, D_MODEL, D_MODEL), lambda b, i: (layer, 0, 0)),
            pl.BlockSpec((None, tm, D_MODEL), lambda b, i: (b, i, 0)),
            pl.BlockSpec((None, None, 1, D_MODEL), lambda b, i: (layer, b, 0, MOD_GATE)),
            pl.BlockSpec((1, D_MODEL), lambda b, i: (0, 0)),
        ],
        out_specs=pl.BlockSpec((None, tm, D_MODEL), lambda b, i: (b, i, 0)),
        out_shape=jax.ShapeDtypeStruct((batch, seq, D_MODEL), F32),
        compiler_params=_params(("arbitrary", "arbitrary")),
    )(*mixed, w_bf16, x, mod, final_g.reshape(1, D_MODEL))


FFT_BLOCK = 16


def _fft_constants(seq):
    rows = seq // GRID_W
    assert rows == GRID_W
    n = np.arange(GRID_W)
    ang1 = 2.0 * np.pi * ((n[:, None] * n[None, :]) % GRID_W) / GRID_W
    f1 = np.concatenate([np.cos(ang1), -np.sin(ang1)], axis=0)
    k1 = n[:, None, None]
    k2 = n[None, :, None]
    s2 = n[None, None, :]
    ang2 = 2.0 * np.pi * ((s2 * (k1 + GRID_W * k2)) % seq) / seq
    mr, mi = np.cos(ang2), -np.sin(ang2)
    m2 = np.concatenate([np.concatenate([mr, -mi], axis=2),
                         np.concatenate([mi, mr], axis=2)], axis=1)
    c = np.arange(FFT_GROUP_DIM)
    angc = 2.0 * np.pi * ((c[:, None] * c[None, :]) % FFT_GROUP_DIM) / FFT_GROUP_DIM
    fc = np.concatenate([np.cos(angc), np.sin(angc)], axis=0)
    return tuple(jnp.asarray(t, F32).astype(BF16) for t in (f1, m2, fc))


def _fft_kernel(u_ref, gate_ref, f1_ref, m2_ref, fc_ref, w_ref, o_ref, u32_ref, g_ref, y_ref, *, norm):
    grp = pl.program_id(1)
    n = GRID_W
    blk = FFT_BLOCK

    u32_ref[...] = u_ref[...].astype(F32)

    def stage1(i, carry):
        for j in range(blk):
            s2 = i * blk + j
            us = u32_ref[pl.ds(s2, n, stride=n), :].astype(BF16)
            gs = jnp.dot(f1_ref[...], us, preferred_element_type=F32)
            g_ref[pl.ds(pl.multiple_of(s2 * 2 * n, 2 * n), 2 * n), :] = gs
        return carry

    lax.fori_loop(0, n // blk, stage1, 0)

    def stage2(i, carry):
        lhs = []
        for j in range(blk):
            k1 = i * blk + j
            gr = g_ref[pl.ds(k1, n, stride=2 * n), :]
            gi = g_ref[pl.ds(n + k1, n, stride=2 * n), :]
            gk = jnp.concatenate([gr, gi], axis=0).astype(BF16)
            xk = jnp.dot(m2_ref[k1], gk, preferred_element_type=F32)
            lhs.append(jnp.concatenate([xk[:n], xk[n:]], axis=-1))
        lhs = jnp.concatenate(lhs, axis=0).astype(BF16)
        y = jnp.dot(lhs, fc_ref[...], preferred_element_type=F32) * norm
        y_ref[grp, pl.ds(pl.multiple_of(i * blk * n, blk * n), blk * n), :] = y
        return carry

    lax.fori_loop(0, n // blk, stage2, 0)

    @pl.when(grp == N_FFT_GROUPS - 1)
    def _():
        def stage3(i, carry):
            tiles = []
            for j in range(blk):
                k2 = i * blk + j
                tiles.append(jnp.concatenate(
                    [y_ref[g, pl.ds(k2, n, stride=n), :] for g in range(N_FFT_GROUPS)], axis=-1))
            y = jnp.concatenate(tiles, axis=0).astype(BF16)
            o = jnp.dot(y, w_ref[...], preferred_element_type=F32)
            rows = pl.ds(pl.multiple_of(i * blk * n, blk * n), blk * n)
            o_ref[rows, :] = (o * _silu(gate_ref[rows, :].astype(F32))).astype(BF16)
            return carry

        lax.fori_loop(0, n // blk, stage3, 0)


def _fourier(zp, w_fft_bf16, consts, layer):
    _, batch, seq, _ = zp.shape
    f1, m2, fc = consts
    n = GRID_W
    const = lambda shape: pl.BlockSpec(shape, lambda b, g: (0,) * len(shape))
    return pl.pallas_call(
        functools.partial(_fft_kernel, norm=1.0 / math.sqrt(seq * FFT_GROUP_DIM)),
        name="fourier",
        grid=(batch, N_FFT_GROUPS),
        in_specs=[
            pl.BlockSpec((None, None, seq, FFT_GROUP_DIM), lambda b, g: (P_FX, b, 0, g)),
            pl.BlockSpec((None, None, seq, D_GROUP), lambda b, g: (P_FG, b, 0, 0)),
            const((2 * n, n)), const((n, 2 * n, 2 * n)), const((2 * FFT_GROUP_DIM, FFT_GROUP_DIM)),
            pl.BlockSpec((None, D_GROUP, D_GROUP), lambda b, g: (layer, 0, 0)),
        ],
        out_specs=pl.BlockSpec((None, seq, D_GROUP), lambda b, g: (b, 0, 0)),
        out_shape=jax.ShapeDtypeStruct((batch, seq, D_GROUP), BF16),
        scratch_shapes=[
            pltpu.VMEM((seq, FFT_GROUP_DIM), F32),
            pltpu.VMEM((n * 2 * n, FFT_GROUP_DIM), F32),
            pltpu.VMEM((N_FFT_GROUPS, seq, FFT_GROUP_DIM), F32),
        ],
        compiler_params=_params(("arbitrary", "arbitrary")),
    )(zp, zp, f1, m2, fc, w_fft_bf16)


NA_ROWS_PER_STEP = 4


def _na_bias_table(rel_bias, rows):
    kh = min(NA_KH, rows)
    n_dc = 2 * NA_KW - 1
    col = np.arange(GRID_W)
    col_start = np.clip(col - NA_KW // 2, 0, GRID_W - NA_KW)
    rel_c = col[None, :] - col_start[:, None]
    col_in = (rel_c >= 0) & (rel_c < NA_KW)
    dc = np.clip(col[None, :] - col[:, None] + NA_KW - 1, 0, n_dc - 1)
    onehot = (dc[None] == np.arange(n_dc)[:, None, None]).astype(np.float32)
    by_col = jnp.einsum('lhdj,jqk->lhdqk', rel_bias.astype(F32), jnp.asarray(onehot),
                        precision=lax.Precision.HIGHEST)
    by_col = jnp.where(col_in[None, None, None], by_col, NEG_BIG)
    tbl = jnp.stack([by_col[:, :, NA_KH - 1 - o:NA_KH - 1 - o + kh] for o in range(kh)], axis=1)
    tbl = tbl.transpose(0, 1, 2, 4, 3, 5)
    return tbl.reshape(rel_bias.shape[0], kh, N_HEADS, GRID_W, kh * GRID_W)


def _head_select_mask():
    r = lax.broadcasted_iota(jnp.int32, (MXU_DIM, MXU_DIM), 0) // HEAD_DIM
    c = lax.broadcasted_iota(jnp.int32, (MXU_DIM, MXU_DIM), 1) // HEAD_DIM
    return r == c


def _stack_heads(t):
    n = t.shape[0]
    reps = jnp.concatenate([t] * HEADS_PER_TILE, axis=0)
    row_head = lax.broadcasted_iota(jnp.int32, reps.shape, 0) // n
    lane_head = lax.broadcasted_iota(jnp.int32, reps.shape, 1) // HEAD_DIM
    return jnp.where(row_head == lane_head, reps, jnp.zeros_like(reps))


def _unstack_heads(t4, n):
    lane_head = lax.broadcasted_iota(jnp.int32, (n, MXU_DIM), 1) // HEAD_DIM
    out = jnp.zeros((n, MXU_DIM), t4.dtype)
    for h in range(HEADS_PER_TILE):
        out = jnp.where(lane_head == h, t4[h * n:(h + 1) * n], out)
    return out


def _na_kernel(q_ref, k_ref, v_ref, g_ref, bias_ref, o_ref, *, rows):
    kh = min(NA_KH, rows)
    step = pl.program_id(1)

    def row_body(rr, carry):
        r = step * NA_ROWS_PER_STEP + rr
        r_start = jnp.clip(r - kh // 2, 0, rows - kh)
        variant = r - r_start
        q_off = pl.multiple_of(rr * GRID_W, GRID_W)
        k_off = pl.multiple_of(r_start * GRID_W, GRID_W)
        for cg in range(D_GROUP // MXU_DIM):
            lanes = slice(cg * MXU_DIM, (cg + 1) * MXU_DIM)
            q = q_ref[pl.ds(q_off, GRID_W), lanes] * (HEAD_DIM ** -0.5)
            kw = k_ref[pl.ds(k_off, kh * GRID_W), lanes]
            vw = v_ref[pl.ds(k_off, kh * GRID_W), lanes]
            q4 = _stack_heads(q)
            s = lax.dot_general(q4, kw, (((1,), (1,)), ((), ())), preferred_element_type=F32)
            bias = bias_ref[variant, cg * HEADS_PER_TILE:(cg + 1) * HEADS_PER_TILE]
            s = s + bias.reshape(HEADS_PER_TILE * GRID_W, kh * GRID_W)
            m = jnp.max(s, axis=-1, keepdims=True)
            e = jnp.exp(s - m)
            l = jnp.sum(e, axis=-1, keepdims=True)
            o4 = jnp.dot(e.astype(BF16), vw, preferred_element_type=F32) / l
            o = _unstack_heads(o4, GRID_W)
            gate = g_ref[pl.ds(q_off, GRID_W), lanes].astype(F32)
            o_ref[pl.ds(q_off, GRID_W), lanes] = (o * _silu(gate)).astype(BF16)
        return carry

    lax.fori_loop(0, NA_ROWS_PER_STEP, row_body, 0, unroll=True)


def _neighbourhood(zp, bias_tbl, layer):
    _, batch, seq, _ = zp.shape
    rows = seq // GRID_W
    tq = NA_ROWS_PER_STEP * GRID_W
    q_spec = lambda p: pl.BlockSpec((None, None, tq, D_GROUP), lambda b, i: (p, b, i, 0))
    kv_spec = lambda p: pl.BlockSpec((None, None, seq, D_GROUP), lambda b, i: (p, b, 0, 0))
    return pl.pallas_call(
        functools.partial(_na_kernel, rows=rows),
        name="natten",
        grid=(batch, rows // NA_ROWS_PER_STEP),
        in_specs=[
            q_spec(P_NQ), kv_spec(P_NK), kv_spec(P_NV), q_spec(P_NG),
            pl.BlockSpec((None,) + bias_tbl.shape[1:], lambda b, i: (layer, 0, 0, 0, 0),
                         pipeline_mode=pl.Buffered(1)),
        ],
        out_specs=pl.BlockSpec((None, tq, D_GROUP), lambda b, i: (b, i, 0)),
        out_shape=jax.ShapeDtypeStruct((batch, seq, D_GROUP), BF16),
        compiler_params=_params(("arbitrary", "arbitrary")),
    )(zp, zp, zp, zp, bias_tbl)


RET_CHUNK = MXU_DIM


def _log_sigmoid(t):
    return jnp.minimum(t, 0.0) - jnp.log1p(jnp.exp(-jnp.abs(t)))


def _rope_tables(seq):
    half = HEAD_DIM // 2
    inv = ROPE_BASE ** (-jnp.arange(half, dtype=F32) / half)
    ang = jnp.arange(seq, dtype=F32)[:, None] * inv[None, :]
    cos, sin = jnp.cos(ang), jnp.sin(ang)
    cos2 = jnp.concatenate([cos, cos], axis=-1)
    sin2 = jnp.concatenate([-sin, sin], axis=-1)
    return jnp.tile(cos2, (1, 2)), jnp.tile(sin2, (1, 2))


def _ret_kernel(lf_s_ref, lb_s_ref, q_ref, k_ref, v_ref, g_ref, cos_ref, sin_ref, lfl_ref, lbl_ref,
                o_ref, qr_ref, kr_ref, ob_ref, dmat_ref, dec_ref, state_ref, ones_ref, *, seq):
    C = RET_CHUNK
    n_chunks = seq // C
    cg = pl.program_id(1)
    ones_bd = jnp.where(_head_select_mask(), 1.0, 0.0).astype(BF16)
    ones_ref[...] = jnp.concatenate([ones_bd, ones_bd], axis=0)

    ri = lax.broadcasted_iota(jnp.int32, (C, C), 0)
    ci = lax.broadcasted_iota(jnp.int32, (C, C), 1)
    diff = (ri - ci).astype(F32)
    for hh in range(HEADS_PER_TILE):
        h = cg * HEADS_PER_TILE + hh
        lf = _log_sigmoid(jnp.full((C, C), lf_s_ref[h], F32))
        lb = _log_sigmoid(jnp.full((C, C), lb_s_ref[h], F32))
        dmat_ref[hh] = jnp.where(diff >= 0, jnp.exp(lf * diff), jnp.exp(lb * (-diff)))
    lfl = _log_sigmoid(lfl_ref[...])
    lbl = _log_sigmoid(lbl_ref[...])
    idx = lax.broadcasted_iota(jnp.int32, (C, MXU_DIM), 0).astype(F32)
    dec_ref[0] = jnp.exp(lfl * (idx + 1.0))
    dec_ref[1] = jnp.exp(lfl * (C - 1.0 - idx))
    dec_ref[2] = jnp.exp(lbl * (C - idx))
    dec_ref[3] = jnp.exp(lbl * idx)
    cd_f = jnp.exp(lfl * float(C))
    cd_b = jnp.exp(lbl * float(C))

    def rope(t, rows):
        lane = lax.broadcasted_iota(jnp.int32, t.shape, 1)
        first_half = (lane % HEAD_DIM) < (HEAD_DIM // 2)
        cs = cos_ref[rows, :]
        sn = sin_ref[rows, :]
        cs = jnp.concatenate([cs, cs], axis=-1)
        sn = jnp.concatenate([sn, sn], axis=-1)
        swapped = jnp.where(first_half, pltpu.roll(t, MXU_DIM - HEAD_DIM // 2, 1),
                            pltpu.roll(t, HEAD_DIM // 2, 1))
        return t * cs + swapped * sn

    def kv_update(k, v, kd, cd):
        kv = lax.dot_general((k * kd).astype(BF16), v, (((0,), (0,)), ((), ())),
                             preferred_element_type=F32)
        state_ref[...] = cd * state_ref[...] + jnp.where(_head_select_mask(), kv, 0.0)

    state_ref[...] = jnp.zeros_like(state_ref)

    def bwd_body(i, carry):
        n = n_chunks - 1 - i
        rows = pl.ds(pl.multiple_of(n * C, C), C)
        q = rope(q_ref[rows, :].astype(F32), rows) * (HEAD_DIM ** -0.5)
        k = rope(k_ref[rows, :].astype(F32), rows)
        qr_ref[rows, :] = q.astype(BF16)
        kr_ref[rows, :] = k.astype(BF16)
        ob_ref[rows, :] = jnp.dot((q * dec_ref[2]).astype(BF16), state_ref[...].astype(BF16),
                                  preferred_element_type=F32)
        kv_update(k, v_ref[rows, :], dec_ref[3], cd_b)
        return carry

    lax.fori_loop(0, n_chunks, bwd_body, 0, unroll=2)

    state_ref[...] = jnp.zeros_like(state_ref)

    def fwd_body(n, carry):
        rows = pl.ds(pl.multiple_of(n * C, C), C)
        qb = qr_ref[rows, :]
        kb = kr_ref[rows, :]
        v = v_ref[rows, :]
        o_cross = jnp.dot((qb.astype(F32) * dec_ref[0]).astype(BF16), state_ref[...].astype(BF16),
                          preferred_element_type=F32)
        q4 = _stack_heads(qb)
        sc = lax.dot_general(q4, kb, (((1,), (1,)), ((), ())), preferred_element_type=F32)
        sc = sc * dmat_ref[...].reshape(HEADS_PER_TILE * C, C)
        o4 = jnp.dot(sc.astype(BF16), v, preferred_element_type=F32)
        o = _unstack_heads(o4, C) + o_cross + ob_ref[rows, :]
        sq = o * o
        hi = sq.astype(BF16)
        lo = (sq - hi.astype(F32)).astype(BF16)
        ss = jnp.dot(jnp.concatenate([hi, lo], axis=-1), ones_ref[...], preferred_element_type=F32)
        inv = lax.rsqrt(ss * (1.0 / HEAD_DIM) + EPS)
        gate = g_ref[rows, :].astype(F32)
        o_ref[rows, :] = (o * inv * _silu(gate)).astype(BF16)
        kv_update(kb.astype(F32), v, dec_ref[1], cd_f)
        return carry

    lax.fori_loop(0, n_chunks, fwd_body, 0, unroll=2)


def _retention(zp, logit_f, logit_b, rope_tbl):
    _, batch, seq, _ = zp.shape
    cos_t, sin_t = rope_tbl
    n_cg = D_GROUP // MXU_DIM
    spec = lambda p: pl.BlockSpec((None, None, seq, MXU_DIM), lambda b, c, *_: (p, b, 0, c))
    tbl_spec = pl.BlockSpec((seq, 2 * HEAD_DIM), lambda b, c, *_: (0, 0))
    lane_spec = pl.BlockSpec((1, MXU_DIM), lambda b, c, *_: (0, c))
    lane_f = jnp.repeat(logit_f.astype(F32), HEAD_DIM)[None, :]
    lane_b = jnp.repeat(logit_b.astype(F32), HEAD_DIM)[None, :]
    return pl.pallas_call(
        functools.partial(_ret_kernel, seq=seq),
        name="retention",
        grid_spec=pltpu.PrefetchScalarGridSpec(
            num_scalar_prefetch=2,
            grid=(batch, n_cg),
            in_specs=[spec(P_RQ), spec(P_RK), spec(P_RV), spec(P_RG), tbl_spec, tbl_spec,
                      lane_spec, lane_spec],
            out_specs=pl.BlockSpec((None, seq, MXU_DIM), lambda b, c, *_: (b, 0, c)),
            scratch_shapes=[
                pltpu.VMEM((seq, MXU_DIM), BF16),
                pltpu.VMEM((seq, MXU_DIM), BF16),
                pltpu.VMEM((seq, MXU_DIM), F32),
                pltpu.VMEM((HEADS_PER_TILE, RET_CHUNK, RET_CHUNK), F32),
                pltpu.VMEM((4, RET_CHUNK, MXU_DIM), F32),
                pltpu.VMEM((MXU_DIM, MXU_DIM), F32),
                pltpu.VMEM((2 * MXU_DIM, MXU_DIM), BF16),
            ],
        ),
        out_shape=jax.ShapeDtypeStruct((batch, seq, D_GROUP), BF16),
        compiler_params=_params(("arbitrary", "arbitrary")),
    )(logit_f.astype(F32), logit_b.astype(F32), zp, zp, zp, zp, cos_t, sin_t, lane_f, lane_b)


CONV_TS = 512
CONV_HALO = SUBLANES_BF16
CONV_ROWS = 64


def _conv_kernel(a_ref, b_ref, g_ref, cw_ref, cb_ref, lg_ref, lb_ref, w_ref, o_ref, u_ref, h_ref, *, seq):
    ts, halo = CONV_TS, CONV_HALO
    j = pl.program_id(1)
    n_tiles = seq // ts

    def glu(start, size):
        rows = pl.ds(pl.multiple_of(start, halo), size)
        a = a_ref[rows, :].astype(F32)
        return a * _sigmoid(b_ref[rows, :].astype(F32))

    base = j * ts
    u_ref[halo:halo + ts, :] = glu(base, ts)
    lo = glu(jnp.maximum(base - halo, 0), halo)
    u_ref[0:halo, :] = jnp.where(j > 0, lo, 0.0)
    hi = glu(jnp.minimum(base + ts, seq - halo), halo)
    u_ref[halo + ts:, :] = jnp.where(j < n_tiles - 1, hi, 0.0)

    rb = CONV_ROWS
    first = halo - CONV_HALF

    def rows_body(i, carry):
        s0 = pl.multiple_of(i * rb, rb)
        parts = []
        for lt in range(D_GROUP // LANES):
            lanes = slice(lt * LANES, (lt + 1) * LANES)
            y = None
            for r in range(SUBLANES_F32):
                acc = None
                for o in range(r, first + CONV_WIDTH, SUBLANES_F32):
                    if o < first:
                        continue
                    w = o - first
                    term = u_ref[pl.ds(s0 + (o - r), rb + SUBLANES_F32), lanes] * cw_ref[w:w + 1, lanes]
                    acc = term if acc is None else acc + term
                shifted = acc[r:r + rb]
                y = shifted if y is None else y + shifted
            parts.append(y)
        y = jnp.concatenate(parts, axis=-1) + cb_ref[...]
        mu = jnp.mean(y, axis=-1, keepdims=True)
        yc = y - mu
        var = jnp.mean(yc * yc, axis=-1, keepdims=True)
        y = (yc * lax.rsqrt(var + EPS)) * lg_ref[...] + lb_ref[...]
        h_ref[pl.ds(s0, rb), :] = _silu(y).astype(BF16)
        return carry

    lax.fori_loop(0, ts // rb, rows_body, 0)
    o = jnp.dot(h_ref[...], w_ref[...], preferred_element_type=F32)
    gate = g_ref[...].astype(F32)
    o_ref[...] = (o * _silu(gate)).astype(BF16)


def _conformer(zp, conv_w, conv_b, ln_g, ln_b, w_pw_bf16, layer):
    _, batch, seq, _ = zp.shape
    ts = CONV_TS
    full = lambda p: pl.BlockSpec((None, None, seq, D_GROUP), lambda b, j: (p, b, 0, 0))
    vec = pl.BlockSpec((1, D_GROUP), lambda b, j: (0, 0))
    return pl.pallas_call(
        functools.partial(_conv_kernel, seq=seq),
        name="conformer",
        grid=(batch, seq // ts),
        in_specs=[
            full(P_CA), full(P_CB),
            pl.BlockSpec((None, None, ts, D_GROUP), lambda b, j: (P_CG, b, j, 0)),
            pl.BlockSpec((CONV_WIDTH, D_GROUP), lambda b, j: (0, 0)),
            vec, vec, vec,
            pl.BlockSpec((None, D_GROUP, D_GROUP), lambda b, j: (layer, 0, 0)),
        ],
        out_specs=pl.BlockSpec((None, ts, D_GROUP), lambda b, j: (b, j, 0)),
        out_shape=jax.ShapeDtypeStruct((batch, seq, D_GROUP), BF16),
        scratch_shapes=[pltpu.VMEM((ts + 2 * CONV_HALO, D_GROUP), F32),
                        pltpu.VMEM((ts, D_GROUP), BF16)],
        compiler_params=_params(("arbitrary", "arbitrary")),
    )(zp, zp, zp, conv_w, conv_b.reshape(1, D_GROUP), ln_g.reshape(1, D_GROUP),
      ln_b.reshape(1, D_GROUP), w_pw_bf16)


def kernel(x, c, norm_g, w_ada, b_ada, w_in, w_fft, na_rel_bias, ret_logit_fwd, ret_logit_bwd,
           conv_w, conv_b, conv_ln_g, conv_ln_b, conv_w_pw, w_out, final_g):
    batch, seq, _ = x.shape
    rows = seq // GRID_W
    mod = _ada(c, w_ada, b_ada)
    fft_consts = _fft_constants(seq)
    rope_tbl = _rope_tables(seq)
    bias_tbl = _na_bias_table(na_rel_bias, rows)
    w_in_b, w_fft_b, w_pw_b, w_out_b = (w.astype(BF16) for w in (w_in, w_fft, conv_w_pw, w_out))
    for l in range(DEPTH):
        zp = _inproj(x, norm_g[l], mod, w_in_b, l)
        o_fft = _fourier(zp, w_fft_b, fft_consts, l)
        o_na = _neighbourhood(zp, bias_tbl, l)
        o_ret = _retention(zp, ret_logit_fwd[l], ret_logit_bwd[l], rope_tbl)
        o_cv = _conformer(zp, conv_w[l], conv_b[l], conv_ln_g[l], conv_ln_b[l], w_pw_b, l)
        x = _outproj((o_fft, o_na, o_ret, o_cv), w_out_b, x, mod, final_g, l,
                     final=(l == DEPTH - 1))
    return x
```

```python
import functools
import math

import numpy as np
import jax
import jax.numpy as jnp
from jax import lax
from jax.experimental import pallas as pl
from jax.experimental.pallas import tpu as pltpu

F32 = jnp.float32
BF16 = jnp.bfloat16

D_MODEL = 2048
DEPTH = 2
GRID_W = 64
D_GROUP = 512
HEAD_DIM = 64
N_HEADS = D_GROUP // HEAD_DIM
N_FFT_GROUPS = 4
FFT_GROUP_DIM = D_GROUP // N_FFT_GROUPS
NA_KH = 8
NA_KW = 16
CONV_WIDTH = 31
CONV_HALF = CONV_WIDTH // 2
ROPE_BASE = 10000.0
EPS = 1e-6
N_PIECES = 13
(P_FX, P_FG, P_NQ, P_NK, P_NV, P_NG, P_RQ, P_RK, P_RV, P_RG, P_CA, P_CB, P_CG) = range(N_PIECES)

MXU_DIM = 256
HEADS_PER_TILE = MXU_DIM // HEAD_DIM
VMEM_LIMIT = 56 * 1024 * 1024
SUBLANES_BF16 = 16
SUBLANES_F32 = 8
LANES = 128

NEG_BIG = -1e30


def _params(sem, vmem=VMEM_LIMIT):
    return pltpu.CompilerParams(dimension_semantics=sem, vmem_limit_bytes=vmem)


def _sigmoid(t):
    return 0.5 * jnp.tanh(0.5 * t) + 0.5


def _silu(t):
    h = 0.5 * t
    return h + h * jnp.tanh(h)


ADA_TN = 768
ADA_ROWS = 8
(MOD_SHIFT, MOD_SCALE, MOD_GATE) = range(3)


def _ada_kernel(c_ref, w_ref, b_ref, o_ref):
    ca = _silu(c_ref[...]).astype(BF16)
    w = w_ref[...].astype(BF16)
    o_ref[...] = jnp.dot(ca, w, preferred_element_type=F32) + b_ref[...]


def _ada(c, w_ada, b_ada):
    batch = c.shape[0]
    c_pad = jnp.zeros((ADA_ROWS, D_MODEL), F32).at[:batch].set(c)
    n3 = 3 * D_MODEL
    out = pl.pallas_call(
        _ada_kernel,
        name="ada",
        grid=(DEPTH, n3 // ADA_TN),
        in_specs=[
            pl.BlockSpec((ADA_ROWS, D_MODEL), lambda l, j: (0, 0)),
            pl.BlockSpec((None, D_MODEL, ADA_TN), lambda l, j: (l, 0, j)),
            pl.BlockSpec((None, 1, ADA_TN), lambda l, j: (l, 0, j)),
        ],
        out_specs=pl.BlockSpec((None, ADA_ROWS, ADA_TN), lambda l, j: (l, 0, j)),
        out_shape=jax.ShapeDtypeStruct((DEPTH, ADA_ROWS, n3), F32),
        compiler_params=_params(("arbitrary", "arbitrary")),
    )(c_pad, w_ada, b_ada.reshape(DEPTH, 1, n3))
    return out.reshape(DEPTH, ADA_ROWS, 1, n3)


INPROJ_TM = 512
INPROJ_VMEM = 60 * 1024 * 1024


def _inproj_kernel(x_ref, g_ref, sc_ref, sh_ref, w_ref, o_ref, h0_ref, h1_ref, *, n_tiles):
    t = pl.program_id(0)

    def norm(h_ref):
        x = x_ref[...]
        ms = jnp.mean(x * x, axis=-1, keepdims=True)
        h = (x * lax.rsqrt(ms + EPS)) * g_ref[...]
        h = h * (1.0 + sc_ref[...]) + sh_ref[...]
        h_ref[...] = h.astype(BF16)

    def project(h_ref):
        h = h_ref[...]
        for p in range(N_PIECES):
            w = w_ref[:, p * D_GROUP:(p + 1) * D_GROUP]
            o_ref[p] = jnp.dot(h, w, preferred_element_type=F32).astype(BF16)

    odd = t % 2 == 1

    @pl.when(t == 0)
    def _():
        norm(h0_ref)

    @pl.when(odd)
    def _():
        project(h0_ref)
        norm(h1_ref)

    @pl.when(jnp.logical_and(jnp.logical_not(odd), jnp.logical_and(t > 0, t < n_tiles)))
    def _():
        project(h1_ref)
        norm(h0_ref)

    @pl.when(t == n_tiles)
    def _():
        project(h1_ref)


def _inproj(x, g, mod, w_bf16, layer):
    batch, seq, _ = x.shape
    tm = INPROJ_TM
    per_batch = seq // tm
    n_tiles = batch * per_batch
    assert n_tiles % 2 == 0
    norm_tile = lambda t: jnp.minimum(t, n_tiles - 1)
    proj_tile = lambda t: jnp.maximum(t - 1, 0)
    return pl.pallas_call(
        functools.partial(_inproj_kernel, n_tiles=n_tiles),
        name="inproj",
        grid=(n_tiles + 1,),
        in_specs=[
            pl.BlockSpec((None, tm, D_MODEL),
                         lambda t: (norm_tile(t) // per_batch, norm_tile(t) % per_batch, 0)),
            pl.BlockSpec((1, D_MODEL), lambda t: (0, 0)),
            pl.BlockSpec((None, None, 1, D_MODEL),
                         lambda t: (layer, norm_tile(t) // per_batch, 0, MOD_SCALE)),
            pl.BlockSpec((None, None, 1, D_MODEL),
                         lambda t: (layer, norm_tile(t) // per_batch, 0, MOD_SHIFT)),
            pl.BlockSpec((None, D_MODEL, N_PIECES * D_GROUP), lambda t: (layer, 0, 0),
                         pipeline_mode=pl.Buffered(1)),
        ],
        out_specs=pl.BlockSpec((N_PIECES, None, tm, D_GROUP),
                               lambda t: (0, proj_tile(t) // per_batch, proj_tile(t) % per_batch, 0)),
        out_shape=jax.ShapeDtypeStruct((N_PIECES, batch, seq, D_GROUP), BF16),
        scratch_shapes=[pltpu.VMEM((tm, D_MODEL), BF16), pltpu.VMEM((tm, D_MODEL), BF16)],
        compiler_params=_params(("arbitrary",), INPROJ_VMEM),
    )(x, g.reshape(1, D_MODEL), mod, mod, w_bf16)


OUTPROJ_TM = 512


def _outproj_kernel(a0_ref, a1_ref, a2_ref, w_ref, x_ref, gate_ref, fg_ref,
                    ca_ref, cb_ref, cg_ref, cw_ref, cbias_ref, lg_ref, lb_ref, wpw_ref,
                    o_ref, u_ref, h_ref, cv0_ref, cv1_ref, *, final, n_tiles, per_batch, seq):
    t = pl.program_id(0)
    j = jnp.minimum(t, n_tiles - 1) % per_batch

    tm = o_ref.shape[0]
    n_chunks = tm // CONV_ROWS
    tn = D_MODEL // n_chunks

    def run(proj_cv_ref, conv_cv_ref):
        if conv_cv_ref is not None:
            _conv_glu(j, ca_ref, cb_ref, u_ref, seq=seq)
        if proj_cv_ref is not None:
            a = jnp.concatenate([a0_ref[...], a1_ref[...], a2_ref[...], proj_cv_ref[...]], axis=-1)
            ss = jnp.zeros((tm, 1), F32)
        zero = 0
        for i in range(n_chunks):
            if proj_cv_ref is not None:
                cols = pl.ds(pl.multiple_of(i * tn + zero, tn), tn)
                y = jnp.dot(a, w_ref[:, cols], preferred_element_type=F32)
                xn = x_ref[:, i * tn:(i + 1) * tn] + gate_ref[:, i * tn:(i + 1) * tn] * y
                if final:
                    ss = ss + jnp.sum(xn * xn, axis=-1, keepdims=True)
                o_ref[:, i * tn:(i + 1) * tn] = xn
            if conv_cv_ref is not None:
                tail = _conv_rows(i * CONV_ROWS, cw_ref, cbias_ref, lg_ref, lb_ref, u_ref, h_ref)
                zero = _runtime_zero(tail)
        if conv_cv_ref is not None:
            _conv_pointwise(cg_ref, wpw_ref, h_ref, conv_cv_ref)
        if proj_cv_ref is not None and final:
            o_ref[...] = (o_ref[...] * lax.rsqrt(ss * (1.0 / D_MODEL) + EPS)) * fg_ref[...]

    odd = t % 2 == 1

    @pl.when(t == 0)
    def _():
        run(None, cv0_ref)

    @pl.when(odd)
    def _():
        run(cv0_ref, cv1_ref)

    @pl.when(jnp.logical_and(jnp.logical_not(odd), jnp.logical_and(t > 0, t < n_tiles)))
    def _():
        run(cv1_ref, cv0_ref)

    @pl.when(t == n_tiles)
    def _():
        run(cv1_ref, None)


def _outproj(mixed, zp, conv_params, w_pw_bf16, w_bf16, x, mod, final_g, layer, final):
    batch, seq, _ = x.shape
    tm = OUTPROJ_TM
    per_batch = seq // tm
    n_tiles = batch * per_batch
    assert n_tiles % 2 == 0
    conv_tile = lambda t: jnp.minimum(t, n_tiles - 1)
    proj_tile = lambda t: jnp.maximum(t - 1, 0)
    a_spec = pl.BlockSpec((None, tm, D_GROUP),
                          lambda t: (proj_tile(t) // per_batch, proj_tile(t) % per_batch, 0))
    x_spec = pl.BlockSpec((None, tm, D_MODEL),
                          lambda t: (proj_tile(t) // per_batch, proj_tile(t) % per_batch, 0))
    seq_spec = lambda p: pl.BlockSpec((None, None, seq, D_GROUP),
                                      lambda t: (p, conv_tile(t) // per_batch, 0, 0))
    vec = pl.BlockSpec((1, D_GROUP), lambda t: (0, 0))
    conv_w, conv_b, ln_g, ln_b = conv_params
    return pl.pallas_call(
        functools.partial(_outproj_kernel, final=final, n_tiles=n_tiles, per_batch=per_batch, seq=seq),
        name="outproj",
        grid=(n_tiles + 1,),
        in_specs=[
            a_spec, a_spec, a_spec,
            pl.BlockSpec((None, D_MODEL, D_MODEL), lambda t: (layer, 0, 0), pipeline_mode=pl.Buffered(1)),
            x_spec,
            pl.BlockSpec((None, None, 1, D_MODEL), lambda t: (layer, proj_tile(t) // per_batch, 0, MOD_GATE)),
            pl.BlockSpec((1, D_MODEL), lambda t: (0, 0)),
            seq_spec(P_CA), seq_spec(P_CB),
            pl.BlockSpec((None, None, tm, D_GROUP),
                         lambda t: (P_CG, conv_tile(t) // per_batch, conv_tile(t) % per_batch, 0)),
            pl.BlockSpec((CONV_WIDTH, D_GROUP), lambda t: (0, 0)),
            vec, vec, vec,
            pl.BlockSpec((None, D_GROUP, D_GROUP), lambda t: (layer, 0, 0)),
        ],
        out_specs=x_spec,
        out_shape=jax.ShapeDtypeStruct((batch, seq, D_MODEL), F32),
        scratch_shapes=[
            pltpu.VMEM((tm + 2 * CONV_HALO, D_GROUP), F32),
            pltpu.VMEM((tm, D_GROUP), BF16),
            pltpu.VMEM((tm, D_GROUP), BF16), pltpu.VMEM((tm, D_GROUP), BF16),
        ],
        compiler_params=_params(("arbitrary",)),
    )(*mixed, w_bf16, x, mod, final_g.reshape(1, D_MODEL), zp, zp, zp, conv_w,
      conv_b.reshape(1, D_GROUP), ln_g.reshape(1, D_GROUP), ln_b.reshape(1, D_GROUP), w_pw_bf16)


FFT_BLOCK = 16


def _fft_constants(seq):
    rows = seq // GRID_W
    assert rows == GRID_W
    n = np.arange(GRID_W)
    ang1 = 2.0 * np.pi * ((n[:, None] * n[None, :]) % GRID_W) / GRID_W
    f1 = np.concatenate([np.cos(ang1), -np.sin(ang1)], axis=0)
    k1 = n[:, None, None]
    k2 = n[None, :, None]
    s2 = n[None, None, :]
    ang2 = 2.0 * np.pi * ((s2 * (k1 + GRID_W * k2)) % seq) / seq
    mr, mi = np.cos(ang2), -np.sin(ang2)
    m2 = np.concatenate([np.concatenate([mr, -mi], axis=2),
                         np.concatenate([mi, mr], axis=2)], axis=1)
    c = np.arange(FFT_GROUP_DIM)
    angc = 2.0 * np.pi * ((c[:, None] * c[None, :]) % FFT_GROUP_DIM) / FFT_GROUP_DIM
    fc = np.concatenate([np.cos(angc), np.sin(angc)], axis=0)
    return tuple(jnp.asarray(t, F32).astype(BF16) for t in (f1, m2, fc))


def _fft_kernel(u_ref, gate_ref, f1_ref, m2_ref, fc_ref, w_ref, o_ref, u32_ref, g_ref, y_ref, *, norm):
    grp = pl.program_id(1)
    n = GRID_W
    blk = FFT_BLOCK

    u32_ref[...] = u_ref[...].astype(F32)

    def stage1(i, carry):
        for j in range(blk):
            s2 = i * blk + j
            us = u32_ref[pl.ds(s2, n, stride=n), :].astype(BF16)
            gs = jnp.dot(f1_ref[...], us, preferred_element_type=F32)
            g_ref[pl.ds(pl.multiple_of(s2 * 2 * n, 2 * n), 2 * n), :] = gs
        return carry

    lax.fori_loop(0, n // blk, stage1, 0)

    def stage2(i, carry):
        lhs = []
        for j in range(blk):
            k1 = i * blk + j
            gr = g_ref[pl.ds(k1, n, stride=2 * n), :]
            gi = g_ref[pl.ds(n + k1, n, stride=2 * n), :]
            gk = jnp.concatenate([gr, gi], axis=0).astype(BF16)
            xk = jnp.dot(m2_ref[k1], gk, preferred_element_type=F32)
            lhs.append(jnp.concatenate([xk[:n], xk[n:]], axis=-1))
        lhs = jnp.concatenate(lhs, axis=0).astype(BF16)
        y = jnp.dot(lhs, fc_ref[...], preferred_element_type=F32) * norm
        y_ref[grp, pl.ds(pl.multiple_of(i * blk * n, blk * n), blk * n), :] = y
        return carry

    lax.fori_loop(0, n // blk, stage2, 0)

    @pl.when(grp == N_FFT_GROUPS - 1)
    def _():
        def stage3(i, carry):
            tiles = []
            for j in range(blk):
                k2 = i * blk + j
                tiles.append(jnp.concatenate(
                    [y_ref[g, pl.ds(k2, n, stride=n), :] for g in range(N_FFT_GROUPS)], axis=-1))
            y = jnp.concatenate(tiles, axis=0).astype(BF16)
            o = jnp.dot(y, w_ref[...], preferred_element_type=F32)
            rows = pl.ds(pl.multiple_of(i * blk * n, blk * n), blk * n)
            o_ref[rows, :] = (o * _silu(gate_ref[rows, :].astype(F32))).astype(BF16)
            return carry

        lax.fori_loop(0, n // blk, stage3, 0)


def _fourier(zp, w_fft_bf16, consts, layer):
    _, batch, seq, _ = zp.shape
    f1, m2, fc = consts
    n = GRID_W
    const = lambda shape: pl.BlockSpec(shape, lambda b, g: (0,) * len(shape))
    return pl.pallas_call(
        functools.partial(_fft_kernel, norm=1.0 / math.sqrt(seq * FFT_GROUP_DIM)),
        name="fourier",
        grid=(batch, N_FFT_GROUPS),
        in_specs=[
            pl.BlockSpec((None, None, seq, FFT_GROUP_DIM), lambda b, g: (P_FX, b, 0, g)),
            pl.BlockSpec((None, None, seq, D_GROUP), lambda b, g: (P_FG, b, 0, 0)),
            const((2 * n, n)), const((n, 2 * n, 2 * n)), const((2 * FFT_GROUP_DIM, FFT_GROUP_DIM)),
            pl.BlockSpec((None, D_GROUP, D_GROUP), lambda b, g: (layer, 0, 0)),
        ],
        out_specs=pl.BlockSpec((None, seq, D_GROUP), lambda b, g: (b, 0, 0)),
        out_shape=jax.ShapeDtypeStruct((batch, seq, D_GROUP), BF16),
        scratch_shapes=[
            pltpu.VMEM((seq, FFT_GROUP_DIM), F32),
            pltpu.VMEM((n * 2 * n, FFT_GROUP_DIM), F32),
            pltpu.VMEM((N_FFT_GROUPS, seq, FFT_GROUP_DIM), F32),
        ],
        compiler_params=_params(("arbitrary", "arbitrary")),
    )(zp, zp, f1, m2, fc, w_fft_bf16)


NA_ROWS_PER_STEP = 4


def _na_bias_table(rel_bias, rows):
    kh = min(NA_KH, rows)
    n_dc = 2 * NA_KW - 1
    col = np.arange(GRID_W)
    col_start = np.clip(col - NA_KW // 2, 0, GRID_W - NA_KW)
    rel_c = col[None, :] - col_start[:, None]
    col_in = (rel_c >= 0) & (rel_c < NA_KW)
    dc = np.clip(col[None, :] - col[:, None] + NA_KW - 1, 0, n_dc - 1)
    onehot = (dc[None] == np.arange(n_dc)[:, None, None]).astype(np.float32)
    by_col = jnp.einsum('lhdj,jqk->lhdqk', rel_bias.astype(F32), jnp.asarray(onehot),
                        precision=lax.Precision.HIGHEST)
    by_col = jnp.where(col_in[None, None, None], by_col, NEG_BIG)
    tbl = jnp.stack([by_col[:, :, NA_KH - 1 - o:NA_KH - 1 - o + kh] for o in range(kh)], axis=1)
    tbl = tbl.transpose(0, 1, 2, 4, 3, 5)
    return tbl.reshape(rel_bias.shape[0], kh, N_HEADS, GRID_W, kh * GRID_W)


def _head_select_mask():
    r = lax.broadcasted_iota(jnp.int32, (MXU_DIM, MXU_DIM), 0) // HEAD_DIM
    c = lax.broadcasted_iota(jnp.int32, (MXU_DIM, MXU_DIM), 1) // HEAD_DIM
    return r == c


def _stack_heads(t):
    n = t.shape[0]
    reps = jnp.concatenate([t] * HEADS_PER_TILE, axis=0)
    row_head = lax.broadcasted_iota(jnp.int32, reps.shape, 0) // n
    lane_head = lax.broadcasted_iota(jnp.int32, reps.shape, 1) // HEAD_DIM
    return jnp.where(row_head == lane_head, reps, jnp.zeros_like(reps))


def _unstack_heads(t4, n):
    lane_head = lax.broadcasted_iota(jnp.int32, (n, MXU_DIM), 1) // HEAD_DIM
    out = jnp.zeros((n, MXU_DIM), t4.dtype)
    for h in range(HEADS_PER_TILE):
        out = jnp.where(lane_head == h, t4[h * n:(h + 1) * n], out)
    return out


def _na_kernel(q_ref, k_ref, v_ref, g_ref, bias_ref, o_ref, *, rows):
    kh = min(NA_KH, rows)
    step = pl.program_id(1)

    def row_body(rr, carry):
        r = step * NA_ROWS_PER_STEP + rr
        r_start = jnp.clip(r - kh // 2, 0, rows - kh)
        variant = r - r_start
        q_off = pl.multiple_of(rr * GRID_W, GRID_W)
        k_off = pl.multiple_of(r_start * GRID_W, GRID_W)
        for cg in range(D_GROUP // MXU_DIM):
            lanes = slice(cg * MXU_DIM, (cg + 1) * MXU_DIM)
            q = q_ref[pl.ds(q_off, GRID_W), lanes] * (HEAD_DIM ** -0.5)
            kw = k_ref[pl.ds(k_off, kh * GRID_W), lanes]
            vw = v_ref[pl.ds(k_off, kh * GRID_W), lanes]
            q4 = _stack_heads(q)
            s = lax.dot_general(q4, kw, (((1,), (1,)), ((), ())), preferred_element_type=F32)
            bias = bias_ref[variant, cg * HEADS_PER_TILE:(cg + 1) * HEADS_PER_TILE]
            s = s + bias.reshape(HEADS_PER_TILE * GRID_W, kh * GRID_W)
            m = jnp.max(s, axis=-1, keepdims=True)
            e = jnp.exp(s - m)
            l = jnp.sum(e, axis=-1, keepdims=True)
            o4 = jnp.dot(e.astype(BF16), vw, preferred_element_type=F32) / l
            o = _unstack_heads(o4, GRID_W)
            gate = g_ref[pl.ds(q_off, GRID_W), lanes].astype(F32)
            o_ref[pl.ds(q_off, GRID_W), lanes] = (o * _silu(gate)).astype(BF16)
        return carry

    lax.fori_loop(0, NA_ROWS_PER_STEP, row_body, 0, unroll=True)


def _neighbourhood(zp, bias_tbl, layer):
    _, batch, seq, _ = zp.shape
    rows = seq // GRID_W
    tq = NA_ROWS_PER_STEP * GRID_W
    q_spec = lambda p: pl.BlockSpec((None, None, tq, D_GROUP), lambda b, i: (p, b, i, 0))
    kv_spec = lambda p: pl.BlockSpec((None, None, seq, D_GROUP), lambda b, i: (p, b, 0, 0))
    return pl.pallas_call(
        functools.partial(_na_kernel, rows=rows),
        name="natten",
        grid=(batch, rows // NA_ROWS_PER_STEP),
        in_specs=[
            q_spec(P_NQ), kv_spec(P_NK), kv_spec(P_NV), q_spec(P_NG),
            pl.BlockSpec((None,) + bias_tbl.shape[1:], lambda b, i: (layer, 0, 0, 0, 0),
                         pipeline_mode=pl.Buffered(1)),
        ],
        out_specs=pl.BlockSpec((None, tq, D_GROUP), lambda b, i: (b, i, 0)),
        out_shape=jax.ShapeDtypeStruct((batch, seq, D_GROUP), BF16),
        compiler_params=_params(("arbitrary", "arbitrary")),
    )(zp, zp, zp, zp, bias_tbl)


RET_CHUNK = MXU_DIM


def _log_sigmoid(t):
    return jnp.minimum(t, 0.0) - jnp.log1p(jnp.exp(-jnp.abs(t)))


def _rope_tables(seq):
    half = HEAD_DIM // 2
    inv = ROPE_BASE ** (-jnp.arange(half, dtype=F32) / half)
    ang = jnp.arange(seq, dtype=F32)[:, None] * inv[None, :]
    cos, sin = jnp.cos(ang), jnp.sin(ang)
    cos2 = jnp.concatenate([cos, cos], axis=-1)
    sin2 = jnp.concatenate([-sin, sin], axis=-1)
    return jnp.tile(cos2, (1, 2)), jnp.tile(sin2, (1, 2))


def _ret_kernel(lf_s_ref, lb_s_ref, q_ref, k_ref, v_ref, g_ref, cos_ref, sin_ref, lfl_ref, lbl_ref,
                o_ref, qr_ref, kr_ref, ob_ref, dmat_ref, dec_ref, state_ref, ones_ref, *, seq):
    C = RET_CHUNK
    n_chunks = seq // C
    cg = pl.program_id(1)
    ones_bd = jnp.where(_head_select_mask(), 1.0, 0.0).astype(BF16)
    ones_ref[...] = jnp.concatenate([ones_bd, ones_bd], axis=0)

    ri = lax.broadcasted_iota(jnp.int32, (C, C), 0)
    ci = lax.broadcasted_iota(jnp.int32, (C, C), 1)
    diff = (ri - ci).astype(F32)
    for hh in range(HEADS_PER_TILE):
        h = cg * HEADS_PER_TILE + hh
        lf = _log_sigmoid(jnp.full((C, C), lf_s_ref[h], F32))
        lb = _log_sigmoid(jnp.full((C, C), lb_s_ref[h], F32))
        dmat_ref[hh] = jnp.where(diff >= 0, jnp.exp(lf * diff), jnp.exp(lb * (-diff)))
    lfl = _log_sigmoid(lfl_ref[...])
    lbl = _log_sigmoid(lbl_ref[...])
    idx = lax.broadcasted_iota(jnp.int32, (C, MXU_DIM), 0).astype(F32)
    dec_ref[0] = jnp.exp(lfl * (idx + 1.0))
    dec_ref[1] = jnp.exp(lfl * (C - 1.0 - idx))
    dec_ref[2] = jnp.exp(lbl * (C - idx))
    dec_ref[3] = jnp.exp(lbl * idx)
    cd_f = jnp.exp(lfl * float(C))
    cd_b = jnp.exp(lbl * float(C))

    def rope(t, rows):
        lane = lax.broadcasted_iota(jnp.int32, t.shape, 1)
        first_half = (lane % HEAD_DIM) < (HEAD_DIM // 2)
        cs = cos_ref[rows, :]
        sn = sin_ref[rows, :]
        cs = jnp.concatenate([cs, cs], axis=-1)
        sn = jnp.concatenate([sn, sn], axis=-1)
        swapped = jnp.where(first_half, pltpu.roll(t, MXU_DIM - HEAD_DIM // 2, 1),
                            pltpu.roll(t, HEAD_DIM // 2, 1))
        return t * cs + swapped * sn

    def kv_update(k, v, kd, cd):
        kv = lax.dot_general((k * kd).astype(BF16), v, (((0,), (0,)), ((), ())),
                             preferred_element_type=F32)
        state_ref[...] = cd * state_ref[...] + jnp.where(_head_select_mask(), kv, 0.0)

    state_ref[...] = jnp.zeros_like(state_ref)

    def bwd_body(i, carry):
        n = n_chunks - 1 - i
        rows = pl.ds(pl.multiple_of(n * C, C), C)
        q = rope(q_ref[rows, :].astype(F32), rows) * (HEAD_DIM ** -0.5)
        k = rope(k_ref[rows, :].astype(F32), rows)
        qr_ref[rows, :] = q.astype(BF16)
        kr_ref[rows, :] = k.astype(BF16)
        ob_ref[rows, :] = jnp.dot((q * dec_ref[2]).astype(BF16), state_ref[...].astype(BF16),
                                  preferred_element_type=F32)
        kv_update(k, v_ref[rows, :], dec_ref[3], cd_b)
        return carry

    lax.fori_loop(0, n_chunks, bwd_body, 0, unroll=2)

    state_ref[...] = jnp.zeros_like(state_ref)

    def fwd_body(n, carry):
        rows = pl.ds(pl.multiple_of(n * C, C), C)
        qb = qr_ref[rows, :]
        kb = kr_ref[rows, :]
        v = v_ref[rows, :]
        o_cross = jnp.dot((qb.astype(F32) * dec_ref[0]).astype(BF16), state_ref[...].astype(BF16),
                          preferred_element_type=F32)
        q4 = _stack_heads(qb)
        sc = lax.dot_general(q4, kb, (((1,), (1,)), ((), ())), preferred_element_type=F32)
        sc = sc * dmat_ref[...].reshape(HEADS_PER_TILE * C, C)
        o4 = jnp.dot(sc.astype(BF16), v, preferred_element_type=F32)
        o = _unstack_heads(o4, C) + o_cross + ob_ref[rows, :]
        sq = o * o
        hi = sq.astype(BF16)
        lo = (sq - hi.astype(F32)).astype(BF16)
        ss = jnp.dot(jnp.concatenate([hi, lo], axis=-1), ones_ref[...], preferred_element_type=F32)
        inv = lax.rsqrt(ss * (1.0 / HEAD_DIM) + EPS)
        gate = g_ref[rows, :].astype(F32)
        o_ref[rows, :] = (o * inv * _silu(gate)).astype(BF16)
        kv_update(kb.astype(F32), v, dec_ref[1], cd_f)
        return carry

    lax.fori_loop(0, n_chunks, fwd_body, 0, unroll=2)


def _retention(zp, logit_f, logit_b, rope_tbl):
    _, batch, seq, _ = zp.shape
    cos_t, sin_t = rope_tbl
    n_cg = D_GROUP // MXU_DIM
    spec = lambda p: pl.BlockSpec((None, None, seq, MXU_DIM), lambda b, c, *_: (p, b, 0, c))
    tbl_spec = pl.BlockSpec((seq, 2 * HEAD_DIM), lambda b, c, *_: (0, 0))
    lane_spec = pl.BlockSpec((1, MXU_DIM), lambda b, c, *_: (0, c))
    lane_f = jnp.repeat(logit_f.astype(F32), HEAD_DIM)[None, :]
    lane_b = jnp.repeat(logit_b.astype(F32), HEAD_DIM)[None, :]
    return pl.pallas_call(
        functools.partial(_ret_kernel, seq=seq),
        name="retention",
        grid_spec=pltpu.PrefetchScalarGridSpec(
            num_scalar_prefetch=2,
            grid=(batch, n_cg),
            in_specs=[spec(P_RQ), spec(P_RK), spec(P_RV), spec(P_RG), tbl_spec, tbl_spec,
                      lane_spec, lane_spec],
            out_specs=pl.BlockSpec((None, seq, MXU_DIM), lambda b, c, *_: (b, 0, c)),
            scratch_shapes=[
                pltpu.VMEM((seq, MXU_DIM), BF16),
                pltpu.VMEM((seq, MXU_DIM), BF16),
                pltpu.VMEM((seq, MXU_DIM), F32),
                pltpu.VMEM((HEADS_PER_TILE, RET_CHUNK, RET_CHUNK), F32),
                pltpu.VMEM((4, RET_CHUNK, MXU_DIM), F32),
                pltpu.VMEM((MXU_DIM, MXU_DIM), F32),
                pltpu.VMEM((2 * MXU_DIM, MXU_DIM), BF16),
            ],
        ),
        out_shape=jax.ShapeDtypeStruct((batch, seq, D_GROUP), BF16),
        compiler_params=_params(("arbitrary", "arbitrary")),
    )(logit_f.astype(F32), logit_b.astype(F32), zp, zp, zp, zp, cos_t, sin_t, lane_f, lane_b)


CONV_HALO = SUBLANES_BF16
CONV_ROWS = 64


def _conv_glu(j, a_ref, b_ref, u_ref, *, seq):
    halo = CONV_HALO
    ts = u_ref.shape[0] - 2 * halo
    n_tiles = seq // ts

    def glu(start, size):
        rows = pl.ds(pl.multiple_of(start, halo), size)
        a = a_ref[rows, :].astype(F32)
        return a * _sigmoid(b_ref[rows, :].astype(F32))

    base = j * ts
    u_ref[halo:halo + ts, :] = glu(base, ts)
    lo = glu(jnp.maximum(base - halo, 0), halo)
    u_ref[0:halo, :] = jnp.where(j > 0, lo, 0.0)
    hi = glu(jnp.minimum(base + ts, seq - halo), halo)
    u_ref[halo + ts:, :] = jnp.where(j < n_tiles - 1, hi, 0.0)


def _conv_rows(s0, cw_ref, cb_ref, lg_ref, lb_ref, u_ref, h_ref):
    rb = CONV_ROWS
    first = CONV_HALO - CONV_HALF
    parts = []
    for lt in range(D_GROUP // LANES):
        lanes = slice(lt * LANES, (lt + 1) * LANES)
        y = None
        for r in range(SUBLANES_F32):
            acc = None
            for o in range(r, first + CONV_WIDTH, SUBLANES_F32):
                if o < first:
                    continue
                w = o - first
                lo_row = s0 + (o - r)
                term = u_ref[lo_row:lo_row + rb + SUBLANES_F32, lanes] * cw_ref[w:w + 1, lanes]
                acc = term if acc is None else acc + term
            shifted = acc[r:r + rb]
            y = shifted if y is None else y + shifted
        parts.append(y)
    y = jnp.concatenate(parts, axis=-1) + cb_ref[...]
    mu = jnp.mean(y, axis=-1, keepdims=True)
    yc = y - mu
    var = jnp.mean(yc * yc, axis=-1, keepdims=True)
    y = (yc * lax.rsqrt(var + EPS)) * lg_ref[...] + lb_ref[...]
    h_ref[s0:s0 + rb, :] = _silu(y).astype(BF16)
    return y[rb - SUBLANES_F32:, D_GROUP - LANES:]


def _runtime_zero(tile):
    bits = pltpu.bitcast(tile, jnp.uint32)
    bits = lax.shift_right_logical(lax.shift_right_logical(bits, jnp.uint32(16)), jnp.uint32(16))
    return bits[0, 0].astype(jnp.int32)


def _conv_pointwise(g_ref, w_ref, h_ref, o_ref):
    o = jnp.dot(h_ref[...], w_ref[...], preferred_element_type=F32)
    o_ref[...] = (o * _silu(g_ref[...].astype(F32))).astype(BF16)


def kernel(x, c, norm_g, w_ada, b_ada, w_in, w_fft, na_rel_bias, ret_logit_fwd, ret_logit_bwd,
           conv_w, conv_b, conv_ln_g, conv_ln_b, conv_w_pw, w_out, final_g):
    batch, seq, _ = x.shape
    rows = seq // GRID_W
    mod = _ada(c, w_ada, b_ada)
    fft_consts = _fft_constants(seq)
    rope_tbl = _rope_tables(seq)
    bias_tbl = _na_bias_table(na_rel_bias, rows)
    w_in_b, w_fft_b, w_pw_b, w_out_b = (w.astype(BF16) for w in (w_in, w_fft, conv_w_pw, w_out))
    for l in range(DEPTH):
        zp = _inproj(x, norm_g[l], mod, w_in_b, l)
        o_fft = _fourier(zp, w_fft_b, fft_consts, l)
        o_na = _neighbourhood(zp, bias_tbl, l)
        o_ret = _retention(zp, ret_logit_fwd[l], ret_logit_bwd[l], rope_tbl)
        conv_params = (conv_w[l], conv_b[l], conv_ln_g[l], conv_ln_b[l])
        x = _outproj((o_fft, o_na, o_ret), zp, conv_params, w_pw_b, w_out_b, x, mod, final_g, l,
                     final=(l == DEPTH - 1))
    return x
```

```python
import functools
import math

import numpy as np
import jax
import jax.numpy as jnp
from jax import lax
from jax.experimental import pallas as pl
from jax.experimental.pallas import tpu as pltpu

F32 = jnp.float32
BF16 = jnp.bfloat16

D_MODEL = 2048
DEPTH = 2
GRID_W = 64
D_GROUP = 512
HEAD_DIM = 64
N_HEADS = D_GROUP // HEAD_DIM
N_FFT_GROUPS = 4
FFT_GROUP_DIM = D_GROUP // N_FFT_GROUPS
NA_KH = 8
NA_KW = 16
CONV_WIDTH = 31
CONV_HALF = CONV_WIDTH // 2
ROPE_BASE = 10000.0
EPS = 1e-6
N_PIECES = 13
(P_FX, P_FG, P_NQ, P_NK, P_NV, P_NG, P_RQ, P_RK, P_RV, P_RG, P_CA, P_CB, P_CG) = range(N_PIECES)

MXU_DIM = 256
HEADS_PER_TILE = MXU_DIM // HEAD_DIM
VMEM_LIMIT = 56 * 1024 * 1024
SUBLANES_BF16 = 16
SUBLANES_F32 = 8
LANES = 128

NEG_BIG = -1e30


def _params(sem, vmem=VMEM_LIMIT):
    return pltpu.CompilerParams(dimension_semantics=sem, vmem_limit_bytes=vmem)


def _sigmoid(t):
    return 0.5 * jnp.tanh(0.5 * t) + 0.5


def _silu(t):
    h = 0.5 * t
    return h + h * jnp.tanh(h)


ADA_TN = 768
ADA_ROWS = 8
(MOD_SHIFT, MOD_SCALE, MOD_GATE) = range(3)


def _ada_kernel(c_ref, w_ref, b_ref, o_ref):
    ca = _silu(c_ref[...]).astype(BF16)
    w = w_ref[...].astype(BF16)
    o_ref[...] = jnp.dot(ca, w, preferred_element_type=F32) + b_ref[...]


def _ada(c, w_ada, b_ada):
    batch = c.shape[0]
    c_pad = jnp.zeros((ADA_ROWS, D_MODEL), F32).at[:batch].set(c)
    n3 = 3 * D_MODEL
    out = pl.pallas_call(
        _ada_kernel,
        name="ada",
        grid=(DEPTH, n3 // ADA_TN),
        in_specs=[
            pl.BlockSpec((ADA_ROWS, D_MODEL), lambda l, j: (0, 0)),
            pl.BlockSpec((None, D_MODEL, ADA_TN), lambda l, j: (l, 0, j)),
            pl.BlockSpec((None, 1, ADA_TN), lambda l, j: (l, 0, j)),
        ],
        out_specs=pl.BlockSpec((None, ADA_ROWS, ADA_TN), lambda l, j: (l, 0, j)),
        out_shape=jax.ShapeDtypeStruct((DEPTH, ADA_ROWS, n3), F32),
        compiler_params=_params(("arbitrary", "arbitrary")),
    )(c_pad, w_ada, b_ada.reshape(DEPTH, 1, n3))
    return out.reshape(DEPTH, ADA_ROWS, 1, n3)


INPROJ_TM = 512
INPROJ_VMEM = 60 * 1024 * 1024


def _inproj_kernel(x_ref, g_ref, sc_ref, sh_ref, w_ref, o_ref, h0_ref, h1_ref, *, n_tiles):
    t = pl.program_id(0)

    def norm(h_ref):
        x = x_ref[...]
        ms = jnp.mean(x * x, axis=-1, keepdims=True)
        h = (x * lax.rsqrt(ms + EPS)) * g_ref[...]
        h = h * (1.0 + sc_ref[...]) + sh_ref[...]
        h_ref[...] = h.astype(BF16)

    def project(h_ref):
        h = h_ref[...]
        for p in range(N_PIECES):
            w = w_ref[:, p * D_GROUP:(p + 1) * D_GROUP]
            o_ref[p] = jnp.dot(h, w, preferred_element_type=F32).astype(BF16)

    odd = t % 2 == 1

    @pl.when(t == 0)
    def _():
        norm(h0_ref)

    @pl.when(odd)
    def _():
        project(h0_ref)
        norm(h1_ref)

    @pl.when(jnp.logical_and(jnp.logical_not(odd), jnp.logical_and(t > 0, t < n_tiles)))
    def _():
        project(h1_ref)
        norm(h0_ref)

    @pl.when(t == n_tiles)
    def _():
        project(h1_ref)


def _inproj(x, norm_g, mod, w_bf16, layer):
    batch, seq, _ = x.shape
    tm = INPROJ_TM
    per_batch = seq // tm
    n_tiles = batch * per_batch
    assert n_tiles % 2 == 0
    norm_tile = lambda t: jnp.minimum(t, n_tiles - 1)
    proj_tile = lambda t: jnp.maximum(t - 1, 0)
    return pl.pallas_call(
        functools.partial(_inproj_kernel, n_tiles=n_tiles),
        name="inproj",
        grid=(n_tiles + 1,),
        in_specs=[
            pl.BlockSpec((None, tm, D_MODEL),
                         lambda t: (norm_tile(t) // per_batch, norm_tile(t) % per_batch, 0)),
            pl.BlockSpec((None, 1, D_MODEL), lambda t: (layer, 0, 0)),
            pl.BlockSpec((None, None, 1, D_MODEL),
                         lambda t: (layer, norm_tile(t) // per_batch, 0, MOD_SCALE)),
            pl.BlockSpec((None, None, 1, D_MODEL),
                         lambda t: (layer, norm_tile(t) // per_batch, 0, MOD_SHIFT)),
            pl.BlockSpec((None, D_MODEL, N_PIECES * D_GROUP), lambda t: (layer, 0, 0),
                         pipeline_mode=pl.Buffered(1)),
        ],
        out_specs=pl.BlockSpec((N_PIECES, None, tm, D_GROUP),
                               lambda t: (0, proj_tile(t) // per_batch, proj_tile(t) % per_batch, 0)),
        out_shape=jax.ShapeDtypeStruct((N_PIECES, batch, seq, D_GROUP), BF16),
        scratch_shapes=[pltpu.VMEM((tm, D_MODEL), BF16), pltpu.VMEM((tm, D_MODEL), BF16)],
        compiler_params=_params(("arbitrary",), INPROJ_VMEM),
    )(x, norm_g.reshape(DEPTH, 1, D_MODEL), mod, mod, w_bf16)


OUTPROJ_TM = 512


def _outproj_kernel(a0_ref, a1_ref, a2_ref, w_ref, x_ref, gate_ref, fg_ref,
                    ca_ref, cb_ref, cg_ref, cw_ref, cbias_ref, lg_ref, lb_ref, wpw_ref,
                    o_ref, u_ref, h_ref, cv0_ref, cv1_ref, *, final, n_tiles, per_batch, seq):
    t = pl.program_id(0)
    j = jnp.minimum(t, n_tiles - 1) % per_batch

    tm = o_ref.shape[0]
    n_chunks = tm // CONV_ROWS
    tn = D_MODEL // n_chunks

    def run(proj_cv_ref, conv_cv_ref):
        if conv_cv_ref is not None:
            _conv_glu(j, ca_ref, cb_ref, u_ref, seq=seq)
        if proj_cv_ref is not None:
            a = jnp.concatenate([a0_ref[...], a1_ref[...], a2_ref[...], proj_cv_ref[...]], axis=-1)
            ss = jnp.zeros((tm, 1), F32)
        zero = 0
        for i in range(n_chunks):
            if proj_cv_ref is not None:
                cols = pl.ds(pl.multiple_of(i * tn + zero, tn), tn)
                y = jnp.dot(a, w_ref[:, cols], preferred_element_type=F32)
                xn = x_ref[:, i * tn:(i + 1) * tn] + gate_ref[:, i * tn:(i + 1) * tn] * y
                if final:
                    ss = ss + jnp.sum(xn * xn, axis=-1, keepdims=True)
                o_ref[:, i * tn:(i + 1) * tn] = xn
            if conv_cv_ref is not None:
                tail = _conv_rows(i * CONV_ROWS, cw_ref, cbias_ref, lg_ref, lb_ref, u_ref, h_ref)
                zero = _runtime_zero(tail)
        if conv_cv_ref is not None:
            _conv_pointwise(cg_ref, wpw_ref, h_ref, conv_cv_ref)
        if proj_cv_ref is not None and final:
            o_ref[...] = (o_ref[...] * lax.rsqrt(ss * (1.0 / D_MODEL) + EPS)) * fg_ref[...]

    odd = t % 2 == 1

    @pl.when(t == 0)
    def _():
        run(None, cv0_ref)

    @pl.when(odd)
    def _():
        run(cv0_ref, cv1_ref)

    @pl.when(jnp.logical_and(jnp.logical_not(odd), jnp.logical_and(t > 0, t < n_tiles)))
    def _():
        run(cv1_ref, cv0_ref)

    @pl.when(t == n_tiles)
    def _():
        run(cv1_ref, None)


def _outproj(mixed, zp, conv_params, w_pw_bf16, w_bf16, x, mod, final_g, layer, final):
    batch, seq, _ = x.shape
    tm = OUTPROJ_TM
    per_batch = seq // tm
    n_tiles = batch * per_batch
    assert n_tiles % 2 == 0
    conv_tile = lambda t: jnp.minimum(t, n_tiles - 1)
    proj_tile = lambda t: jnp.maximum(t - 1, 0)
    a_spec = pl.BlockSpec((None, tm, D_GROUP),
                          lambda t: (proj_tile(t) // per_batch, proj_tile(t) % per_batch, 0))
    x_spec = pl.BlockSpec((None, tm, D_MODEL),
                          lambda t: (proj_tile(t) // per_batch, proj_tile(t) % per_batch, 0))
    seq_spec = lambda p: pl.BlockSpec((None, None, seq, D_GROUP),
                                      lambda t: (p, conv_tile(t) // per_batch, 0, 0))
    vec = pl.BlockSpec((None, 1, D_GROUP), lambda t: (layer, 0, 0))
    conv_w, conv_b, ln_g, ln_b = conv_params
    as_rows = lambda v: v.reshape(DEPTH, 1, D_GROUP)
    return pl.pallas_call(
        functools.partial(_outproj_kernel, final=final, n_tiles=n_tiles, per_batch=per_batch, seq=seq),
        name="outproj",
        grid=(n_tiles + 1,),
        in_specs=[
            a_spec, a_spec, a_spec,
            pl.BlockSpec((None, D_MODEL, D_MODEL), lambda t: (layer, 0, 0), pipeline_mode=pl.Buffered(1)),
            x_spec,
            pl.BlockSpec((None, None, 1, D_MODEL), lambda t: (layer, proj_tile(t) // per_batch, 0, MOD_GATE)),
            pl.BlockSpec((1, D_MODEL), lambda t: (0, 0)),
            seq_spec(P_CA), seq_spec(P_CB),
            pl.BlockSpec((None, None, tm, D_GROUP),
                         lambda t: (P_CG, conv_tile(t) // per_batch, conv_tile(t) % per_batch, 0)),
            pl.BlockSpec((None, CONV_WIDTH, D_GROUP), lambda t: (layer, 0, 0)),
            vec, vec, vec,
            pl.BlockSpec((None, D_GROUP, D_GROUP), lambda t: (layer, 0, 0)),
        ],
        out_specs=x_spec,
        out_shape=jax.ShapeDtypeStruct((batch, seq, D_MODEL), F32),
        scratch_shapes=[
            pltpu.VMEM((tm + 2 * CONV_HALO, D_GROUP), F32),
            pltpu.VMEM((tm, D_GROUP), BF16),
            pltpu.VMEM((tm, D_GROUP), BF16), pltpu.VMEM((tm, D_GROUP), BF16),
        ],
        compiler_params=_params(("arbitrary",)),
    )(*mixed, w_bf16, x, mod, final_g.reshape(1, D_MODEL), zp, zp, zp, conv_w,
      as_rows(conv_b), as_rows(ln_g), as_rows(ln_b), w_pw_bf16)


FFT_BLOCK = 16


def _fft_constants(seq):
    rows = seq // GRID_W
    assert rows == GRID_W
    n = np.arange(GRID_W)
    ang1 = 2.0 * np.pi * ((n[:, None] * n[None, :]) % GRID_W) / GRID_W
    f1 = np.concatenate([np.cos(ang1), -np.sin(ang1)], axis=0)
    k1 = n[:, None, None]
    k2 = n[None, :, None]
    s2 = n[None, None, :]
    ang2 = 2.0 * np.pi * ((s2 * (k1 + GRID_W * k2)) % seq) / seq
    mr, mi = np.cos(ang2), -np.sin(ang2)
    m2 = np.concatenate([np.concatenate([mr, -mi], axis=2),
                         np.concatenate([mi, mr], axis=2)], axis=1)
    c = np.arange(FFT_GROUP_DIM)
    angc = 2.0 * np.pi * ((c[:, None] * c[None, :]) % FFT_GROUP_DIM) / FFT_GROUP_DIM
    fc = np.concatenate([np.cos(angc), np.sin(angc)], axis=0)
    return tuple(jnp.asarray(t, F32).astype(BF16) for t in (f1, m2, fc))


def _fft_kernel(u_ref, gate_ref, f1_ref, m2_ref, fc_ref, w_ref, o_ref, u32_ref, g_ref, y_ref, *, norm):
    grp = pl.program_id(1)
    n = GRID_W
    blk = FFT_BLOCK

    u32_ref[...] = u_ref[...].astype(F32)

    def stage1(i, carry):
        for j in range(blk):
            s2 = i * blk + j
            us = u32_ref[pl.ds(s2, n, stride=n), :].astype(BF16)
            gs = jnp.dot(f1_ref[...], us, preferred_element_type=F32)
            g_ref[pl.ds(pl.multiple_of(s2 * 2 * n, 2 * n), 2 * n), :] = gs
        return carry

    lax.fori_loop(0, n // blk, stage1, 0)

    def stage2(i, carry):
        lhs = []
        for j in range(blk):
            k1 = i * blk + j
            gr = g_ref[pl.ds(k1, n, stride=2 * n), :]
            gi = g_ref[pl.ds(n + k1, n, stride=2 * n), :]
            gk = jnp.concatenate([gr, gi], axis=0).astype(BF16)
            xk = jnp.dot(m2_ref[k1], gk, preferred_element_type=F32)
            lhs.append(jnp.concatenate([xk[:n], xk[n:]], axis=-1))
        lhs = jnp.concatenate(lhs, axis=0).astype(BF16)
        y = jnp.dot(lhs, fc_ref[...], preferred_element_type=F32) * norm
        y_ref[grp, pl.ds(pl.multiple_of(i * blk * n, blk * n), blk * n), :] = y
        return carry

    lax.fori_loop(0, n // blk, stage2, 0)

    @pl.when(grp == N_FFT_GROUPS - 1)
    def _():
        def stage3(i, carry):
            tiles = []
            for j in range(blk):
                k2 = i * blk + j
                tiles.append(jnp.concatenate(
                    [y_ref[g, pl.ds(k2, n, stride=n), :] for g in range(N_FFT_GROUPS)], axis=-1))
            y = jnp.concatenate(tiles, axis=0).astype(BF16)
            o = jnp.dot(y, w_ref[...], preferred_element_type=F32)
            rows = pl.ds(pl.multiple_of(i * blk * n, blk * n), blk * n)
            o_ref[rows, :] = (o * _silu(gate_ref[rows, :].astype(F32))).astype(BF16)
            return carry

        lax.fori_loop(0, n // blk, stage3, 0)


def _fourier(zp, w_fft_bf16, consts, layer):
    _, batch, seq, _ = zp.shape
    f1, m2, fc = consts
    n = GRID_W
    const = lambda shape: pl.BlockSpec(shape, lambda b, g: (0,) * len(shape))
    return pl.pallas_call(
        functools.partial(_fft_kernel, norm=1.0 / math.sqrt(seq * FFT_GROUP_DIM)),
        name="fourier",
        grid=(batch, N_FFT_GROUPS),
        in_specs=[
            pl.BlockSpec((None, None, seq, FFT_GROUP_DIM), lambda b, g: (P_FX, b, 0, g)),
            pl.BlockSpec((None, None, seq, D_GROUP), lambda b, g: (P_FG, b, 0, 0)),
            const((2 * n, n)), const((n, 2 * n, 2 * n)), const((2 * FFT_GROUP_DIM, FFT_GROUP_DIM)),
            pl.BlockSpec((None, D_GROUP, D_GROUP), lambda b, g: (layer, 0, 0)),
        ],
        out_specs=pl.BlockSpec((None, seq, D_GROUP), lambda b, g: (b, 0, 0)),
        out_shape=jax.ShapeDtypeStruct((batch, seq, D_GROUP), BF16),
        scratch_shapes=[
            pltpu.VMEM((seq, FFT_GROUP_DIM), F32),
            pltpu.VMEM((n * 2 * n, FFT_GROUP_DIM), F32),
            pltpu.VMEM((N_FFT_GROUPS, seq, FFT_GROUP_DIM), F32),
        ],
        compiler_params=_params(("arbitrary", "arbitrary")),
    )(zp, zp, f1, m2, fc, w_fft_bf16)


NA_ROWS_PER_STEP = 4


def _na_bias_table(rel_bias):
    n_dc = 2 * NA_KW - 1
    col = np.arange(GRID_W)
    col_start = np.clip(col - NA_KW // 2, 0, GRID_W - NA_KW)
    rel_c = col[None, :] - col_start[:, None]
    col_in = (rel_c >= 0) & (rel_c < NA_KW)
    dc = np.clip(col[None, :] - col[:, None] + NA_KW - 1, 0, n_dc - 1)
    onehot = (dc[None] == np.arange(n_dc)[:, None, None]).astype(np.float32)
    by_col = jnp.einsum('lhdj,jqk->ldhqk', rel_bias.astype(F32), jnp.asarray(onehot),
                        precision=lax.Precision.HIGHEST)
    by_col = jnp.where(col_in[None, None, None], by_col, NEG_BIG)
    return jnp.concatenate([by_col[:, :-1], by_col[:, 1:]], axis=-1)


def _head_select_mask():
    r = lax.broadcasted_iota(jnp.int32, (MXU_DIM, MXU_DIM), 0) // HEAD_DIM
    c = lax.broadcasted_iota(jnp.int32, (MXU_DIM, MXU_DIM), 1) // HEAD_DIM
    return r == c


def _stack_heads(t):
    n = t.shape[0]
    reps = jnp.concatenate([t] * HEADS_PER_TILE, axis=0)
    row_head = lax.broadcasted_iota(jnp.int32, reps.shape, 0) // n
    lane_head = lax.broadcasted_iota(jnp.int32, reps.shape, 1) // HEAD_DIM
    return jnp.where(row_head == lane_head, reps, jnp.zeros_like(reps))


def _unstack_heads(t4, n):
    lane_head = lax.broadcasted_iota(jnp.int32, (n, MXU_DIM), 1) // HEAD_DIM
    out = jnp.zeros((n, MXU_DIM), t4.dtype)
    for h in range(HEADS_PER_TILE):
        out = jnp.where(lane_head == h, t4[h * n:(h + 1) * n], out)
    return out


def _na_kernel(q_ref, k_ref, v_ref, g_ref, bias_ref, o_ref, *, rows):
    kh = min(NA_KH, rows)
    step = pl.program_id(1)

    def row_body(rr, carry):
        r = step * NA_ROWS_PER_STEP + rr
        r_start = jnp.clip(r - kh // 2, 0, rows - kh)
        variant = r - r_start
        q_off = pl.multiple_of(rr * GRID_W, GRID_W)
        k_off = pl.multiple_of(r_start * GRID_W, GRID_W)
        for cg in range(D_GROUP // MXU_DIM):
            lanes = slice(cg * MXU_DIM, (cg + 1) * MXU_DIM)
            q = q_ref[pl.ds(q_off, GRID_W), lanes] * (HEAD_DIM ** -0.5)
            kw = k_ref[pl.ds(k_off, kh * GRID_W), lanes]
            vw = v_ref[pl.ds(k_off, kh * GRID_W), lanes]
            q4 = _stack_heads(q)
            s = lax.dot_general(q4, kw, (((1,), (1,)), ((), ())), preferred_element_type=F32)
            heads = slice(cg * HEADS_PER_TILE, (cg + 1) * HEADS_PER_TILE)
            bias = jnp.concatenate(
                [bias_ref[NA_KH - 1 - variant + a, heads].reshape(HEADS_PER_TILE * GRID_W, 2 * GRID_W)
                 for a in range(0, kh, 2)], axis=-1)
            s = s + bias
            m = jnp.max(s, axis=-1, keepdims=True)
            e = jnp.exp(s - m)
            l = jnp.sum(e, axis=-1, keepdims=True)
            o4 = jnp.dot(e.astype(BF16), vw, preferred_element_type=F32) / l
            o = _unstack_heads(o4, GRID_W)
            gate = g_ref[pl.ds(q_off, GRID_W), lanes].astype(F32)
            o_ref[pl.ds(q_off, GRID_W), lanes] = (o * _silu(gate)).astype(BF16)
        return carry

    lax.fori_loop(0, NA_ROWS_PER_STEP, row_body, 0, unroll=True)


def _neighbourhood(zp, bias_tbl, layer):
    _, batch, seq, _ = zp.shape
    rows = seq // GRID_W
    tq = NA_ROWS_PER_STEP * GRID_W
    q_spec = lambda p: pl.BlockSpec((None, None, tq, D_GROUP), lambda b, i: (p, b, i, 0))
    kv_spec = lambda p: pl.BlockSpec((None, None, seq, D_GROUP), lambda b, i: (p, b, 0, 0))
    return pl.pallas_call(
        functools.partial(_na_kernel, rows=rows),
        name="natten",
        grid=(batch, rows // NA_ROWS_PER_STEP),
        in_specs=[
            q_spec(P_NQ), kv_spec(P_NK), kv_spec(P_NV), q_spec(P_NG),
            pl.BlockSpec((None,) + bias_tbl.shape[1:], lambda b, i: (layer, 0, 0, 0, 0),
                         pipeline_mode=pl.Buffered(1)),
        ],
        out_specs=pl.BlockSpec((None, tq, D_GROUP), lambda b, i: (b, i, 0)),
        out_shape=jax.ShapeDtypeStruct((batch, seq, D_GROUP), BF16),
        compiler_params=_params(("arbitrary", "arbitrary")),
    )(zp, zp, zp, zp, bias_tbl)


RET_CHUNK = MXU_DIM


def _log_sigmoid(t):
    return jnp.minimum(t, 0.0) - jnp.log1p(jnp.exp(-jnp.abs(t)))


def _rope_tables(seq):
    half = HEAD_DIM // 2
    inv = ROPE_BASE ** (-jnp.arange(half, dtype=F32) / half)
    ang = jnp.arange(seq, dtype=F32)[:, None] * inv[None, :]
    cos, sin = jnp.cos(ang), jnp.sin(ang)
    cos2 = jnp.concatenate([cos, cos], axis=-1)
    sin2 = jnp.concatenate([-sin, sin], axis=-1)
    return jnp.tile(cos2, (1, 2)), jnp.tile(sin2, (1, 2))


def _ret_kernel(lf_s_ref, lb_s_ref, q_ref, k_ref, v_ref, g_ref, cos_ref, sin_ref, lfl_ref, lbl_ref,
                o_ref, qr_ref, kr_ref, ob_ref, dmat_ref, dec_ref, state_ref, ones_ref, *, seq, layer):
    C = RET_CHUNK
    n_chunks = seq // C
    cg = pl.program_id(1)
    ones_bd = jnp.where(_head_select_mask(), 1.0, 0.0).astype(BF16)
    ones_ref[...] = jnp.concatenate([ones_bd, ones_bd], axis=0)

    ri = lax.broadcasted_iota(jnp.int32, (C, C), 0)
    ci = lax.broadcasted_iota(jnp.int32, (C, C), 1)
    diff = (ri - ci).astype(F32)
    for hh in range(HEADS_PER_TILE):
        h = layer * N_HEADS + cg * HEADS_PER_TILE + hh
        lf = _log_sigmoid(jnp.full((C, C), lf_s_ref[h], F32))
        lb = _log_sigmoid(jnp.full((C, C), lb_s_ref[h], F32))
        dmat_ref[hh] = jnp.where(diff >= 0, jnp.exp(lf * diff), jnp.exp(lb * (-diff)))
    lfl = _log_sigmoid(lfl_ref[...])
    lbl = _log_sigmoid(lbl_ref[...])
    idx = lax.broadcasted_iota(jnp.int32, (C, MXU_DIM), 0).astype(F32)
    dec_ref[0] = jnp.exp(lfl * (idx + 1.0))
    dec_ref[1] = jnp.exp(lfl * (C - 1.0 - idx))
    dec_ref[2] = jnp.exp(lbl * (C - idx))
    dec_ref[3] = jnp.exp(lbl * idx)
    cd_f = jnp.exp(lfl * float(C))
    cd_b = jnp.exp(lbl * float(C))

    def rope(t, rows):
        lane = lax.broadcasted_iota(jnp.int32, t.shape, 1)
        first_half = (lane % HEAD_DIM) < (HEAD_DIM // 2)
        cs = cos_ref[rows, :]
        sn = sin_ref[rows, :]
        cs = jnp.concatenate([cs, cs], axis=-1)
        sn = jnp.concatenate([sn, sn], axis=-1)
        swapped = jnp.where(first_half, pltpu.roll(t, MXU_DIM - HEAD_DIM // 2, 1),
                            pltpu.roll(t, HEAD_DIM // 2, 1))
        return t * cs + swapped * sn

    def kv_update(k, v, kd, cd):
        kv = lax.dot_general((k * kd).astype(BF16), v, (((0,), (0,)), ((), ())),
                             preferred_element_type=F32)
        state_ref[...] = cd * state_ref[...] + jnp.where(_head_select_mask(), kv, 0.0)

    state_ref[...] = jnp.zeros_like(state_ref)

    def bwd_body(i, carry):
        n = n_chunks - 1 - i
        rows = pl.ds(pl.multiple_of(n * C, C), C)
        q = rope(q_ref[rows, :].astype(F32), rows) * (HEAD_DIM ** -0.5)
        k = rope(k_ref[rows, :].astype(F32), rows)
        qr_ref[rows, :] = q.astype(BF16)
        kr_ref[rows, :] = k.astype(BF16)
        ob_ref[rows, :] = jnp.dot((q * dec_ref[2]).astype(BF16), state_ref[...].astype(BF16),
                                  preferred_element_type=F32)
        kv_update(k, v_ref[rows, :], dec_ref[3], cd_b)
        return carry

    lax.fori_loop(0, n_chunks, bwd_body, 0, unroll=2)

    state_ref[...] = jnp.zeros_like(state_ref)

    def fwd_body(n, carry):
        rows = pl.ds(pl.multiple_of(n * C, C), C)
        qb = qr_ref[rows, :]
        kb = kr_ref[rows, :]
        v = v_ref[rows, :]
        o_cross = jnp.dot((qb.astype(F32) * dec_ref[0]).astype(BF16), state_ref[...].astype(BF16),
                          preferred_element_type=F32)
        q4 = _stack_heads(qb)
        sc = lax.dot_general(q4, kb, (((1,), (1,)), ((), ())), preferred_element_type=F32)
        sc = sc * dmat_ref[...].reshape(HEADS_PER_TILE * C, C)
        o4 = jnp.dot(sc.astype(BF16), v, preferred_element_type=F32)
        o = _unstack_heads(o4, C) + o_cross + ob_ref[rows, :]
        sq = o * o
        hi = sq.astype(BF16)
        lo = (sq - hi.astype(F32)).astype(BF16)
        ss = jnp.dot(jnp.concatenate([hi, lo], axis=-1), ones_ref[...], preferred_element_type=F32)
        inv = lax.rsqrt(ss * (1.0 / HEAD_DIM) + EPS)
        gate = g_ref[rows, :].astype(F32)
        o_ref[rows, :] = (o * inv * _silu(gate)).astype(BF16)
        kv_update(kb.astype(F32), v, dec_ref[1], cd_f)
        return carry

    lax.fori_loop(0, n_chunks, fwd_body, 0, unroll=2)


def _retention(zp, logit_f, logit_b, rope_tbl, layer):
    _, batch, seq, _ = zp.shape
    cos_t, sin_t = rope_tbl
    n_cg = D_GROUP // MXU_DIM
    spec = lambda p: pl.BlockSpec((None, None, seq, MXU_DIM), lambda b, c, *_: (p, b, 0, c))
    tbl_spec = pl.BlockSpec((seq, 2 * HEAD_DIM), lambda b, c, *_: (0, 0))
    lane_spec = pl.BlockSpec((None, 1, MXU_DIM), lambda b, c, *_: (layer, 0, c))
    per_lane = lambda v: jnp.repeat(v.astype(F32), HEAD_DIM, axis=1)[:, None, :]
    return pl.pallas_call(
        functools.partial(_ret_kernel, seq=seq, layer=layer),
        name="retention",
        grid_spec=pltpu.PrefetchScalarGridSpec(
            num_scalar_prefetch=2,
            grid=(batch, n_cg),
            in_specs=[spec(P_RQ), spec(P_RK), spec(P_RV), spec(P_RG), tbl_spec, tbl_spec,
                      lane_spec, lane_spec],
            out_specs=pl.BlockSpec((None, seq, MXU_DIM), lambda b, c, *_: (b, 0, c)),
            scratch_shapes=[
                pltpu.VMEM((seq, MXU_DIM), BF16),
                pltpu.VMEM((seq, MXU_DIM), BF16),
                pltpu.VMEM((seq, MXU_DIM), F32),
                pltpu.VMEM((HEADS_PER_TILE, RET_CHUNK, RET_CHUNK), F32),
                pltpu.VMEM((4, RET_CHUNK, MXU_DIM), F32),
                pltpu.VMEM((MXU_DIM, MXU_DIM), F32),
                pltpu.VMEM((2 * MXU_DIM, MXU_DIM), BF16),
            ],
        ),
        out_shape=jax.ShapeDtypeStruct((batch, seq, D_GROUP), BF16),
        compiler_params=_params(("arbitrary", "arbitrary")),
    )(logit_f.astype(F32).reshape(-1), logit_b.astype(F32).reshape(-1), zp, zp, zp, zp, cos_t, sin_t,
      per_lane(logit_f), per_lane(logit_b))


CONV_HALO = SUBLANES_BF16
CONV_ROWS = 64


def _conv_glu(j, a_ref, b_ref, u_ref, *, seq):
    halo = CONV_HALO
    ts = u_ref.shape[0] - 2 * halo
    n_tiles = seq // ts

    def glu(start, size):
        rows = pl.ds(pl.multiple_of(start, halo), size)
        a = a_ref[rows, :].astype(F32)
        return a * _sigmoid(b_ref[rows, :].astype(F32))

    base = j * ts
    u_ref[halo:halo + ts, :] = glu(base, ts)
    lo = glu(jnp.maximum(base - halo, 0), halo)
    u_ref[0:halo, :] = jnp.where(j > 0, lo, 0.0)
    hi = glu(jnp.minimum(base + ts, seq - halo), halo)
    u_ref[halo + ts:, :] = jnp.where(j < n_tiles - 1, hi, 0.0)


def _conv_rows(s0, cw_ref, cb_ref, lg_ref, lb_ref, u_ref, h_ref):
    rb = CONV_ROWS
    first = CONV_HALO - CONV_HALF
    parts = []
    for lt in range(D_GROUP // LANES):
        lanes = slice(lt * LANES, (lt + 1) * LANES)
        y = None
        for r in range(SUBLANES_F32):
            acc = None
            for o in range(r, first + CONV_WIDTH, SUBLANES_F32):
                if o < first:
                    continue
                w = o - first
                lo_row = s0 + (o - r)
                term = u_ref[lo_row:lo_row + rb + SUBLANES_F32, lanes] * cw_ref[w:w + 1, lanes]
                acc = term if acc is None else acc + term
            shifted = acc[r:r + rb]
            y = shifted if y is None else y + shifted
        parts.append(y)
    y = jnp.concatenate(parts, axis=-1) + cb_ref[...]
    mu = jnp.mean(y, axis=-1, keepdims=True)
    yc = y - mu
    var = jnp.mean(yc * yc, axis=-1, keepdims=True)
    y = (yc * lax.rsqrt(var + EPS)) * lg_ref[...] + lb_ref[...]
    h_ref[s0:s0 + rb, :] = _silu(y).astype(BF16)
    return y[rb - SUBLANES_F32:, D_GROUP - LANES:]


def _runtime_zero(tile):
    bits = pltpu.bitcast(tile, jnp.uint32)
    bits = lax.shift_right_logical(lax.shift_right_logical(bits, jnp.uint32(16)), jnp.uint32(16))
    return bits[0, 0].astype(jnp.int32)


def _conv_pointwise(g_ref, w_ref, h_ref, o_ref):
    o = jnp.dot(h_ref[...], w_ref[...], preferred_element_type=F32)
    o_ref[...] = (o * _silu(g_ref[...].astype(F32))).astype(BF16)


def kernel(x, c, norm_g, w_ada, b_ada, w_in, w_fft, na_rel_bias, ret_logit_fwd, ret_logit_bwd,
           conv_w, conv_b, conv_ln_g, conv_ln_b, conv_w_pw, w_out, final_g):
    seq = x.shape[1]
    mod = _ada(c, w_ada, b_ada)
    fft_consts = _fft_constants(seq)
    rope_tbl = _rope_tables(seq)
    bias_tbl = _na_bias_table(na_rel_bias)
    w_in_b, w_fft_b, w_pw_b, w_out_b = (w.astype(BF16) for w in (w_in, w_fft, conv_w_pw, w_out))
    for l in range(DEPTH):
        zp = _inproj(x, norm_g, mod, w_in_b, l)
        o_fft = _fourier(zp, w_fft_b, fft_consts, l)
        o_na = _neighbourhood(zp, bias_tbl, l)
        o_ret = _retention(zp, ret_logit_fwd, ret_logit_bwd, rope_tbl, l)
        conv_params = (conv_w, conv_b, conv_ln_g, conv_ln_b)
        x = _outproj((o_fft, o_na, o_ret), zp, conv_params, w_pw_b, w_out_b, x, mod, final_g, l,
                     final=(l == DEPTH - 1))
    return x
```

```python
import functools
import math

import numpy as np
import jax
import jax.numpy as jnp
from jax import lax
from jax.experimental import pallas as pl
from jax.experimental.pallas import tpu as pltpu

F32 = jnp.float32
BF16 = jnp.bfloat16

D_MODEL = 2048
DEPTH = 2
GRID_W = 64
D_GROUP = 512
HEAD_DIM = 64
N_HEADS = D_GROUP // HEAD_DIM
N_FFT_GROUPS = 4
FFT_GROUP_DIM = D_GROUP // N_FFT_GROUPS
NA_KH = 8
NA_KW = 16
CONV_WIDTH = 31
CONV_HALF = CONV_WIDTH // 2
ROPE_BASE = 10000.0
EPS = 1e-6
N_PIECES = 13
(P_FX, P_FG, P_NQ, P_NK, P_NV, P_NG, P_RQ, P_RK, P_RV, P_RG, P_CA, P_CB, P_CG) = range(N_PIECES)

MXU_DIM = 256
HEADS_PER_TILE = MXU_DIM // HEAD_DIM
VMEM_LIMIT = 56 * 1024 * 1024
SUBLANES_BF16 = 16
SUBLANES_F32 = 8
LANES = 128

NEG_BIG = -1e30


def _params(sem, vmem=VMEM_LIMIT):
    return pltpu.CompilerParams(dimension_semantics=sem, vmem_limit_bytes=vmem)


def _sigmoid(t):
    return 0.5 * jnp.tanh(0.5 * t) + 0.5


def _silu(t):
    h = 0.5 * t
    return h + h * jnp.tanh(h)


ADA_TN = 768
ADA_ROWS = 8
(MOD_SHIFT, MOD_SCALE, MOD_GATE) = range(3)


def _ada_kernel(c_ref, w_ref, b_ref, o_ref):
    ca = _silu(c_ref[...]).astype(BF16)
    w = w_ref[...].astype(BF16)
    o_ref[...] = jnp.dot(ca, w, preferred_element_type=F32) + b_ref[...]


def _ada(c, w_ada, b_ada):
    batch = c.shape[0]
    c_pad = jnp.zeros((ADA_ROWS, D_MODEL), F32).at[:batch].set(c)
    n3 = 3 * D_MODEL
    out = pl.pallas_call(
        _ada_kernel,
        name="ada",
        grid=(DEPTH, n3 // ADA_TN),
        in_specs=[
            pl.BlockSpec((ADA_ROWS, D_MODEL), lambda l, j: (0, 0)),
            pl.BlockSpec((None, D_MODEL, ADA_TN), lambda l, j: (l, 0, j)),
            pl.BlockSpec((None, 1, ADA_TN), lambda l, j: (l, 0, j)),
        ],
        out_specs=pl.BlockSpec((None, ADA_ROWS, ADA_TN), lambda l, j: (l, 0, j)),
        out_shape=jax.ShapeDtypeStruct((DEPTH, ADA_ROWS, n3), F32),
        compiler_params=_params(("arbitrary", "arbitrary")),
    )(c_pad, w_ada, b_ada.reshape(DEPTH, 1, n3))
    return out.reshape(DEPTH, ADA_ROWS, 1, n3)


INPROJ_TM = 512
INPROJ_WCHUNK = MXU_DIM
INPROJ_VMEM = 60 * 1024 * 1024


def _inproj_kernel(x_ref, g_ref, sc_ref, sh_ref, w_hbm, o_ref, h0_ref, h1_ref, w_ref, stage_ref, sem,
                   *, n_tiles, layer):
    t = pl.program_id(0)
    n_wchunks = w_ref.shape[0]
    per_piece = D_GROUP // INPROJ_WCHUNK

    def weight_copy(c, slot):
        cols = pl.ds(pl.multiple_of(c * INPROJ_WCHUNK, INPROJ_WCHUNK), INPROJ_WCHUNK)
        return pltpu.make_async_copy(w_hbm.at[layer, :, cols], stage_ref.at[slot], sem.at[slot])

    def load_weight():
        weight_copy(0, 0).start()

        def pair(i, carry):
            for slot in range(2):
                c = 2 * i + slot

                @pl.when(c + 1 < n_wchunks)
                def _():
                    weight_copy(c + 1, 1 - slot).start()

                weight_copy(c, slot).wait()
                w_ref[c] = stage_ref[slot].astype(BF16)
            return carry

        lax.fori_loop(0, n_wchunks // 2, pair, 0)

    def norm(h_ref):
        x = x_ref[...]
        ms = jnp.mean(x * x, axis=-1, keepdims=True)
        h = (x * lax.rsqrt(ms + EPS)) * g_ref[...]
        h = h * (1.0 + sc_ref[...]) + sh_ref[...]
        h_ref[...] = h.astype(BF16)

    def project(h_ref):
        h = h_ref[...]
        for c in range(n_wchunks):
            cols = slice((c % per_piece) * INPROJ_WCHUNK, (c % per_piece + 1) * INPROJ_WCHUNK)
            o_ref[c // per_piece, :, cols] = jnp.dot(
                h, w_ref[c], preferred_element_type=F32).astype(BF16)

    odd = t % 2 == 1

    @pl.when(t == 0)
    def _():
        load_weight()
        norm(h0_ref)

    @pl.when(odd)
    def _():
        project(h0_ref)
        norm(h1_ref)

    @pl.when(jnp.logical_and(jnp.logical_not(odd), jnp.logical_and(t > 0, t < n_tiles)))
    def _():
        project(h1_ref)
        norm(h0_ref)

    @pl.when(t == n_tiles)
    def _():
        project(h1_ref)


def _inproj(x, norm_g, mod, w_in, layer):
    batch, seq, _ = x.shape
    tm = INPROJ_TM
    per_batch = seq // tm
    n_tiles = batch * per_batch
    n_wchunks = N_PIECES * D_GROUP // INPROJ_WCHUNK
    assert n_tiles % 2 == 0 and n_wchunks % 2 == 0
    norm_tile = lambda t: jnp.minimum(t, n_tiles - 1)
    proj_tile = lambda t: jnp.maximum(t - 1, 0)
    return pl.pallas_call(
        functools.partial(_inproj_kernel, n_tiles=n_tiles, layer=layer),
        name="inproj",
        grid=(n_tiles + 1,),
        in_specs=[
            pl.BlockSpec((None, tm, D_MODEL),
                         lambda t: (norm_tile(t) // per_batch, norm_tile(t) % per_batch, 0)),
            pl.BlockSpec((None, 1, D_MODEL), lambda t: (layer, 0, 0)),
            pl.BlockSpec((None, None, 1, D_MODEL),
                         lambda t: (layer, norm_tile(t) // per_batch, 0, MOD_SCALE)),
            pl.BlockSpec((None, None, 1, D_MODEL),
                         lambda t: (layer, norm_tile(t) // per_batch, 0, MOD_SHIFT)),
            pl.BlockSpec(memory_space=pl.ANY),
        ],
        out_specs=pl.BlockSpec((N_PIECES, None, tm, D_GROUP),
                               lambda t: (0, proj_tile(t) // per_batch, proj_tile(t) % per_batch, 0)),
        out_shape=jax.ShapeDtypeStruct((N_PIECES, batch, seq, D_GROUP), BF16),
        scratch_shapes=[
            pltpu.VMEM((tm, D_MODEL), BF16), pltpu.VMEM((tm, D_MODEL), BF16),
            pltpu.VMEM((n_wchunks, D_MODEL, INPROJ_WCHUNK), BF16),
            pltpu.VMEM((2, D_MODEL, INPROJ_WCHUNK), F32),
            pltpu.SemaphoreType.DMA((2,)),
        ],
        compiler_params=_params(("arbitrary",), INPROJ_VMEM),
    )(x, norm_g.reshape(DEPTH, 1, D_MODEL), mod, mod, w_in)


OUTPROJ_TM = 512


def _outproj_kernel(a0_ref, a1_ref, a2_ref, w_ref, x_ref, gate_ref, fg_ref,
                    ca_ref, cb_ref, cg_ref, cw_ref, cbias_ref, lg_ref, lb_ref, wpw_ref,
                    o_ref, u_ref, h_ref, cv0_ref, cv1_ref, *, final, n_tiles, per_batch, seq):
    t = pl.program_id(0)
    j = jnp.minimum(t, n_tiles - 1) % per_batch

    tm = o_ref.shape[0]
    n_chunks = tm // CONV_ROWS
    tn = D_MODEL // n_chunks

    def run(proj_cv_ref, conv_cv_ref):
        if conv_cv_ref is not None:
            _conv_glu(j, ca_ref, cb_ref, u_ref, seq=seq)
        if proj_cv_ref is not None:
            a = jnp.concatenate([a0_ref[...], a1_ref[...], a2_ref[...], proj_cv_ref[...]], axis=-1)
            ss = jnp.zeros((tm, 1), F32)
        zero = 0
        for i in range(n_chunks):
            if proj_cv_ref is not None:
                cols = pl.ds(pl.multiple_of(i * tn + zero, tn), tn)
                y = jnp.dot(a, w_ref[:, cols], preferred_element_type=F32)
                xn = x_ref[:, i * tn:(i + 1) * tn] + gate_ref[:, i * tn:(i + 1) * tn] * y
                if final:
                    ss = ss + jnp.sum(xn * xn, axis=-1, keepdims=True)
                o_ref[:, i * tn:(i + 1) * tn] = xn
            if conv_cv_ref is not None:
                tail = _conv_rows(i * CONV_ROWS, cw_ref, cbias_ref, lg_ref, lb_ref, u_ref, h_ref)
                zero = _runtime_zero(tail)
        if conv_cv_ref is not None:
            _conv_pointwise(cg_ref, wpw_ref, h_ref, conv_cv_ref)
        if proj_cv_ref is not None and final:
            o_ref[...] = (o_ref[...] * lax.rsqrt(ss * (1.0 / D_MODEL) + EPS)) * fg_ref[...]

    odd = t % 2 == 1

    @pl.when(t == 0)
    def _():
        run(None, cv0_ref)

    @pl.when(odd)
    def _():
        run(cv0_ref, cv1_ref)

    @pl.when(jnp.logical_and(jnp.logical_not(odd), jnp.logical_and(t > 0, t < n_tiles)))
    def _():
        run(cv1_ref, cv0_ref)

    @pl.when(t == n_tiles)
    def _():
        run(cv1_ref, None)


def _outproj(mixed, zp, conv_params, w_pw_bf16, w_bf16, x, mod, final_g, layer, final):
    batch, seq, _ = x.shape
    tm = OUTPROJ_TM
    per_batch = seq // tm
    n_tiles = batch * per_batch
    assert n_tiles % 2 == 0
    conv_tile = lambda t: jnp.minimum(t, n_tiles - 1)
    proj_tile = lambda t: jnp.maximum(t - 1, 0)
    a_spec = pl.BlockSpec((None, tm, D_GROUP),
                          lambda t: (proj_tile(t) // per_batch, proj_tile(t) % per_batch, 0))
    x_spec = pl.BlockSpec((None, tm, D_MODEL),
                          lambda t: (proj_tile(t) // per_batch, proj_tile(t) % per_batch, 0))
    seq_spec = lambda p: pl.BlockSpec((None, None, seq, D_GROUP),
                                      lambda t: (p, conv_tile(t) // per_batch, 0, 0))
    vec = pl.BlockSpec((None, 1, D_GROUP), lambda t: (layer, 0, 0))
    conv_w, conv_b, ln_g, ln_b = conv_params
    as_rows = lambda v: v.reshape(DEPTH, 1, D_GROUP)
    return pl.pallas_call(
        functools.partial(_outproj_kernel, final=final, n_tiles=n_tiles, per_batch=per_batch, seq=seq),
        name="outproj",
        grid=(n_tiles + 1,),
        in_specs=[
            a_spec, a_spec, a_spec,
            pl.BlockSpec((None, D_MODEL, D_MODEL), lambda t: (layer, 0, 0), pipeline_mode=pl.Buffered(1)),
            x_spec,
            pl.BlockSpec((None, None, 1, D_MODEL), lambda t: (layer, proj_tile(t) // per_batch, 0, MOD_GATE)),
            pl.BlockSpec((1, D_MODEL), lambda t: (0, 0)),
            seq_spec(P_CA), seq_spec(P_CB),
            pl.BlockSpec((None, None, tm, D_GROUP),
                         lambda t: (P_CG, conv_tile(t) // per_batch, conv_tile(t) % per_batch, 0)),
            pl.BlockSpec((None, CONV_WIDTH, D_GROUP), lambda t: (layer, 0, 0)),
            vec, vec, vec,
            pl.BlockSpec((None, D_GROUP, D_GROUP), lambda t: (layer, 0, 0)),
        ],
        out_specs=x_spec,
        out_shape=jax.ShapeDtypeStruct((batch, seq, D_MODEL), F32),
        scratch_shapes=[
            pltpu.VMEM((tm + 2 * CONV_HALO, D_GROUP), F32),
            pltpu.VMEM((tm, D_GROUP), BF16),
            pltpu.VMEM((tm, D_GROUP), BF16), pltpu.VMEM((tm, D_GROUP), BF16),
        ],
        compiler_params=_params(("arbitrary",)),
    )(*mixed, w_bf16, x, mod, final_g.reshape(1, D_MODEL), zp, zp, zp, conv_w,
      as_rows(conv_b), as_rows(ln_g), as_rows(ln_b), w_pw_bf16)


FFT_BLOCK = 16


def _fft_constants(seq):
    rows = seq // GRID_W
    assert rows == GRID_W
    n = np.arange(GRID_W)
    ang1 = 2.0 * np.pi * ((n[:, None] * n[None, :]) % GRID_W) / GRID_W
    f1 = np.concatenate([np.cos(ang1), -np.sin(ang1)], axis=0)
    k1 = n[:, None, None]
    k2 = n[None, :, None]
    s2 = n[None, None, :]
    ang2 = 2.0 * np.pi * ((s2 * (k1 + GRID_W * k2)) % seq) / seq
    mr, mi = np.cos(ang2), -np.sin(ang2)
    m2 = np.concatenate([np.concatenate([mr, -mi], axis=2),
                         np.concatenate([mi, mr], axis=2)], axis=1)
    c = np.arange(FFT_GROUP_DIM)
    angc = 2.0 * np.pi * ((c[:, None] * c[None, :]) % FFT_GROUP_DIM) / FFT_GROUP_DIM
    fc = np.concatenate([np.cos(angc), np.sin(angc)], axis=0)
    return tuple(jnp.asarray(t, F32).astype(BF16) for t in (f1, m2, fc))


def _fft_kernel(u_ref, gate_ref, f1_ref, m2_ref, fc_ref, w_ref, o_ref, u32_ref, g_ref, y_ref, *, norm):
    grp = pl.program_id(1)
    n = GRID_W
    blk = FFT_BLOCK

    u32_ref[...] = u_ref[...].astype(F32)

    def stage1(i, carry):
        for j in range(blk):
            s2 = i * blk + j
            us = u32_ref[pl.ds(s2, n, stride=n), :].astype(BF16)
            gs = jnp.dot(f1_ref[...], us, preferred_element_type=F32)
            g_ref[pl.ds(pl.multiple_of(s2 * 2 * n, 2 * n), 2 * n), :] = gs
        return carry

    lax.fori_loop(0, n // blk, stage1, 0)

    def stage2(i, carry):
        lhs = []
        for j in range(blk):
            k1 = i * blk + j
            gr = g_ref[pl.ds(k1, n, stride=2 * n), :]
            gi = g_ref[pl.ds(n + k1, n, stride=2 * n), :]
            gk = jnp.concatenate([gr, gi], axis=0).astype(BF16)
            xk = jnp.dot(m2_ref[k1], gk, preferred_element_type=F32)
            lhs.append(jnp.concatenate([xk[:n], xk[n:]], axis=-1))
        lhs = jnp.concatenate(lhs, axis=0).astype(BF16)
        y = jnp.dot(lhs, fc_ref[...], preferred_element_type=F32) * norm
        y_ref[grp, pl.ds(pl.multiple_of(i * blk * n, blk * n), blk * n), :] = y
        return carry

    lax.fori_loop(0, n // blk, stage2, 0)

    @pl.when(grp == N_FFT_GROUPS - 1)
    def _():
        def stage3(i, carry):
            tiles = []
            for j in range(blk):
                k2 = i * blk + j
                tiles.append(jnp.concatenate(
                    [y_ref[g, pl.ds(k2, n, stride=n), :] for g in range(N_FFT_GROUPS)], axis=-1))
            y = jnp.concatenate(tiles, axis=0).astype(BF16)
            o = jnp.dot(y, w_ref[...], preferred_element_type=F32)
            rows = pl.ds(pl.multiple_of(i * blk * n, blk * n), blk * n)
            o_ref[rows, :] = (o * _silu(gate_ref[rows, :].astype(F32))).astype(BF16)
            return carry

        lax.fori_loop(0, n // blk, stage3, 0)


def _fourier(zp, w_fft_bf16, consts, layer):
    _, batch, seq, _ = zp.shape
    f1, m2, fc = consts
    n = GRID_W
    const = lambda shape: pl.BlockSpec(shape, lambda b, g: (0,) * len(shape))
    return pl.pallas_call(
        functools.partial(_fft_kernel, norm=1.0 / math.sqrt(seq * FFT_GROUP_DIM)),
        name="fourier",
        grid=(batch, N_FFT_GROUPS),
        in_specs=[
            pl.BlockSpec((None, None, seq, FFT_GROUP_DIM), lambda b, g: (P_FX, b, 0, g)),
            pl.BlockSpec((None, None, seq, D_GROUP), lambda b, g: (P_FG, b, 0, 0)),
            const((2 * n, n)), const((n, 2 * n, 2 * n)), const((2 * FFT_GROUP_DIM, FFT_GROUP_DIM)),
            pl.BlockSpec((None, D_GROUP, D_GROUP), lambda b, g: (layer, 0, 0)),
        ],
        out_specs=pl.BlockSpec((None, seq, D_GROUP), lambda b, g: (b, 0, 0)),
        out_shape=jax.ShapeDtypeStruct((batch, seq, D_GROUP), BF16),
        scratch_shapes=[
            pltpu.VMEM((seq, FFT_GROUP_DIM), F32),
            pltpu.VMEM((n * 2 * n, FFT_GROUP_DIM), F32),
            pltpu.VMEM((N_FFT_GROUPS, seq, FFT_GROUP_DIM), F32),
        ],
        compiler_params=_params(("arbitrary", "arbitrary")),
    )(zp, zp, f1, m2, fc, w_fft_bf16)


NA_ROWS_PER_STEP = 8


def _na_bias_table(rel_bias):
    n_dc = 2 * NA_KW - 1
    col = np.arange(GRID_W)
    col_start = np.clip(col - NA_KW // 2, 0, GRID_W - NA_KW)
    rel_c = col[None, :] - col_start[:, None]
    col_in = (rel_c >= 0) & (rel_c < NA_KW)
    dc = np.clip(col[None, :] - col[:, None] + NA_KW - 1, 0, n_dc - 1)
    onehot = (dc[None] == np.arange(n_dc)[:, None, None]).astype(np.float32)
    by_col = jnp.einsum('lhdj,jqk->ldhqk', rel_bias.astype(F32), jnp.asarray(onehot),
                        precision=lax.Precision.HIGHEST)
    by_col = jnp.where(col_in[None, None, None], by_col, NEG_BIG)
    return jnp.concatenate([by_col[:, :-1], by_col[:, 1:]], axis=-1)


def _head_select_mask():
    r = lax.broadcasted_iota(jnp.int32, (MXU_DIM, MXU_DIM), 0) // HEAD_DIM
    c = lax.broadcasted_iota(jnp.int32, (MXU_DIM, MXU_DIM), 1) // HEAD_DIM
    return r == c


def _stack_heads(t):
    n = t.shape[0]
    reps = jnp.concatenate([t] * HEADS_PER_TILE, axis=0)
    row_head = lax.broadcasted_iota(jnp.int32, reps.shape, 0) // n
    lane_head = lax.broadcasted_iota(jnp.int32, reps.shape, 1) // HEAD_DIM
    return jnp.where(row_head == lane_head, reps, jnp.zeros_like(reps))


def _unstack_heads(t4, n):
    lane_head = lax.broadcasted_iota(jnp.int32, (n, MXU_DIM), 1) // HEAD_DIM
    out = jnp.zeros((n, MXU_DIM), t4.dtype)
    for h in range(HEADS_PER_TILE):
        out = jnp.where(lane_head == h, t4[h * n:(h + 1) * n], out)
    return out


def _na_kernel(q_ref, k_ref, v_ref, g_ref, bias_ref, o_ref, *, rows):
    kh = min(NA_KH, rows)
    step = pl.program_id(1)

    def row_body(rr, carry):
        r = step * NA_ROWS_PER_STEP + rr
        r_start = jnp.clip(r - kh // 2, 0, rows - kh)
        variant = r - r_start
        q_off = pl.multiple_of(rr * GRID_W, GRID_W)
        k_off = pl.multiple_of(r_start * GRID_W, GRID_W)
        for cg in range(D_GROUP // MXU_DIM):
            lanes = slice(cg * MXU_DIM, (cg + 1) * MXU_DIM)
            q = q_ref[pl.ds(q_off, GRID_W), lanes] * (HEAD_DIM ** -0.5)
            kw = k_ref[pl.ds(k_off, kh * GRID_W), lanes]
            vw = v_ref[pl.ds(k_off, kh * GRID_W), lanes]
            q4 = _stack_heads(q)
            s = lax.dot_general(q4, kw, (((1,), (1,)), ((), ())), preferred_element_type=F32)
            heads = slice(cg * HEADS_PER_TILE, (cg + 1) * HEADS_PER_TILE)
            bias = jnp.concatenate(
                [bias_ref[NA_KH - 1 - variant + a, heads].reshape(HEADS_PER_TILE * GRID_W, 2 * GRID_W)
                 for a in range(0, kh, 2)], axis=-1)
            s = s + bias
            m = jnp.max(s, axis=-1, keepdims=True)
            e = jnp.exp(s - m)
            l = jnp.sum(e, axis=-1, keepdims=True)
            o4 = jnp.dot(e.astype(BF16), vw, preferred_element_type=F32) / l
            o = _unstack_heads(o4, GRID_W)
            gate = g_ref[pl.ds(q_off, GRID_W), lanes].astype(F32)
            o_ref[pl.ds(q_off, GRID_W), lanes] = (o * _silu(gate)).astype(BF16)
        return carry

    lax.fori_loop(0, NA_ROWS_PER_STEP, row_body, 0, unroll=True)


def _neighbourhood(zp, bias_tbl, layer):
    _, batch, seq, _ = zp.shape
    rows = seq // GRID_W
    tq = NA_ROWS_PER_STEP * GRID_W
    q_spec = lambda p: pl.BlockSpec((None, None, tq, D_GROUP), lambda b, i: (p, b, i, 0))
    kv_spec = lambda p: pl.BlockSpec((None, None, seq, D_GROUP), lambda b, i: (p, b, 0, 0))
    return pl.pallas_call(
        functools.partial(_na_kernel, rows=rows),
        name="natten",
        grid=(batch, rows // NA_ROWS_PER_STEP),
        in_specs=[
            q_spec(P_NQ), kv_spec(P_NK), kv_spec(P_NV), q_spec(P_NG),
            pl.BlockSpec((None,) + bias_tbl.shape[1:], lambda b, i: (layer, 0, 0, 0, 0),
                         pipeline_mode=pl.Buffered(1)),
        ],
        out_specs=pl.BlockSpec((None, tq, D_GROUP), lambda b, i: (b, i, 0)),
        out_shape=jax.ShapeDtypeStruct((batch, seq, D_GROUP), BF16),
        compiler_params=_params(("arbitrary", "arbitrary")),
    )(zp, zp, zp, zp, bias_tbl)


RET_CHUNK = MXU_DIM


def _log_sigmoid(t):
    return jnp.minimum(t, 0.0) - jnp.log1p(jnp.exp(-jnp.abs(t)))


def _rope_tables(seq):
    half = HEAD_DIM // 2
    inv = ROPE_BASE ** (-jnp.arange(half, dtype=F32) / half)
    ang = jnp.arange(seq, dtype=F32)[:, None] * inv[None, :]
    cos, sin = jnp.cos(ang), jnp.sin(ang)
    cos2 = jnp.concatenate([cos, cos], axis=-1)
    sin2 = jnp.concatenate([-sin, sin], axis=-1)
    return jnp.tile(cos2, (1, 2)), jnp.tile(sin2, (1, 2))


def _ret_kernel(lf_s_ref, lb_s_ref, q_ref, k_ref, v_ref, g_ref, cos_ref, sin_ref, lfl_ref, lbl_ref,
                o_ref, qr_ref, kr_ref, ob_ref, dmat_ref, dec_ref, state_ref, ones_ref, *, seq, layer):
    C = RET_CHUNK
    n_chunks = seq // C
    cg = pl.program_id(1)
    ones_bd = jnp.where(_head_select_mask(), 1.0, 0.0).astype(BF16)
    ones_ref[...] = jnp.concatenate([ones_bd, ones_bd], axis=0)

    ri = lax.broadcasted_iota(jnp.int32, (C, C), 0)
    ci = lax.broadcasted_iota(jnp.int32, (C, C), 1)
    diff = (ri - ci).astype(F32)
    for hh in range(HEADS_PER_TILE):
        h = layer * N_HEADS + cg * HEADS_PER_TILE + hh
        lf = _log_sigmoid(jnp.full((C, C), lf_s_ref[h], F32))
        lb = _log_sigmoid(jnp.full((C, C), lb_s_ref[h], F32))
        dmat_ref[hh] = jnp.where(diff >= 0, jnp.exp(lf * diff), jnp.exp(lb * (-diff)))
    lfl = _log_sigmoid(lfl_ref[...])
    lbl = _log_sigmoid(lbl_ref[...])
    idx = lax.broadcasted_iota(jnp.int32, (C, MXU_DIM), 0).astype(F32)
    dec_ref[0] = jnp.exp(lfl * (idx + 1.0))
    dec_ref[1] = jnp.exp(lfl * (C - 1.0 - idx))
    dec_ref[2] = jnp.exp(lbl * (C - idx))
    dec_ref[3] = jnp.exp(lbl * idx)
    cd_f = jnp.exp(lfl * float(C))
    cd_b = jnp.exp(lbl * float(C))

    def rope(t, rows):
        lane = lax.broadcasted_iota(jnp.int32, t.shape, 1)
        first_half = (lane % HEAD_DIM) < (HEAD_DIM // 2)
        cs = cos_ref[rows, :]
        sn = sin_ref[rows, :]
        cs = jnp.concatenate([cs, cs], axis=-1)
        sn = jnp.concatenate([sn, sn], axis=-1)
        swapped = jnp.where(first_half, pltpu.roll(t, MXU_DIM - HEAD_DIM // 2, 1),
                            pltpu.roll(t, HEAD_DIM // 2, 1))
        return t * cs + swapped * sn

    def kv_update(k, v, kd, cd):
        kv = lax.dot_general((k * kd).astype(BF16), v, (((0,), (0,)), ((), ())),
                             preferred_element_type=F32)
        state_ref[...] = cd * state_ref[...] + jnp.where(_head_select_mask(), kv, 0.0)

    state_ref[...] = jnp.zeros_like(state_ref)

    def bwd_body(i, carry):
        n = n_chunks - 1 - i
        rows = pl.ds(pl.multiple_of(n * C, C), C)
        q = rope(q_ref[rows, :].astype(F32), rows) * (HEAD_DIM ** -0.5)
        k = rope(k_ref[rows, :].astype(F32), rows)
        qr_ref[rows, :] = q.astype(BF16)
        kr_ref[rows, :] = k.astype(BF16)
        ob_ref[rows, :] = jnp.dot((q * dec_ref[2]).astype(BF16), state_ref[...].astype(BF16),
                                  preferred_element_type=F32)
        kv_update(k, v_ref[rows, :], dec_ref[3], cd_b)
        return carry

    lax.fori_loop(0, n_chunks, bwd_body, 0, unroll=4)

    state_ref[...] = jnp.zeros_like(state_ref)

    def fwd_body(n, carry):
        rows = pl.ds(pl.multiple_of(n * C, C), C)
        qb = qr_ref[rows, :]
        kb = kr_ref[rows, :]
        v = v_ref[rows, :]
        o_cross = jnp.dot((qb.astype(F32) * dec_ref[0]).astype(BF16), state_ref[...].astype(BF16),
                          preferred_element_type=F32)
        q4 = _stack_heads(qb)
        sc = lax.dot_general(q4, kb, (((1,), (1,)), ((), ())), preferred_element_type=F32)
        sc = sc * dmat_ref[...].reshape(HEADS_PER_TILE * C, C)
        o4 = jnp.dot(sc.astype(BF16), v, preferred_element_type=F32)
        o = _unstack_heads(o4, C) + o_cross + ob_ref[rows, :]
        sq = o * o
        hi = sq.astype(BF16)
        lo = (sq - hi.astype(F32)).astype(BF16)
        ss = jnp.dot(jnp.concatenate([hi, lo], axis=-1), ones_ref[...], preferred_element_type=F32)
        inv = lax.rsqrt(ss * (1.0 / HEAD_DIM) + EPS)
        gate = g_ref[rows, :].astype(F32)
        o_ref[rows, :] = (o * inv * _silu(gate)).astype(BF16)
        kv_update(kb.astype(F32), v, dec_ref[1], cd_f)
        return carry

    lax.fori_loop(0, n_chunks, fwd_body, 0, unroll=4)


def _retention(zp, logit_f, logit_b, rope_tbl, layer):
    _, batch, seq, _ = zp.shape
    cos_t, sin_t = rope_tbl
    n_cg = D_GROUP // MXU_DIM
    spec = lambda p: pl.BlockSpec((None, None, seq, MXU_DIM), lambda b, c, *_: (p, b, 0, c))
    tbl_spec = pl.BlockSpec((seq, 2 * HEAD_DIM), lambda b, c, *_: (0, 0))
    lane_spec = pl.BlockSpec((None, 1, MXU_DIM), lambda b, c, *_: (layer, 0, c))
    per_lane = lambda v: jnp.repeat(v.astype(F32), HEAD_DIM, axis=1)[:, None, :]
    return pl.pallas_call(
        functools.partial(_ret_kernel, seq=seq, layer=layer),
        name="retention",
        grid_spec=pltpu.PrefetchScalarGridSpec(
            num_scalar_prefetch=2,
            grid=(batch, n_cg),
            in_specs=[spec(P_RQ), spec(P_RK), spec(P_RV), spec(P_RG), tbl_spec, tbl_spec,
                      lane_spec, lane_spec],
            out_specs=pl.BlockSpec((None, seq, MXU_DIM), lambda b, c, *_: (b, 0, c)),
            scratch_shapes=[
                pltpu.VMEM((seq, MXU_DIM), BF16),
                pltpu.VMEM((seq, MXU_DIM), BF16),
                pltpu.VMEM((seq, MXU_DIM), F32),
                pltpu.VMEM((HEADS_PER_TILE, RET_CHUNK, RET_CHUNK), F32),
                pltpu.VMEM((4, RET_CHUNK, MXU_DIM), F32),
                pltpu.VMEM((MXU_DIM, MXU_DIM), F32),
                pltpu.VMEM((2 * MXU_DIM, MXU_DIM), BF16),
            ],
        ),
        out_shape=jax.ShapeDtypeStruct((batch, seq, D_GROUP), BF16),
        compiler_params=_params(("arbitrary", "arbitrary")),
    )(logit_f.astype(F32).reshape(-1), logit_b.astype(F32).reshape(-1), zp, zp, zp, zp, cos_t, sin_t,
      per_lane(logit_f), per_lane(logit_b))


CONV_HALO = SUBLANES_BF16
CONV_ROWS = 64


def _conv_glu(j, a_ref, b_ref, u_ref, *, seq):
    halo = CONV_HALO
    ts = u_ref.shape[0] - 2 * halo
    n_tiles = seq // ts

    def glu(start, size):
        rows = pl.ds(pl.multiple_of(start, halo), size)
        a = a_ref[rows, :].astype(F32)
        return a * _sigmoid(b_ref[rows, :].astype(F32))

    base = j * ts
    u_ref[halo:halo + ts, :] = glu(base, ts)
    lo = glu(jnp.maximum(base - halo, 0), halo)
    u_ref[0:halo, :] = jnp.where(j > 0, lo, 0.0)
    hi = glu(jnp.minimum(base + ts, seq - halo), halo)
    u_ref[halo + ts:, :] = jnp.where(j < n_tiles - 1, hi, 0.0)


def _conv_rows(s0, cw_ref, cb_ref, lg_ref, lb_ref, u_ref, h_ref):
    rb = CONV_ROWS
    first = CONV_HALO - CONV_HALF
    parts = []
    for lt in range(D_GROUP // LANES):
        lanes = slice(lt * LANES, (lt + 1) * LANES)
        y = None
        for r in range(SUBLANES_F32):
            acc = None
            for o in range(r, first + CONV_WIDTH, SUBLANES_F32):
                if o < first:
                    continue
                w = o - first
                lo_row = s0 + (o - r)
                term = u_ref[lo_row:lo_row + rb + SUBLANES_F32, lanes] * cw_ref[w:w + 1, lanes]
                acc = term if acc is None else acc + term
            shifted = acc[r:r + rb]
            y = shifted if y is None else y + shifted
        parts.append(y)
    y = jnp.concatenate(parts, axis=-1) + cb_ref[...]
    mu = jnp.mean(y, axis=-1, keepdims=True)
    yc = y - mu
    var = jnp.mean(yc * yc, axis=-1, keepdims=True)
    y = (yc * lax.rsqrt(var + EPS)) * lg_ref[...] + lb_ref[...]
    h_ref[s0:s0 + rb, :] = _silu(y).astype(BF16)
    return y[rb - SUBLANES_F32:, D_GROUP - LANES:]


def _runtime_zero(tile):
    bits = pltpu.bitcast(tile, jnp.uint32)
    bits = lax.shift_right_logical(lax.shift_right_logical(bits, jnp.uint32(16)), jnp.uint32(16))
    return bits[0, 0].astype(jnp.int32)


def _conv_pointwise(g_ref, w_ref, h_ref, o_ref):
    o = jnp.dot(h_ref[...], w_ref[...], preferred_element_type=F32)
    o_ref[...] = (o * _silu(g_ref[...].astype(F32))).astype(BF16)


def kernel(x, c, norm_g, w_ada, b_ada, w_in, w_fft, na_rel_bias, ret_logit_fwd, ret_logit_bwd,
           conv_w, conv_b, conv_ln_g, conv_ln_b, conv_w_pw, w_out, final_g):
    seq = x.shape[1]
    mod = _ada(c, w_ada, b_ada)
    fft_consts = _fft_constants(seq)
    rope_tbl = _rope_tables(seq)
    bias_tbl = _na_bias_table(na_rel_bias)
    w_fft_b, w_pw_b, w_out_b = (w.astype(BF16) for w in (w_fft, conv_w_pw, w_out))
    for l in range(DEPTH):
        zp = _inproj(x, norm_g, mod, w_in, l)
        o_fft = _fourier(zp, w_fft_b, fft_consts, l)
        o_na = _neighbourhood(zp, bias_tbl, l)
        o_ret = _retention(zp, ret_logit_fwd, ret_logit_bwd, rope_tbl, l)
        conv_params = (conv_w, conv_b, conv_ln_g, conv_ln_b)
        x = _outproj((o_fft, o_na, o_ret), zp, conv_params, w_pw_b, w_out_b, x, mod, final_g, l,
                     final=(l == DEPTH - 1))
    return x
```

```python
import functools
import math

import numpy as np
import jax
import jax.numpy as jnp
from jax import lax
from jax.experimental import pallas as pl
from jax.experimental.pallas import tpu as pltpu

F32 = jnp.float32
BF16 = jnp.bfloat16

D_MODEL = 2048
DEPTH = 2
GRID_W = 64
D_GROUP = 512
HEAD_DIM = 64
N_HEADS = D_GROUP // HEAD_DIM
N_FFT_GROUPS = 4
FFT_GROUP_DIM = D_GROUP // N_FFT_GROUPS
NA_KH = 8
NA_KW = 16
CONV_WIDTH = 31
CONV_HALF = CONV_WIDTH // 2
ROPE_BASE = 10000.0
EPS = 1e-6
N_PIECES = 13
(P_FX, P_FG, P_NQ, P_NK, P_NV, P_NG, P_RQ, P_RK, P_RV, P_RG, P_CA, P_CB, P_CG) = range(N_PIECES)

MXU_DIM = 256
HEADS_PER_TILE = MXU_DIM // HEAD_DIM
VMEM_LIMIT = 56 * 1024 * 1024
SUBLANES_BF16 = 16
SUBLANES_F32 = 8
LANES = 128

NEG_BIG = -1e30


def _params(sem, vmem=VMEM_LIMIT):
    return pltpu.CompilerParams(dimension_semantics=sem, vmem_limit_bytes=vmem)


def _sigmoid(t):
    return 0.5 * jnp.tanh(0.5 * t) + 0.5


def _silu(t):
    h = 0.5 * t
    return h + h * jnp.tanh(h)


ADA_TN = 768
ADA_ROWS = 8
(MOD_SHIFT, MOD_SCALE, MOD_GATE) = range(3)


def _ada_kernel(c_ref, w_ref, b_ref, o_ref):
    ca = _silu(c_ref[...]).astype(BF16)
    w = w_ref[...].astype(BF16)
    o_ref[...] = jnp.dot(ca, w, preferred_element_type=F32) + b_ref[...]


def _ada(c, w_ada, b_ada):
    batch = c.shape[0]
    c_pad = jnp.zeros((ADA_ROWS, D_MODEL), F32).at[:batch].set(c)
    n3 = 3 * D_MODEL
    out = pl.pallas_call(
        _ada_kernel,
        name="ada",
        grid=(DEPTH, n3 // ADA_TN),
        in_specs=[
            pl.BlockSpec((ADA_ROWS, D_MODEL), lambda l, j: (0, 0)),
            pl.BlockSpec((None, D_MODEL, ADA_TN), lambda l, j: (l, 0, j)),
            pl.BlockSpec((None, 1, ADA_TN), lambda l, j: (l, 0, j)),
        ],
        out_specs=pl.BlockSpec((None, ADA_ROWS, ADA_TN), lambda l, j: (l, 0, j)),
        out_shape=jax.ShapeDtypeStruct((DEPTH, ADA_ROWS, n3), F32),
        compiler_params=_params(("arbitrary", "arbitrary")),
    )(c_pad, w_ada, b_ada.reshape(DEPTH, 1, n3))
    return out.reshape(DEPTH, ADA_ROWS, 1, n3)


INPROJ_TM = 512
INPROJ_WCHUNK = MXU_DIM
INPROJ_VMEM = 60 * 1024 * 1024


def _inproj_kernel(x_ref, g_ref, sc_ref, sh_ref, w_hbm, o_ref, h0_ref, h1_ref, w_ref, stage_ref, sem,
                   *, n_tiles, layer):
    t = pl.program_id(0)
    n_wchunks = w_ref.shape[0]
    per_piece = D_GROUP // INPROJ_WCHUNK

    def weight_copy(c, slot):
        cols = pl.ds(pl.multiple_of(c * INPROJ_WCHUNK, INPROJ_WCHUNK), INPROJ_WCHUNK)
        return pltpu.make_async_copy(w_hbm.at[layer, :, cols], stage_ref.at[slot], sem.at[slot])

    def load_weight():
        weight_copy(0, 0).start()

        def pair(i, carry):
            for slot in range(2):
                c = 2 * i + slot

                @pl.when(c + 1 < n_wchunks)
                def _():
                    weight_copy(c + 1, 1 - slot).start()

                weight_copy(c, slot).wait()
                w_ref[c] = stage_ref[slot].astype(BF16)
            return carry

        lax.fori_loop(0, n_wchunks // 2, pair, 0)

    def norm(h_ref):
        x = x_ref[...]
        ms = jnp.mean(x * x, axis=-1, keepdims=True)
        h = (x * lax.rsqrt(ms + EPS)) * g_ref[...]
        h = h * (1.0 + sc_ref[...]) + sh_ref[...]
        h_ref[...] = h.astype(BF16)

    def project(h_ref):
        h = h_ref[...]
        for c in range(n_wchunks):
            cols = slice((c % per_piece) * INPROJ_WCHUNK, (c % per_piece + 1) * INPROJ_WCHUNK)
            o_ref[c // per_piece, :, cols] = jnp.dot(
                h, w_ref[c], preferred_element_type=F32).astype(BF16)

    odd = t % 2 == 1

    @pl.when(t == 0)
    def _():
        load_weight()
        norm(h0_ref)

    @pl.when(odd)
    def _():
        project(h0_ref)
        norm(h1_ref)

    @pl.when(jnp.logical_and(jnp.logical_not(odd), jnp.logical_and(t > 0, t < n_tiles)))
    def _():
        project(h1_ref)
        norm(h0_ref)

    @pl.when(t == n_tiles)
    def _():
        project(h1_ref)


def _inproj(x, norm_g, mod, w_in, layer):
    batch, seq, _ = x.shape
    tm = INPROJ_TM
    per_batch = seq // tm
    n_tiles = batch * per_batch
    n_wchunks = N_PIECES * D_GROUP // INPROJ_WCHUNK
    assert n_tiles % 2 == 0 and n_wchunks % 2 == 0
    norm_tile = lambda t: jnp.minimum(t, n_tiles - 1)
    proj_tile = lambda t: jnp.maximum(t - 1, 0)
    return pl.pallas_call(
        functools.partial(_inproj_kernel, n_tiles=n_tiles, layer=layer),
        name="inproj",
        grid=(n_tiles + 1,),
        in_specs=[
            pl.BlockSpec((None, tm, D_MODEL),
                         lambda t: (norm_tile(t) // per_batch, norm_tile(t) % per_batch, 0)),
            pl.BlockSpec((None, 1, D_MODEL), lambda t: (layer, 0, 0)),
            pl.BlockSpec((None, None, 1, D_MODEL),
                         lambda t: (layer, norm_tile(t) // per_batch, 0, MOD_SCALE)),
            pl.BlockSpec((None, None, 1, D_MODEL),
                         lambda t: (layer, norm_tile(t) // per_batch, 0, MOD_SHIFT)),
            pl.BlockSpec(memory_space=pl.ANY),
        ],
        out_specs=pl.BlockSpec((N_PIECES, None, tm, D_GROUP),
                               lambda t: (0, proj_tile(t) // per_batch, proj_tile(t) % per_batch, 0)),
        out_shape=jax.ShapeDtypeStruct((N_PIECES, batch, seq, D_GROUP), BF16),
        scratch_shapes=[
            pltpu.VMEM((tm, D_MODEL), BF16), pltpu.VMEM((tm, D_MODEL), BF16),
            pltpu.VMEM((n_wchunks, D_MODEL, INPROJ_WCHUNK), BF16),
            pltpu.VMEM((2, D_MODEL, INPROJ_WCHUNK), F32),
            pltpu.SemaphoreType.DMA((2,)),
        ],
        compiler_params=_params(("arbitrary",), INPROJ_VMEM),
    )(x, norm_g.reshape(DEPTH, 1, D_MODEL), mod, mod, w_in)


OUTPROJ_TM = 512


def _outproj_kernel(a0_ref, a1_ref, a2_ref, w_ref, x_ref, gate_ref, fg_ref,
                    ca_ref, cb_ref, cg_ref, cw_ref, cbias_ref, lg_ref, lb_ref, wpw_ref,
                    o_ref, u_ref, h_ref, cv0_ref, cv1_ref, *, final, n_tiles, per_batch, seq):
    t = pl.program_id(0)
    j = jnp.minimum(t, n_tiles - 1) % per_batch

    tm = o_ref.shape[0]
    n_chunks = tm // CONV_ROWS
    tn = D_MODEL // n_chunks

    def run(proj_cv_ref, conv_cv_ref):
        if conv_cv_ref is not None:
            _conv_glu(j, ca_ref, cb_ref, u_ref, seq=seq)
        if proj_cv_ref is not None:
            a = jnp.concatenate([a0_ref[...], a1_ref[...], a2_ref[...], proj_cv_ref[...]], axis=-1)
            ss = jnp.zeros((tm, 1), F32)
        zero = 0
        for i in range(n_chunks):
            if proj_cv_ref is not None:
                cols = pl.ds(pl.multiple_of(i * tn + zero, tn), tn)
                y = jnp.dot(a, w_ref[:, cols], preferred_element_type=F32)
                xn = x_ref[:, i * tn:(i + 1) * tn] + gate_ref[:, i * tn:(i + 1) * tn] * y
                if final:
                    ss = ss + jnp.sum(xn * xn, axis=-1, keepdims=True)
                o_ref[:, i * tn:(i + 1) * tn] = xn
            if conv_cv_ref is not None:
                tail = _conv_rows(i * CONV_ROWS, cw_ref, cbias_ref, lg_ref, lb_ref, u_ref, h_ref)
                zero = _runtime_zero(tail)
        if conv_cv_ref is not None:
            _conv_pointwise(cg_ref, wpw_ref, h_ref, conv_cv_ref)
        if proj_cv_ref is not None and final:
            o_ref[...] = (o_ref[...] * lax.rsqrt(ss * (1.0 / D_MODEL) + EPS)) * fg_ref[...]

    odd = t % 2 == 1

    @pl.when(t == 0)
    def _():
        run(None, cv0_ref)

    @pl.when(odd)
    def _():
        run(cv0_ref, cv1_ref)

    @pl.when(jnp.logical_and(jnp.logical_not(odd), jnp.logical_and(t > 0, t < n_tiles)))
    def _():
        run(cv1_ref, cv0_ref)

    @pl.when(t == n_tiles)
    def _():
        run(cv1_ref, None)


def _outproj(mixed, zp, conv_params, w_pw_bf16, w_bf16, x, mod, final_g, layer, final):
    batch, seq, _ = x.shape
    tm = OUTPROJ_TM
    per_batch = seq // tm
    n_tiles = batch * per_batch
    assert n_tiles % 2 == 0
    conv_tile = lambda t: jnp.minimum(t, n_tiles - 1)
    proj_tile = lambda t: jnp.maximum(t - 1, 0)
    a_spec = pl.BlockSpec((None, tm, D_GROUP),
                          lambda t: (proj_tile(t) // per_batch, proj_tile(t) % per_batch, 0))
    x_spec = pl.BlockSpec((None, tm, D_MODEL),
                          lambda t: (proj_tile(t) // per_batch, proj_tile(t) % per_batch, 0))
    seq_spec = lambda p: pl.BlockSpec((None, None, seq, D_GROUP),
                                      lambda t: (p, conv_tile(t) // per_batch, 0, 0))
    vec = pl.BlockSpec((None, 1, D_GROUP), lambda t: (layer, 0, 0))
    conv_w, conv_b, ln_g, ln_b = conv_params
    as_rows = lambda v: v.reshape(DEPTH, 1, D_GROUP)
    return pl.pallas_call(
        functools.partial(_outproj_kernel, final=final, n_tiles=n_tiles, per_batch=per_batch, seq=seq),
        name="outproj",
        grid=(n_tiles + 1,),
        in_specs=[
            a_spec, a_spec, a_spec,
            pl.BlockSpec((None, D_MODEL, D_MODEL), lambda t: (layer, 0, 0), pipeline_mode=pl.Buffered(1)),
            x_spec,
            pl.BlockSpec((None, None, 1, D_MODEL), lambda t: (layer, proj_tile(t) // per_batch, 0, MOD_GATE)),
            pl.BlockSpec((1, D_MODEL), lambda t: (0, 0)),
            seq_spec(P_CA), seq_spec(P_CB),
            pl.BlockSpec((None, None, tm, D_GROUP),
                         lambda t: (P_CG, conv_tile(t) // per_batch, conv_tile(t) % per_batch, 0)),
            pl.BlockSpec((None, CONV_WIDTH, D_GROUP), lambda t: (layer, 0, 0)),
            vec, vec, vec,
            pl.BlockSpec((None, D_GROUP, D_GROUP), lambda t: (layer, 0, 0)),
        ],
        out_specs=x_spec,
        out_shape=jax.ShapeDtypeStruct((batch, seq, D_MODEL), F32),
        scratch_shapes=[
            pltpu.VMEM((tm + 2 * CONV_HALO, D_GROUP), F32),
            pltpu.VMEM((tm, D_GROUP), BF16),
            pltpu.VMEM((tm, D_GROUP), BF16), pltpu.VMEM((tm, D_GROUP), BF16),
        ],
        compiler_params=_params(("arbitrary",)),
    )(*mixed, w_bf16, x, mod, final_g.reshape(1, D_MODEL), zp, zp, zp, conv_w,
      as_rows(conv_b), as_rows(ln_g), as_rows(ln_b), w_pw_bf16)


FFT_BLOCK = 16


def _fft_constants(seq):
    rows = seq // GRID_W
    assert rows == GRID_W
    n = np.arange(GRID_W)
    ang1 = 2.0 * np.pi * ((n[:, None] * n[None, :]) % GRID_W) / GRID_W
    f1 = np.concatenate([np.cos(ang1), -np.sin(ang1)], axis=0)
    k1 = n[:, None, None]
    k2 = n[None, :, None]
    s2 = n[None, None, :]
    ang2 = 2.0 * np.pi * ((s2 * (k1 + GRID_W * k2)) % seq) / seq
    mr, mi = np.cos(ang2), -np.sin(ang2)
    m2 = np.concatenate([np.concatenate([mr, -mi], axis=2),
                         np.concatenate([mi, mr], axis=2)], axis=1)
    c = np.arange(FFT_GROUP_DIM)
    angc = 2.0 * np.pi * ((c[:, None] * c[None, :]) % FFT_GROUP_DIM) / FFT_GROUP_DIM
    fc = np.concatenate([np.cos(angc), np.sin(angc)], axis=0)
    return tuple(jnp.asarray(t, F32).astype(BF16) for t in (f1, m2, fc))


def _fft_kernel(u_ref, gate_ref, f1_ref, m2_ref, fc_ref, w_ref, o_ref, ut_ref, g_ref, gt_ref, y_ref, *, norm):
    grp = pl.program_id(1)
    n = GRID_W
    blk = FFT_BLOCK
    sub = SUBLANES_F32

    u3 = u_ref[...].astype(F32).reshape(n, n, FFT_GROUP_DIM)
    ut_ref[...] = jnp.swapaxes(u3, 0, 1).astype(BF16)

    def stage1(i, carry):
        for j in range(blk):
            s2 = i * blk + j
            g_ref[s2] = jnp.dot(f1_ref[...], ut_ref[s2], preferred_element_type=F32)
        return carry

    lax.fori_loop(0, n // blk, stage1, 0)

    for q in range(2 * n // sub):
        part = g_ref[:, q * sub:(q + 1) * sub, :]
        gt_ref[q * sub:(q + 1) * sub] = jnp.swapaxes(part, 0, 1).astype(BF16)

    def stage2(i, carry):
        lhs = []
        for j in range(blk):
            k1 = i * blk + j
            gk = jnp.concatenate([gt_ref[k1], gt_ref[n + k1]], axis=0)
            xk = jnp.dot(m2_ref[k1], gk, preferred_element_type=F32)
            lhs.append(jnp.concatenate([xk[:n], xk[n:]], axis=-1))
        lhs = jnp.concatenate(lhs, axis=0).astype(BF16)
        y = jnp.dot(lhs, fc_ref[...], preferred_element_type=F32) * norm
        y_ref[grp, pl.ds(pl.multiple_of(i * blk, blk), blk)] = y.reshape(blk, n, FFT_GROUP_DIM)
        return carry

    lax.fori_loop(0, n // blk, stage2, 0)

    @pl.when(grp == N_FFT_GROUPS - 1)
    def _():
        def stage3(i, carry):
            k2 = pl.ds(pl.multiple_of(i * sub, sub), sub)
            tiles = []
            for g in range(N_FFT_GROUPS):
                part = y_ref[g, :, k2, :]
                tiles.append(jnp.swapaxes(part, 0, 1).reshape(sub * n, FFT_GROUP_DIM))
            y = jnp.concatenate(tiles, axis=-1).astype(BF16)
            o = jnp.dot(y, w_ref[...], preferred_element_type=F32)
            rows = pl.ds(pl.multiple_of(i * sub * n, sub * n), sub * n)
            o_ref[rows, :] = (o * _silu(gate_ref[rows, :].astype(F32))).astype(BF16)
            return carry

        lax.fori_loop(0, n // sub, stage3, 0)


def _fourier(zp, w_fft_bf16, consts, layer):
    _, batch, seq, _ = zp.shape
    f1, m2, fc = consts
    n = GRID_W
    const = lambda shape: pl.BlockSpec(shape, lambda b, g: (0,) * len(shape))
    return pl.pallas_call(
        functools.partial(_fft_kernel, norm=1.0 / math.sqrt(seq * FFT_GROUP_DIM)),
        name="fourier",
        grid=(batch, N_FFT_GROUPS),
        in_specs=[
            pl.BlockSpec((None, None, seq, FFT_GROUP_DIM), lambda b, g: (P_FX, b, 0, g)),
            pl.BlockSpec((None, None, seq, D_GROUP), lambda b, g: (P_FG, b, 0, 0)),
            const((2 * n, n)), const((n, 2 * n, 2 * n)), const((2 * FFT_GROUP_DIM, FFT_GROUP_DIM)),
            pl.BlockSpec((None, D_GROUP, D_GROUP), lambda b, g: (layer, 0, 0)),
        ],
        out_specs=pl.BlockSpec((None, seq, D_GROUP), lambda b, g: (b, 0, 0)),
        out_shape=jax.ShapeDtypeStruct((batch, seq, D_GROUP), BF16),
        scratch_shapes=[
            pltpu.VMEM((n, n, FFT_GROUP_DIM), BF16),
            pltpu.VMEM((n, 2 * n, FFT_GROUP_DIM), F32),
            pltpu.VMEM((2 * n, n, FFT_GROUP_DIM), BF16),
            pltpu.VMEM((N_FFT_GROUPS, n, n, FFT_GROUP_DIM), F32),
        ],
        compiler_params=_params(("arbitrary", "arbitrary")),
    )(zp, zp, f1, m2, fc, w_fft_bf16)


NA_ROWS_PER_STEP = 8


def _na_bias_table(rel_bias):
    n_dc = 2 * NA_KW - 1
    col = np.arange(GRID_W)
    col_start = np.clip(col - NA_KW // 2, 0, GRID_W - NA_KW)
    rel_c = col[None, :] - col_start[:, None]
    col_in = (rel_c >= 0) & (rel_c < NA_KW)
    dc = np.clip(col[None, :] - col[:, None] + NA_KW - 1, 0, n_dc - 1)
    onehot = (dc[None] == np.arange(n_dc)[:, None, None]).astype(np.float32)
    by_col = jnp.einsum('lhdj,jqk->ldhqk', rel_bias.astype(F32), jnp.asarray(onehot),
                        precision=lax.Precision.HIGHEST)
    by_col = jnp.where(col_in[None, None, None], by_col, NEG_BIG)
    return jnp.concatenate([by_col[:, :-1], by_col[:, 1:]], axis=-1)


def _head_select_mask():
    r = lax.broadcasted_iota(jnp.int32, (MXU_DIM, MXU_DIM), 0) // HEAD_DIM
    c = lax.broadcasted_iota(jnp.int32, (MXU_DIM, MXU_DIM), 1) // HEAD_DIM
    return r == c


def _stack_heads(t):
    n = t.shape[0]
    reps = jnp.concatenate([t] * HEADS_PER_TILE, axis=0)
    row_head = lax.broadcasted_iota(jnp.int32, reps.shape, 0) // n
    lane_head = lax.broadcasted_iota(jnp.int32, reps.shape, 1) // HEAD_DIM
    return jnp.where(row_head == lane_head, reps, jnp.zeros_like(reps))


def _unstack_heads(t4, n):
    lane_head = lax.broadcasted_iota(jnp.int32, (n, MXU_DIM), 1) // HEAD_DIM
    out = jnp.zeros((n, MXU_DIM), t4.dtype)
    for h in range(HEADS_PER_TILE):
        out = jnp.where(lane_head == h, t4[h * n:(h + 1) * n], out)
    return out


def _na_kernel(q_ref, k_ref, v_ref, g_ref, bias_ref, o_ref, *, rows):
    kh = min(NA_KH, rows)
    step = pl.program_id(1)

    def row_body(rr, carry):
        r = step * NA_ROWS_PER_STEP + rr
        r_start = jnp.clip(r - kh // 2, 0, rows - kh)
        variant = r - r_start
        q_off = pl.multiple_of(rr * GRID_W, GRID_W)
        k_off = pl.multiple_of(r_start * GRID_W, GRID_W)
        for cg in range(D_GROUP // MXU_DIM):
            lanes = slice(cg * MXU_DIM, (cg + 1) * MXU_DIM)
            q = q_ref[pl.ds(q_off, GRID_W), lanes] * (HEAD_DIM ** -0.5)
            kw = k_ref[pl.ds(k_off, kh * GRID_W), lanes]
            vw = v_ref[pl.ds(k_off, kh * GRID_W), lanes]
            q4 = _stack_heads(q)
            s = lax.dot_general(q4, kw, (((1,), (1,)), ((), ())), preferred_element_type=F32)
            heads = slice(cg * HEADS_PER_TILE, (cg + 1) * HEADS_PER_TILE)
            bias = jnp.concatenate(
                [bias_ref[NA_KH - 1 - variant + a, heads].reshape(HEADS_PER_TILE * GRID_W, 2 * GRID_W)
                 for a in range(0, kh, 2)], axis=-1)
            s = s + bias
            m = jnp.max(s, axis=-1, keepdims=True)
            e = jnp.exp(s - m)
            l = jnp.sum(e, axis=-1, keepdims=True)
            o4 = jnp.dot(e.astype(BF16), vw, preferred_element_type=F32) / l
            o = _unstack_heads(o4, GRID_W)
            gate = g_ref[pl.ds(q_off, GRID_W), lanes].astype(F32)
            o_ref[pl.ds(q_off, GRID_W), lanes] = (o * _silu(gate)).astype(BF16)
        return carry

    lax.fori_loop(0, NA_ROWS_PER_STEP, row_body, 0, unroll=True)


def _neighbourhood(zp, bias_tbl, layer):
    _, batch, seq, _ = zp.shape
    rows = seq // GRID_W
    tq = NA_ROWS_PER_STEP * GRID_W
    q_spec = lambda p: pl.BlockSpec((None, None, tq, D_GROUP), lambda b, i: (p, b, i, 0))
    kv_spec = lambda p: pl.BlockSpec((None, None, seq, D_GROUP), lambda b, i: (p, b, 0, 0))
    return pl.pallas_call(
        functools.partial(_na_kernel, rows=rows),
        name="natten",
        grid=(batch, rows // NA_ROWS_PER_STEP),
        in_specs=[
            q_spec(P_NQ), kv_spec(P_NK), kv_spec(P_NV), q_spec(P_NG),
            pl.BlockSpec((None,) + bias_tbl.shape[1:], lambda b, i: (layer, 0, 0, 0, 0),
                         pipeline_mode=pl.Buffered(1)),
        ],
        out_specs=pl.BlockSpec((None, tq, D_GROUP), lambda b, i: (b, i, 0)),
        out_shape=jax.ShapeDtypeStruct((batch, seq, D_GROUP), BF16),
        compiler_params=_params(("arbitrary", "arbitrary")),
    )(zp, zp, zp, zp, bias_tbl)


RET_CHUNK = MXU_DIM


def _log_sigmoid(t):
    return jnp.minimum(t, 0.0) - jnp.log1p(jnp.exp(-jnp.abs(t)))


def _rope_tables(seq):
    half = HEAD_DIM // 2
    inv = ROPE_BASE ** (-jnp.arange(half, dtype=F32) / half)
    ang = jnp.arange(seq, dtype=F32)[:, None] * inv[None, :]
    cos, sin = jnp.cos(ang), jnp.sin(ang)
    cos2 = jnp.concatenate([cos, cos], axis=-1)
    sin2 = jnp.concatenate([-sin, sin], axis=-1)
    return jnp.tile(cos2, (1, 2)), jnp.tile(sin2, (1, 2))


def _ret_kernel(lf_s_ref, lb_s_ref, q_ref, k_ref, v_ref, g_ref, cos_ref, sin_ref, lfl_ref, lbl_ref,
                o_ref, qr_ref, kr_ref, ob_ref, dmat_ref, dec_ref, state_ref, ones_ref, *, seq, layer):
    C = RET_CHUNK
    n_chunks = seq // C
    cg = pl.program_id(1)
    ones_bd = jnp.where(_head_select_mask(), 1.0, 0.0).astype(BF16)
    ones_ref[...] = jnp.concatenate([ones_bd, ones_bd], axis=0)

    ri = lax.broadcasted_iota(jnp.int32, (C, C), 0)
    ci = lax.broadcasted_iota(jnp.int32, (C, C), 1)
    diff = (ri - ci).astype(F32)
    for hh in range(HEADS_PER_TILE):
        h = layer * N_HEADS + cg * HEADS_PER_TILE + hh
        lf = _log_sigmoid(jnp.full((C, C), lf_s_ref[h], F32))
        lb = _log_sigmoid(jnp.full((C, C), lb_s_ref[h], F32))
        dmat_ref[hh] = jnp.where(diff >= 0, jnp.exp(lf * diff), jnp.exp(lb * (-diff)))
    lfl = _log_sigmoid(lfl_ref[...])
    lbl = _log_sigmoid(lbl_ref[...])
    idx = lax.broadcasted_iota(jnp.int32, (C, MXU_DIM), 0).astype(F32)
    dec_ref[0] = jnp.exp(lfl * (idx + 1.0))
    dec_ref[1] = jnp.exp(lfl * (C - 1.0 - idx))
    dec_ref[2] = jnp.exp(lbl * (C - idx))
    dec_ref[3] = jnp.exp(lbl * idx)
    cd_f = jnp.exp(lfl * float(C))
    cd_b = jnp.exp(lbl * float(C))

    def rope(t, rows):
        lane = lax.broadcasted_iota(jnp.int32, t.shape, 1)
        first_half = (lane % HEAD_DIM) < (HEAD_DIM // 2)
        cs = cos_ref[rows, :]
        sn = sin_ref[rows, :]
        cs = jnp.concatenate([cs, cs], axis=-1)
        sn = jnp.concatenate([sn, sn], axis=-1)
        swapped = jnp.where(first_half, pltpu.roll(t, MXU_DIM - HEAD_DIM // 2, 1),
                            pltpu.roll(t, HEAD_DIM // 2, 1))
        return t * cs + swapped * sn

    def kv_update(k, v, kd, cd):
        kv = lax.dot_general((k * kd).astype(BF16), v, (((0,), (0,)), ((), ())),
                             preferred_element_type=F32)
        state_ref[...] = cd * state_ref[...] + jnp.where(_head_select_mask(), kv, 0.0)

    state_ref[...] = jnp.zeros_like(state_ref)

    def bwd_body(i, carry):
        n = n_chunks - 1 - i
        rows = pl.ds(pl.multiple_of(n * C, C), C)
        q = rope(q_ref[rows, :].astype(F32), rows) * (HEAD_DIM ** -0.5)
        k = rope(k_ref[rows, :].astype(F32), rows)
        qr_ref[rows, :] = q.astype(BF16)
        kr_ref[rows, :] = k.astype(BF16)
        ob_ref[rows, :] = jnp.dot((q * dec_ref[2]).astype(BF16), state_ref[...].astype(BF16),
                                  preferred_element_type=F32)
        kv_update(k, v_ref[rows, :], dec_ref[3], cd_b)
        return carry

    lax.fori_loop(0, n_chunks, bwd_body, 0, unroll=4)

    state_ref[...] = jnp.zeros_like(state_ref)

    def fwd_body(n, carry):
        rows = pl.ds(pl.multiple_of(n * C, C), C)
        qb = qr_ref[rows, :]
        kb = kr_ref[rows, :]
        v = v_ref[rows, :]
        o_cross = jnp.dot((qb.astype(F32) * dec_ref[0]).astype(BF16), state_ref[...].astype(BF16),
                          preferred_element_type=F32)
        q4 = _stack_heads(qb)
        sc = lax.dot_general(q4, kb, (((1,), (1,)), ((), ())), preferred_element_type=F32)
        sc = sc * dmat_ref[...].reshape(HEADS_PER_TILE * C, C)
        o4 = jnp.dot(sc.astype(BF16), v, preferred_element_type=F32)
        o = _unstack_heads(o4, C) + o_cross + ob_ref[rows, :]
        sq = o * o
        hi = sq.astype(BF16)
        lo = (sq - hi.astype(F32)).astype(BF16)
        ss = jnp.dot(jnp.concatenate([hi, lo], axis=-1), ones_ref[...], preferred_element_type=F32)
        inv = lax.rsqrt(ss * (1.0 / HEAD_DIM) + EPS)
        gate = g_ref[rows, :].astype(F32)
        o_ref[rows, :] = (o * inv * _silu(gate)).astype(BF16)
        kv_update(kb.astype(F32), v, dec_ref[1], cd_f)
        return carry

    lax.fori_loop(0, n_chunks, fwd_body, 0, unroll=4)


def _retention(zp, logit_f, logit_b, rope_tbl, layer):
    _, batch, seq, _ = zp.shape
    cos_t, sin_t = rope_tbl
    n_cg = D_GROUP // MXU_DIM
    spec = lambda p: pl.BlockSpec((None, None, seq, MXU_DIM), lambda b, c, *_: (p, b, 0, c))
    tbl_spec = pl.BlockSpec((seq, 2 * HEAD_DIM), lambda b, c, *_: (0, 0))
    lane_spec = pl.BlockSpec((None, 1, MXU_DIM), lambda b, c, *_: (layer, 0, c))
    per_lane = lambda v: jnp.repeat(v.astype(F32), HEAD_DIM, axis=1)[:, None, :]
    return pl.pallas_call(
        functools.partial(_ret_kernel, seq=seq, layer=layer),
        name="retention",
        grid_spec=pltpu.PrefetchScalarGridSpec(
            num_scalar_prefetch=2,
            grid=(batch, n_cg),
            in_specs=[spec(P_RQ), spec(P_RK), spec(P_RV), spec(P_RG), tbl_spec, tbl_spec,
                      lane_spec, lane_spec],
            out_specs=pl.BlockSpec((None, seq, MXU_DIM), lambda b, c, *_: (b, 0, c)),
            scratch_shapes=[
                pltpu.VMEM((seq, MXU_DIM), BF16),
                pltpu.VMEM((seq, MXU_DIM), BF16),
                pltpu.VMEM((seq, MXU_DIM), F32),
                pltpu.VMEM((HEADS_PER_TILE, RET_CHUNK, RET_CHUNK), F32),
                pltpu.VMEM((4, RET_CHUNK, MXU_DIM), F32),
                pltpu.VMEM((MXU_DIM, MXU_DIM), F32),
                pltpu.VMEM((2 * MXU_DIM, MXU_DIM), BF16),
            ],
        ),
        out_shape=jax.ShapeDtypeStruct((batch, seq, D_GROUP), BF16),
        compiler_params=_params(("arbitrary", "arbitrary")),
    )(logit_f.astype(F32).reshape(-1), logit_b.astype(F32).reshape(-1), zp, zp, zp, zp, cos_t, sin_t,
      per_lane(logit_f), per_lane(logit_b))


CONV_HALO = SUBLANES_BF16
CONV_ROWS = 64


def _conv_glu(j, a_ref, b_ref, u_ref, *, seq):
    halo = CONV_HALO
    ts = u_ref.shape[0] - 2 * halo
    n_tiles = seq // ts

    def glu(start, size):
        rows = pl.ds(pl.multiple_of(start, halo), size)
        a = a_ref[rows, :].astype(F32)
        return a * _sigmoid(b_ref[rows, :].astype(F32))

    base = j * ts
    u_ref[halo:halo + ts, :] = glu(base, ts)
    lo = glu(jnp.maximum(base - halo, 0), halo)
    u_ref[0:halo, :] = jnp.where(j > 0, lo, 0.0)
    hi = glu(jnp.minimum(base + ts, seq - halo), halo)
    u_ref[halo + ts:, :] = jnp.where(j < n_tiles - 1, hi, 0.0)


def _conv_rows(s0, cw_ref, cb_ref, lg_ref, lb_ref, u_ref, h_ref):
    rb = CONV_ROWS
    first = CONV_HALO - CONV_HALF
    parts = []
    for lt in range(D_GROUP // LANES):
        lanes = slice(lt * LANES, (lt + 1) * LANES)
        y = None
        for r in range(SUBLANES_F32):
            acc = None
            for o in range(r, first + CONV_WIDTH, SUBLANES_F32):
                if o < first:
                    continue
                w = o - first
                lo_row = s0 + (o - r)
                term = u_ref[lo_row:lo_row + rb + SUBLANES_F32, lanes] * cw_ref[w:w + 1, lanes]
                acc = term if acc is None else acc + term
            shifted = acc[r:r + rb]
            y = shifted if y is None else y + shifted
        parts.append(y)
    y = jnp.concatenate(parts, axis=-1) + cb_ref[...]
    mu = jnp.mean(y, axis=-1, keepdims=True)
    yc = y - mu
    var = jnp.mean(yc * yc, axis=-1, keepdims=True)
    y = (yc * lax.rsqrt(var + EPS)) * lg_ref[...] + lb_ref[...]
    h_ref[s0:s0 + rb, :] = _silu(y).astype(BF16)
    return y[rb - SUBLANES_F32:, D_GROUP - LANES:]


def _runtime_zero(tile):
    bits = pltpu.bitcast(tile, jnp.uint32)
    bits = lax.shift_right_logical(lax.shift_right_logical(bits, jnp.uint32(16)), jnp.uint32(16))
    return bits[0, 0].astype(jnp.int32)


def _conv_pointwise(g_ref, w_ref, h_ref, o_ref):
    o = jnp.dot(h_ref[...], w_ref[...], preferred_element_type=F32)
    o_ref[...] = (o * _silu(g_ref[...].astype(F32))).astype(BF16)


def kernel(x, c, norm_g, w_ada, b_ada, w_in, w_fft, na_rel_bias, ret_logit_fwd, ret_logit_bwd,
           conv_w, conv_b, conv_ln_g, conv_ln_b, conv_w_pw, w_out, final_g):
    seq = x.shape[1]
    mod = _ada(c, w_ada, b_ada)
    fft_consts = _fft_constants(seq)
    rope_tbl = _rope_tables(seq)
    bias_tbl = _na_bias_table(na_rel_bias)
    w_fft_b, w_pw_b, w_out_b = (w.astype(BF16) for w in (w_fft, conv_w_pw, w_out))
    for l in range(DEPTH):
        zp = _inproj(x, norm_g, mod, w_in, l)
        o_fft = _fourier(zp, w_fft_b, fft_consts, l)
        o_na = _neighbourhood(zp, bias_tbl, l)
        o_ret = _retention(zp, ret_logit_fwd, ret_logit_bwd, rope_tbl, l)
        conv_params = (conv_w, conv_b, conv_ln_g, conv_ln_b)
        x = _outproj((o_fft, o_na, o_ret), zp, conv_params, w_pw_b, w_out_b, x, mod, final_g, l,
                     final=(l == DEPTH - 1))
    return x
```

```python
import functools
import math

import numpy as np
import jax
import jax.numpy as jnp
from jax import lax
from jax.experimental import pallas as pl
from jax.experimental.pallas import tpu as pltpu

F32 = jnp.float32
BF16 = jnp.bfloat16

D_MODEL = 2048
DEPTH = 2
GRID_W = 64
D_GROUP = 512
HEAD_DIM = 64
N_HEADS = D_GROUP // HEAD_DIM
N_FFT_GROUPS = 4
FFT_GROUP_DIM = D_GROUP // N_FFT_GROUPS
NA_KH = 8
NA_KW = 16
CONV_WIDTH = 31
CONV_HALF = CONV_WIDTH // 2
ROPE_BASE = 10000.0
EPS = 1e-6
N_PIECES = 13
(P_FX, P_FG, P_NQ, P_NK, P_NV, P_NG, P_RQ, P_RK, P_RV, P_RG, P_CA, P_CB, P_CG) = range(N_PIECES)

MXU_DIM = 256
HEADS_PER_TILE = MXU_DIM // HEAD_DIM
VMEM_LIMIT = 56 * 1024 * 1024
SUBLANES_BF16 = 16
SUBLANES_F32 = 8
LANES = 128

NEG_BIG = -1e30


def _params(sem, vmem=VMEM_LIMIT):
    return pltpu.CompilerParams(dimension_semantics=sem, vmem_limit_bytes=vmem)


def _sigmoid(t):
    return 0.5 * jnp.tanh(0.5 * t) + 0.5


def _silu(t):
    h = 0.5 * t
    return h + h * jnp.tanh(h)


ADA_TN = 768
ADA_ROWS = 8
(MOD_SHIFT, MOD_SCALE, MOD_GATE) = range(3)


def _ada_kernel(c_ref, w_ref, b_ref, o_ref):
    ca = _silu(c_ref[...]).astype(BF16)
    w = w_ref[...].astype(BF16)
    o_ref[...] = jnp.dot(ca, w, preferred_element_type=F32) + b_ref[...]


def _ada(c, w_ada, b_ada):
    batch = c.shape[0]
    c_pad = jnp.zeros((ADA_ROWS, D_MODEL), F32).at[:batch].set(c)
    n3 = 3 * D_MODEL
    out = pl.pallas_call(
        _ada_kernel,
        name="ada",
        grid=(DEPTH, n3 // ADA_TN),
        in_specs=[
            pl.BlockSpec((ADA_ROWS, D_MODEL), lambda l, j: (0, 0)),
            pl.BlockSpec((None, D_MODEL, ADA_TN), lambda l, j: (l, 0, j)),
            pl.BlockSpec((None, 1, ADA_TN), lambda l, j: (l, 0, j)),
        ],
        out_specs=pl.BlockSpec((None, ADA_ROWS, ADA_TN), lambda l, j: (l, 0, j)),
        out_shape=jax.ShapeDtypeStruct((DEPTH, ADA_ROWS, n3), F32),
        compiler_params=_params(("arbitrary", "arbitrary")),
    )(c_pad, w_ada, b_ada.reshape(DEPTH, 1, n3))
    return out.reshape(DEPTH, ADA_ROWS, 1, n3)


INPROJ_TM = 512
INPROJ_WCHUNK = MXU_DIM
INPROJ_VMEM = 60 * 1024 * 1024


def _inproj_kernel(x_ref, g_ref, sc_ref, sh_ref, w_hbm, o_ref, h0_ref, h1_ref, w_ref, stage_ref, sem,
                   *, n_tiles, layer):
    t = pl.program_id(0)
    n_wchunks = w_ref.shape[0]
    per_piece = D_GROUP // INPROJ_WCHUNK

    def weight_copy(c, slot):
        cols = pl.ds(pl.multiple_of(c * INPROJ_WCHUNK, INPROJ_WCHUNK), INPROJ_WCHUNK)
        return pltpu.make_async_copy(w_hbm.at[layer, :, cols], stage_ref.at[slot], sem.at[slot])

    def load_weight():
        weight_copy(0, 0).start()

        def pair(i, carry):
            for slot in range(2):
                c = 2 * i + slot

                @pl.when(c + 1 < n_wchunks)
                def _():
                    weight_copy(c + 1, 1 - slot).start()

                weight_copy(c, slot).wait()
                w_ref[c] = stage_ref[slot].astype(BF16)
            return carry

        lax.fori_loop(0, n_wchunks // 2, pair, 0)

    def norm(h_ref):
        x = x_ref[...]
        ms = jnp.mean(x * x, axis=-1, keepdims=True)
        h = (x * lax.rsqrt(ms + EPS)) * g_ref[...]
        h = h * (1.0 + sc_ref[...]) + sh_ref[...]
        h_ref[...] = h.astype(BF16)

    def project(h_ref):
        h = h_ref[...]
        for c in range(n_wchunks):
            cols = slice((c % per_piece) * INPROJ_WCHUNK, (c % per_piece + 1) * INPROJ_WCHUNK)
            o_ref[c // per_piece, :, cols] = jnp.dot(
                h, w_ref[c], preferred_element_type=F32).astype(BF16)

    odd = t % 2 == 1

    @pl.when(t == 0)
    def _():
        load_weight()
        norm(h0_ref)

    @pl.when(odd)
    def _():
        project(h0_ref)
        norm(h1_ref)

    @pl.when(jnp.logical_and(jnp.logical_not(odd), jnp.logical_and(t > 0, t < n_tiles)))
    def _():
        project(h1_ref)
        norm(h0_ref)

    @pl.when(t == n_tiles)
    def _():
        project(h1_ref)


def _inproj(x, norm_g, mod, w_in, layer):
    batch, seq, _ = x.shape
    tm = INPROJ_TM
    per_batch = seq // tm
    n_tiles = batch * per_batch
    n_wchunks = N_PIECES * D_GROUP // INPROJ_WCHUNK
    assert n_tiles % 2 == 0 and n_wchunks % 2 == 0
    norm_tile = lambda t: jnp.minimum(t, n_tiles - 1)
    proj_tile = lambda t: jnp.maximum(t - 1, 0)
    return pl.pallas_call(
        functools.partial(_inproj_kernel, n_tiles=n_tiles, layer=layer),
        name="inproj",
        grid=(n_tiles + 1,),
        in_specs=[
            pl.BlockSpec((None, tm, D_MODEL),
                         lambda t: (norm_tile(t) // per_batch, norm_tile(t) % per_batch, 0)),
            pl.BlockSpec((None, 1, D_MODEL), lambda t: (layer, 0, 0)),
            pl.BlockSpec((None, None, 1, D_MODEL),
                         lambda t: (layer, norm_tile(t) // per_batch, 0, MOD_SCALE)),
            pl.BlockSpec((None, None, 1, D_MODEL),
                         lambda t: (layer, norm_tile(t) // per_batch, 0, MOD_SHIFT)),
            pl.BlockSpec(memory_space=pl.ANY),
        ],
        out_specs=pl.BlockSpec((N_PIECES, None, tm, D_GROUP),
                               lambda t: (0, proj_tile(t) // per_batch, proj_tile(t) % per_batch, 0)),
        out_shape=jax.ShapeDtypeStruct((N_PIECES, batch, seq, D_GROUP), BF16),
        scratch_shapes=[
            pltpu.VMEM((tm, D_MODEL), BF16), pltpu.VMEM((tm, D_MODEL), BF16),
            pltpu.VMEM((n_wchunks, D_MODEL, INPROJ_WCHUNK), BF16),
            pltpu.VMEM((2, D_MODEL, INPROJ_WCHUNK), F32),
            pltpu.SemaphoreType.DMA((2,)),
        ],
        compiler_params=_params(("arbitrary",), INPROJ_VMEM),
    )(x, norm_g.reshape(DEPTH, 1, D_MODEL), mod, mod, w_in)


OUTPROJ_TM = 512
OUTPROJ_VMEM = 60 * 1024 * 1024


def _outproj_kernel(a0_ref, a1_ref, a2_ref, w_hbm, x_ref, gate_ref, fg_ref,
                    ca_ref, cb_ref, cg_ref, cw_ref, cbias_ref, lg_ref, lb_ref, wpw_ref,
                    o_ref, u_ref, h_ref, cv0_ref, cv1_ref, w_ref, stage_ref, sem,
                    *, final, n_tiles, per_batch, seq, layer):
    t = pl.program_id(0)
    j = jnp.minimum(t, n_tiles - 1) % per_batch

    tm = o_ref.shape[0]
    n_chunks = tm // CONV_ROWS
    tn = D_MODEL // n_chunks
    n_slots = stage_ref.shape[0]

    def weight_copy(c):
        slot = c % n_slots
        return pltpu.make_async_copy(w_hbm.at[layer, :, pl.ds(c * tn, tn)], stage_ref.at[slot],
                                     sem.at[slot])

    def run(proj_cv_ref, conv_cv_ref, load_weight=False):
        if load_weight:
            for c in range(n_slots):
                weight_copy(c).start()
        if conv_cv_ref is not None:
            _conv_glu(j, ca_ref, cb_ref, u_ref, seq=seq)
        if proj_cv_ref is not None:
            a = jnp.concatenate([a0_ref[...], a1_ref[...], a2_ref[...], proj_cv_ref[...]], axis=-1)
            ss = jnp.zeros((tm, 1), F32)
        zero = 0
        for i in range(n_chunks):
            if proj_cv_ref is not None:
                y = jnp.dot(a, w_ref[i + zero], preferred_element_type=F32)
                xn = x_ref[:, i * tn:(i + 1) * tn] + gate_ref[:, i * tn:(i + 1) * tn] * y
                if final:
                    ss = ss + jnp.sum(xn * xn, axis=-1, keepdims=True)
                o_ref[:, i * tn:(i + 1) * tn] = xn
            if conv_cv_ref is not None:
                tail = _conv_rows(i * CONV_ROWS, cw_ref, cbias_ref, lg_ref, lb_ref, u_ref, h_ref)
                zero = _runtime_zero(tail)
            if load_weight:
                weight_copy(i).wait()
                w_ref[i] = stage_ref[i % n_slots].astype(BF16)
                if i + n_slots < n_chunks:
                    weight_copy(i + n_slots).start()
        if conv_cv_ref is not None:
            _conv_pointwise(cg_ref, wpw_ref, h_ref, conv_cv_ref)
        if proj_cv_ref is not None and final:
            o_ref[...] = (o_ref[...] * lax.rsqrt(ss * (1.0 / D_MODEL) + EPS)) * fg_ref[...]

    odd = t % 2 == 1

    @pl.when(t == 0)
    def _():
        run(None, cv0_ref, load_weight=True)

    @pl.when(odd)
    def _():
        run(cv0_ref, cv1_ref)

    @pl.when(jnp.logical_and(jnp.logical_not(odd), jnp.logical_and(t > 0, t < n_tiles)))
    def _():
        run(cv1_ref, cv0_ref)

    @pl.when(t == n_tiles)
    def _():
        run(cv1_ref, None)


def _outproj(mixed, zp, conv_params, w_pw_bf16, w_out, x, mod, final_g, layer, final):
    batch, seq, _ = x.shape
    tm = OUTPROJ_TM
    per_batch = seq // tm
    n_tiles = batch * per_batch
    n_wchunks = tm // CONV_ROWS
    assert n_tiles % 2 == 0
    conv_tile = lambda t: jnp.minimum(t, n_tiles - 1)
    proj_tile = lambda t: jnp.maximum(t - 1, 0)
    a_spec = pl.BlockSpec((None, tm, D_GROUP),
                          lambda t: (proj_tile(t) // per_batch, proj_tile(t) % per_batch, 0))
    x_spec = pl.BlockSpec((None, tm, D_MODEL),
                          lambda t: (proj_tile(t) // per_batch, proj_tile(t) % per_batch, 0))
    seq_spec = lambda p: pl.BlockSpec((None, None, seq, D_GROUP),
                                      lambda t: (p, conv_tile(t) // per_batch, 0, 0))
    vec = pl.BlockSpec((None, 1, D_GROUP), lambda t: (layer, 0, 0))
    conv_w, conv_b, ln_g, ln_b = conv_params
    as_rows = lambda v: v.reshape(DEPTH, 1, D_GROUP)
    return pl.pallas_call(
        functools.partial(_outproj_kernel, final=final, n_tiles=n_tiles, per_batch=per_batch, seq=seq,
                          layer=layer),
        name="outproj",
        grid=(n_tiles + 1,),
        in_specs=[
            a_spec, a_spec, a_spec,
            pl.BlockSpec(memory_space=pl.ANY),
            x_spec,
            pl.BlockSpec((None, None, 1, D_MODEL), lambda t: (layer, proj_tile(t) // per_batch, 0, MOD_GATE)),
            pl.BlockSpec((1, D_MODEL), lambda t: (0, 0)),
            seq_spec(P_CA), seq_spec(P_CB),
            pl.BlockSpec((None, None, tm, D_GROUP),
                         lambda t: (P_CG, conv_tile(t) // per_batch, conv_tile(t) % per_batch, 0)),
            pl.BlockSpec((None, CONV_WIDTH, D_GROUP), lambda t: (layer, 0, 0)),
            vec, vec, vec,
            pl.BlockSpec((None, D_GROUP, D_GROUP), lambda t: (layer, 0, 0)),
        ],
        out_specs=x_spec,
        out_shape=jax.ShapeDtypeStruct((batch, seq, D_MODEL), F32),
        scratch_shapes=[
            pltpu.VMEM((tm + 2 * CONV_HALO, D_GROUP), F32),
            pltpu.VMEM((tm, D_GROUP), BF16),
            pltpu.VMEM((tm, D_GROUP), BF16), pltpu.VMEM((tm, D_GROUP), BF16),
            pltpu.VMEM((n_wchunks, D_MODEL, D_MODEL // n_wchunks), BF16),
            pltpu.VMEM((2, D_MODEL, D_MODEL // n_wchunks), F32),
            pltpu.SemaphoreType.DMA((2,)),
        ],
        compiler_params=_params(("arbitrary",), OUTPROJ_VMEM),
    )(*mixed, w_out, x, mod, final_g.reshape(1, D_MODEL), zp, zp, zp, conv_w,
      as_rows(conv_b), as_rows(ln_g), as_rows(ln_b), w_pw_bf16)


FFT_BLOCK = 16


def _fft_constants(seq):
    rows = seq // GRID_W
    assert rows == GRID_W
    n = np.arange(GRID_W)
    ang1 = 2.0 * np.pi * ((n[:, None] * n[None, :]) % GRID_W) / GRID_W
    f1 = np.concatenate([np.cos(ang1), -np.sin(ang1)], axis=0)
    k1 = n[:, None, None]
    k2 = n[None, :, None]
    s2 = n[None, None, :]
    ang2 = 2.0 * np.pi * ((s2 * (k1 + GRID_W * k2)) % seq) / seq
    mr, mi = np.cos(ang2), -np.sin(ang2)
    m2 = np.concatenate([np.concatenate([mr, -mi], axis=2),
                         np.concatenate([mi, mr], axis=2)], axis=1)
    c = np.arange(FFT_GROUP_DIM)
    angc = 2.0 * np.pi * ((c[:, None] * c[None, :]) % FFT_GROUP_DIM) / FFT_GROUP_DIM
    fc = np.concatenate([np.cos(angc), np.sin(angc)], axis=0)
    return tuple(jnp.asarray(t, F32).astype(BF16) for t in (f1, m2, fc))


def _fft_kernel(u_ref, gate_ref, f1_ref, m2_ref, fc_ref, w_ref, o_ref, ut_ref, g_ref, gt_ref, y_ref, *, norm):
    grp = pl.program_id(1)
    n = GRID_W
    blk = FFT_BLOCK
    sub = SUBLANES_F32

    u3 = u_ref[...].astype(F32).reshape(n, n, FFT_GROUP_DIM)
    ut_ref[...] = jnp.swapaxes(u3, 0, 1).astype(BF16)

    def stage1(i, carry):
        for j in range(blk):
            s2 = i * blk + j
            g_ref[s2] = jnp.dot(f1_ref[...], ut_ref[s2], preferred_element_type=F32)
        return carry

    lax.fori_loop(0, n // blk, stage1, 0)

    for q in range(2 * n // sub):
        part = g_ref[:, q * sub:(q + 1) * sub, :]
        gt_ref[q * sub:(q + 1) * sub] = jnp.swapaxes(part, 0, 1).astype(BF16)

    def stage2(i, carry):
        lhs = []
        for j in range(blk):
            k1 = i * blk + j
            gk = jnp.concatenate([gt_ref[k1], gt_ref[n + k1]], axis=0)
            xk = jnp.dot(m2_ref[k1], gk, preferred_element_type=F32)
            lhs.append(jnp.concatenate([xk[:n], xk[n:]], axis=-1))
        lhs = jnp.concatenate(lhs, axis=0).astype(BF16)
        y = jnp.dot(lhs, fc_ref[...], preferred_element_type=F32) * norm
        y_ref[grp, pl.ds(pl.multiple_of(i * blk, blk), blk)] = y.reshape(blk, n, FFT_GROUP_DIM)
        return carry

    lax.fori_loop(0, n // blk, stage2, 0)

    @pl.when(grp == N_FFT_GROUPS - 1)
    def _():
        def stage3(i, carry):
            k2 = pl.ds(pl.multiple_of(i * sub, sub), sub)
            tiles = []
            for g in range(N_FFT_GROUPS):
                part = y_ref[g, :, k2, :]
                tiles.append(jnp.swapaxes(part, 0, 1).reshape(sub * n, FFT_GROUP_DIM))
            y = jnp.concatenate(tiles, axis=-1).astype(BF16)
            o = jnp.dot(y, w_ref[...], preferred_element_type=F32)
            rows = pl.ds(pl.multiple_of(i * sub * n, sub * n), sub * n)
            o_ref[rows, :] = (o * _silu(gate_ref[rows, :].astype(F32))).astype(BF16)
            return carry

        lax.fori_loop(0, n // sub, stage3, 0)


def _fourier(zp, w_fft_bf16, consts, layer):
    _, batch, seq, _ = zp.shape
    f1, m2, fc = consts
    n = GRID_W
    const = lambda shape: pl.BlockSpec(shape, lambda b, g: (0,) * len(shape))
    return pl.pallas_call(
        functools.partial(_fft_kernel, norm=1.0 / math.sqrt(seq * FFT_GROUP_DIM)),
        name="fourier",
        grid=(batch, N_FFT_GROUPS),
        in_specs=[
            pl.BlockSpec((None, None, seq, FFT_GROUP_DIM), lambda b, g: (P_FX, b, 0, g)),
            pl.BlockSpec((None, None, seq, D_GROUP), lambda b, g: (P_FG, b, 0, 0)),
            const((2 * n, n)), const((n, 2 * n, 2 * n)), const((2 * FFT_GROUP_DIM, FFT_GROUP_DIM)),
            pl.BlockSpec((None, D_GROUP, D_GROUP), lambda b, g: (layer, 0, 0)),
        ],
        out_specs=pl.BlockSpec((None, seq, D_GROUP), lambda b, g: (b, 0, 0)),
        out_shape=jax.ShapeDtypeStruct((batch, seq, D_GROUP), BF16),
        scratch_shapes=[
            pltpu.VMEM((n, n, FFT_GROUP_DIM), BF16),
            pltpu.VMEM((n, 2 * n, FFT_GROUP_DIM), F32),
            pltpu.VMEM((2 * n, n, FFT_GROUP_DIM), BF16),
            pltpu.VMEM((N_FFT_GROUPS, n, n, FFT_GROUP_DIM), F32),
        ],
        compiler_params=_params(("arbitrary", "arbitrary")),
    )(zp, zp, f1, m2, fc, w_fft_bf16)


NA_ROWS_PER_STEP = 8


def _na_bias_table(rel_bias):
    n_dc = 2 * NA_KW - 1
    col = np.arange(GRID_W)
    col_start = np.clip(col - NA_KW // 2, 0, GRID_W - NA_KW)
    rel_c = col[None, :] - col_start[:, None]
    col_in = (rel_c >= 0) & (rel_c < NA_KW)
    dc = np.clip(col[None, :] - col[:, None] + NA_KW - 1, 0, n_dc - 1)
    onehot = (dc[None] == np.arange(n_dc)[:, None, None]).astype(np.float32)
    zeros = np.zeros_like(onehot)
    onehot2 = np.concatenate([np.concatenate([onehot, zeros], axis=-1),
                              np.concatenate([zeros, onehot], axis=-1)], axis=0)
    rb = rel_bias.astype(F32)
    rb2 = jnp.concatenate([rb[:, :, :-1], rb[:, :, 1:]], axis=-1)
    tbl = jnp.einsum('lhdj,jqk->ldhqk', rb2, jnp.asarray(onehot2),
                     precision=lax.Precision.HIGHEST)
    col_in2 = np.concatenate([col_in, col_in], axis=-1)
    return jnp.where(col_in2[None, None, None], tbl, NEG_BIG)


def _head_select_mask():
    r = lax.broadcasted_iota(jnp.int32, (MXU_DIM, MXU_DIM), 0) // HEAD_DIM
    c = lax.broadcasted_iota(jnp.int32, (MXU_DIM, MXU_DIM), 1) // HEAD_DIM
    return r == c


def _stack_heads(t):
    n = t.shape[0]
    reps = jnp.concatenate([t] * HEADS_PER_TILE, axis=0)
    row_head = lax.broadcasted_iota(jnp.int32, reps.shape, 0) // n
    lane_head = lax.broadcasted_iota(jnp.int32, reps.shape, 1) // HEAD_DIM
    return jnp.where(row_head == lane_head, reps, jnp.zeros_like(reps))


def _unstack_heads(t4, n):
    lane_head = lax.broadcasted_iota(jnp.int32, (n, MXU_DIM), 1) // HEAD_DIM
    out = jnp.zeros((n, MXU_DIM), t4.dtype)
    for h in range(HEADS_PER_TILE):
        out = jnp.where(lane_head == h, t4[h * n:(h + 1) * n], out)
    return out


def _na_kernel(q_ref, k_ref, v_ref, g_ref, bias_ref, o_ref, *, rows):
    kh = min(NA_KH, rows)
    step = pl.program_id(1)

    def row_body(rr, carry):
        r = step * NA_ROWS_PER_STEP + rr
        r_start = jnp.clip(r - kh // 2, 0, rows - kh)
        variant = r - r_start
        q_off = pl.multiple_of(rr * GRID_W, GRID_W)
        k_off = pl.multiple_of(r_start * GRID_W, GRID_W)
        for cg in range(D_GROUP // MXU_DIM):
            lanes = slice(cg * MXU_DIM, (cg + 1) * MXU_DIM)
            q = q_ref[pl.ds(q_off, GRID_W), lanes] * (HEAD_DIM ** -0.5)
            kw = k_ref[pl.ds(k_off, kh * GRID_W), lanes]
            vw = v_ref[pl.ds(k_off, kh * GRID_W), lanes]
            q4 = _stack_heads(q)
            s = lax.dot_general(q4, kw, (((1,), (1,)), ((), ())), preferred_element_type=F32)
            heads = slice(cg * HEADS_PER_TILE, (cg + 1) * HEADS_PER_TILE)
            bias = jnp.concatenate(
                [bias_ref[NA_KH - 1 - variant + a, heads].reshape(HEADS_PER_TILE * GRID_W, 2 * GRID_W)
                 for a in range(0, kh, 2)], axis=-1)
            s = s + bias
            m = jnp.max(s, axis=-1, keepdims=True)
            e = jnp.exp(s - m)
            l = jnp.sum(e, axis=-1, keepdims=True)
            o4 = jnp.dot(e.astype(BF16), vw, preferred_element_type=F32) / l
            o = _unstack_heads(o4, GRID_W)
            gate = g_ref[pl.ds(q_off, GRID_W), lanes].astype(F32)
            o_ref[pl.ds(q_off, GRID_W), lanes] = (o * _silu(gate)).astype(BF16)
        return carry

    lax.fori_loop(0, NA_ROWS_PER_STEP, row_body, 0, unroll=True)


def _neighbourhood(zp, bias_tbl, layer):
    _, batch, seq, _ = zp.shape
    rows = seq // GRID_W
    tq = NA_ROWS_PER_STEP * GRID_W
    q_spec = lambda p: pl.BlockSpec((None, None, tq, D_GROUP), lambda b, i: (p, b, i, 0))
    kv_spec = lambda p: pl.BlockSpec((None, None, seq, D_GROUP), lambda b, i: (p, b, 0, 0))
    return pl.pallas_call(
        functools.partial(_na_kernel, rows=rows),
        name="natten",
        grid=(batch, rows // NA_ROWS_PER_STEP),
        in_specs=[
            q_spec(P_NQ), kv_spec(P_NK), kv_spec(P_NV), q_spec(P_NG),
            pl.BlockSpec((None,) + bias_tbl.shape[1:], lambda b, i: (layer, 0, 0, 0, 0),
                         pipeline_mode=pl.Buffered(1)),
        ],
        out_specs=pl.BlockSpec((None, tq, D_GROUP), lambda b, i: (b, i, 0)),
        out_shape=jax.ShapeDtypeStruct((batch, seq, D_GROUP), BF16),
        compiler_params=_params(("arbitrary", "arbitrary")),
    )(zp, zp, zp, zp, bias_tbl)


RET_CHUNK = MXU_DIM


def _log_sigmoid(t):
    return jnp.minimum(t, 0.0) - jnp.log1p(jnp.exp(-jnp.abs(t)))


def _rope_tables(seq):
    half = HEAD_DIM // 2
    inv = ROPE_BASE ** (-jnp.arange(half, dtype=F32) / half)
    ang = jnp.arange(seq, dtype=F32)[:, None] * inv[None, :]
    cos, sin = jnp.cos(ang), jnp.sin(ang)
    cos2 = jnp.concatenate([cos, cos], axis=-1)
    sin2 = jnp.concatenate([-sin, sin], axis=-1)
    return jnp.tile(cos2, (1, 2)), jnp.tile(sin2, (1, 2))


def _ret_kernel(lf_s_ref, lb_s_ref, q_ref, k_ref, v_ref, g_ref, cos_ref, sin_ref, lfl_ref, lbl_ref,
                o_ref, qr_ref, kr_ref, ob_ref, dmat_ref, dec_ref, state_ref, ones_ref, *, seq, layer):
    C = RET_CHUNK
    n_chunks = seq // C
    cg = pl.program_id(1)
    ones_bd = jnp.where(_head_select_mask(), 1.0, 0.0).astype(BF16)
    ones_ref[...] = jnp.concatenate([ones_bd, ones_bd], axis=0)

    ri = lax.broadcasted_iota(jnp.int32, (C, C), 0)
    ci = lax.broadcasted_iota(jnp.int32, (C, C), 1)
    diff = (ri - ci).astype(F32)
    for hh in range(HEADS_PER_TILE):
        h = layer * N_HEADS + cg * HEADS_PER_TILE + hh
        lf = _log_sigmoid(jnp.full((C, C), lf_s_ref[h], F32))
        lb = _log_sigmoid(jnp.full((C, C), lb_s_ref[h], F32))
        dmat_ref[hh] = jnp.where(diff >= 0, jnp.exp(lf * diff), jnp.exp(lb * (-diff)))
    lfl = _log_sigmoid(lfl_ref[...])
    lbl = _log_sigmoid(lbl_ref[...])
    idx = lax.broadcasted_iota(jnp.int32, (C, MXU_DIM), 0).astype(F32)
    dec_ref[0] = jnp.exp(lfl * (idx + 1.0))
    dec_ref[1] = jnp.exp(lfl * (C - 1.0 - idx))
    dec_ref[2] = jnp.exp(lbl * (C - idx))
    dec_ref[3] = jnp.exp(lbl * idx)
    cd_f = jnp.exp(lfl * float(C))
    cd_b = jnp.exp(lbl * float(C))

    def rope(t, rows):
        lane = lax.broadcasted_iota(jnp.int32, t.shape, 1)
        first_half = (lane % HEAD_DIM) < (HEAD_DIM // 2)
        cs = cos_ref[rows, :]
        sn = sin_ref[rows, :]
        cs = jnp.concatenate([cs, cs], axis=-1)
        sn = jnp.concatenate([sn, sn], axis=-1)
        swapped = jnp.where(first_half, pltpu.roll(t, MXU_DIM - HEAD_DIM // 2, 1),
                            pltpu.roll(t, HEAD_DIM // 2, 1))
        return t * cs + swapped * sn

    def kv_update(k, v, kd, cd):
        kv = lax.dot_general((k * kd).astype(BF16), v, (((0,), (0,)), ((), ())),
                             preferred_element_type=F32)
        state_ref[...] = cd * state_ref[...] + jnp.where(_head_select_mask(), kv, 0.0)

    state_ref[...] = jnp.zeros_like(state_ref)

    def bwd_body(i, carry):
        n = n_chunks - 1 - i
        rows = pl.ds(pl.multiple_of(n * C, C), C)
        q = rope(q_ref[rows, :].astype(F32), rows) * (HEAD_DIM ** -0.5)
        k = rope(k_ref[rows, :].astype(F32), rows)
        qr_ref[rows, :] = q.astype(BF16)
        kr_ref[rows, :] = k.astype(BF16)
        ob_ref[rows, :] = jnp.dot((q * dec_ref[2]).astype(BF16), state_ref[...].astype(BF16),
                                  preferred_element_type=F32)
        kv_update(k, v_ref[rows, :], dec_ref[3], cd_b)
        return carry

    lax.fori_loop(0, n_chunks, bwd_body, 0, unroll=4)

    state_ref[...] = jnp.zeros_like(state_ref)

    def fwd_body(n, carry):
        rows = pl.ds(pl.multiple_of(n * C, C), C)
        qb = qr_ref[rows, :]
        kb = kr_ref[rows, :]
        v = v_ref[rows, :]
        o_cross = jnp.dot((qb.astype(F32) * dec_ref[0]).astype(BF16), state_ref[...].astype(BF16),
                          preferred_element_type=F32)
        q4 = _stack_heads(qb)
        sc = lax.dot_general(q4, kb, (((1,), (1,)), ((), ())), preferred_element_type=F32)
        sc = sc * dmat_ref[...].reshape(HEADS_PER_TILE * C, C)
        o4 = jnp.dot(sc.astype(BF16), v, preferred_element_type=F32)
        o = _unstack_heads(o4, C) + o_cross + ob_ref[rows, :]
        sq = o * o
        hi = sq.astype(BF16)
        lo = (sq - hi.astype(F32)).astype(BF16)
        ss = jnp.dot(jnp.concatenate([hi, lo], axis=-1), ones_ref[...], preferred_element_type=F32)
        inv = lax.rsqrt(ss * (1.0 / HEAD_DIM) + EPS)
        gate = g_ref[rows, :].astype(F32)
        o_ref[rows, :] = (o * inv * _silu(gate)).astype(BF16)
        kv_update(kb.astype(F32), v, dec_ref[1], cd_f)
        return carry

    lax.fori_loop(0, n_chunks, fwd_body, 0, unroll=4)


def _retention(zp, logit_f, logit_b, rope_tbl, layer):
    _, batch, seq, _ = zp.shape
    cos_t, sin_t = rope_tbl
    n_cg = D_GROUP // MXU_DIM
    spec = lambda p: pl.BlockSpec((None, None, seq, MXU_DIM), lambda b, c, *_: (p, b, 0, c))
    tbl_spec = pl.BlockSpec((seq, 2 * HEAD_DIM), lambda b, c, *_: (0, 0))
    lane_spec = pl.BlockSpec((None, 1, MXU_DIM), lambda b, c, *_: (layer, 0, c))
    per_lane = lambda v: jnp.repeat(v.astype(F32), HEAD_DIM, axis=1)[:, None, :]
    return pl.pallas_call(
        functools.partial(_ret_kernel, seq=seq, layer=layer),
        name="retention",
        grid_spec=pltpu.PrefetchScalarGridSpec(
            num_scalar_prefetch=2,
            grid=(batch, n_cg),
            in_specs=[spec(P_RQ), spec(P_RK), spec(P_RV), spec(P_RG), tbl_spec, tbl_spec,
                      lane_spec, lane_spec],
            out_specs=pl.BlockSpec((None, seq, MXU_DIM), lambda b, c, *_: (b, 0, c)),
            scratch_shapes=[
                pltpu.VMEM((seq, MXU_DIM), BF16),
                pltpu.VMEM((seq, MXU_DIM), BF16),
                pltpu.VMEM((seq, MXU_DIM), F32),
                pltpu.VMEM((HEADS_PER_TILE, RET_CHUNK, RET_CHUNK), F32),
                pltpu.VMEM((4, RET_CHUNK, MXU_DIM), F32),
                pltpu.VMEM((MXU_DIM, MXU_DIM), F32),
                pltpu.VMEM((2 * MXU_DIM, MXU_DIM), BF16),
            ],
        ),
        out_shape=jax.ShapeDtypeStruct((batch, seq, D_GROUP), BF16),
        compiler_params=_params(("arbitrary", "arbitrary")),
    )(logit_f.astype(F32).reshape(-1), logit_b.astype(F32).reshape(-1), zp, zp, zp, zp, cos_t, sin_t,
      per_lane(logit_f), per_lane(logit_b))


CONV_HALO = SUBLANES_BF16
CONV_ROWS = 64


def _conv_glu(j, a_ref, b_ref, u_ref, *, seq):
    halo = CONV_HALO
    ts = u_ref.shape[0] - 2 * halo
    n_tiles = seq // ts

    def glu(start, size):
        rows = pl.ds(pl.multiple_of(start, halo), size)
        a = a_ref[rows, :].astype(F32)
        return a * _sigmoid(b_ref[rows, :].astype(F32))

    base = j * ts
    u_ref[halo:halo + ts, :] = glu(base, ts)
    lo = glu(jnp.maximum(base - halo, 0), halo)
    u_ref[0:halo, :] = jnp.where(j > 0, lo, 0.0)
    hi = glu(jnp.minimum(base + ts, seq - halo), halo)
    u_ref[halo + ts:, :] = jnp.where(j < n_tiles - 1, hi, 0.0)


def _conv_rows(s0, cw_ref, cb_ref, lg_ref, lb_ref, u_ref, h_ref):
    rb = CONV_ROWS
    first = CONV_HALO - CONV_HALF
    parts = []
    for lt in range(D_GROUP // LANES):
        lanes = slice(lt * LANES, (lt + 1) * LANES)
        y = None
        for r in range(SUBLANES_F32):
            acc = None
            for o in range(r, first + CONV_WIDTH, SUBLANES_F32):
                if o < first:
                    continue
                w = o - first
                lo_row = s0 + (o - r)
                term = u_ref[lo_row:lo_row + rb + SUBLANES_F32, lanes] * cw_ref[w:w + 1, lanes]
                acc = term if acc is None else acc + term
            shifted = acc[r:r + rb]
            y = shifted if y is None else y + shifted
        parts.append(y)
    y = jnp.concatenate(parts, axis=-1) + cb_ref[...]
    mu = jnp.mean(y, axis=-1, keepdims=True)
    yc = y - mu
    var = jnp.mean(yc * yc, axis=-1, keepdims=True)
    y = (yc * lax.rsqrt(var + EPS)) * lg_ref[...] + lb_ref[...]
    h_ref[s0:s0 + rb, :] = _silu(y).astype(BF16)
    return y[rb - SUBLANES_F32:, D_GROUP - LANES:]


def _runtime_zero(tile):
    bits = pltpu.bitcast(tile, jnp.uint32)
    bits = lax.shift_right_logical(lax.shift_right_logical(bits, jnp.uint32(16)), jnp.uint32(16))
    return bits[0, 0].astype(jnp.int32)


def _conv_pointwise(g_ref, w_ref, h_ref, o_ref):
    o = jnp.dot(h_ref[...], w_ref[...], preferred_element_type=F32)
    o_ref[...] = (o * _silu(g_ref[...].astype(F32))).astype(BF16)


def kernel(x, c, norm_g, w_ada, b_ada, w_in, w_fft, na_rel_bias, ret_logit_fwd, ret_logit_bwd,
           conv_w, conv_b, conv_ln_g, conv_ln_b, conv_w_pw, w_out, final_g):
    seq = x.shape[1]
    mod = _ada(c, w_ada, b_ada)
    fft_consts = _fft_constants(seq)
    rope_tbl = _rope_tables(seq)
    bias_tbl = _na_bias_table(na_rel_bias)
    w_fft_b, w_pw_b = w_fft.astype(BF16), conv_w_pw.astype(BF16)
    for l in range(DEPTH):
        zp = _inproj(x, norm_g, mod, w_in, l)
        o_fft = _fourier(zp, w_fft_b, fft_consts, l)
        o_na = _neighbourhood(zp, bias_tbl, l)
        o_ret = _retention(zp, ret_logit_fwd, ret_logit_bwd, rope_tbl, l)
        conv_params = (conv_w, conv_b, conv_ln_g, conv_ln_b)
        x = _outproj((o_fft, o_na, o_ret), zp, conv_params, w_pw_b, w_out, x, mod, final_g, l,
                     final=(l == DEPTH - 1))
    return x
```

```python
import functools
import math

import numpy as np
import jax
import jax.numpy as jnp
from jax import lax
from jax.experimental import pallas as pl
from jax.experimental.pallas import tpu as pltpu

F32 = jnp.float32
BF16 = jnp.bfloat16

D_MODEL = 2048
DEPTH = 2
GRID_W = 64
D_GROUP = 512
HEAD_DIM = 64
N_HEADS = D_GROUP // HEAD_DIM
N_FFT_GROUPS = 4
FFT_GROUP_DIM = D_GROUP // N_FFT_GROUPS
NA_KH = 8
NA_KW = 16
CONV_WIDTH = 31
CONV_HALF = CONV_WIDTH // 2
ROPE_BASE = 10000.0
EPS = 1e-6
N_PIECES = 13
(P_FX, P_FG, P_NQ, P_NK, P_NV, P_NG, P_RQ, P_RK, P_RV, P_RG, P_CA, P_CB, P_CG) = range(N_PIECES)

MXU_DIM = 256
HEADS_PER_TILE = MXU_DIM // HEAD_DIM
VMEM_LIMIT = 56 * 1024 * 1024
SUBLANES_BF16 = 16
SUBLANES_F32 = 8
LANES = 128

NEG_BIG = -1e30


def _params(sem, vmem=VMEM_LIMIT):
    return pltpu.CompilerParams(dimension_semantics=sem, vmem_limit_bytes=vmem)


def _sigmoid(t):
    return 0.5 * jnp.tanh(0.5 * t) + 0.5


def _silu(t):
    h = 0.5 * t
    return h + h * jnp.tanh(h)


ADA_TN = 768
ADA_ROWS = 8
(MOD_SHIFT, MOD_SCALE, MOD_GATE) = range(3)


def _ada_kernel(c_ref, w_ref, b_ref, o_ref):
    ca = _silu(c_ref[...]).astype(BF16)
    w = w_ref[...].astype(BF16)
    o_ref[...] = jnp.dot(ca, w, preferred_element_type=F32) + b_ref[...]


def _ada(c, w_ada, b_ada):
    batch = c.shape[0]
    c_pad = jnp.zeros((ADA_ROWS, D_MODEL), F32).at[:batch].set(c)
    n3 = 3 * D_MODEL
    out = pl.pallas_call(
        _ada_kernel,
        name="ada",
        grid=(DEPTH, n3 // ADA_TN),
        in_specs=[
            pl.BlockSpec((ADA_ROWS, D_MODEL), lambda l, j: (0, 0)),
            pl.BlockSpec((None, D_MODEL, ADA_TN), lambda l, j: (l, 0, j)),
            pl.BlockSpec((None, 1, ADA_TN), lambda l, j: (l, 0, j)),
        ],
        out_specs=pl.BlockSpec((None, ADA_ROWS, ADA_TN), lambda l, j: (l, 0, j)),
        out_shape=jax.ShapeDtypeStruct((DEPTH, ADA_ROWS, n3), F32),
        compiler_params=_params(("arbitrary", "arbitrary")),
    )(c_pad, w_ada, b_ada.reshape(DEPTH, 1, n3))
    return out.reshape(DEPTH, ADA_ROWS, 1, n3)


INPROJ_TM = 512
INPROJ_WCHUNK = MXU_DIM
INPROJ_WROWS = 64
INPROJ_VMEM = 60 * 1024 * 1024


def _inproj_kernel(x_ref, g_ref, sc_ref, sh_ref, w_hbm, o_ref, h0_ref, h1_ref, w_ref, stage_ref, sem,
                   *, n_tiles, layer):
    t = pl.program_id(0)
    n_wchunks = w_ref.shape[0]
    per_piece = D_GROUP // INPROJ_WCHUNK

    wrows = stage_ref.shape[1]
    n_rchunks = D_MODEL // wrows

    def weight_copy(r, slot):
        rows = pl.ds(pl.multiple_of(r * wrows, wrows), wrows)
        return pltpu.make_async_copy(w_hbm.at[layer, rows, :], stage_ref.at[slot], sem.at[slot])

    def load_weight():
        weight_copy(0, 0).start()

        def pair(i, carry):
            for slot in range(2):
                r = 2 * i + slot

                @pl.when(r + 1 < n_rchunks)
                def _():
                    weight_copy(r + 1, 1 - slot).start()

                weight_copy(r, slot).wait()
                rows = pl.ds(pl.multiple_of(r * wrows, wrows), wrows)
                for c in range(n_wchunks):
                    w_ref[c, rows, :] = stage_ref[
                        slot, :, c * INPROJ_WCHUNK:(c + 1) * INPROJ_WCHUNK].astype(BF16)
            return carry

        lax.fori_loop(0, n_rchunks // 2, pair, 0)

    def norm(h_ref):
        x = x_ref[...]
        ms = jnp.mean(x * x, axis=-1, keepdims=True)
        h = (x * lax.rsqrt(ms + EPS)) * g_ref[...]
        h = h * (1.0 + sc_ref[...]) + sh_ref[...]
        h_ref[...] = h.astype(BF16)

    def project(h_ref):
        h = h_ref[...]
        for c in range(n_wchunks):
            cols = slice((c % per_piece) * INPROJ_WCHUNK, (c % per_piece + 1) * INPROJ_WCHUNK)
            o_ref[c // per_piece, :, cols] = jnp.dot(
                h, w_ref[c], preferred_element_type=F32).astype(BF16)

    odd = t % 2 == 1

    @pl.when(t == 0)
    def _():
        load_weight()
        norm(h0_ref)

    @pl.when(odd)
    def _():
        project(h0_ref)
        norm(h1_ref)

    @pl.when(jnp.logical_and(jnp.logical_not(odd), jnp.logical_and(t > 0, t < n_tiles)))
    def _():
        project(h1_ref)
        norm(h0_ref)

    @pl.when(t == n_tiles)
    def _():
        project(h1_ref)


def _inproj(x, norm_g, mod, w_in, layer):
    batch, seq, _ = x.shape
    tm = INPROJ_TM
    per_batch = seq // tm
    n_tiles = batch * per_batch
    n_wchunks = N_PIECES * D_GROUP // INPROJ_WCHUNK
    assert n_tiles % 2 == 0 and (D_MODEL // INPROJ_WROWS) % 2 == 0
    norm_tile = lambda t: jnp.minimum(t, n_tiles - 1)
    proj_tile = lambda t: jnp.maximum(t - 1, 0)
    return pl.pallas_call(
        functools.partial(_inproj_kernel, n_tiles=n_tiles, layer=layer),
        name="inproj",
        grid=(n_tiles + 1,),
        in_specs=[
            pl.BlockSpec((None, tm, D_MODEL),
                         lambda t: (norm_tile(t) // per_batch, norm_tile(t) % per_batch, 0)),
            pl.BlockSpec((None, 1, D_MODEL), lambda t: (layer, 0, 0)),
            pl.BlockSpec((None, None, 1, D_MODEL),
                         lambda t: (layer, norm_tile(t) // per_batch, 0, MOD_SCALE)),
            pl.BlockSpec((None, None, 1, D_MODEL),
                         lambda t: (layer, norm_tile(t) // per_batch, 0, MOD_SHIFT)),
            pl.BlockSpec(memory_space=pl.ANY),
        ],
        out_specs=pl.BlockSpec((N_PIECES, None, tm, D_GROUP),
                               lambda t: (0, proj_tile(t) // per_batch, proj_tile(t) % per_batch, 0)),
        out_shape=jax.ShapeDtypeStruct((N_PIECES, batch, seq, D_GROUP), BF16),
        scratch_shapes=[
            pltpu.VMEM((tm, D_MODEL), BF16), pltpu.VMEM((tm, D_MODEL), BF16),
            pltpu.VMEM((n_wchunks, D_MODEL, INPROJ_WCHUNK), BF16),
            pltpu.VMEM((2, INPROJ_WROWS, N_PIECES * D_GROUP), F32),
            pltpu.SemaphoreType.DMA((2,)),
        ],
        compiler_params=_params(("arbitrary",), INPROJ_VMEM),
    )(x, norm_g.reshape(DEPTH, 1, D_MODEL), mod, mod, w_in)


OUTPROJ_TM = 512
OUTPROJ_VMEM = 60 * 1024 * 1024


def _outproj_kernel(a0_ref, a1_ref, a2_ref, w_hbm, x_ref, gate_ref, fg_ref,
                    ca_ref, cb_ref, cg_ref, cw_ref, cbias_ref, lg_ref, lb_ref, wpw_ref,
                    o_ref, u_ref, h_ref, cv0_ref, cv1_ref, w_ref, stage_ref, sem,
                    *, final, n_tiles, per_batch, seq, layer):
    t = pl.program_id(0)
    j = jnp.minimum(t, n_tiles - 1) % per_batch

    tm = o_ref.shape[0]
    n_chunks = tm // CONV_ROWS
    tn = D_MODEL // n_chunks
    n_slots = stage_ref.shape[0]

    wrows = stage_ref.shape[1]

    def weight_copy(r):
        slot = r % n_slots
        return pltpu.make_async_copy(w_hbm.at[layer, pl.ds(r * wrows, wrows), :], stage_ref.at[slot],
                                     sem.at[slot])

    def run(proj_cv_ref, conv_cv_ref, load_weight=False):
        if load_weight:
            for c in range(n_slots):
                weight_copy(c).start()
        if conv_cv_ref is not None:
            _conv_glu(j, ca_ref, cb_ref, u_ref, seq=seq)
        if proj_cv_ref is not None:
            a = jnp.concatenate([a0_ref[...], a1_ref[...], a2_ref[...], proj_cv_ref[...]], axis=-1)
            ss = jnp.zeros((tm, 1), F32)
        zero = 0
        for i in range(n_chunks):
            if proj_cv_ref is not None:
                y = jnp.dot(a, w_ref[i + zero], preferred_element_type=F32)
                xn = x_ref[:, i * tn:(i + 1) * tn] + gate_ref[:, i * tn:(i + 1) * tn] * y
                if final:
                    ss = ss + jnp.sum(xn * xn, axis=-1, keepdims=True)
                o_ref[:, i * tn:(i + 1) * tn] = xn
            if conv_cv_ref is not None:
                tail = _conv_rows(i * CONV_ROWS, cw_ref, cbias_ref, lg_ref, lb_ref, u_ref, h_ref)
                zero = _runtime_zero(tail)
            if load_weight:
                weight_copy(i).wait()
                for c in range(n_chunks):
                    w_ref[c, i * wrows:(i + 1) * wrows, :] = stage_ref[
                        i % n_slots, :, c * tn:(c + 1) * tn].astype(BF16)
                if i + n_slots < n_chunks:
                    weight_copy(i + n_slots).start()
        if conv_cv_ref is not None:
            _conv_pointwise(cg_ref, wpw_ref, h_ref, conv_cv_ref)
        if proj_cv_ref is not None and final:
            o_ref[...] = (o_ref[...] * lax.rsqrt(ss * (1.0 / D_MODEL) + EPS)) * fg_ref[...]

    odd = t % 2 == 1

    @pl.when(t == 0)
    def _():
        run(None, cv0_ref, load_weight=True)

    @pl.when(odd)
    def _():
        run(cv0_ref, cv1_ref)

    @pl.when(jnp.logical_and(jnp.logical_not(odd), jnp.logical_and(t > 0, t < n_tiles)))
    def _():
        run(cv1_ref, cv0_ref)

    @pl.when(t == n_tiles)
    def _():
        run(cv1_ref, None)


def _outproj(mixed, zp, conv_params, w_pw_bf16, w_out, x, mod, final_g, layer, final):
    batch, seq, _ = x.shape
    tm = OUTPROJ_TM
    per_batch = seq // tm
    n_tiles = batch * per_batch
    n_wchunks = tm // CONV_ROWS
    assert n_tiles % 2 == 0
    conv_tile = lambda t: jnp.minimum(t, n_tiles - 1)
    proj_tile = lambda t: jnp.maximum(t - 1, 0)
    a_spec = pl.BlockSpec((None, tm, D_GROUP),
                          lambda t: (proj_tile(t) // per_batch, proj_tile(t) % per_batch, 0))
    x_spec = pl.BlockSpec((None, tm, D_MODEL),
                          lambda t: (proj_tile(t) // per_batch, proj_tile(t) % per_batch, 0))
    seq_spec = lambda p: pl.BlockSpec((None, None, seq, D_GROUP),
                                      lambda t: (p, conv_tile(t) // per_batch, 0, 0))
    vec = pl.BlockSpec((None, 1, D_GROUP), lambda t: (layer, 0, 0))
    conv_w, conv_b, ln_g, ln_b = conv_params
    as_rows = lambda v: v.reshape(DEPTH, 1, D_GROUP)
    return pl.pallas_call(
        functools.partial(_outproj_kernel, final=final, n_tiles=n_tiles, per_batch=per_batch, seq=seq,
                          layer=layer),
        name="outproj",
        grid=(n_tiles + 1,),
        in_specs=[
            a_spec, a_spec, a_spec,
            pl.BlockSpec(memory_space=pl.ANY),
            x_spec,
            pl.BlockSpec((None, None, 1, D_MODEL), lambda t: (layer, proj_tile(t) // per_batch, 0, MOD_GATE)),
            pl.BlockSpec((1, D_MODEL), lambda t: (0, 0)),
            seq_spec(P_CA), seq_spec(P_CB),
            pl.BlockSpec((None, None, tm, D_GROUP),
                         lambda t: (P_CG, conv_tile(t) // per_batch, conv_tile(t) % per_batch, 0)),
            pl.BlockSpec((None, CONV_WIDTH, D_GROUP), lambda t: (layer, 0, 0)),
            vec, vec, vec,
            pl.BlockSpec((None, D_GROUP, D_GROUP), lambda t: (layer, 0, 0)),
        ],
        out_specs=x_spec,
        out_shape=jax.ShapeDtypeStruct((batch, seq, D_MODEL), F32),
        scratch_shapes=[
            pltpu.VMEM((tm + 2 * CONV_HALO, D_GROUP), F32),
            pltpu.VMEM((tm, D_GROUP), BF16),
            pltpu.VMEM((tm, D_GROUP), BF16), pltpu.VMEM((tm, D_GROUP), BF16),
            pltpu.VMEM((n_wchunks, D_MODEL, D_MODEL // n_wchunks), BF16),
            pltpu.VMEM((2, D_MODEL // n_wchunks, D_MODEL), F32),
            pltpu.SemaphoreType.DMA((2,)),
        ],
        compiler_params=_params(("arbitrary",), OUTPROJ_VMEM),
    )(*mixed, w_out, x, mod, final_g.reshape(1, D_MODEL), zp, zp, zp, conv_w,
      as_rows(conv_b), as_rows(ln_g), as_rows(ln_b), w_pw_bf16)


FFT_BLOCK = 16


def _fft_constants(seq):
    rows = seq // GRID_W
    assert rows == GRID_W
    n = np.arange(GRID_W)
    ang1 = 2.0 * np.pi * ((n[:, None] * n[None, :]) % GRID_W) / GRID_W
    f1 = np.concatenate([np.cos(ang1), -np.sin(ang1)], axis=0)
    k1 = n[:, None, None]
    k2 = n[None, :, None]
    s2 = n[None, None, :]
    ang2 = 2.0 * np.pi * ((s2 * (k1 + GRID_W * k2)) % seq) / seq
    mr, mi = np.cos(ang2), -np.sin(ang2)
    m2 = np.concatenate([np.concatenate([mr, -mi], axis=2),
                         np.concatenate([mi, mr], axis=2)], axis=1)
    c = np.arange(FFT_GROUP_DIM)
    angc = 2.0 * np.pi * ((c[:, None] * c[None, :]) % FFT_GROUP_DIM) / FFT_GROUP_DIM
    fc = np.concatenate([np.cos(angc), np.sin(angc)], axis=0)
    return tuple(jnp.asarray(t, F32).astype(BF16) for t in (f1, m2, fc))


def _fft_kernel(u_ref, gate_ref, f1_ref, m2_ref, fc_ref, w_ref, o_ref, ut_ref, g_ref, gt_ref, y_ref, *, norm):
    grp = pl.program_id(1)
    n = GRID_W
    blk = FFT_BLOCK
    sub = SUBLANES_F32

    u3 = u_ref[...].astype(F32).reshape(n, n, FFT_GROUP_DIM)
    ut_ref[...] = jnp.swapaxes(u3, 0, 1).astype(BF16)

    def stage1(i, carry):
        for j in range(blk):
            s2 = i * blk + j
            g_ref[s2] = jnp.dot(f1_ref[...], ut_ref[s2], preferred_element_type=F32)
        return carry

    lax.fori_loop(0, n // blk, stage1, 0)

    for q in range(2 * n // sub):
        part = g_ref[:, q * sub:(q + 1) * sub, :]
        gt_ref[q * sub:(q + 1) * sub] = jnp.swapaxes(part, 0, 1).astype(BF16)

    def stage2(i, carry):
        lhs = []
        for j in range(blk):
            k1 = i * blk + j
            gk = jnp.concatenate([gt_ref[k1], gt_ref[n + k1]], axis=0)
            xk = jnp.dot(m2_ref[k1], gk, preferred_element_type=F32)
            lhs.append(jnp.concatenate([xk[:n], xk[n:]], axis=-1))
        lhs = jnp.concatenate(lhs, axis=0).astype(BF16)
        y = jnp.dot(lhs, fc_ref[...], preferred_element_type=F32) * norm
        y_ref[grp, pl.ds(pl.multiple_of(i * blk, blk), blk)] = y.reshape(blk, n, FFT_GROUP_DIM)
        return carry

    lax.fori_loop(0, n // blk, stage2, 0)

    @pl.when(grp == N_FFT_GROUPS - 1)
    def _():
        def stage3(i, carry):
            k2 = pl.ds(pl.multiple_of(i * sub, sub), sub)
            tiles = []
            for g in range(N_FFT_GROUPS):
                part = y_ref[g, :, k2, :]
                tiles.append(jnp.swapaxes(part, 0, 1).reshape(sub * n, FFT_GROUP_DIM))
            y = jnp.concatenate(tiles, axis=-1).astype(BF16)
            o = jnp.dot(y, w_ref[...], preferred_element_type=F32)
            rows = pl.ds(pl.multiple_of(i * sub * n, sub * n), sub * n)
            o_ref[rows, :] = (o * _silu(gate_ref[rows, :].astype(F32))).astype(BF16)
            return carry

        lax.fori_loop(0, n // sub, stage3, 0)


def _fourier(zp, w_fft_bf16, consts, layer):
    _, batch, seq, _ = zp.shape
    f1, m2, fc = consts
    n = GRID_W
    const = lambda shape: pl.BlockSpec(shape, lambda b, g: (0,) * len(shape))
    return pl.pallas_call(
        functools.partial(_fft_kernel, norm=1.0 / math.sqrt(seq * FFT_GROUP_DIM)),
        name="fourier",
        grid=(batch, N_FFT_GROUPS),
        in_specs=[
            pl.BlockSpec((None, None, seq, FFT_GROUP_DIM), lambda b, g: (P_FX, b, 0, g)),
            pl.BlockSpec((None, None, seq, D_GROUP), lambda b, g: (P_FG, b, 0, 0)),
            const((2 * n, n)), const((n, 2 * n, 2 * n)), const((2 * FFT_GROUP_DIM, FFT_GROUP_DIM)),
            pl.BlockSpec((None, D_GROUP, D_GROUP), lambda b, g: (layer, 0, 0)),
        ],
        out_specs=pl.BlockSpec((None, seq, D_GROUP), lambda b, g: (b, 0, 0)),
        out_shape=jax.ShapeDtypeStruct((batch, seq, D_GROUP), BF16),
        scratch_shapes=[
            pltpu.VMEM((n, n, FFT_GROUP_DIM), BF16),
            pltpu.VMEM((n, 2 * n, FFT_GROUP_DIM), F32),
            pltpu.VMEM((2 * n, n, FFT_GROUP_DIM), BF16),
            pltpu.VMEM((N_FFT_GROUPS, n, n, FFT_GROUP_DIM), F32),
        ],
        compiler_params=_params(("arbitrary", "arbitrary")),
    )(zp, zp, f1, m2, fc, w_fft_bf16)


NA_ROWS_PER_STEP = 8


def _na_bias_table(rel_bias):
    n_dc = 2 * NA_KW - 1
    col = np.arange(GRID_W)
    col_start = np.clip(col - NA_KW // 2, 0, GRID_W - NA_KW)
    rel_c = col[None, :] - col_start[:, None]
    col_in = (rel_c >= 0) & (rel_c < NA_KW)
    dc = np.clip(col[None, :] - col[:, None] + NA_KW - 1, 0, n_dc - 1)
    onehot = (dc[None] == np.arange(n_dc)[:, None, None]).astype(np.float32)
    zeros = np.zeros_like(onehot)
    onehot2 = np.concatenate([np.concatenate([onehot, zeros], axis=-1),
                              np.concatenate([zeros, onehot], axis=-1)], axis=0)
    rb = rel_bias.astype(F32)
    rb2 = jnp.concatenate([rb[:, :, :-1], rb[:, :, 1:]], axis=-1)
    tbl = jnp.einsum('lhdj,jqk->ldhqk', rb2, jnp.asarray(onehot2),
                     precision=lax.Precision.HIGHEST)
    col_in2 = np.concatenate([col_in, col_in], axis=-1)
    return jnp.where(col_in2[None, None, None], tbl, NEG_BIG)


def _head_select_mask():
    r = lax.broadcasted_iota(jnp.int32, (MXU_DIM, MXU_DIM), 0) // HEAD_DIM
    c = lax.broadcasted_iota(jnp.int32, (MXU_DIM, MXU_DIM), 1) // HEAD_DIM
    return r == c


def _stack_heads(t):
    n = t.shape[0]
    reps = jnp.concatenate([t] * HEADS_PER_TILE, axis=0)
    row_head = lax.broadcasted_iota(jnp.int32, reps.shape, 0) // n
    lane_head = lax.broadcasted_iota(jnp.int32, reps.shape, 1) // HEAD_DIM
    return jnp.where(row_head == lane_head, reps, jnp.zeros_like(reps))


def _unstack_heads(t4, n):
    lane_head = lax.broadcasted_iota(jnp.int32, (n, MXU_DIM), 1) // HEAD_DIM
    out = jnp.zeros((n, MXU_DIM), t4.dtype)
    for h in range(HEADS_PER_TILE):
        out = jnp.where(lane_head == h, t4[h * n:(h + 1) * n], out)
    return out


def _na_kernel(q_ref, k_ref, v_ref, g_ref, bias_ref, o_ref, *, rows):
    kh = min(NA_KH, rows)
    step = pl.program_id(1)

    def row_body(rr, carry):
        r = step * NA_ROWS_PER_STEP + rr
        r_start = jnp.clip(r - kh // 2, 0, rows - kh)
        variant = r - r_start
        q_off = pl.multiple_of(rr * GRID_W, GRID_W)
        k_off = pl.multiple_of(r_start * GRID_W, GRID_W)
        for cg in range(D_GROUP // MXU_DIM):
            lanes = slice(cg * MXU_DIM, (cg + 1) * MXU_DIM)
            q = q_ref[pl.ds(q_off, GRID_W), lanes] * (HEAD_DIM ** -0.5)
            kw = k_ref[pl.ds(k_off, kh * GRID_W), lanes]
            vw = v_ref[pl.ds(k_off, kh * GRID_W), lanes]
            q4 = _stack_heads(q)
            s = lax.dot_general(q4, kw, (((1,), (1,)), ((), ())), preferred_element_type=F32)
            heads = slice(cg * HEADS_PER_TILE, (cg + 1) * HEADS_PER_TILE)
            bias = jnp.concatenate(
                [bias_ref[NA_KH - 1 - variant + a, heads].reshape(HEADS_PER_TILE * GRID_W, 2 * GRID_W)
                 for a in range(0, kh, 2)], axis=-1)
            s = s + bias
            m = jnp.max(s, axis=-1, keepdims=True)
            e = jnp.exp(s - m)
            l = jnp.sum(e, axis=-1, keepdims=True)
            o4 = jnp.dot(e.astype(BF16), vw, preferred_element_type=F32) / l
            o = _unstack_heads(o4, GRID_W)
            gate = g_ref[pl.ds(q_off, GRID_W), lanes].astype(F32)
            o_ref[pl.ds(q_off, GRID_W), lanes] = (o * _silu(gate)).astype(BF16)
        return carry

    lax.fori_loop(0, NA_ROWS_PER_STEP, row_body, 0, unroll=True)


def _neighbourhood(zp, bias_tbl, layer):
    _, batch, seq, _ = zp.shape
    rows = seq // GRID_W
    tq = NA_ROWS_PER_STEP * GRID_W
    q_spec = lambda p: pl.BlockSpec((None, None, tq, D_GROUP), lambda b, i: (p, b, i, 0))
    kv_spec = lambda p: pl.BlockSpec((None, None, seq, D_GROUP), lambda b, i: (p, b, 0, 0))
    return pl.pallas_call(
        functools.partial(_na_kernel, rows=rows),
        name="natten",
        grid=(batch, rows // NA_ROWS_PER_STEP),
        in_specs=[
            q_spec(P_NQ), kv_spec(P_NK), kv_spec(P_NV), q_spec(P_NG),
            pl.BlockSpec((None,) + bias_tbl.shape[1:], lambda b, i: (layer, 0, 0, 0, 0),
                         pipeline_mode=pl.Buffered(1)),
        ],
        out_specs=pl.BlockSpec((None, tq, D_GROUP), lambda b, i: (b, i, 0)),
        out_shape=jax.ShapeDtypeStruct((batch, seq, D_GROUP), BF16),
        compiler_params=_params(("arbitrary", "arbitrary")),
    )(zp, zp, zp, zp, bias_tbl)


RET_CHUNK = MXU_DIM


def _log_sigmoid(t):
    return jnp.minimum(t, 0.0) - jnp.log1p(jnp.exp(-jnp.abs(t)))


def _rope_tables(seq):
    half = HEAD_DIM // 2
    inv = ROPE_BASE ** (-jnp.arange(half, dtype=F32) / half)
    ang = jnp.arange(seq, dtype=F32)[:, None] * inv[None, :]
    cos, sin = jnp.cos(ang), jnp.sin(ang)
    cos2 = jnp.concatenate([cos, cos], axis=-1)
    sin2 = jnp.concatenate([-sin, sin], axis=-1)
    return jnp.tile(cos2, (1, 2)), jnp.tile(sin2, (1, 2))


def _ret_kernel(lf_s_ref, lb_s_ref, q_ref, k_ref, v_ref, g_ref, cos_ref, sin_ref, lfl_ref, lbl_ref,
                o_ref, qr_ref, kr_ref, ob_ref, dmat_ref, dec_ref, state_ref, ones_ref, *, seq, layer):
    C = RET_CHUNK
    n_chunks = seq // C
    cg = pl.program_id(1)
    ones_bd = jnp.where(_head_select_mask(), 1.0, 0.0).astype(BF16)
    ones_ref[...] = jnp.concatenate([ones_bd, ones_bd], axis=0)

    ri = lax.broadcasted_iota(jnp.int32, (C, C), 0)
    ci = lax.broadcasted_iota(jnp.int32, (C, C), 1)
    diff = (ri - ci).astype(F32)
    for hh in range(HEADS_PER_TILE):
        h = layer * N_HEADS + cg * HEADS_PER_TILE + hh
        lf = _log_sigmoid(jnp.full((C, C), lf_s_ref[h], F32))
        lb = _log_sigmoid(jnp.full((C, C), lb_s_ref[h], F32))
        dmat_ref[hh] = jnp.where(diff >= 0, jnp.exp(lf * diff), jnp.exp(lb * (-diff)))
    lfl = _log_sigmoid(lfl_ref[...])
    lbl = _log_sigmoid(lbl_ref[...])
    idx = lax.broadcasted_iota(jnp.int32, (C, MXU_DIM), 0).astype(F32)
    dec_ref[0] = jnp.exp(lfl * (idx + 1.0))
    dec_ref[1] = jnp.exp(lfl * (C - 1.0 - idx))
    dec_ref[2] = jnp.exp(lbl * (C - idx))
    dec_ref[3] = jnp.exp(lbl * idx)
    cd_f = jnp.exp(lfl * float(C))
    cd_b = jnp.exp(lbl * float(C))

    def rope(t, rows):
        lane = lax.broadcasted_iota(jnp.int32, t.shape, 1)
        first_half = (lane % HEAD_DIM) < (HEAD_DIM // 2)
        cs = cos_ref[rows, :]
        sn = sin_ref[rows, :]
        cs = jnp.concatenate([cs, cs], axis=-1)
        sn = jnp.concatenate([sn, sn], axis=-1)
        swapped = jnp.where(first_half, pltpu.roll(t, MXU_DIM - HEAD_DIM // 2, 1),
                            pltpu.roll(t, HEAD_DIM // 2, 1))
        return t * cs + swapped * sn

    def kv_update(k, v, kd, cd):
        kv = lax.dot_general((k * kd).astype(BF16), v, (((0,), (0,)), ((), ())),
                             preferred_element_type=F32)
        state_ref[...] = cd * state_ref[...] + jnp.where(_head_select_mask(), kv, 0.0)

    state_ref[...] = jnp.zeros_like(state_ref)

    def bwd_body(i, carry):
        n = n_chunks - 1 - i
        rows = pl.ds(pl.multiple_of(n * C, C), C)
        q = rope(q_ref[rows, :].astype(F32), rows) * (HEAD_DIM ** -0.5)
        k = rope(k_ref[rows, :].astype(F32), rows)
        qr_ref[rows, :] = q.astype(BF16)
        kr_ref[rows, :] = k.astype(BF16)
        ob_ref[rows, :] = jnp.dot((q * dec_ref[2]).astype(BF16), state_ref[...].astype(BF16),
                                  preferred_element_type=F32)
        kv_update(k, v_ref[rows, :], dec_ref[3], cd_b)
        return carry

    lax.fori_loop(0, n_chunks, bwd_body, 0, unroll=4)

    state_ref[...] = jnp.zeros_like(state_ref)

    def fwd_body(n, carry):
        rows = pl.ds(pl.multiple_of(n * C, C), C)
        qb = qr_ref[rows, :]
        kb = kr_ref[rows, :]
        v = v_ref[rows, :]
        o_cross = jnp.dot((qb.astype(F32) * dec_ref[0]).astype(BF16), state_ref[...].astype(BF16),
                          preferred_element_type=F32)
        q4 = _stack_heads(qb)
        sc = lax.dot_general(q4, kb, (((1,), (1,)), ((), ())), preferred_element_type=F32)
        sc = sc * dmat_ref[...].reshape(HEADS_PER_TILE * C, C)
        o4 = jnp.dot(sc.astype(BF16), v, preferred_element_type=F32)
        o = _unstack_heads(o4, C) + o_cross + ob_ref[rows, :]
        sq = o * o
        hi = sq.astype(BF16)
        lo = (sq - hi.astype(F32)).astype(BF16)
        ss = jnp.dot(jnp.concatenate([hi, lo], axis=-1), ones_ref[...], preferred_element_type=F32)
        inv = lax.rsqrt(ss * (1.0 / HEAD_DIM) + EPS)
        gate = g_ref[rows, :].astype(F32)
        o_ref[rows, :] = (o * inv * _silu(gate)).astype(BF16)
        kv_update(kb.astype(F32), v, dec_ref[1], cd_f)
        return carry

    lax.fori_loop(0, n_chunks, fwd_body, 0, unroll=4)


def _retention(zp, logit_f, logit_b, rope_tbl, layer):
    _, batch, seq, _ = zp.shape
    cos_t, sin_t = rope_tbl
    n_cg = D_GROUP // MXU_DIM
    spec = lambda p: pl.BlockSpec((None, None, seq, MXU_DIM), lambda b, c, *_: (p, b, 0, c))
    tbl_spec = pl.BlockSpec((seq, 2 * HEAD_DIM), lambda b, c, *_: (0, 0))
    lane_spec = pl.BlockSpec((None, 1, MXU_DIM), lambda b, c, *_: (layer, 0, c))
    per_lane = lambda v: jnp.repeat(v.astype(F32), HEAD_DIM, axis=1)[:, None, :]
    return pl.pallas_call(
        functools.partial(_ret_kernel, seq=seq, layer=layer),
        name="retention",
        grid_spec=pltpu.PrefetchScalarGridSpec(
            num_scalar_prefetch=2,
            grid=(batch, n_cg),
            in_specs=[spec(P_RQ), spec(P_RK), spec(P_RV), spec(P_RG), tbl_spec, tbl_spec,
                      lane_spec, lane_spec],
            out_specs=pl.BlockSpec((None, seq, MXU_DIM), lambda b, c, *_: (b, 0, c)),
            scratch_shapes=[
                pltpu.VMEM((seq, MXU_DIM), BF16),
                pltpu.VMEM((seq, MXU_DIM), BF16),
                pltpu.VMEM((seq, MXU_DIM), F32),
                pltpu.VMEM((HEADS_PER_TILE, RET_CHUNK, RET_CHUNK), F32),
                pltpu.VMEM((4, RET_CHUNK, MXU_DIM), F32),
                pltpu.VMEM((MXU_DIM, MXU_DIM), F32),
                pltpu.VMEM((2 * MXU_DIM, MXU_DIM), BF16),
            ],
        ),
        out_shape=jax.ShapeDtypeStruct((batch, seq, D_GROUP), BF16),
        compiler_params=_params(("arbitrary", "arbitrary")),
    )(logit_f.astype(F32).reshape(-1), logit_b.astype(F32).reshape(-1), zp, zp, zp, zp, cos_t, sin_t,
      per_lane(logit_f), per_lane(logit_b))


CONV_HALO = SUBLANES_BF16
CONV_ROWS = 64


def _conv_glu(j, a_ref, b_ref, u_ref, *, seq):
    halo = CONV_HALO
    ts = u_ref.shape[0] - 2 * halo
    n_tiles = seq // ts

    def glu(start, size):
        rows = pl.ds(pl.multiple_of(start, halo), size)
        a = a_ref[rows, :].astype(F32)
        return a * _sigmoid(b_ref[rows, :].astype(F32))

    base = j * ts
    u_ref[halo:halo + ts, :] = glu(base, ts)
    lo = glu(jnp.maximum(base - halo, 0), halo)
    u_ref[0:halo, :] = jnp.where(j > 0, lo, 0.0)
    hi = glu(jnp.minimum(base + ts, seq - halo), halo)
    u_ref[halo + ts:, :] = jnp.where(j < n_tiles - 1, hi, 0.0)


def _conv_rows(s0, cw_ref, cb_ref, lg_ref, lb_ref, u_ref, h_ref):
    rb = CONV_ROWS
    first = CONV_HALO - CONV_HALF
    parts = []
    for lt in range(D_GROUP // LANES):
        lanes = slice(lt * LANES, (lt + 1) * LANES)
        y = None
        for r in range(SUBLANES_F32):
            acc = None
            for o in range(r, first + CONV_WIDTH, SUBLANES_F32):
                if o < first:
                    continue
                w = o - first
                lo_row = s0 + (o - r)
                term = u_ref[lo_row:lo_row + rb + SUBLANES_F32, lanes] * cw_ref[w:w + 1, lanes]
                acc = term if acc is None else acc + term
            shifted = acc[r:r + rb]
            y = shifted if y is None else y + shifted
        parts.append(y)
    y = jnp.concatenate(parts, axis=-1) + cb_ref[...]
    mu = jnp.mean(y, axis=-1, keepdims=True)
    yc = y - mu
    var = jnp.mean(yc * yc, axis=-1, keepdims=True)
    y = (yc * lax.rsqrt(var + EPS)) * lg_ref[...] + lb_ref[...]
    h_ref[s0:s0 + rb, :] = _silu(y).astype(BF16)
    return y[rb - SUBLANES_F32:, D_GROUP - LANES:]


def _runtime_zero(tile):
    bits = pltpu.bitcast(tile, jnp.uint32)
    bits = lax.shift_right_logical(lax.shift_right_logical(bits, jnp.uint32(16)), jnp.uint32(16))
    return bits[0, 0].astype(jnp.int32)


def _conv_pointwise(g_ref, w_ref, h_ref, o_ref):
    o = jnp.dot(h_ref[...], w_ref[...], preferred_element_type=F32)
    o_ref[...] = (o * _silu(g_ref[...].astype(F32))).astype(BF16)


def kernel(x, c, norm_g, w_ada, b_ada, w_in, w_fft, na_rel_bias, ret_logit_fwd, ret_logit_bwd,
           conv_w, conv_b, conv_ln_g, conv_ln_b, conv_w_pw, w_out, final_g):
    seq = x.shape[1]
    mod = _ada(c, w_ada, b_ada)
    fft_consts = _fft_constants(seq)
    rope_tbl = _rope_tables(seq)
    bias_tbl = _na_bias_table(na_rel_bias)
    w_fft_b, w_pw_b = w_fft.astype(BF16), conv_w_pw.astype(BF16)
    for l in range(DEPTH):
        zp = _inproj(x, norm_g, mod, w_in, l)
        o_fft = _fourier(zp, w_fft_b, fft_consts, l)
        o_na = _neighbourhood(zp, bias_tbl, l)
        o_ret = _retention(zp, ret_logit_fwd, ret_logit_bwd, rope_tbl, l)
        conv_params = (conv_w, conv_b, conv_ln_g, conv_ln_b)
        x = _outproj((o_fft, o_na, o_ret), zp, conv_params, w_pw_b, w_out, x, mod, final_g, l,
                     final=(l == DEPTH - 1))
    return x
```

```python
import functools
import math

import numpy as np
import jax
import jax.numpy as jnp
from jax import lax
from jax.experimental import pallas as pl
from jax.experimental.pallas import tpu as pltpu

F32 = jnp.float32
BF16 = jnp.bfloat16

D_MODEL = 2048
DEPTH = 2
GRID_W = 64
D_GROUP = 512
HEAD_DIM = 64
N_HEADS = D_GROUP // HEAD_DIM
N_FFT_GROUPS = 4
FFT_GROUP_DIM = D_GROUP // N_FFT_GROUPS
NA_KH = 8
NA_KW = 16
CONV_WIDTH = 31
CONV_HALF = CONV_WIDTH // 2
ROPE_BASE = 10000.0
EPS = 1e-6
N_PIECES = 13
(P_FX, P_FG, P_NQ, P_NK, P_NV, P_NG, P_RQ, P_RK, P_RV, P_RG, P_CA, P_CB, P_CG) = range(N_PIECES)

MXU_DIM = 256
HEADS_PER_TILE = MXU_DIM // HEAD_DIM
VMEM_LIMIT = 56 * 1024 * 1024
SUBLANES_BF16 = 16
SUBLANES_F32 = 8
LANES = 128

NEG_BIG = -1e30


def _params(sem, vmem=VMEM_LIMIT):
    return pltpu.CompilerParams(dimension_semantics=sem, vmem_limit_bytes=vmem)


def _sigmoid(t):
    return 0.5 * jnp.tanh(0.5 * t) + 0.5


def _silu(t):
    h = 0.5 * t
    return h + h * jnp.tanh(h)


ADA_TN = 768
ADA_ROWS = 8
(MOD_SHIFT, MOD_SCALE, MOD_GATE) = range(3)


def _ada_kernel(c_ref, w_ref, b_ref, o_ref):
    ca = _silu(c_ref[...]).astype(BF16)
    w = w_ref[...].astype(BF16)
    o_ref[...] = jnp.dot(ca, w, preferred_element_type=F32) + b_ref[...]


def _ada(c, w_ada, b_ada):
    batch = c.shape[0]
    c_pad = jnp.zeros((ADA_ROWS, D_MODEL), F32).at[:batch].set(c)
    n3 = 3 * D_MODEL
    out = pl.pallas_call(
        _ada_kernel,
        name="ada",
        grid=(DEPTH, n3 // ADA_TN),
        in_specs=[
            pl.BlockSpec((ADA_ROWS, D_MODEL), lambda l, j: (0, 0)),
            pl.BlockSpec((None, D_MODEL, ADA_TN), lambda l, j: (l, 0, j)),
            pl.BlockSpec((None, 1, ADA_TN), lambda l, j: (l, 0, j)),
        ],
        out_specs=pl.BlockSpec((None, ADA_ROWS, ADA_TN), lambda l, j: (l, 0, j)),
        out_shape=jax.ShapeDtypeStruct((DEPTH, ADA_ROWS, n3), F32),
        compiler_params=_params(("arbitrary", "arbitrary")),
    )(c_pad, w_ada, b_ada.reshape(DEPTH, 1, n3))
    return out.reshape(DEPTH, ADA_ROWS, 1, n3)


INPROJ_TM = 512
INPROJ_WCHUNK = MXU_DIM
INPROJ_WROWS = SUBLANES_BF16
INPROJ_WSLOTS = 8
INPROJ_VMEM = 60 * 1024 * 1024


def _inproj_kernel(x_ref, g_ref, sc_ref, sh_ref, w_hbm, o_ref, h0_ref, h1_ref, w_ref, stage_ref, sem,
                   *, n_tiles, layer):
    t = pl.program_id(0)
    n_wchunks = w_ref.shape[0]
    per_piece = D_GROUP // INPROJ_WCHUNK

    wrows = stage_ref.shape[1]
    n_rchunks = D_MODEL // wrows

    def weight_copy(r, slot):
        rows = pl.ds(pl.multiple_of(r * wrows, wrows), wrows)
        return pltpu.make_async_copy(w_hbm.at[layer, rows, :], stage_ref.at[slot], sem.at[slot])

    def load_weight():
        n_slots = stage_ref.shape[0]
        for slot in range(n_slots):
            weight_copy(slot, slot).start()

        def group(i, carry):
            for slot in range(n_slots):
                r = n_slots * i + slot
                weight_copy(r, slot).wait()
                rows = pl.ds(pl.multiple_of(r * wrows, wrows), wrows)
                for c in range(n_wchunks):
                    w_ref[c, rows, :] = stage_ref[
                        slot, :, c * INPROJ_WCHUNK:(c + 1) * INPROJ_WCHUNK].astype(BF16)

                @pl.when(r + n_slots < n_rchunks)
                def _():
                    weight_copy(r + n_slots, slot).start()
            return carry

        lax.fori_loop(0, n_rchunks // n_slots, group, 0)

    def norm(h_ref):
        x = x_ref[...]
        ms = jnp.mean(x * x, axis=-1, keepdims=True)
        h = (x * lax.rsqrt(ms + EPS)) * g_ref[...]
        h = h * (1.0 + sc_ref[...]) + sh_ref[...]
        h_ref[...] = h.astype(BF16)

    def project(h_ref):
        h = h_ref[...]
        for c in range(n_wchunks):
            cols = slice((c % per_piece) * INPROJ_WCHUNK, (c % per_piece + 1) * INPROJ_WCHUNK)
            o_ref[c // per_piece, :, cols] = jnp.dot(
                h, w_ref[c], preferred_element_type=F32).astype(BF16)

    odd = t % 2 == 1

    @pl.when(t == 0)
    def _():
        load_weight()
        norm(h0_ref)

    @pl.when(odd)
    def _():
        project(h0_ref)
        norm(h1_ref)

    @pl.when(jnp.logical_and(jnp.logical_not(odd), jnp.logical_and(t > 0, t < n_tiles)))
    def _():
        project(h1_ref)
        norm(h0_ref)

    @pl.when(t == n_tiles)
    def _():
        project(h1_ref)


def _inproj(x, norm_g, mod, w_in, layer):
    batch, seq, _ = x.shape
    tm = INPROJ_TM
    per_batch = seq // tm
    n_tiles = batch * per_batch
    n_wchunks = N_PIECES * D_GROUP // INPROJ_WCHUNK
    assert n_tiles % 2 == 0 and (D_MODEL // INPROJ_WROWS) % INPROJ_WSLOTS == 0
    norm_tile = lambda t: jnp.minimum(t, n_tiles - 1)
    proj_tile = lambda t: jnp.maximum(t - 1, 0)
    return pl.pallas_call(
        functools.partial(_inproj_kernel, n_tiles=n_tiles, layer=layer),
        name="inproj",
        grid=(n_tiles + 1,),
        in_specs=[
            pl.BlockSpec((None, tm, D_MODEL),
                         lambda t: (norm_tile(t) // per_batch, norm_tile(t) % per_batch, 0)),
            pl.BlockSpec((None, 1, D_MODEL), lambda t: (layer, 0, 0)),
            pl.BlockSpec((None, None, 1, D_MODEL),
                         lambda t: (layer, norm_tile(t) // per_batch, 0, MOD_SCALE)),
            pl.BlockSpec((None, None, 1, D_MODEL),
                         lambda t: (layer, norm_tile(t) // per_batch, 0, MOD_SHIFT)),
            pl.BlockSpec(memory_space=pl.ANY),
        ],
        out_specs=pl.BlockSpec((N_PIECES, None, tm, D_GROUP),
                               lambda t: (0, proj_tile(t) // per_batch, proj_tile(t) % per_batch, 0)),
        out_shape=jax.ShapeDtypeStruct((N_PIECES, batch, seq, D_GROUP), BF16),
        scratch_shapes=[
            pltpu.VMEM((tm, D_MODEL), BF16), pltpu.VMEM((tm, D_MODEL), BF16),
            pltpu.VMEM((n_wchunks, D_MODEL, INPROJ_WCHUNK), BF16),
            pltpu.VMEM((INPROJ_WSLOTS, INPROJ_WROWS, N_PIECES * D_GROUP), F32),
            pltpu.SemaphoreType.DMA((INPROJ_WSLOTS,)),
        ],
        compiler_params=_params(("arbitrary",), INPROJ_VMEM),
    )(x, norm_g.reshape(DEPTH, 1, D_MODEL), mod, mod, w_in)


OUTPROJ_TM = 512
OUTPROJ_WROWS = 64
OUTPROJ_WSLOTS = 8
OUTPROJ_VMEM = 60 * 1024 * 1024


def _outproj_kernel(a0_ref, a1_ref, a2_ref, w_hbm, x_ref, gate_ref, fg_ref,
                    ca_ref, cb_ref, cg_ref, cw_ref, cbias_ref, lg_ref, lb_ref, wpw_ref,
                    o_ref, u_ref, h_ref, cv0_ref, cv1_ref, w_ref, stage_ref, sem,
                    *, final, n_tiles, per_batch, seq, layer):
    t = pl.program_id(0)
    j = jnp.minimum(t, n_tiles - 1) % per_batch

    tm = o_ref.shape[0]
    n_chunks = tm // CONV_ROWS
    tn = D_MODEL // n_chunks
    n_slots = stage_ref.shape[0]

    wrows = stage_ref.shape[1]

    def weight_copy(r):
        slot = r % n_slots
        return pltpu.make_async_copy(w_hbm.at[layer, pl.ds(r * wrows, wrows), :], stage_ref.at[slot],
                                     sem.at[slot])

    def run(proj_cv_ref, conv_cv_ref, load_weight=False):
        if load_weight:
            for c in range(n_slots):
                weight_copy(c).start()
        if conv_cv_ref is not None:
            _conv_glu(j, ca_ref, cb_ref, u_ref, seq=seq)
        if proj_cv_ref is not None:
            a = jnp.concatenate([a0_ref[...], a1_ref[...], a2_ref[...], proj_cv_ref[...]], axis=-1)
            ss = jnp.zeros((tm, 1), F32)
        zero = 0
        for i in range(n_chunks):
            if proj_cv_ref is not None:
                y = jnp.dot(a, w_ref[i + zero], preferred_element_type=F32)
                xn = x_ref[:, i * tn:(i + 1) * tn] + gate_ref[:, i * tn:(i + 1) * tn] * y
                if final:
                    ss = ss + jnp.sum(xn * xn, axis=-1, keepdims=True)
                o_ref[:, i * tn:(i + 1) * tn] = xn
            if conv_cv_ref is not None:
                tail = _conv_rows(i * CONV_ROWS, cw_ref, cbias_ref, lg_ref, lb_ref, u_ref, h_ref)
                zero = _runtime_zero(tail)
            if load_weight:
                n_rchunks = D_MODEL // wrows
                per_block = n_rchunks // n_chunks
                for r in range(i * per_block, (i + 1) * per_block):
                    weight_copy(r).wait()
                    for c in range(n_chunks):
                        w_ref[c, r * wrows:(r + 1) * wrows, :] = stage_ref[
                            r % n_slots, :, c * tn:(c + 1) * tn].astype(BF16)
                    if r + n_slots < n_rchunks:
                        weight_copy(r + n_slots).start()
        if conv_cv_ref is not None:
            _conv_pointwise(cg_ref, wpw_ref, h_ref, conv_cv_ref)
        if proj_cv_ref is not None and final:
            o_ref[...] = (o_ref[...] * lax.rsqrt(ss * (1.0 / D_MODEL) + EPS)) * fg_ref[...]

    odd = t % 2 == 1

    @pl.when(t == 0)
    def _():
        run(None, cv0_ref, load_weight=True)

    @pl.when(odd)
    def _():
        run(cv0_ref, cv1_ref)

    @pl.when(jnp.logical_and(jnp.logical_not(odd), jnp.logical_and(t > 0, t < n_tiles)))
    def _():
        run(cv1_ref, cv0_ref)

    @pl.when(t == n_tiles)
    def _():
        run(cv1_ref, None)


def _outproj(mixed, zp, conv_params, w_pw_bf16, w_out, x, mod, final_g, layer, final):
    batch, seq, _ = x.shape
    tm = OUTPROJ_TM
    per_batch = seq // tm
    n_tiles = batch * per_batch
    n_wchunks = tm // CONV_ROWS
    assert n_tiles % 2 == 0
    conv_tile = lambda t: jnp.minimum(t, n_tiles - 1)
    proj_tile = lambda t: jnp.maximum(t - 1, 0)
    a_spec = pl.BlockSpec((None, tm, D_GROUP),
                          lambda t: (proj_tile(t) // per_batch, proj_tile(t) % per_batch, 0))
    x_spec = pl.BlockSpec((None, tm, D_MODEL),
                          lambda t: (proj_tile(t) // per_batch, proj_tile(t) % per_batch, 0))
    seq_spec = lambda p: pl.BlockSpec((None, None, seq, D_GROUP),
                                      lambda t: (p, conv_tile(t) // per_batch, 0, 0))
    vec = pl.BlockSpec((None, 1, D_GROUP), lambda t: (layer, 0, 0))
    conv_w, conv_b, ln_g, ln_b = conv_params
    as_rows = lambda v: v.reshape(DEPTH, 1, D_GROUP)
    return pl.pallas_call(
        functools.partial(_outproj_kernel, final=final, n_tiles=n_tiles, per_batch=per_batch, seq=seq,
                          layer=layer),
        name="outproj",
        grid=(n_tiles + 1,),
        in_specs=[
            a_spec, a_spec, a_spec,
            pl.BlockSpec(memory_space=pl.ANY),
            x_spec,
            pl.BlockSpec((None, None, 1, D_MODEL), lambda t: (layer, proj_tile(t) // per_batch, 0, MOD_GATE)),
            pl.BlockSpec((1, D_MODEL), lambda t: (0, 0)),
            seq_spec(P_CA), seq_spec(P_CB),
            pl.BlockSpec((None, None, tm, D_GROUP),
                         lambda t: (P_CG, conv_tile(t) // per_batch, conv_tile(t) % per_batch, 0)),
            pl.BlockSpec((None, CONV_WIDTH, D_GROUP), lambda t: (layer, 0, 0)),
            vec, vec, vec,
            pl.BlockSpec((None, D_GROUP, D_GROUP), lambda t: (layer, 0, 0)),
        ],
        out_specs=x_spec,
        out_shape=jax.ShapeDtypeStruct((batch, seq, D_MODEL), F32),
        scratch_shapes=[
            pltpu.VMEM((tm + 2 * CONV_HALO, D_GROUP), F32),
            pltpu.VMEM((tm, D_GROUP), BF16),
            pltpu.VMEM((tm, D_GROUP), BF16), pltpu.VMEM((tm, D_GROUP), BF16),
            pltpu.VMEM((n_wchunks, D_MODEL, D_MODEL // n_wchunks), BF16),
            pltpu.VMEM((OUTPROJ_WSLOTS, OUTPROJ_WROWS, D_MODEL), F32),
            pltpu.SemaphoreType.DMA((OUTPROJ_WSLOTS,)),
        ],
        compiler_params=_params(("arbitrary",), OUTPROJ_VMEM),
    )(*mixed, w_out, x, mod, final_g.reshape(1, D_MODEL), zp, zp, zp, conv_w,
      as_rows(conv_b), as_rows(ln_g), as_rows(ln_b), w_pw_bf16)


FFT_BLOCK = 16


def _fft_constants(seq):
    rows = seq // GRID_W
    assert rows == GRID_W
    n = np.arange(GRID_W)
    ang1 = 2.0 * np.pi * ((n[:, None] * n[None, :]) % GRID_W) / GRID_W
    f1 = np.concatenate([np.cos(ang1), -np.sin(ang1)], axis=0)
    k1 = n[:, None, None]
    k2 = n[None, :, None]
    s2 = n[None, None, :]
    ang2 = 2.0 * np.pi * ((s2 * (k1 + GRID_W * k2)) % seq) / seq
    mr, mi = np.cos(ang2), -np.sin(ang2)
    m2 = np.concatenate([np.concatenate([mr, -mi], axis=2),
                         np.concatenate([mi, mr], axis=2)], axis=1)
    c = np.arange(FFT_GROUP_DIM)
    angc = 2.0 * np.pi * ((c[:, None] * c[None, :]) % FFT_GROUP_DIM) / FFT_GROUP_DIM
    fc = np.concatenate([np.cos(angc), np.sin(angc)], axis=0)
    return tuple(jnp.asarray(t, F32).astype(BF16) for t in (f1, m2, fc))


def _fft_kernel(u_ref, gate_ref, f1_ref, m2_ref, fc_ref, w_ref, o_ref, ut_ref, g_ref, gt_ref, y_ref, *, norm):
    grp = pl.program_id(1)
    n = GRID_W
    blk = FFT_BLOCK
    sub = SUBLANES_F32

    u3 = u_ref[...].astype(F32).reshape(n, n, FFT_GROUP_DIM)
    ut_ref[...] = jnp.swapaxes(u3, 0, 1).astype(BF16)

    def stage1(i, carry):
        for j in range(blk):
            s2 = i * blk + j
            g_ref[s2] = jnp.dot(f1_ref[...], ut_ref[s2], preferred_element_type=F32)
        return carry

    lax.fori_loop(0, n // blk, stage1, 0)

    for q in range(2 * n // sub):
        part = g_ref[:, q * sub:(q + 1) * sub, :]
        gt_ref[q * sub:(q + 1) * sub] = jnp.swapaxes(part, 0, 1).astype(BF16)

    def stage2(i, carry):
        lhs = []
        for j in range(blk):
            k1 = i * blk + j
            gk = jnp.concatenate([gt_ref[k1], gt_ref[n + k1]], axis=0)
            xk = jnp.dot(m2_ref[k1], gk, preferred_element_type=F32)
            lhs.append(jnp.concatenate([xk[:n], xk[n:]], axis=-1))
        lhs = jnp.concatenate(lhs, axis=0).astype(BF16)
        y = jnp.dot(lhs, fc_ref[...], preferred_element_type=F32) * norm
        y_ref[grp, pl.ds(pl.multiple_of(i * blk, blk), blk)] = y.reshape(blk, n, FFT_GROUP_DIM)
        return carry

    lax.fori_loop(0, n // blk, stage2, 0)

    @pl.when(grp == N_FFT_GROUPS - 1)
    def _():
        def stage3(i, carry):
            k2 = pl.ds(pl.multiple_of(i * sub, sub), sub)
            tiles = []
            for g in range(N_FFT_GROUPS):
                part = y_ref[g, :, k2, :]
                tiles.append(jnp.swapaxes(part, 0, 1).reshape(sub * n, FFT_GROUP_DIM))
            y = jnp.concatenate(tiles, axis=-1).astype(BF16)
            o = jnp.dot(y, w_ref[...], preferred_element_type=F32)
            rows = pl.ds(pl.multiple_of(i * sub * n, sub * n), sub * n)
            o_ref[rows, :] = (o * _silu(gate_ref[rows, :].astype(F32))).astype(BF16)
            return carry

        lax.fori_loop(0, n // sub, stage3, 0)


def _fourier(zp, w_fft_bf16, consts, layer):
    _, batch, seq, _ = zp.shape
    f1, m2, fc = consts
    n = GRID_W
    const = lambda shape: pl.BlockSpec(shape, lambda b, g: (0,) * len(shape))
    return pl.pallas_call(
        functools.partial(_fft_kernel, norm=1.0 / math.sqrt(seq * FFT_GROUP_DIM)),
        name="fourier",
        grid=(batch, N_FFT_GROUPS),
        in_specs=[
            pl.BlockSpec((None, None, seq, FFT_GROUP_DIM), lambda b, g: (P_FX, b, 0, g)),
            pl.BlockSpec((None, None, seq, D_GROUP), lambda b, g: (P_FG, b, 0, 0)),
            const((2 * n, n)), const((n, 2 * n, 2 * n)), const((2 * FFT_GROUP_DIM, FFT_GROUP_DIM)),
            pl.BlockSpec((None, D_GROUP, D_GROUP), lambda b, g: (layer, 0, 0)),
        ],
        out_specs=pl.BlockSpec((None, seq, D_GROUP), lambda b, g: (b, 0, 0)),
        out_shape=jax.ShapeDtypeStruct((batch, seq, D_GROUP), BF16),
        scratch_shapes=[
            pltpu.VMEM((n, n, FFT_GROUP_DIM), BF16),
            pltpu.VMEM((n, 2 * n, FFT_GROUP_DIM), F32),
            pltpu.VMEM((2 * n, n, FFT_GROUP_DIM), BF16),
            pltpu.VMEM((N_FFT_GROUPS, n, n, FFT_GROUP_DIM), F32),
        ],
        compiler_params=_params(("arbitrary", "arbitrary")),
    )(zp, zp, f1, m2, fc, w_fft_bf16)


NA_ROWS_PER_STEP = 8


def _na_bias_table(rel_bias):
    n_dc = 2 * NA_KW - 1
    col = np.arange(GRID_W)
    col_start = np.clip(col - NA_KW // 2, 0, GRID_W - NA_KW)
    rel_c = col[None, :] - col_start[:, None]
    col_in = (rel_c >= 0) & (rel_c < NA_KW)
    dc = np.clip(col[None, :] - col[:, None] + NA_KW - 1, 0, n_dc - 1)
    onehot = (dc[None] == np.arange(n_dc)[:, None, None]).astype(np.float32)
    zeros = np.zeros_like(onehot)
    onehot2 = np.concatenate([np.concatenate([onehot, zeros], axis=-1),
                              np.concatenate([zeros, onehot], axis=-1)], axis=0)
    rb = rel_bias.astype(F32)
    rb2 = jnp.concatenate([rb[:, :, :-1], rb[:, :, 1:]], axis=-1)
    tbl = jnp.einsum('lhdj,jqk->ldhqk', rb2, jnp.asarray(onehot2),
                     precision=lax.Precision.HIGHEST)
    col_in2 = np.concatenate([col_in, col_in], axis=-1)
    return jnp.where(col_in2[None, None, None], tbl, NEG_BIG)


def _head_select_mask():
    r = lax.broadcasted_iota(jnp.int32, (MXU_DIM, MXU_DIM), 0) // HEAD_DIM
    c = lax.broadcasted_iota(jnp.int32, (MXU_DIM, MXU_DIM), 1) // HEAD_DIM
    return r == c


def _stack_heads(t):
    n = t.shape[0]
    reps = jnp.concatenate([t] * HEADS_PER_TILE, axis=0)
    row_head = lax.broadcasted_iota(jnp.int32, reps.shape, 0) // n
    lane_head = lax.broadcasted_iota(jnp.int32, reps.shape, 1) // HEAD_DIM
    return jnp.where(row_head == lane_head, reps, jnp.zeros_like(reps))


def _unstack_heads(t4, n):
    lane_head = lax.broadcasted_iota(jnp.int32, (n, MXU_DIM), 1) // HEAD_DIM
    out = jnp.zeros((n, MXU_DIM), t4.dtype)
    for h in range(HEADS_PER_TILE):
        out = jnp.where(lane_head == h, t4[h * n:(h + 1) * n], out)
    return out


def _na_kernel(q_ref, k_ref, v_ref, g_ref, bias_ref, o_ref, *, rows):
    kh = min(NA_KH, rows)
    step = pl.program_id(1)

    def row_body(rr, carry):
        r = step * NA_ROWS_PER_STEP + rr
        r_start = jnp.clip(r - kh // 2, 0, rows - kh)
        variant = r - r_start
        q_off = pl.multiple_of(rr * GRID_W, GRID_W)
        k_off = pl.multiple_of(r_start * GRID_W, GRID_W)
        for cg in range(D_GROUP // MXU_DIM):
            lanes = slice(cg * MXU_DIM, (cg + 1) * MXU_DIM)
            q = q_ref[pl.ds(q_off, GRID_W), lanes] * (HEAD_DIM ** -0.5)
            kw = k_ref[pl.ds(k_off, kh * GRID_W), lanes]
            vw = v_ref[pl.ds(k_off, kh * GRID_W), lanes]
            q4 = _stack_heads(q)
            s = lax.dot_general(q4, kw, (((1,), (1,)), ((), ())), preferred_element_type=F32)
            heads = slice(cg * HEADS_PER_TILE, (cg + 1) * HEADS_PER_TILE)
            bias = jnp.concatenate(
                [bias_ref[NA_KH - 1 - variant + a, heads].reshape(HEADS_PER_TILE * GRID_W, 2 * GRID_W)
                 for a in range(0, kh, 2)], axis=-1)
            s = s + bias
            m = jnp.max(s, axis=-1, keepdims=True)
            e = jnp.exp(s - m)
            l = jnp.sum(e, axis=-1, keepdims=True)
            o4 = jnp.dot(e.astype(BF16), vw, preferred_element_type=F32) / l
            o = _unstack_heads(o4, GRID_W)
            gate = g_ref[pl.ds(q_off, GRID_W), lanes].astype(F32)
            o_ref[pl.ds(q_off, GRID_W), lanes] = (o * _silu(gate)).astype(BF16)
        return carry

    lax.fori_loop(0, NA_ROWS_PER_STEP, row_body, 0, unroll=True)


def _neighbourhood(zp, bias_tbl, layer):
    _, batch, seq, _ = zp.shape
    rows = seq // GRID_W
    tq = NA_ROWS_PER_STEP * GRID_W
    q_spec = lambda p: pl.BlockSpec((None, None, tq, D_GROUP), lambda b, i: (p, b, i, 0))
    kv_spec = lambda p: pl.BlockSpec((None, None, seq, D_GROUP), lambda b, i: (p, b, 0, 0))
    return pl.pallas_call(
        functools.partial(_na_kernel, rows=rows),
        name="natten",
        grid=(batch, rows // NA_ROWS_PER_STEP),
        in_specs=[
            q_spec(P_NQ), kv_spec(P_NK), kv_spec(P_NV), q_spec(P_NG),
            pl.BlockSpec((None,) + bias_tbl.shape[1:], lambda b, i: (layer, 0, 0, 0, 0),
                         pipeline_mode=pl.Buffered(1)),
        ],
        out_specs=pl.BlockSpec((None, tq, D_GROUP), lambda b, i: (b, i, 0)),
        out_shape=jax.ShapeDtypeStruct((batch, seq, D_GROUP), BF16),
        compiler_params=_params(("arbitrary", "arbitrary")),
    )(zp, zp, zp, zp, bias_tbl)


RET_CHUNK = MXU_DIM


def _log_sigmoid(t):
    return jnp.minimum(t, 0.0) - jnp.log1p(jnp.exp(-jnp.abs(t)))


def _rope_tables(seq):
    half = HEAD_DIM // 2
    inv = ROPE_BASE ** (-jnp.arange(half, dtype=F32) / half)
    ang = jnp.arange(seq, dtype=F32)[:, None] * inv[None, :]
    cos, sin = jnp.cos(ang), jnp.sin(ang)
    cos2 = jnp.concatenate([cos, cos], axis=-1)
    sin2 = jnp.concatenate([-sin, sin], axis=-1)
    return jnp.tile(cos2, (1, 2)), jnp.tile(sin2, (1, 2))


def _ret_kernel(lf_s_ref, lb_s_ref, q_ref, k_ref, v_ref, g_ref, cos_ref, sin_ref, lfl_ref, lbl_ref,
                o_ref, qr_ref, kr_ref, ob_ref, dmat_ref, dec_ref, state_ref, ones_ref, *, seq, layer):
    C = RET_CHUNK
    n_chunks = seq // C
    cg = pl.program_id(1)
    ones_bd = jnp.where(_head_select_mask(), 1.0, 0.0).astype(BF16)
    ones_ref[...] = jnp.concatenate([ones_bd, ones_bd], axis=0)

    ri = lax.broadcasted_iota(jnp.int32, (C, C), 0)
    ci = lax.broadcasted_iota(jnp.int32, (C, C), 1)
    diff = (ri - ci).astype(F32)
    for hh in range(HEADS_PER_TILE):
        h = layer * N_HEADS + cg * HEADS_PER_TILE + hh
        lf = _log_sigmoid(jnp.full((C, C), lf_s_ref[h], F32))
        lb = _log_sigmoid(jnp.full((C, C), lb_s_ref[h], F32))
        dmat_ref[hh] = jnp.where(diff >= 0, jnp.exp(lf * diff), jnp.exp(lb * (-diff)))
    lfl = _log_sigmoid(lfl_ref[...])
    lbl = _log_sigmoid(lbl_ref[...])
    idx = lax.broadcasted_iota(jnp.int32, (C, MXU_DIM), 0).astype(F32)
    dec_ref[0] = jnp.exp(lfl * (idx + 1.0))
    dec_ref[1] = jnp.exp(lfl * (C - 1.0 - idx))
    dec_ref[2] = jnp.exp(lbl * (C - idx))
    dec_ref[3] = jnp.exp(lbl * idx)
    cd_f = jnp.exp(lfl * float(C))
    cd_b = jnp.exp(lbl * float(C))

    def rope(t, rows):
        lane = lax.broadcasted_iota(jnp.int32, t.shape, 1)
        first_half = (lane % HEAD_DIM) < (HEAD_DIM // 2)
        cs = cos_ref[rows, :]
        sn = sin_ref[rows, :]
        cs = jnp.concatenate([cs, cs], axis=-1)
        sn = jnp.concatenate([sn, sn], axis=-1)
        swapped = jnp.where(first_half, pltpu.roll(t, MXU_DIM - HEAD_DIM // 2, 1),
                            pltpu.roll(t, HEAD_DIM // 2, 1))
        return t * cs + swapped * sn

    def kv_update(k, v, kd, cd):
        kv = lax.dot_general((k * kd).astype(BF16), v, (((0,), (0,)), ((), ())),
                             preferred_element_type=F32)
        state_ref[...] = cd * state_ref[...] + jnp.where(_head_select_mask(), kv, 0.0)

    state_ref[...] = jnp.zeros_like(state_ref)

    def bwd_body(i, carry):
        n = n_chunks - 1 - i
        rows = pl.ds(pl.multiple_of(n * C, C), C)
        q = rope(q_ref[rows, :].astype(F32), rows) * (HEAD_DIM ** -0.5)
        k = rope(k_ref[rows, :].astype(F32), rows)
        qr_ref[rows, :] = q.astype(BF16)
        kr_ref[rows, :] = k.astype(BF16)
        ob_ref[rows, :] = jnp.dot((q * dec_ref[2]).astype(BF16), state_ref[...].astype(BF16),
                                  preferred_element_type=F32)
        kv_update(k, v_ref[rows, :], dec_ref[3], cd_b)
        return carry

    lax.fori_loop(0, n_chunks, bwd_body, 0, unroll=4)

    state_ref[...] = jnp.zeros_like(state_ref)

    def fwd_body(n, carry):
        rows = pl.ds(pl.multiple_of(n * C, C), C)
        qb = qr_ref[rows, :]
        kb = kr_ref[rows, :]
        v = v_ref[rows, :]
        o_cross = jnp.dot((qb.astype(F32) * dec_ref[0]).astype(BF16), state_ref[...].astype(BF16),
                          preferred_element_type=F32)
        q4 = _stack_heads(qb)
        sc = lax.dot_general(q4, kb, (((1,), (1,)), ((), ())), preferred_element_type=F32)
        sc = sc * dmat_ref[...].reshape(HEADS_PER_TILE * C, C)
        o4 = jnp.dot(sc.astype(BF16), v, preferred_element_type=F32)
        o = _unstack_heads(o4, C) + o_cross + ob_ref[rows, :]
        sq = o * o
        hi = sq.astype(BF16)
        lo = (sq - hi.astype(F32)).astype(BF16)
        ss = jnp.dot(jnp.concatenate([hi, lo], axis=-1), ones_ref[...], preferred_element_type=F32)
        inv = lax.rsqrt(ss * (1.0 / HEAD_DIM) + EPS)
        gate = g_ref[rows, :].astype(F32)
        o_ref[rows, :] = (o * inv * _silu(gate)).astype(BF16)
        kv_update(kb.astype(F32), v, dec_ref[1], cd_f)
        return carry

    lax.fori_loop(0, n_chunks, fwd_body, 0, unroll=4)


def _retention(zp, logit_f, logit_b, rope_tbl, layer):
    _, batch, seq, _ = zp.shape
    cos_t, sin_t = rope_tbl
    n_cg = D_GROUP // MXU_DIM
    spec = lambda p: pl.BlockSpec((None, None, seq, MXU_DIM), lambda b, c, *_: (p, b, 0, c))
    tbl_spec = pl.BlockSpec((seq, 2 * HEAD_DIM), lambda b, c, *_: (0, 0))
    lane_spec = pl.BlockSpec((None, 1, MXU_DIM), lambda b, c, *_: (layer, 0, c))
    per_lane = lambda v: jnp.repeat(v.astype(F32), HEAD_DIM, axis=1)[:, None, :]
    return pl.pallas_call(
        functools.partial(_ret_kernel, seq=seq, layer=layer),
        name="retention",
        grid_spec=pltpu.PrefetchScalarGridSpec(
            num_scalar_prefetch=2,
            grid=(batch, n_cg),
            in_specs=[spec(P_RQ), spec(P_RK), spec(P_RV), spec(P_RG), tbl_spec, tbl_spec,
                      lane_spec, lane_spec],
            out_specs=pl.BlockSpec((None, seq, MXU_DIM), lambda b, c, *_: (b, 0, c)),
            scratch_shapes=[
                pltpu.VMEM((seq, MXU_DIM), BF16),
                pltpu.VMEM((seq, MXU_DIM), BF16),
                pltpu.VMEM((seq, MXU_DIM), F32),
                pltpu.VMEM((HEADS_PER_TILE, RET_CHUNK, RET_CHUNK), F32),
                pltpu.VMEM((4, RET_CHUNK, MXU_DIM), F32),
                pltpu.VMEM((MXU_DIM, MXU_DIM), F32),
                pltpu.VMEM((2 * MXU_DIM, MXU_DIM), BF16),
            ],
        ),
        out_shape=jax.ShapeDtypeStruct((batch, seq, D_GROUP), BF16),
        compiler_params=_params(("arbitrary", "arbitrary")),
    )(logit_f.astype(F32).reshape(-1), logit_b.astype(F32).reshape(-1), zp, zp, zp, zp, cos_t, sin_t,
      per_lane(logit_f), per_lane(logit_b))


CONV_HALO = SUBLANES_BF16
CONV_ROWS = 64


def _conv_glu(j, a_ref, b_ref, u_ref, *, seq):
    halo = CONV_HALO
    ts = u_ref.shape[0] - 2 * halo
    n_tiles = seq // ts

    def glu(start, size):
        rows = pl.ds(pl.multiple_of(start, halo), size)
        a = a_ref[rows, :].astype(F32)
        return a * _sigmoid(b_ref[rows, :].astype(F32))

    base = j * ts
    u_ref[halo:halo + ts, :] = glu(base, ts)
    lo = glu(jnp.maximum(base - halo, 0), halo)
    u_ref[0:halo, :] = jnp.where(j > 0, lo, 0.0)
    hi = glu(jnp.minimum(base + ts, seq - halo), halo)
    u_ref[halo + ts:, :] = jnp.where(j < n_tiles - 1, hi, 0.0)


def _conv_rows(s0, cw_ref, cb_ref, lg_ref, lb_ref, u_ref, h_ref):
    rb = CONV_ROWS
    first = CONV_HALO - CONV_HALF
    parts = []
    for lt in range(D_GROUP // LANES):
        lanes = slice(lt * LANES, (lt + 1) * LANES)
        y = None
        for r in range(SUBLANES_F32):
            acc = None
            for o in range(r, first + CONV_WIDTH, SUBLANES_F32):
                if o < first:
                    continue
                w = o - first
                lo_row = s0 + (o - r)
                term = u_ref[lo_row:lo_row + rb + SUBLANES_F32, lanes] * cw_ref[w:w + 1, lanes]
                acc = term if acc is None else acc + term
            shifted = acc[r:r + rb]
            y = shifted if y is None else y + shifted
        parts.append(y)
    y = jnp.concatenate(parts, axis=-1) + cb_ref[...]
    mu = jnp.mean(y, axis=-1, keepdims=True)
    yc = y - mu
    var = jnp.mean(yc * yc, axis=-1, keepdims=True)
    y = (yc * lax.rsqrt(var + EPS)) * lg_ref[...] + lb_ref[...]
    h_ref[s0:s0 + rb, :] = _silu(y).astype(BF16)
    return y[rb - SUBLANES_F32:, D_GROUP - LANES:]


def _runtime_zero(tile):
    bits = pltpu.bitcast(tile, jnp.uint32)
    bits = lax.shift_right_logical(lax.shift_right_logical(bits, jnp.uint32(16)), jnp.uint32(16))
    return bits[0, 0].astype(jnp.int32)


def _conv_pointwise(g_ref, w_ref, h_ref, o_ref):
    o = jnp.dot(h_ref[...], w_ref[...], preferred_element_type=F32)
    o_ref[...] = (o * _silu(g_ref[...].astype(F32))).astype(BF16)


def kernel(x, c, norm_g, w_ada, b_ada, w_in, w_fft, na_rel_bias, ret_logit_fwd, ret_logit_bwd,
           conv_w, conv_b, conv_ln_g, conv_ln_b, conv_w_pw, w_out, final_g):
    seq = x.shape[1]
    mod = _ada(c, w_ada, b_ada)
    fft_consts = _fft_constants(seq)
    rope_tbl = _rope_tables(seq)
    bias_tbl = _na_bias_table(na_rel_bias)
    w_fft_b, w_pw_b = w_fft.astype(BF16), conv_w_pw.astype(BF16)
    for l in range(DEPTH):
        zp = _inproj(x, norm_g, mod, w_in, l)
        o_fft = _fourier(zp, w_fft_b, fft_consts, l)
        o_na = _neighbourhood(zp, bias_tbl, l)
        o_ret = _retention(zp, ret_logit_fwd, ret_logit_bwd, rope_tbl, l)
        conv_params = (conv_w, conv_b, conv_ln_g, conv_ln_b)
        x = _outproj((o_fft, o_na, o_ret), zp, conv_params, w_pw_b, w_out, x, mod, final_g, l,
                     final=(l == DEPTH - 1))
    return x
```

```python
import functools
import math

import numpy as np
import jax
import jax.numpy as jnp
from jax import lax
from jax.experimental import pallas as pl
from jax.experimental.pallas import tpu as pltpu

F32 = jnp.float32
BF16 = jnp.bfloat16

D_MODEL = 2048
DEPTH = 2
GRID_W = 64
D_GROUP = 512
HEAD_DIM = 64
N_HEADS = D_GROUP // HEAD_DIM
N_FFT_GROUPS = 4
FFT_GROUP_DIM = D_GROUP // N_FFT_GROUPS
NA_KH = 8
NA_KW = 16
CONV_WIDTH = 31
CONV_HALF = CONV_WIDTH // 2
ROPE_BASE = 10000.0
EPS = 1e-6
N_PIECES = 13
(P_FX, P_FG, P_NQ, P_NK, P_NV, P_NG, P_RQ, P_RK, P_RV, P_RG, P_CA, P_CB, P_CG) = range(N_PIECES)

MXU_DIM = 256
HEADS_PER_TILE = MXU_DIM // HEAD_DIM
VMEM_LIMIT = 56 * 1024 * 1024
SUBLANES_BF16 = 16
SUBLANES_F32 = 8
LANES = 128

NEG_BIG = -1e30


def _params(sem, vmem=VMEM_LIMIT):
    return pltpu.CompilerParams(dimension_semantics=sem, vmem_limit_bytes=vmem)


def _sigmoid(t):
    return 0.5 * jnp.tanh(0.5 * t) + 0.5


def _silu(t):
    h = 0.5 * t
    return h + h * jnp.tanh(h)


ADA_TN = 768
ADA_ROWS = 8
(MOD_SHIFT, MOD_SCALE, MOD_GATE) = range(3)


def _ada_kernel(c_ref, w_ref, b_ref, o_ref):
    ca = _silu(c_ref[...]).astype(BF16)
    w = w_ref[...].astype(BF16)
    o_ref[...] = jnp.dot(ca, w, preferred_element_type=F32) + b_ref[...]


def _ada(c, w_ada, b_ada):
    batch = c.shape[0]
    c_pad = jnp.zeros((ADA_ROWS, D_MODEL), F32).at[:batch].set(c)
    n3 = 3 * D_MODEL
    out = pl.pallas_call(
        _ada_kernel,
        name="ada",
        grid=(DEPTH, n3 // ADA_TN),
        in_specs=[
            pl.BlockSpec((ADA_ROWS, D_MODEL), lambda l, j: (0, 0)),
            pl.BlockSpec((None, D_MODEL, ADA_TN), lambda l, j: (l, 0, j)),
            pl.BlockSpec((None, 1, ADA_TN), lambda l, j: (l, 0, j)),
        ],
        out_specs=pl.BlockSpec((None, ADA_ROWS, ADA_TN), lambda l, j: (l, 0, j)),
        out_shape=jax.ShapeDtypeStruct((DEPTH, ADA_ROWS, n3), F32),
        compiler_params=_params(("arbitrary", "arbitrary")),
    )(c_pad, w_ada, b_ada.reshape(DEPTH, 1, n3))
    return out.reshape(DEPTH, ADA_ROWS, 1, n3)


INPROJ_TM = 512
INPROJ_WROWS = SUBLANES_BF16
INPROJ_WSLOTS = 8
INPROJ_VMEM = 60 * 1024 * 1024


def _inproj_kernel(x_ref, g_ref, sc_ref, sh_ref, w_hbm, o_ref, h0_ref, h1_ref, w_ref, stage_ref, sem,
                   *, n_tiles, layer):
    t = pl.program_id(0)
    wrows = stage_ref.shape[1]
    n_rchunks = D_MODEL // wrows

    def weight_copy(r, slot):
        rows = pl.ds(pl.multiple_of(r * wrows, wrows), wrows)
        return pltpu.make_async_copy(w_hbm.at[layer, rows, :], stage_ref.at[slot], sem.at[slot])

    def load_weight():
        n_slots = stage_ref.shape[0]
        for slot in range(n_slots):
            weight_copy(slot, slot).start()

        def group(i, carry):
            for slot in range(n_slots):
                r = n_slots * i + slot
                weight_copy(r, slot).wait()
                rows = pl.ds(pl.multiple_of(r * wrows, wrows), wrows)
                w_ref[rows, :] = stage_ref[slot].astype(BF16)

                @pl.when(r + n_slots < n_rchunks)
                def _():
                    weight_copy(r + n_slots, slot).start()
            return carry

        lax.fori_loop(0, n_rchunks // n_slots, group, 0)

    def norm(h_ref):
        x = x_ref[...]
        ms = jnp.mean(x * x, axis=-1, keepdims=True)
        h = (x * lax.rsqrt(ms + EPS)) * g_ref[...]
        h = h * (1.0 + sc_ref[...]) + sh_ref[...]
        h_ref[...] = h.astype(BF16)

    def project(h_ref):
        h = h_ref[...]
        for p in range(N_PIECES):
            w = w_ref[:, p * D_GROUP:(p + 1) * D_GROUP]
            o_ref[p] = jnp.dot(h, w, preferred_element_type=F32).astype(BF16)

    odd = t % 2 == 1

    @pl.when(t == 0)
    def _():
        load_weight()
        norm(h0_ref)

    @pl.when(odd)
    def _():
        project(h0_ref)
        norm(h1_ref)

    @pl.when(jnp.logical_and(jnp.logical_not(odd), jnp.logical_and(t > 0, t < n_tiles)))
    def _():
        project(h1_ref)
        norm(h0_ref)

    @pl.when(t == n_tiles)
    def _():
        project(h1_ref)


def _inproj(x, norm_g, mod, w_in, layer):
    batch, seq, _ = x.shape
    tm = INPROJ_TM
    per_batch = seq // tm
    n_tiles = batch * per_batch
    assert n_tiles % 2 == 0 and (D_MODEL // INPROJ_WROWS) % INPROJ_WSLOTS == 0
    norm_tile = lambda t: jnp.minimum(t, n_tiles - 1)
    proj_tile = lambda t: jnp.maximum(t - 1, 0)
    return pl.pallas_call(
        functools.partial(_inproj_kernel, n_tiles=n_tiles, layer=layer),
        name="inproj",
        grid=(n_tiles + 1,),
        in_specs=[
            pl.BlockSpec((None, tm, D_MODEL),
                         lambda t: (norm_tile(t) // per_batch, norm_tile(t) % per_batch, 0)),
            pl.BlockSpec((None, 1, D_MODEL), lambda t: (layer, 0, 0)),
            pl.BlockSpec((None, None, 1, D_MODEL),
                         lambda t: (layer, norm_tile(t) // per_batch, 0, MOD_SCALE)),
            pl.BlockSpec((None, None, 1, D_MODEL),
                         lambda t: (layer, norm_tile(t) // per_batch, 0, MOD_SHIFT)),
            pl.BlockSpec(memory_space=pl.ANY),
        ],
        out_specs=pl.BlockSpec((N_PIECES, None, tm, D_GROUP),
                               lambda t: (0, proj_tile(t) // per_batch, proj_tile(t) % per_batch, 0)),
        out_shape=jax.ShapeDtypeStruct((N_PIECES, batch, seq, D_GROUP), BF16),
        scratch_shapes=[
            pltpu.VMEM((tm, D_MODEL), BF16), pltpu.VMEM((tm, D_MODEL), BF16),
            pltpu.VMEM((D_MODEL, N_PIECES * D_GROUP), BF16),
            pltpu.VMEM((INPROJ_WSLOTS, INPROJ_WROWS, N_PIECES * D_GROUP), F32),
            pltpu.SemaphoreType.DMA((INPROJ_WSLOTS,)),
        ],
        compiler_params=_params(("arbitrary",), INPROJ_VMEM),
    )(x, norm_g.reshape(DEPTH, 1, D_MODEL), mod, mod, w_in)


OUTPROJ_TM = 512
OUTPROJ_WROWS = 64
OUTPROJ_WSLOTS = 8
OUTPROJ_VMEM = 60 * 1024 * 1024


def _outproj_kernel(a0_ref, a1_ref, a2_ref, w_hbm, x_ref, gate_ref, fg_ref,
                    ca_ref, cb_ref, cg_ref, cw_ref, cbias_ref, lg_ref, lb_ref, wpw_ref,
                    o_ref, u_ref, h_ref, cv0_ref, cv1_ref, w_ref, stage_ref, sem,
                    *, final, n_tiles, per_batch, seq, layer):
    t = pl.program_id(0)
    j = jnp.minimum(t, n_tiles - 1) % per_batch

    tm = o_ref.shape[0]
    n_chunks = tm // CONV_ROWS
    tn = D_MODEL // n_chunks
    n_slots = stage_ref.shape[0]

    wrows = stage_ref.shape[1]

    def weight_copy(r):
        slot = r % n_slots
        return pltpu.make_async_copy(w_hbm.at[layer, pl.ds(r * wrows, wrows), :], stage_ref.at[slot],
                                     sem.at[slot])

    def run(proj_cv_ref, conv_cv_ref, load_weight=False):
        if load_weight:
            for c in range(n_slots):
                weight_copy(c).start()
        if conv_cv_ref is not None:
            _conv_glu(j, ca_ref, cb_ref, u_ref, seq=seq)
        if proj_cv_ref is not None:
            a = jnp.concatenate([a0_ref[...], a1_ref[...], a2_ref[...], proj_cv_ref[...]], axis=-1)
            ss = jnp.zeros((tm, 1), F32)
        zero = 0
        for i in range(n_chunks):
            if proj_cv_ref is not None:
                y = jnp.dot(a, w_ref[i + zero], preferred_element_type=F32)
                xn = x_ref[:, i * tn:(i + 1) * tn] + gate_ref[:, i * tn:(i + 1) * tn] * y
                if final:
                    ss = ss + jnp.sum(xn * xn, axis=-1, keepdims=True)
                o_ref[:, i * tn:(i + 1) * tn] = xn
            if conv_cv_ref is not None:
                tail = _conv_rows(i * CONV_ROWS, cw_ref, cbias_ref, lg_ref, lb_ref, u_ref, h_ref)
                zero = _runtime_zero(tail)
            if load_weight:
                n_rchunks = D_MODEL // wrows
                per_block = n_rchunks // n_chunks
                for r in range(i * per_block, (i + 1) * per_block):
                    weight_copy(r).wait()
                    for c in range(n_chunks):
                        w_ref[c, r * wrows:(r + 1) * wrows, :] = stage_ref[
                            r % n_slots, :, c * tn:(c + 1) * tn].astype(BF16)
                    if r + n_slots < n_rchunks:
                        weight_copy(r + n_slots).start()
        if conv_cv_ref is not None:
            _conv_pointwise(cg_ref, wpw_ref, h_ref, conv_cv_ref)
        if proj_cv_ref is not None and final:
            o_ref[...] = (o_ref[...] * lax.rsqrt(ss * (1.0 / D_MODEL) + EPS)) * fg_ref[...]

    odd = t % 2 == 1

    @pl.when(t == 0)
    def _():
        run(None, cv0_ref, load_weight=True)

    @pl.when(odd)
    def _():
        run(cv0_ref, cv1_ref)

    @pl.when(jnp.logical_and(jnp.logical_not(odd), jnp.logical_and(t > 0, t < n_tiles)))
    def _():
        run(cv1_ref, cv0_ref)

    @pl.when(t == n_tiles)
    def _():
        run(cv1_ref, None)


def _outproj(mixed, zp, conv_params, w_pw_bf16, w_out, x, mod, final_g, layer, final):
    batch, seq, _ = x.shape
    tm = OUTPROJ_TM
    per_batch = seq // tm
    n_tiles = batch * per_batch
    n_wchunks = tm // CONV_ROWS
    assert n_tiles % 2 == 0
    conv_tile = lambda t: jnp.minimum(t, n_tiles - 1)
    proj_tile = lambda t: jnp.maximum(t - 1, 0)
    a_spec = pl.BlockSpec((None, tm, D_GROUP),
                          lambda t: (proj_tile(t) // per_batch, proj_tile(t) % per_batch, 0))
    x_spec = pl.BlockSpec((None, tm, D_MODEL),
                          lambda t: (proj_tile(t) // per_batch, proj_tile(t) % per_batch, 0))
    seq_spec = lambda p: pl.BlockSpec((None, None, seq, D_GROUP),
                                      lambda t: (p, conv_tile(t) // per_batch, 0, 0))
    vec = pl.BlockSpec((None, 1, D_GROUP), lambda t: (layer, 0, 0))
    conv_w, conv_b, ln_g, ln_b = conv_params
    as_rows = lambda v: v.reshape(DEPTH, 1, D_GROUP)
    return pl.pallas_call(
        functools.partial(_outproj_kernel, final=final, n_tiles=n_tiles, per_batch=per_batch, seq=seq,
                          layer=layer),
        name="outproj",
        grid=(n_tiles + 1,),
        in_specs=[
            a_spec, a_spec, a_spec,
            pl.BlockSpec(memory_space=pl.ANY),
            x_spec,
            pl.BlockSpec((None, None, 1, D_MODEL), lambda t: (layer, proj_tile(t) // per_batch, 0, MOD_GATE)),
            pl.BlockSpec((1, D_MODEL), lambda t: (0, 0)),
            seq_spec(P_CA), seq_spec(P_CB),
            pl.BlockSpec((None, None, tm, D_GROUP),
                         lambda t: (P_CG, conv_tile(t) // per_batch, conv_tile(t) % per_batch, 0)),
            pl.BlockSpec((None, CONV_WIDTH, D_GROUP), lambda t: (layer, 0, 0)),
            vec, vec, vec,
            pl.BlockSpec((None, D_GROUP, D_GROUP), lambda t: (layer, 0, 0)),
        ],
        out_specs=x_spec,
        out_shape=jax.ShapeDtypeStruct((batch, seq, D_MODEL), F32),
        scratch_shapes=[
            pltpu.VMEM((tm + 2 * CONV_HALO, D_GROUP), F32),
            pltpu.VMEM((tm, D_GROUP), BF16),
            pltpu.VMEM((tm, D_GROUP), BF16), pltpu.VMEM((tm, D_GROUP), BF16),
            pltpu.VMEM((n_wchunks, D_MODEL, D_MODEL // n_wchunks), BF16),
            pltpu.VMEM((OUTPROJ_WSLOTS, OUTPROJ_WROWS, D_MODEL), F32),
            pltpu.SemaphoreType.DMA((OUTPROJ_WSLOTS,)),
        ],
        compiler_params=_params(("arbitrary",), OUTPROJ_VMEM),
    )(*mixed, w_out, x, mod, final_g.reshape(1, D_MODEL), zp, zp, zp, conv_w,
      as_rows(conv_b), as_rows(ln_g), as_rows(ln_b), w_pw_bf16)


FFT_BLOCK = 16


def _fft_constants(seq):
    rows = seq // GRID_W
    assert rows == GRID_W
    n = np.arange(GRID_W)
    ang1 = 2.0 * np.pi * ((n[:, None] * n[None, :]) % GRID_W) / GRID_W
    f1 = np.concatenate([np.cos(ang1), -np.sin(ang1)], axis=0)
    k1 = n[:, None, None]
    k2 = n[None, :, None]
    s2 = n[None, None, :]
    ang2 = 2.0 * np.pi * ((s2 * (k1 + GRID_W * k2)) % seq) / seq
    mr, mi = np.cos(ang2), -np.sin(ang2)
    m2 = np.concatenate([np.concatenate([mr, -mi], axis=2),
                         np.concatenate([mi, mr], axis=2)], axis=1)
    c = np.arange(FFT_GROUP_DIM)
    angc = 2.0 * np.pi * ((c[:, None] * c[None, :]) % FFT_GROUP_DIM) / FFT_GROUP_DIM
    fc = np.concatenate([np.cos(angc), np.sin(angc)], axis=0)
    return tuple(jnp.asarray(t, F32).astype(BF16) for t in (f1, m2, fc))


def _fft_kernel(u_ref, gate_ref, f1_ref, m2_ref, fc_ref, w_ref, o_ref, ut_ref, g_ref, gt_ref, y_ref, *, norm):
    grp = pl.program_id(1)
    n = GRID_W
    blk = FFT_BLOCK
    sub = SUBLANES_F32

    u3 = u_ref[...].astype(F32).reshape(n, n, FFT_GROUP_DIM)
    ut_ref[...] = jnp.swapaxes(u3, 0, 1).astype(BF16)

    def stage1(i, carry):
        for j in range(blk):
            s2 = i * blk + j
            g_ref[s2] = jnp.dot(f1_ref[...], ut_ref[s2], preferred_element_type=F32)
        return carry

    lax.fori_loop(0, n // blk, stage1, 0)

    for q in range(2 * n // sub):
        part = g_ref[:, q * sub:(q + 1) * sub, :]
        gt_ref[q * sub:(q + 1) * sub] = jnp.swapaxes(part, 0, 1).astype(BF16)

    def stage2(i, carry):
        lhs = []
        for j in range(blk):
            k1 = i * blk + j
            gk = jnp.concatenate([gt_ref[k1], gt_ref[n + k1]], axis=0)
            xk = jnp.dot(m2_ref[k1], gk, preferred_element_type=F32)
            lhs.append(jnp.concatenate([xk[:n], xk[n:]], axis=-1))
        lhs = jnp.concatenate(lhs, axis=0).astype(BF16)
        y = jnp.dot(lhs, fc_ref[...], preferred_element_type=F32) * norm
        y_ref[grp, pl.ds(pl.multiple_of(i * blk, blk), blk)] = y.reshape(blk, n, FFT_GROUP_DIM)
        return carry

    lax.fori_loop(0, n // blk, stage2, 0)

    @pl.when(grp == N_FFT_GROUPS - 1)
    def _():
        def stage3(i, carry):
            k2 = pl.ds(pl.multiple_of(i * sub, sub), sub)
            tiles = []
            for g in range(N_FFT_GROUPS):
                part = y_ref[g, :, k2, :]
                tiles.append(jnp.swapaxes(part, 0, 1).reshape(sub * n, FFT_GROUP_DIM))
            y = jnp.concatenate(tiles, axis=-1).astype(BF16)
            o = jnp.dot(y, w_ref[...], preferred_element_type=F32)
            rows = pl.ds(pl.multiple_of(i * sub * n, sub * n), sub * n)
            o_ref[rows, :] = (o * _silu(gate_ref[rows, :].astype(F32))).astype(BF16)
            return carry

        lax.fori_loop(0, n // sub, stage3, 0)


def _fourier(zp, w_fft_bf16, consts, layer):
    _, batch, seq, _ = zp.shape
    f1, m2, fc = consts
    n = GRID_W
    const = lambda shape: pl.BlockSpec(shape, lambda b, g: (0,) * len(shape))
    return pl.pallas_call(
        functools.partial(_fft_kernel, norm=1.0 / math.sqrt(seq * FFT_GROUP_DIM)),
        name="fourier",
        grid=(batch, N_FFT_GROUPS),
        in_specs=[
            pl.BlockSpec((None, None, seq, FFT_GROUP_DIM), lambda b, g: (P_FX, b, 0, g)),
            pl.BlockSpec((None, None, seq, D_GROUP), lambda b, g: (P_FG, b, 0, 0)),
            const((2 * n, n)), const((n, 2 * n, 2 * n)), const((2 * FFT_GROUP_DIM, FFT_GROUP_DIM)),
            pl.BlockSpec((None, D_GROUP, D_GROUP), lambda b, g: (layer, 0, 0)),
        ],
        out_specs=pl.BlockSpec((None, seq, D_GROUP), lambda b, g: (b, 0, 0)),
        out_shape=jax.ShapeDtypeStruct((batch, seq, D_GROUP), BF16),
        scratch_shapes=[
            pltpu.VMEM((n, n, FFT_GROUP_DIM), BF16),
            pltpu.VMEM((n, 2 * n, FFT_GROUP_DIM), F32),
            pltpu.VMEM((2 * n, n, FFT_GROUP_DIM), BF16),
            pltpu.VMEM((N_FFT_GROUPS, n, n, FFT_GROUP_DIM), F32),
        ],
        compiler_params=_params(("arbitrary", "arbitrary")),
    )(zp, zp, f1, m2, fc, w_fft_bf16)


NA_ROWS_PER_STEP = 8


def _na_bias_table(rel_bias):
    n_dc = 2 * NA_KW - 1
    col = np.arange(GRID_W)
    col_start = np.clip(col - NA_KW // 2, 0, GRID_W - NA_KW)
    rel_c = col[None, :] - col_start[:, None]
    col_in = (rel_c >= 0) & (rel_c < NA_KW)
    dc = np.clip(col[None, :] - col[:, None] + NA_KW - 1, 0, n_dc - 1)
    onehot = (dc[None] == np.arange(n_dc)[:, None, None]).astype(np.float32)
    zeros = np.zeros_like(onehot)
    onehot2 = np.concatenate([np.concatenate([onehot, zeros], axis=-1),
                              np.concatenate([zeros, onehot], axis=-1)], axis=0)
    rb = rel_bias.astype(F32)
    rb2 = jnp.concatenate([rb[:, :, :-1], rb[:, :, 1:]], axis=-1)
    tbl = jnp.einsum('lhdj,jqk->ldhqk', rb2, jnp.asarray(onehot2),
                     precision=lax.Precision.HIGHEST)
    col_in2 = np.concatenate([col_in, col_in], axis=-1)
    return jnp.where(col_in2[None, None, None], tbl, NEG_BIG)


def _head_select_mask():
    r = lax.broadcasted_iota(jnp.int32, (MXU_DIM, MXU_DIM), 0) // HEAD_DIM
    c = lax.broadcasted_iota(jnp.int32, (MXU_DIM, MXU_DIM), 1) // HEAD_DIM
    return r == c


def _stack_heads(t):
    n = t.shape[0]
    reps = jnp.concatenate([t] * HEADS_PER_TILE, axis=0)
    row_head = lax.broadcasted_iota(jnp.int32, reps.shape, 0) // n
    lane_head = lax.broadcasted_iota(jnp.int32, reps.shape, 1) // HEAD_DIM
    return jnp.where(row_head == lane_head, reps, jnp.zeros_like(reps))


def _unstack_heads(t4, n):
    lane_head = lax.broadcasted_iota(jnp.int32, (n, MXU_DIM), 1) // HEAD_DIM
    out = jnp.zeros((n, MXU_DIM), t4.dtype)
    for h in range(HEADS_PER_TILE):
        out = jnp.where(lane_head == h, t4[h * n:(h + 1) * n], out)
    return out


def _na_kernel(q_ref, k_ref, v_ref, g_ref, bias_ref, o_ref, *, rows):
    kh = min(NA_KH, rows)
    step = pl.program_id(1)

    def row_body(rr, carry):
        r = step * NA_ROWS_PER_STEP + rr
        r_start = jnp.clip(r - kh // 2, 0, rows - kh)
        variant = r - r_start
        q_off = pl.multiple_of(rr * GRID_W, GRID_W)
        k_off = pl.multiple_of(r_start * GRID_W, GRID_W)
        for cg in range(D_GROUP // MXU_DIM):
            lanes = slice(cg * MXU_DIM, (cg + 1) * MXU_DIM)
            q = q_ref[pl.ds(q_off, GRID_W), lanes] * (HEAD_DIM ** -0.5)
            kw = k_ref[pl.ds(k_off, kh * GRID_W), lanes]
            vw = v_ref[pl.ds(k_off, kh * GRID_W), lanes]
            q4 = _stack_heads(q)
            s = lax.dot_general(q4, kw, (((1,), (1,)), ((), ())), preferred_element_type=F32)
            heads = slice(cg * HEADS_PER_TILE, (cg + 1) * HEADS_PER_TILE)
            bias = jnp.concatenate(
                [bias_ref[NA_KH - 1 - variant + a, heads].reshape(HEADS_PER_TILE * GRID_W, 2 * GRID_W)
                 for a in range(0, kh, 2)], axis=-1)
            s = s + bias
            m = jnp.max(s, axis=-1, keepdims=True)
            e = jnp.exp(s - m)
            l = jnp.sum(e, axis=-1, keepdims=True)
            o4 = jnp.dot(e.astype(BF16), vw, preferred_element_type=F32) / l
            o = _unstack_heads(o4, GRID_W)
            gate = g_ref[pl.ds(q_off, GRID_W), lanes].astype(F32)
            o_ref[pl.ds(q_off, GRID_W), lanes] = (o * _silu(gate)).astype(BF16)
        return carry

    lax.fori_loop(0, NA_ROWS_PER_STEP, row_body, 0, unroll=True)


def _neighbourhood(zp, bias_tbl, layer):
    _, batch, seq, _ = zp.shape
    rows = seq // GRID_W
    tq = NA_ROWS_PER_STEP * GRID_W
    q_spec = lambda p: pl.BlockSpec((None, None, tq, D_GROUP), lambda b, i: (p, b, i, 0))
    kv_spec = lambda p: pl.BlockSpec((None, None, seq, D_GROUP), lambda b, i: (p, b, 0, 0))
    return pl.pallas_call(
        functools.partial(_na_kernel, rows=rows),
        name="natten",
        grid=(batch, rows // NA_ROWS_PER_STEP),
        in_specs=[
            q_spec(P_NQ), kv_spec(P_NK), kv_spec(P_NV), q_spec(P_NG),
            pl.BlockSpec((None,) + bias_tbl.shape[1:], lambda b, i: (layer, 0, 0, 0, 0),
                         pipeline_mode=pl.Buffered(1)),
        ],
        out_specs=pl.BlockSpec((None, tq, D_GROUP), lambda b, i: (b, i, 0)),
        out_shape=jax.ShapeDtypeStruct((batch, seq, D_GROUP), BF16),
        compiler_params=_params(("arbitrary", "arbitrary")),
    )(zp, zp, zp, zp, bias_tbl)


RET_CHUNK = MXU_DIM


def _log_sigmoid(t):
    return jnp.minimum(t, 0.0) - jnp.log1p(jnp.exp(-jnp.abs(t)))


def _rope_tables(seq):
    half = HEAD_DIM // 2
    inv = ROPE_BASE ** (-jnp.arange(half, dtype=F32) / half)
    ang = jnp.arange(seq, dtype=F32)[:, None] * inv[None, :]
    cos, sin = jnp.cos(ang), jnp.sin(ang)
    cos2 = jnp.concatenate([cos, cos], axis=-1)
    sin2 = jnp.concatenate([-sin, sin], axis=-1)
    return jnp.tile(cos2, (1, 2)), jnp.tile(sin2, (1, 2))


def _ret_kernel(lf_s_ref, lb_s_ref, q_ref, k_ref, v_ref, g_ref, cos_ref, sin_ref, lfl_ref, lbl_ref,
                o_ref, qr_ref, kr_ref, ob_ref, dmat_ref, dec_ref, state_ref, ones_ref, *, seq, layer):
    C = RET_CHUNK
    n_chunks = seq // C
    cg = pl.program_id(1)
    ones_bd = jnp.where(_head_select_mask(), 1.0, 0.0).astype(BF16)
    ones_ref[...] = jnp.concatenate([ones_bd, ones_bd], axis=0)

    ri = lax.broadcasted_iota(jnp.int32, (C, C), 0)
    ci = lax.broadcasted_iota(jnp.int32, (C, C), 1)
    diff = (ri - ci).astype(F32)
    for hh in range(HEADS_PER_TILE):
        h = layer * N_HEADS + cg * HEADS_PER_TILE + hh
        lf = _log_sigmoid(jnp.full((C, C), lf_s_ref[h], F32))
        lb = _log_sigmoid(jnp.full((C, C), lb_s_ref[h], F32))
        dmat_ref[hh] = jnp.where(diff >= 0, jnp.exp(lf * diff), jnp.exp(lb * (-diff)))
    lfl = _log_sigmoid(lfl_ref[...])
    lbl = _log_sigmoid(lbl_ref[...])
    idx = lax.broadcasted_iota(jnp.int32, (C, MXU_DIM), 0).astype(F32)
    dec_ref[0] = jnp.exp(lfl * (idx + 1.0))
    dec_ref[1] = jnp.exp(lfl * (C - 1.0 - idx))
    dec_ref[2] = jnp.exp(lbl * (C - idx))
    dec_ref[3] = jnp.exp(lbl * idx)
    cd_f = jnp.exp(lfl * float(C))
    cd_b = jnp.exp(lbl * float(C))

    def rope(t, rows):
        lane = lax.broadcasted_iota(jnp.int32, t.shape, 1)
        first_half = (lane % HEAD_DIM) < (HEAD_DIM // 2)
        cs = cos_ref[rows, :]
        sn = sin_ref[rows, :]
        cs = jnp.concatenate([cs, cs], axis=-1)
        sn = jnp.concatenate([sn, sn], axis=-1)
        swapped = jnp.where(first_half, pltpu.roll(t, MXU_DIM - HEAD_DIM // 2, 1),
                            pltpu.roll(t, HEAD_DIM // 2, 1))
        return t * cs + swapped * sn

    def kv_update(k, v, kd, cd):
        kv = lax.dot_general((k * kd).astype(BF16), v, (((0,), (0,)), ((), ())),
                             preferred_element_type=F32)
        state_ref[...] = cd * state_ref[...] + jnp.where(_head_select_mask(), kv, 0.0)

    state_ref[...] = jnp.zeros_like(state_ref)

    def bwd_body(i, carry):
        n = n_chunks - 1 - i
        rows = pl.ds(pl.multiple_of(n * C, C), C)
        q = rope(q_ref[rows, :].astype(F32), rows) * (HEAD_DIM ** -0.5)
        k = rope(k_ref[rows, :].astype(F32), rows)
        qr_ref[rows, :] = q.astype(BF16)
        kr_ref[rows, :] = k.astype(BF16)
        ob_ref[rows, :] = jnp.dot((q * dec_ref[2]).astype(BF16), state_ref[...].astype(BF16),
                                  preferred_element_type=F32)
        kv_update(k, v_ref[rows, :], dec_ref[3], cd_b)
        return carry

    lax.fori_loop(0, n_chunks, bwd_body, 0, unroll=4)

    state_ref[...] = jnp.zeros_like(state_ref)

    def fwd_body(n, carry):
        rows = pl.ds(pl.multiple_of(n * C, C), C)
        qb = qr_ref[rows, :]
        kb = kr_ref[rows, :]
        v = v_ref[rows, :]
        o_cross = jnp.dot((qb.astype(F32) * dec_ref[0]).astype(BF16), state_ref[...].astype(BF16),
                          preferred_element_type=F32)
        q4 = _stack_heads(qb)
        sc = lax.dot_general(q4, kb, (((1,), (1,)), ((), ())), preferred_element_type=F32)
        sc = sc * dmat_ref[...].reshape(HEADS_PER_TILE * C, C)
        o4 = jnp.dot(sc.astype(BF16), v, preferred_element_type=F32)
        o = _unstack_heads(o4, C) + o_cross + ob_ref[rows, :]
        sq = o * o
        hi = sq.astype(BF16)
        lo = (sq - hi.astype(F32)).astype(BF16)
        ss = jnp.dot(jnp.concatenate([hi, lo], axis=-1), ones_ref[...], preferred_element_type=F32)
        inv = lax.rsqrt(ss * (1.0 / HEAD_DIM) + EPS)
        gate = g_ref[rows, :].astype(F32)
        o_ref[rows, :] = (o * inv * _silu(gate)).astype(BF16)
        kv_update(kb.astype(F32), v, dec_ref[1], cd_f)
        return carry

    lax.fori_loop(0, n_chunks, fwd_body, 0, unroll=4)


def _retention(zp, logit_f, logit_b, rope_tbl, layer):
    _, batch, seq, _ = zp.shape
    cos_t, sin_t = rope_tbl
    n_cg = D_GROUP // MXU_DIM
    spec = lambda p: pl.BlockSpec((None, None, seq, MXU_DIM), lambda b, c, *_: (p, b, 0, c))
    tbl_spec = pl.BlockSpec((seq, 2 * HEAD_DIM), lambda b, c, *_: (0, 0))
    lane_spec = pl.BlockSpec((None, 1, MXU_DIM), lambda b, c, *_: (layer, 0, c))
    per_lane = lambda v: jnp.repeat(v.astype(F32), HEAD_DIM, axis=1)[:, None, :]
    return pl.pallas_call(
        functools.partial(_ret_kernel, seq=seq, layer=layer),
        name="retention",
        grid_spec=pltpu.PrefetchScalarGridSpec(
            num_scalar_prefetch=2,
            grid=(batch, n_cg),
            in_specs=[spec(P_RQ), spec(P_RK), spec(P_RV), spec(P_RG), tbl_spec, tbl_spec,
                      lane_spec, lane_spec],
            out_specs=pl.BlockSpec((None, seq, MXU_DIM), lambda b, c, *_: (b, 0, c)),
            scratch_shapes=[
                pltpu.VMEM((seq, MXU_DIM), BF16),
                pltpu.VMEM((seq, MXU_DIM), BF16),
                pltpu.VMEM((seq, MXU_DIM), F32),
                pltpu.VMEM((HEADS_PER_TILE, RET_CHUNK, RET_CHUNK), F32),
                pltpu.VMEM((4, RET_CHUNK, MXU_DIM), F32),
                pltpu.VMEM((MXU_DIM, MXU_DIM), F32),
                pltpu.VMEM((2 * MXU_DIM, MXU_DIM), BF16),
            ],
        ),
        out_shape=jax.ShapeDtypeStruct((batch, seq, D_GROUP), BF16),
        compiler_params=_params(("arbitrary", "arbitrary")),
    )(logit_f.astype(F32).reshape(-1), logit_b.astype(F32).reshape(-1), zp, zp, zp, zp, cos_t, sin_t,
      per_lane(logit_f), per_lane(logit_b))


CONV_HALO = SUBLANES_BF16
CONV_ROWS = 64


def _conv_glu(j, a_ref, b_ref, u_ref, *, seq):
    halo = CONV_HALO
    ts = u_ref.shape[0] - 2 * halo
    n_tiles = seq // ts

    def glu(start, size):
        rows = pl.ds(pl.multiple_of(start, halo), size)
        a = a_ref[rows, :].astype(F32)
        return a * _sigmoid(b_ref[rows, :].astype(F32))

    base = j * ts
    u_ref[halo:halo + ts, :] = glu(base, ts)
    lo = glu(jnp.maximum(base - halo, 0), halo)
    u_ref[0:halo, :] = jnp.where(j > 0, lo, 0.0)
    hi = glu(jnp.minimum(base + ts, seq - halo), halo)
    u_ref[halo + ts:, :] = jnp.where(j < n_tiles - 1, hi, 0.0)


def _conv_rows(s0, cw_ref, cb_ref, lg_ref, lb_ref, u_ref, h_ref):
    rb = CONV_ROWS
    first = CONV_HALO - CONV_HALF
    parts = []
    for lt in range(D_GROUP // LANES):
        lanes = slice(lt * LANES, (lt + 1) * LANES)
        y = None
        for r in range(SUBLANES_F32):
            acc = None
            for o in range(r, first + CONV_WIDTH, SUBLANES_F32):
                if o < first:
                    continue
                w = o - first
                lo_row = s0 + (o - r)
                term = u_ref[lo_row:lo_row + rb + SUBLANES_F32, lanes] * cw_ref[w:w + 1, lanes]
                acc = term if acc is None else acc + term
            shifted = acc[r:r + rb]
            y = shifted if y is None else y + shifted
        parts.append(y)
    y = jnp.concatenate(parts, axis=-1) + cb_ref[...]
    mu = jnp.mean(y, axis=-1, keepdims=True)
    yc = y - mu
    var = jnp.mean(yc * yc, axis=-1, keepdims=True)
    y = (yc * lax.rsqrt(var + EPS)) * lg_ref[...] + lb_ref[...]
    h_ref[s0:s0 + rb, :] = _silu(y).astype(BF16)
    return y[rb - SUBLANES_F32:, D_GROUP - LANES:]


def _runtime_zero(tile):
    bits = pltpu.bitcast(tile, jnp.uint32)
    bits = lax.shift_right_logical(lax.shift_right_logical(bits, jnp.uint32(16)), jnp.uint32(16))
    return bits[0, 0].astype(jnp.int32)


def _conv_pointwise(g_ref, w_ref, h_ref, o_ref):
    o = jnp.dot(h_ref[...], w_ref[...], preferred_element_type=F32)
    o_ref[...] = (o * _silu(g_ref[...].astype(F32))).astype(BF16)


def kernel(x, c, norm_g, w_ada, b_ada, w_in, w_fft, na_rel_bias, ret_logit_fwd, ret_logit_bwd,
           conv_w, conv_b, conv_ln_g, conv_ln_b, conv_w_pw, w_out, final_g):
    seq = x.shape[1]
    mod = _ada(c, w_ada, b_ada)
    fft_consts = _fft_constants(seq)
    rope_tbl = _rope_tables(seq)
    bias_tbl = _na_bias_table(na_rel_bias)
    w_fft_b, w_pw_b = w_fft.astype(BF16), conv_w_pw.astype(BF16)
    for l in range(DEPTH):
        zp = _inproj(x, norm_g, mod, w_in, l)
        o_fft = _fourier(zp, w_fft_b, fft_consts, l)
        o_na = _neighbourhood(zp, bias_tbl, l)
        o_ret = _retention(zp, ret_logit_fwd, ret_logit_bwd, rope_tbl, l)
        conv_params = (conv_w, conv_b, conv_ln_g, conv_ln_b)
        x = _outproj((o_fft, o_na, o_ret), zp, conv_params, w_pw_b, w_out, x, mod, final_g, l,
                     final=(l == DEPTH - 1))
    return x
```

```python
import functools
import math

import numpy as np
import jax
import jax.numpy as jnp
from jax import lax
from jax.experimental import pallas as pl
from jax.experimental.pallas import tpu as pltpu

F32 = jnp.float32
BF16 = jnp.bfloat16

D_MODEL = 2048
DEPTH = 2
GRID_W = 64
D_GROUP = 512
HEAD_DIM = 64
N_HEADS = D_GROUP // HEAD_DIM
N_FFT_GROUPS = 4
FFT_GROUP_DIM = D_GROUP // N_FFT_GROUPS
NA_KH = 8
NA_KW = 16
CONV_WIDTH = 31
CONV_HALF = CONV_WIDTH // 2
ROPE_BASE = 10000.0
EPS = 1e-6
N_PIECES = 13
(P_FX, P_FG, P_NQ, P_NK, P_NV, P_NG, P_RQ, P_RK, P_RV, P_RG, P_CA, P_CB, P_CG) = range(N_PIECES)

MXU_DIM = 256
HEADS_PER_TILE = MXU_DIM // HEAD_DIM
VMEM_LIMIT = 56 * 1024 * 1024
SUBLANES_BF16 = 16
SUBLANES_F32 = 8
LANES = 128

NEG_BIG = -1e30


def _params(sem, vmem=VMEM_LIMIT):
    return pltpu.CompilerParams(dimension_semantics=sem, vmem_limit_bytes=vmem)


def _sigmoid(t):
    return 0.5 * jnp.tanh(0.5 * t) + 0.5


def _silu(t):
    h = 0.5 * t
    return h + h * jnp.tanh(h)


ADA_TN = 768
ADA_ROWS = 8
(MOD_SHIFT, MOD_SCALE, MOD_GATE) = range(3)


def _ada_kernel(c_ref, w_ref, b_ref, o_ref):
    ca = _silu(c_ref[...]).astype(BF16)
    w = w_ref[...].astype(BF16)
    o_ref[...] = jnp.dot(ca, w, preferred_element_type=F32) + b_ref[...]


def _ada(c, w_ada, b_ada):
    batch = c.shape[0]
    c_pad = jnp.zeros((ADA_ROWS, D_MODEL), F32).at[:batch].set(c)
    n3 = 3 * D_MODEL
    out = pl.pallas_call(
        _ada_kernel,
        name="ada",
        grid=(DEPTH, n3 // ADA_TN),
        in_specs=[
            pl.BlockSpec((ADA_ROWS, D_MODEL), lambda l, j: (0, 0)),
            pl.BlockSpec((None, D_MODEL, ADA_TN), lambda l, j: (l, 0, j)),
            pl.BlockSpec((None, 1, ADA_TN), lambda l, j: (l, 0, j)),
        ],
        out_specs=pl.BlockSpec((None, ADA_ROWS, ADA_TN), lambda l, j: (l, 0, j)),
        out_shape=jax.ShapeDtypeStruct((DEPTH, ADA_ROWS, n3), F32),
        compiler_params=_params(("arbitrary", "arbitrary")),
    )(c_pad, w_ada, b_ada.reshape(DEPTH, 1, n3))
    return out.reshape(DEPTH, ADA_ROWS, 1, n3)


INPROJ_TM = 512
INPROJ_WROWS = SUBLANES_BF16
INPROJ_WSLOTS = 8
INPROJ_VMEM = 60 * 1024 * 1024


def _inproj_kernel(x_ref, g_ref, sc_ref, sh_ref, w_in_ref, o_ref, h0_ref, h1_ref, *staging,
                   n_tiles, layer):
    t = pl.program_id(0)
    if staging:
        w_hbm = w_in_ref
        w_ref, stage_ref, sem = staging
        wrows = stage_ref.shape[1]
        n_rchunks = D_MODEL // wrows
    else:
        w_ref = w_in_ref

    def weight_copy(r, slot):
        rows = pl.ds(pl.multiple_of(r * wrows, wrows), wrows)
        return pltpu.make_async_copy(w_hbm.at[layer, rows, :], stage_ref.at[slot], sem.at[slot])

    def load_weight():
        n_slots = stage_ref.shape[0]
        for slot in range(n_slots):
            weight_copy(slot, slot).start()

        def group(i, carry):
            for slot in range(n_slots):
                r = n_slots * i + slot
                weight_copy(r, slot).wait()
                rows = pl.ds(pl.multiple_of(r * wrows, wrows), wrows)
                w_ref[rows, :] = stage_ref[slot].astype(BF16)

                @pl.when(r + n_slots < n_rchunks)
                def _():
                    weight_copy(r + n_slots, slot).start()
            return carry

        lax.fori_loop(0, n_rchunks // n_slots, group, 0)

    def norm(h_ref):
        x = x_ref[...]
        ms = jnp.mean(x * x, axis=-1, keepdims=True)
        h = (x * lax.rsqrt(ms + EPS)) * g_ref[...]
        h = h * (1.0 + sc_ref[...]) + sh_ref[...]
        h_ref[...] = h.astype(BF16)

    def project(h_ref):
        h = h_ref[...]
        for p in range(N_PIECES):
            w = w_ref[:, p * D_GROUP:(p + 1) * D_GROUP]
            o_ref[p] = jnp.dot(h, w, preferred_element_type=F32).astype(BF16)

    odd = t % 2 == 1

    @pl.when(t == 0)
    def _():
        if staging:
            load_weight()
        norm(h0_ref)

    @pl.when(odd)
    def _():
        project(h0_ref)
        norm(h1_ref)

    @pl.when(jnp.logical_and(jnp.logical_not(odd), jnp.logical_and(t > 0, t < n_tiles)))
    def _():
        project(h1_ref)
        norm(h0_ref)

    @pl.when(t == n_tiles)
    def _():
        project(h1_ref)


def _inproj(x, norm_g, mod, w_in, layer, w_bf16=None):
    batch, seq, _ = x.shape
    tm = INPROJ_TM
    per_batch = seq // tm
    n_tiles = batch * per_batch
    assert n_tiles % 2 == 0 and (D_MODEL // INPROJ_WROWS) % INPROJ_WSLOTS == 0
    norm_tile = lambda t: jnp.minimum(t, n_tiles - 1)
    proj_tile = lambda t: jnp.maximum(t - 1, 0)
    return pl.pallas_call(
        functools.partial(_inproj_kernel, n_tiles=n_tiles, layer=layer),
        name="inproj",
        grid=(n_tiles + 1,),
        in_specs=[
            pl.BlockSpec((None, tm, D_MODEL),
                         lambda t: (norm_tile(t) // per_batch, norm_tile(t) % per_batch, 0)),
            pl.BlockSpec((None, 1, D_MODEL), lambda t: (layer, 0, 0)),
            pl.BlockSpec((None, None, 1, D_MODEL),
                         lambda t: (layer, norm_tile(t) // per_batch, 0, MOD_SCALE)),
            pl.BlockSpec((None, None, 1, D_MODEL),
                         lambda t: (layer, norm_tile(t) // per_batch, 0, MOD_SHIFT)),
            (pl.BlockSpec(memory_space=pl.ANY) if w_bf16 is None else
             pl.BlockSpec((D_MODEL, N_PIECES * D_GROUP), lambda t: (0, 0), pipeline_mode=pl.Buffered(1))),
        ],
        out_specs=pl.BlockSpec((N_PIECES, None, tm, D_GROUP),
                               lambda t: (0, proj_tile(t) // per_batch, proj_tile(t) % per_batch, 0)),
        out_shape=jax.ShapeDtypeStruct((N_PIECES, batch, seq, D_GROUP), BF16),
        scratch_shapes=[
            pltpu.VMEM((tm, D_MODEL), BF16), pltpu.VMEM((tm, D_MODEL), BF16),
        ] + ([] if w_bf16 is not None else [
            pltpu.VMEM((D_MODEL, N_PIECES * D_GROUP), BF16),
            pltpu.VMEM((INPROJ_WSLOTS, INPROJ_WROWS, N_PIECES * D_GROUP), F32),
            pltpu.SemaphoreType.DMA((INPROJ_WSLOTS,)),
        ]),
        compiler_params=_params(("arbitrary",), INPROJ_VMEM),
    )(x, norm_g.reshape(DEPTH, 1, D_MODEL), mod, mod, w_in if w_bf16 is None else w_bf16)


OUTPROJ_TM = 512
OUTPROJ_WROWS = 64
OUTPROJ_WSLOTS = 8
OUTPROJ_VMEM = 60 * 1024 * 1024


def _outproj_kernel(a0_ref, a1_ref, a2_ref, w_hbm, x_ref, gate_ref, fg_ref,
                    ca_ref, cb_ref, cg_ref, cw_ref, cbias_ref, lg_ref, lb_ref, wpw_ref,
                    o_ref, u_ref, h_ref, cv0_ref, cv1_ref, w_ref, stage_ref, sem,
                    *, final, n_tiles, per_batch, seq, layer):
    t = pl.program_id(0)
    j = jnp.minimum(t, n_tiles - 1) % per_batch

    tm = o_ref.shape[0]
    n_chunks = tm // CONV_ROWS
    tn = D_MODEL // n_chunks
    n_slots = stage_ref.shape[0]

    wrows = stage_ref.shape[1]

    def weight_copy(r):
        slot = r % n_slots
        return pltpu.make_async_copy(w_hbm.at[layer, pl.ds(r * wrows, wrows), :], stage_ref.at[slot],
                                     sem.at[slot])

    def run(proj_cv_ref, conv_cv_ref, load_weight=False):
        if load_weight:
            for c in range(n_slots):
                weight_copy(c).start()
        if conv_cv_ref is not None:
            _conv_glu(j, ca_ref, cb_ref, u_ref, seq=seq)
        if proj_cv_ref is not None:
            a = jnp.concatenate([a0_ref[...], a1_ref[...], a2_ref[...], proj_cv_ref[...]], axis=-1)
            ss = jnp.zeros((tm, 1), F32)
        zero = 0
        for i in range(n_chunks):
            if proj_cv_ref is not None:
                y = jnp.dot(a, w_ref[i + zero], preferred_element_type=F32)
                xn = x_ref[:, i * tn:(i + 1) * tn] + gate_ref[:, i * tn:(i + 1) * tn] * y
                if final:
                    ss = ss + jnp.sum(xn * xn, axis=-1, keepdims=True)
                o_ref[:, i * tn:(i + 1) * tn] = xn
            if conv_cv_ref is not None:
                tail = _conv_rows(i * CONV_ROWS, cw_ref, cbias_ref, lg_ref, lb_ref, u_ref, h_ref)
                zero = _runtime_zero(tail)
            if load_weight:
                n_rchunks = D_MODEL // wrows
                per_block = n_rchunks // n_chunks
                for r in range(i * per_block, (i + 1) * per_block):
                    weight_copy(r).wait()
                    for c in range(n_chunks):
                        w_ref[c, r * wrows:(r + 1) * wrows, :] = stage_ref[
                            r % n_slots, :, c * tn:(c + 1) * tn].astype(BF16)
                    if r + n_slots < n_rchunks:
                        weight_copy(r + n_slots).start()
        if conv_cv_ref is not None:
            _conv_pointwise(cg_ref, wpw_ref, h_ref, conv_cv_ref)
        if proj_cv_ref is not None and final:
            o_ref[...] = (o_ref[...] * lax.rsqrt(ss * (1.0 / D_MODEL) + EPS)) * fg_ref[...]

    odd = t % 2 == 1

    @pl.when(t == 0)
    def _():
        run(None, cv0_ref, load_weight=True)

    @pl.when(odd)
    def _():
        run(cv0_ref, cv1_ref)

    @pl.when(jnp.logical_and(jnp.logical_not(odd), jnp.logical_and(t > 0, t < n_tiles)))
    def _():
        run(cv1_ref, cv0_ref)

    @pl.when(t == n_tiles)
    def _():
        run(cv1_ref, None)


def _outproj(mixed, zp, conv_params, w_pw_bf16, w_out, x, mod, final_g, layer, final):
    batch, seq, _ = x.shape
    tm = OUTPROJ_TM
    per_batch = seq // tm
    n_tiles = batch * per_batch
    n_wchunks = tm // CONV_ROWS
    assert n_tiles % 2 == 0
    conv_tile = lambda t: jnp.minimum(t, n_tiles - 1)
    proj_tile = lambda t: jnp.maximum(t - 1, 0)
    a_spec = pl.BlockSpec((None, tm, D_GROUP),
                          lambda t: (proj_tile(t) // per_batch, proj_tile(t) % per_batch, 0))
    x_spec = pl.BlockSpec((None, tm, D_MODEL),
                          lambda t: (proj_tile(t) // per_batch, proj_tile(t) % per_batch, 0))
    seq_spec = lambda p: pl.BlockSpec((None, None, seq, D_GROUP),
                                      lambda t: (p, conv_tile(t) // per_batch, 0, 0))
    vec = pl.BlockSpec((None, 1, D_GROUP), lambda t: (layer, 0, 0))
    conv_w, conv_b, ln_g, ln_b = conv_params
    as_rows = lambda v: v.reshape(DEPTH, 1, D_GROUP)
    return pl.pallas_call(
        functools.partial(_outproj_kernel, final=final, n_tiles=n_tiles, per_batch=per_batch, seq=seq,
                          layer=layer),
        name="outproj",
        grid=(n_tiles + 1,),
        in_specs=[
            a_spec, a_spec, a_spec,
            pl.BlockSpec(memory_space=pl.ANY),
            x_spec,
            pl.BlockSpec((None, None, 1, D_MODEL), lambda t: (layer, proj_tile(t) // per_batch, 0, MOD_GATE)),
            pl.BlockSpec((1, D_MODEL), lambda t: (0, 0)),
            seq_spec(P_CA), seq_spec(P_CB),
            pl.BlockSpec((None, None, tm, D_GROUP),
                         lambda t: (P_CG, conv_tile(t) // per_batch, conv_tile(t) % per_batch, 0)),
            pl.BlockSpec((None, CONV_WIDTH, D_GROUP), lambda t: (layer, 0, 0)),
            vec, vec, vec,
            pl.BlockSpec((None, D_GROUP, D_GROUP), lambda t: (layer, 0, 0)),
        ],
        out_specs=x_spec,
        out_shape=jax.ShapeDtypeStruct((batch, seq, D_MODEL), F32),
        scratch_shapes=[
            pltpu.VMEM((tm + 2 * CONV_HALO, D_GROUP), F32),
            pltpu.VMEM((tm, D_GROUP), BF16),
            pltpu.VMEM((tm, D_GROUP), BF16), pltpu.VMEM((tm, D_GROUP), BF16),
            pltpu.VMEM((n_wchunks, D_MODEL, D_MODEL // n_wchunks), BF16),
            pltpu.VMEM((OUTPROJ_WSLOTS, OUTPROJ_WROWS, D_MODEL), F32),
            pltpu.SemaphoreType.DMA((OUTPROJ_WSLOTS,)),
        ],
        compiler_params=_params(("arbitrary",), OUTPROJ_VMEM),
    )(*mixed, w_out, x, mod, final_g.reshape(1, D_MODEL), zp, zp, zp, conv_w,
      as_rows(conv_b), as_rows(ln_g), as_rows(ln_b), w_pw_bf16)


FFT_BLOCK = 16


def _fft_constants(seq):
    rows = seq // GRID_W
    assert rows == GRID_W
    n = np.arange(GRID_W)
    ang1 = 2.0 * np.pi * ((n[:, None] * n[None, :]) % GRID_W) / GRID_W
    f1 = np.concatenate([np.cos(ang1), -np.sin(ang1)], axis=0)
    k1 = n[:, None, None]
    k2 = n[None, :, None]
    s2 = n[None, None, :]
    ang2 = 2.0 * np.pi * ((s2 * (k1 + GRID_W * k2)) % seq) / seq
    mr, mi = np.cos(ang2), -np.sin(ang2)
    m2 = np.concatenate([np.concatenate([mr, -mi], axis=2),
                         np.concatenate([mi, mr], axis=2)], axis=1)
    c = np.arange(FFT_GROUP_DIM)
    angc = 2.0 * np.pi * ((c[:, None] * c[None, :]) % FFT_GROUP_DIM) / FFT_GROUP_DIM
    fc = np.concatenate([np.cos(angc), np.sin(angc)], axis=0)
    return tuple(jnp.asarray(t, F32).astype(BF16) for t in (f1, m2, fc))


def _fft_kernel(u_ref, gate_ref, f1_ref, m2_ref, fc_ref, w_ref, o_ref, ut_ref, g_ref, gt_ref, y_ref, *, norm):
    grp = pl.program_id(1)
    n = GRID_W
    blk = FFT_BLOCK
    sub = SUBLANES_F32

    u3 = u_ref[...].astype(F32).reshape(n, n, FFT_GROUP_DIM)
    ut_ref[...] = jnp.swapaxes(u3, 0, 1).astype(BF16)

    def stage1(i, carry):
        for j in range(blk):
            s2 = i * blk + j
            g_ref[s2] = jnp.dot(f1_ref[...], ut_ref[s2], preferred_element_type=F32)
        return carry

    lax.fori_loop(0, n // blk, stage1, 0)

    for q in range(2 * n // sub):
        part = g_ref[:, q * sub:(q + 1) * sub, :]
        gt_ref[q * sub:(q + 1) * sub] = jnp.swapaxes(part, 0, 1).astype(BF16)

    def stage2(i, carry):
        lhs = []
        for j in range(blk):
            k1 = i * blk + j
            gk = jnp.concatenate([gt_ref[k1], gt_ref[n + k1]], axis=0)
            xk = jnp.dot(m2_ref[k1], gk, preferred_element_type=F32)
            lhs.append(jnp.concatenate([xk[:n], xk[n:]], axis=-1))
        lhs = jnp.concatenate(lhs, axis=0).astype(BF16)
        y = jnp.dot(lhs, fc_ref[...], preferred_element_type=F32) * norm
        y_ref[grp, pl.ds(pl.multiple_of(i * blk, blk), blk)] = y.reshape(blk, n, FFT_GROUP_DIM)
        return carry

    lax.fori_loop(0, n // blk, stage2, 0)

    @pl.when(grp == N_FFT_GROUPS - 1)
    def _():
        def stage3(i, carry):
            k2 = pl.ds(pl.multiple_of(i * sub, sub), sub)
            tiles = []
            for g in range(N_FFT_GROUPS):
                part = y_ref[g, :, k2, :]
                tiles.append(jnp.swapaxes(part, 0, 1).reshape(sub * n, FFT_GROUP_DIM))
            y = jnp.concatenate(tiles, axis=-1).astype(BF16)
            o = jnp.dot(y, w_ref[...], preferred_element_type=F32)
            rows = pl.ds(pl.multiple_of(i * sub * n, sub * n), sub * n)
            o_ref[rows, :] = (o * _silu(gate_ref[rows, :].astype(F32))).astype(BF16)
            return carry

        lax.fori_loop(0, n // sub, stage3, 0)


def _fourier(zp, w_fft_bf16, consts, layer):
    _, batch, seq, _ = zp.shape
    f1, m2, fc = consts
    n = GRID_W
    const = lambda shape: pl.BlockSpec(shape, lambda b, g: (0,) * len(shape))
    return pl.pallas_call(
        functools.partial(_fft_kernel, norm=1.0 / math.sqrt(seq * FFT_GROUP_DIM)),
        name="fourier",
        grid=(batch, N_FFT_GROUPS),
        in_specs=[
            pl.BlockSpec((None, None, seq, FFT_GROUP_DIM), lambda b, g: (P_FX, b, 0, g)),
            pl.BlockSpec((None, None, seq, D_GROUP), lambda b, g: (P_FG, b, 0, 0)),
            const((2 * n, n)), const((n, 2 * n, 2 * n)), const((2 * FFT_GROUP_DIM, FFT_GROUP_DIM)),
            pl.BlockSpec((None, D_GROUP, D_GROUP), lambda b, g: (layer, 0, 0)),
        ],
        out_specs=pl.BlockSpec((None, seq, D_GROUP), lambda b, g: (b, 0, 0)),
        out_shape=jax.ShapeDtypeStruct((batch, seq, D_GROUP), BF16),
        scratch_shapes=[
            pltpu.VMEM((n, n, FFT_GROUP_DIM), BF16),
            pltpu.VMEM((n, 2 * n, FFT_GROUP_DIM), F32),
            pltpu.VMEM((2 * n, n, FFT_GROUP_DIM), BF16),
            pltpu.VMEM((N_FFT_GROUPS, n, n, FFT_GROUP_DIM), F32),
        ],
        compiler_params=_params(("arbitrary", "arbitrary")),
    )(zp, zp, f1, m2, fc, w_fft_bf16)


NA_ROWS_PER_STEP = 8


def _na_bias_table(rel_bias):
    n_dc = 2 * NA_KW - 1
    col = np.arange(GRID_W)
    col_start = np.clip(col - NA_KW // 2, 0, GRID_W - NA_KW)
    rel_c = col[None, :] - col_start[:, None]
    col_in = (rel_c >= 0) & (rel_c < NA_KW)
    dc = np.clip(col[None, :] - col[:, None] + NA_KW - 1, 0, n_dc - 1)
    onehot = (dc[None] == np.arange(n_dc)[:, None, None]).astype(np.float32)
    zeros = np.zeros_like(onehot)
    onehot2 = np.concatenate([np.concatenate([onehot, zeros], axis=-1),
                              np.concatenate([zeros, onehot], axis=-1)], axis=0)
    rb = rel_bias.astype(F32)
    rb2 = jnp.concatenate([rb[:, :, :-1], rb[:, :, 1:]], axis=-1)
    tbl = jnp.einsum('lhdj,jqk->ldhqk', rb2, jnp.asarray(onehot2),
                     precision=lax.Precision.HIGHEST)
    col_in2 = np.concatenate([col_in, col_in], axis=-1)
    return jnp.where(col_in2[None, None, None], tbl, NEG_BIG)


def _head_select_mask():
    r = lax.broadcasted_iota(jnp.int32, (MXU_DIM, MXU_DIM), 0) // HEAD_DIM
    c = lax.broadcasted_iota(jnp.int32, (MXU_DIM, MXU_DIM), 1) // HEAD_DIM
    return r == c


def _stack_heads(t):
    n = t.shape[0]
    reps = jnp.concatenate([t] * HEADS_PER_TILE, axis=0)
    row_head = lax.broadcasted_iota(jnp.int32, reps.shape, 0) // n
    lane_head = lax.broadcasted_iota(jnp.int32, reps.shape, 1) // HEAD_DIM
    return jnp.where(row_head == lane_head, reps, jnp.zeros_like(reps))


def _unstack_heads(t4, n):
    lane_head = lax.broadcasted_iota(jnp.int32, (n, MXU_DIM), 1) // HEAD_DIM
    out = jnp.zeros((n, MXU_DIM), t4.dtype)
    for h in range(HEADS_PER_TILE):
        out = jnp.where(lane_head == h, t4[h * n:(h + 1) * n], out)
    return out


def _na_kernel(q_ref, k_ref, v_ref, g_ref, bias_ref, *rest, rows, cast_weight):
    if cast_weight:
        w32_ref, o_ref, w16_ref = rest
        w16_ref[...] = w32_ref[...].astype(BF16)
    else:
        (o_ref,) = rest
    kh = min(NA_KH, rows)
    step = pl.program_id(1)

    def row_body(rr, carry):
        r = step * NA_ROWS_PER_STEP + rr
        r_start = jnp.clip(r - kh // 2, 0, rows - kh)
        variant = r - r_start
        q_off = pl.multiple_of(rr * GRID_W, GRID_W)
        k_off = pl.multiple_of(r_start * GRID_W, GRID_W)
        for cg in range(D_GROUP // MXU_DIM):
            lanes = slice(cg * MXU_DIM, (cg + 1) * MXU_DIM)
            q = q_ref[pl.ds(q_off, GRID_W), lanes] * (HEAD_DIM ** -0.5)
            kw = k_ref[pl.ds(k_off, kh * GRID_W), lanes]
            vw = v_ref[pl.ds(k_off, kh * GRID_W), lanes]
            q4 = _stack_heads(q)
            s = lax.dot_general(q4, kw, (((1,), (1,)), ((), ())), preferred_element_type=F32)
            heads = slice(cg * HEADS_PER_TILE, (cg + 1) * HEADS_PER_TILE)
            bias = jnp.concatenate(
                [bias_ref[NA_KH - 1 - variant + a, heads].reshape(HEADS_PER_TILE * GRID_W, 2 * GRID_W)
                 for a in range(0, kh, 2)], axis=-1)
            s = s + bias
            m = jnp.max(s, axis=-1, keepdims=True)
            e = jnp.exp(s - m)
            l = jnp.sum(e, axis=-1, keepdims=True)
            o4 = jnp.dot(e.astype(BF16), vw, preferred_element_type=F32) / l
            o = _unstack_heads(o4, GRID_W)
            gate = g_ref[pl.ds(q_off, GRID_W), lanes].astype(F32)
            o_ref[pl.ds(q_off, GRID_W), lanes] = (o * _silu(gate)).astype(BF16)
        return carry

    lax.fori_loop(0, NA_ROWS_PER_STEP, row_body, 0, unroll=True)


def _neighbourhood(zp, bias_tbl, layer, w_in=None):
    _, batch, seq, _ = zp.shape
    rows = seq // GRID_W
    tq = NA_ROWS_PER_STEP * GRID_W
    steps = rows // NA_ROWS_PER_STEP
    q_spec = lambda p: pl.BlockSpec((None, None, tq, D_GROUP), lambda b, i: (p, b, i, 0))
    kv_spec = lambda p: pl.BlockSpec((None, None, seq, D_GROUP), lambda b, i: (p, b, 0, 0))
    in_specs = [
        q_spec(P_NQ), kv_spec(P_NK), kv_spec(P_NV), q_spec(P_NG),
        pl.BlockSpec((None,) + bias_tbl.shape[1:], lambda b, i: (layer, 0, 0, 0, 0),
                     pipeline_mode=pl.Buffered(1)),
    ]
    out_specs = [pl.BlockSpec((None, tq, D_GROUP), lambda b, i: (b, i, 0))]
    out_shape = [jax.ShapeDtypeStruct((batch, seq, D_GROUP), BF16)]
    operands = [zp, zp, zp, zp, bias_tbl]
    if w_in is not None:
        d_in = w_in.shape[-1]
        wrows = D_MODEL // (batch * steps)
        in_specs.append(pl.BlockSpec((None, wrows, d_in), lambda b, i: (layer + 1, b * steps + i, 0)))
        out_specs.append(pl.BlockSpec((wrows, d_in), lambda b, i: (b * steps + i, 0)))
        out_shape.append(jax.ShapeDtypeStruct((D_MODEL, d_in), BF16))
        operands.append(w_in)
    out = pl.pallas_call(
        functools.partial(_na_kernel, rows=rows, cast_weight=w_in is not None),
        name="natten",
        grid=(batch, steps),
        in_specs=in_specs,
        out_specs=out_specs,
        out_shape=out_shape,
        compiler_params=_params(("arbitrary", "arbitrary")),
    )(*operands)
    return (out[0], out[1]) if w_in is not None else (out[0], None)


RET_CHUNK = MXU_DIM


def _log_sigmoid(t):
    return jnp.minimum(t, 0.0) - jnp.log1p(jnp.exp(-jnp.abs(t)))


def _rope_tables(seq):
    half = HEAD_DIM // 2
    inv = ROPE_BASE ** (-jnp.arange(half, dtype=F32) / half)
    ang = jnp.arange(seq, dtype=F32)[:, None] * inv[None, :]
    cos, sin = lax.optimization_barrier((jnp.cos(ang), jnp.sin(ang)))
    cos2 = jnp.concatenate([cos, cos], axis=-1)
    sin2 = jnp.concatenate([-sin, sin], axis=-1)
    return jnp.tile(cos2, (1, 2)), jnp.tile(sin2, (1, 2))


def _ret_kernel(lf_s_ref, lb_s_ref, q_ref, k_ref, v_ref, g_ref, cos_ref, sin_ref, lfl_ref, lbl_ref,
                o_ref, qr_ref, kr_ref, ob_ref, dmat_ref, dec_ref, state_ref, ones_ref, *, seq, layer):
    C = RET_CHUNK
    n_chunks = seq // C
    cg = pl.program_id(1)
    ones_bd = jnp.where(_head_select_mask(), 1.0, 0.0).astype(BF16)
    ones_ref[...] = jnp.concatenate([ones_bd, ones_bd], axis=0)

    ri = lax.broadcasted_iota(jnp.int32, (C, C), 0)
    ci = lax.broadcasted_iota(jnp.int32, (C, C), 1)
    diff = (ri - ci).astype(F32)
    for hh in range(HEADS_PER_TILE):
        h = layer * N_HEADS + cg * HEADS_PER_TILE + hh
        lf = _log_sigmoid(jnp.full((C, C), lf_s_ref[h], F32))
        lb = _log_sigmoid(jnp.full((C, C), lb_s_ref[h], F32))
        dmat_ref[hh] = jnp.where(diff >= 0, jnp.exp(lf * diff), jnp.exp(lb * (-diff)))
    lfl = _log_sigmoid(lfl_ref[...])
    lbl = _log_sigmoid(lbl_ref[...])
    idx = lax.broadcasted_iota(jnp.int32, (C, MXU_DIM), 0).astype(F32)
    dec_ref[0] = jnp.exp(lfl * (idx + 1.0))
    dec_ref[1] = jnp.exp(lfl * (C - 1.0 - idx))
    dec_ref[2] = jnp.exp(lbl * (C - idx))
    dec_ref[3] = jnp.exp(lbl * idx)
    cd_f = jnp.exp(lfl * float(C))
    cd_b = jnp.exp(lbl * float(C))

    def rope(t, rows):
        lane = lax.broadcasted_iota(jnp.int32, t.shape, 1)
        first_half = (lane % HEAD_DIM) < (HEAD_DIM // 2)
        cs = cos_ref[rows, :]
        sn = sin_ref[rows, :]
        cs = jnp.concatenate([cs, cs], axis=-1)
        sn = jnp.concatenate([sn, sn], axis=-1)
        swapped = jnp.where(first_half, pltpu.roll(t, MXU_DIM - HEAD_DIM // 2, 1),
                            pltpu.roll(t, HEAD_DIM // 2, 1))
        return t * cs + swapped * sn

    def kv_update(k, v, kd, cd):
        kv = lax.dot_general((k * kd).astype(BF16), v, (((0,), (0,)), ((), ())),
                             preferred_element_type=F32)
        state_ref[...] = cd * state_ref[...] + jnp.where(_head_select_mask(), kv, 0.0)

    state_ref[...] = jnp.zeros_like(state_ref)

    def bwd_body(i, carry):
        n = n_chunks - 1 - i
        rows = pl.ds(pl.multiple_of(n * C, C), C)
        q = rope(q_ref[rows, :].astype(F32), rows) * (HEAD_DIM ** -0.5)
        k = rope(k_ref[rows, :].astype(F32), rows)
        qr_ref[rows, :] = q.astype(BF16)
        kr_ref[rows, :] = k.astype(BF16)
        ob_ref[rows, :] = jnp.dot((q * dec_ref[2]).astype(BF16), state_ref[...].astype(BF16),
                                  preferred_element_type=F32)
        kv_update(k, v_ref[rows, :], dec_ref[3], cd_b)
        return carry

    lax.fori_loop(0, n_chunks, bwd_body, 0, unroll=8)

    state_ref[...] = jnp.zeros_like(state_ref)

    def fwd_body(n, carry):
        rows = pl.ds(pl.multiple_of(n * C, C), C)
        qb = qr_ref[rows, :]
        kb = kr_ref[rows, :]
        v = v_ref[rows, :]
        o_cross = jnp.dot((qb.astype(F32) * dec_ref[0]).astype(BF16), state_ref[...].astype(BF16),
                          preferred_element_type=F32)
        q4 = _stack_heads(qb)
        sc = lax.dot_general(q4, kb, (((1,), (1,)), ((), ())), preferred_element_type=F32)
        sc = sc * dmat_ref[...].reshape(HEADS_PER_TILE * C, C)
        o4 = jnp.dot(sc.astype(BF16), v, preferred_element_type=F32)
        o = _unstack_heads(o4, C) + o_cross + ob_ref[rows, :]
        sq = o * o
        hi = sq.astype(BF16)
        lo = (sq - hi.astype(F32)).astype(BF16)
        ss = jnp.dot(jnp.concatenate([hi, lo], axis=-1), ones_ref[...], preferred_element_type=F32)
        inv = lax.rsqrt(ss * (1.0 / HEAD_DIM) + EPS)
        gate = g_ref[rows, :].astype(F32)
        o_ref[rows, :] = (o * inv * _silu(gate)).astype(BF16)
        kv_update(kb.astype(F32), v, dec_ref[1], cd_f)
        return carry

    lax.fori_loop(0, n_chunks, fwd_body, 0, unroll=8)


def _retention(zp, logit_f, logit_b, rope_tbl, layer):
    _, batch, seq, _ = zp.shape
    cos_t, sin_t = rope_tbl
    n_cg = D_GROUP // MXU_DIM
    spec = lambda p: pl.BlockSpec((None, None, seq, MXU_DIM), lambda b, c, *_: (p, b, 0, c))
    tbl_spec = pl.BlockSpec((seq, 2 * HEAD_DIM), lambda b, c, *_: (0, 0))
    lane_spec = pl.BlockSpec((None, 1, MXU_DIM), lambda b, c, *_: (layer, 0, c))
    per_lane = lambda v: jnp.repeat(v.astype(F32), HEAD_DIM, axis=1)[:, None, :]
    return pl.pallas_call(
        functools.partial(_ret_kernel, seq=seq, layer=layer),
        name="retention",
        grid_spec=pltpu.PrefetchScalarGridSpec(
            num_scalar_prefetch=2,
            grid=(batch, n_cg),
            in_specs=[spec(P_RQ), spec(P_RK), spec(P_RV), spec(P_RG), tbl_spec, tbl_spec,
                      lane_spec, lane_spec],
            out_specs=pl.BlockSpec((None, seq, MXU_DIM), lambda b, c, *_: (b, 0, c)),
            scratch_shapes=[
                pltpu.VMEM((seq, MXU_DIM), BF16),
                pltpu.VMEM((seq, MXU_DIM), BF16),
                pltpu.VMEM((seq, MXU_DIM), F32),
                pltpu.VMEM((HEADS_PER_TILE, RET_CHUNK, RET_CHUNK), F32),
                pltpu.VMEM((4, RET_CHUNK, MXU_DIM), F32),
                pltpu.VMEM((MXU_DIM, MXU_DIM), F32),
                pltpu.VMEM((2 * MXU_DIM, MXU_DIM), BF16),
            ],
        ),
        out_shape=jax.ShapeDtypeStruct((batch, seq, D_GROUP), BF16),
        compiler_params=_params(("arbitrary", "arbitrary")),
    )(logit_f.astype(F32).reshape(-1), logit_b.astype(F32).reshape(-1), zp, zp, zp, zp, cos_t, sin_t,
      per_lane(logit_f), per_lane(logit_b))


CONV_HALO = SUBLANES_BF16
CONV_ROWS = 64


def _conv_glu(j, a_ref, b_ref, u_ref, *, seq):
    halo = CONV_HALO
    ts = u_ref.shape[0] - 2 * halo
    n_tiles = seq // ts

    def glu(start, size):
        rows = pl.ds(pl.multiple_of(start, halo), size)
        a = a_ref[rows, :].astype(F32)
        return a * _sigmoid(b_ref[rows, :].astype(F32))

    base = j * ts
    u_ref[halo:halo + ts, :] = glu(base, ts)
    lo = glu(jnp.maximum(base - halo, 0), halo)
    u_ref[0:halo, :] = jnp.where(j > 0, lo, 0.0)
    hi = glu(jnp.minimum(base + ts, seq - halo), halo)
    u_ref[halo + ts:, :] = jnp.where(j < n_tiles - 1, hi, 0.0)


def _conv_rows(s0, cw_ref, cb_ref, lg_ref, lb_ref, u_ref, h_ref):
    rb = CONV_ROWS
    first = CONV_HALO - CONV_HALF
    parts = []
    for lt in range(D_GROUP // LANES):
        lanes = slice(lt * LANES, (lt + 1) * LANES)
        y = None
        for r in range(SUBLANES_F32):
            acc = None
            for o in range(r, first + CONV_WIDTH, SUBLANES_F32):
                if o < first:
                    continue
                w = o - first
                lo_row = s0 + (o - r)
                term = u_ref[lo_row:lo_row + rb + SUBLANES_F32, lanes] * cw_ref[w:w + 1, lanes]
                acc = term if acc is None else acc + term
            shifted = acc[r:r + rb]
            y = shifted if y is None else y + shifted
        parts.append(y)
    y = jnp.concatenate(parts, axis=-1) + cb_ref[...]
    mu = jnp.mean(y, axis=-1, keepdims=True)
    yc = y - mu
    var = jnp.mean(yc * yc, axis=-1, keepdims=True)
    y = (yc * lax.rsqrt(var + EPS)) * lg_ref[...] + lb_ref[...]
    h_ref[s0:s0 + rb, :] = _silu(y).astype(BF16)
    return y[rb - SUBLANES_F32:, D_GROUP - LANES:]


def _runtime_zero(tile):
    bits = pltpu.bitcast(tile, jnp.uint32)
    bits = lax.shift_right_logical(lax.shift_right_logical(bits, jnp.uint32(16)), jnp.uint32(16))
    return bits[0, 0].astype(jnp.int32)


def _conv_pointwise(g_ref, w_ref, h_ref, o_ref):
    o = jnp.dot(h_ref[...], w_ref[...], preferred_element_type=F32)
    o_ref[...] = (o * _silu(g_ref[...].astype(F32))).astype(BF16)


def kernel(x, c, norm_g, w_ada, b_ada, w_in, w_fft, na_rel_bias, ret_logit_fwd, ret_logit_bwd,
           conv_w, conv_b, conv_ln_g, conv_ln_b, conv_w_pw, w_out, final_g):
    seq = x.shape[1]
    mod = _ada(c, w_ada, b_ada)
    fft_consts = _fft_constants(seq)
    rope_tbl = _rope_tables(seq)
    bias_tbl = _na_bias_table(na_rel_bias)
    w_fft_b, w_pw_b = w_fft.astype(BF16), conv_w_pw.astype(BF16)
    w_next = None
    for l in range(DEPTH):
        zp = _inproj(x, norm_g, mod, w_in, l, w_bf16=w_next)
        o_fft = _fourier(zp, w_fft_b, fft_consts, l)
        o_na, w_next = _neighbourhood(zp, bias_tbl, l, w_in if l + 1 < DEPTH else None)
        o_ret = _retention(zp, ret_logit_fwd, ret_logit_bwd, rope_tbl, l)
        conv_params = (conv_w, conv_b, conv_ln_g, conv_ln_b)
        x = _outproj((o_fft, o_na, o_ret), zp, conv_params, w_pw_b, w_out, x, mod, final_g, l,
                     final=(l == DEPTH - 1))
    return x
```

```python
import functools
import math

import numpy as np
import jax
import jax.numpy as jnp
from jax import lax
from jax.experimental import pallas as pl
from jax.experimental.pallas import tpu as pltpu

F32 = jnp.float32
BF16 = jnp.bfloat16

D_MODEL = 2048
DEPTH = 2
GRID_W = 64
D_GROUP = 512
HEAD_DIM = 64
N_HEADS = D_GROUP // HEAD_DIM
N_FFT_GROUPS = 4
FFT_GROUP_DIM = D_GROUP // N_FFT_GROUPS
NA_KH = 8
NA_KW = 16
CONV_WIDTH = 31
CONV_HALF = CONV_WIDTH // 2
ROPE_BASE = 10000.0
EPS = 1e-6
N_PIECES = 13
(P_FX, P_FG, P_NQ, P_NK, P_NV, P_NG, P_RQ, P_RK, P_RV, P_RG, P_CA, P_CB, P_CG) = range(N_PIECES)

MXU_DIM = 256
HEADS_PER_TILE = MXU_DIM // HEAD_DIM
VMEM_LIMIT = 56 * 1024 * 1024
SUBLANES_BF16 = 16
SUBLANES_F32 = 8
LANES = 128

NEG_BIG = -1e30


def _params(sem, vmem=VMEM_LIMIT):
    return pltpu.CompilerParams(dimension_semantics=sem, vmem_limit_bytes=vmem)


def _sigmoid(t):
    return 0.5 * jnp.tanh(0.5 * t) + 0.5


def _silu(t):
    h = 0.5 * t
    return h + h * jnp.tanh(h)


ADA_TN = 768
ADA_ROWS = 8
(MOD_SHIFT, MOD_SCALE, MOD_GATE) = range(3)


def _ada_kernel(c_ref, w_ref, b_ref, o_ref):
    ca = _silu(c_ref[...]).astype(BF16)
    w = w_ref[...].astype(BF16)
    o_ref[...] = jnp.dot(ca, w, preferred_element_type=F32) + b_ref[...]


def _ada(c, w_ada, b_ada):
    batch = c.shape[0]
    c_pad = jnp.zeros((ADA_ROWS, D_MODEL), F32).at[:batch].set(c)
    n3 = 3 * D_MODEL
    out = pl.pallas_call(
        _ada_kernel,
        name="ada",
        grid=(DEPTH, n3 // ADA_TN),
        in_specs=[
            pl.BlockSpec((ADA_ROWS, D_MODEL), lambda l, j: (0, 0)),
            pl.BlockSpec((None, D_MODEL, ADA_TN), lambda l, j: (l, 0, j)),
            pl.BlockSpec((None, 1, ADA_TN), lambda l, j: (l, 0, j)),
        ],
        out_specs=pl.BlockSpec((None, ADA_ROWS, ADA_TN), lambda l, j: (l, 0, j)),
        out_shape=jax.ShapeDtypeStruct((DEPTH, ADA_ROWS, n3), F32),
        compiler_params=_params(("arbitrary", "arbitrary")),
    )(c_pad, w_ada, b_ada.reshape(DEPTH, 1, n3))
    return out.reshape(DEPTH, ADA_ROWS, 1, n3)


INPROJ_TM = 512
INPROJ_WROWS = SUBLANES_BF16
INPROJ_WSLOTS = 8
INPROJ_VMEM = 60 * 1024 * 1024


def _inproj_kernel(x_ref, g_ref, sc_ref, sh_ref, w_hbm, o_ref, h0_ref, h1_ref, w_ref, stage_ref, sem,
                   *, n_tiles, layer):
    t = pl.program_id(0)
    wrows = stage_ref.shape[1]
    n_rchunks = D_MODEL // wrows

    def weight_copy(r, slot):
        rows = pl.ds(pl.multiple_of(r * wrows, wrows), wrows)
        return pltpu.make_async_copy(w_hbm.at[layer, rows, :], stage_ref.at[slot], sem.at[slot])

    def load_weight():
        n_slots = stage_ref.shape[0]
        for slot in range(n_slots):
            weight_copy(slot, slot).start()

        def group(i, carry):
            for slot in range(n_slots):
                r = n_slots * i + slot
                weight_copy(r, slot).wait()
                rows = pl.ds(pl.multiple_of(r * wrows, wrows), wrows)
                w_ref[rows, :] = stage_ref[slot].astype(BF16)

                @pl.when(r + n_slots < n_rchunks)
                def _():
                    weight_copy(r + n_slots, slot).start()
            return carry

        lax.fori_loop(0, n_rchunks // n_slots, group, 0)

    def norm(h_ref):
        x = x_ref[...]
        ms = jnp.mean(x * x, axis=-1, keepdims=True)
        h = (x * lax.rsqrt(ms + EPS)) * g_ref[...]
        h = h * (1.0 + sc_ref[...]) + sh_ref[...]
        h_ref[...] = h.astype(BF16)

    def project(h_ref):
        h = h_ref[...]
        for p in range(N_PIECES):
            w = w_ref[:, p * D_GROUP:(p + 1) * D_GROUP]
            o_ref[p] = jnp.dot(h, w, preferred_element_type=F32).astype(BF16)

    odd = t % 2 == 1

    @pl.when(t == 0)
    def _():
        load_weight()
        norm(h0_ref)

    @pl.when(odd)
    def _():
        project(h0_ref)
        norm(h1_ref)

    @pl.when(jnp.logical_and(jnp.logical_not(odd), jnp.logical_and(t > 0, t < n_tiles)))
    def _():
        project(h1_ref)
        norm(h0_ref)

    @pl.when(t == n_tiles)
    def _():
        project(h1_ref)


def _inproj(x, norm_g, mod, w_in, layer):
    batch, seq, _ = x.shape
    tm = INPROJ_TM
    per_batch = seq // tm
    n_tiles = batch * per_batch
    assert n_tiles % 2 == 0 and (D_MODEL // INPROJ_WROWS) % INPROJ_WSLOTS == 0
    norm_tile = lambda t: jnp.minimum(t, n_tiles - 1)
    proj_tile = lambda t: jnp.maximum(t - 1, 0)
    return pl.pallas_call(
        functools.partial(_inproj_kernel, n_tiles=n_tiles, layer=layer),
        name="inproj",
        grid=(n_tiles + 1,),
        in_specs=[
            pl.BlockSpec((None, tm, D_MODEL),
                         lambda t: (norm_tile(t) // per_batch, norm_tile(t) % per_batch, 0)),
            pl.BlockSpec((None, 1, D_MODEL), lambda t: (layer, 0, 0)),
            pl.BlockSpec((None, None, 1, D_MODEL),
                         lambda t: (layer, norm_tile(t) // per_batch, 0, MOD_SCALE)),
            pl.BlockSpec((None, None, 1, D_MODEL),
                         lambda t: (layer, norm_tile(t) // per_batch, 0, MOD_SHIFT)),
            pl.BlockSpec(memory_space=pl.ANY),
        ],
        out_specs=pl.BlockSpec((N_PIECES, None, tm, D_GROUP),
                               lambda t: (0, proj_tile(t) // per_batch, proj_tile(t) % per_batch, 0)),
        out_shape=jax.ShapeDtypeStruct((N_PIECES, batch, seq, D_GROUP), BF16),
        scratch_shapes=[
            pltpu.VMEM((tm, D_MODEL), BF16), pltpu.VMEM((tm, D_MODEL), BF16),
            pltpu.VMEM((D_MODEL, N_PIECES * D_GROUP), BF16),
            pltpu.VMEM((INPROJ_WSLOTS, INPROJ_WROWS, N_PIECES * D_GROUP), F32),
            pltpu.SemaphoreType.DMA((INPROJ_WSLOTS,)),
        ],
        compiler_params=_params(("arbitrary",), INPROJ_VMEM),
    )(x, norm_g.reshape(DEPTH, 1, D_MODEL), mod, mod, w_in)


OUTPROJ_TM = 512
OUTPROJ_WROWS = 64
OUTPROJ_WSLOTS = 8
OUTPROJ_VMEM = 60 * 1024 * 1024


def _outproj_kernel(a0_ref, a1_ref, a2_ref, w_hbm, x_ref, gate_ref, fg_ref,
                    ca_ref, cb_ref, cg_ref, cw_ref, cbias_ref, lg_ref, lb_ref, wpw_ref,
                    o_ref, u_ref, h_ref, cv0_ref, cv1_ref, w_ref, stage_ref, sem,
                    *, final, n_tiles, per_batch, seq, layer):
    t = pl.program_id(0)
    j = jnp.minimum(t, n_tiles - 1) % per_batch

    tm = o_ref.shape[0]
    n_chunks = tm // CONV_ROWS
    tn = D_MODEL // n_chunks
    n_slots = stage_ref.shape[0]

    wrows = stage_ref.shape[1]

    def weight_copy(r):
        slot = r % n_slots
        return pltpu.make_async_copy(w_hbm.at[layer, pl.ds(r * wrows, wrows), :], stage_ref.at[slot],
                                     sem.at[slot])

    def run(proj_cv_ref, conv_cv_ref, load_weight=False):
        if load_weight:
            for c in range(n_slots):
                weight_copy(c).start()
        if conv_cv_ref is not None:
            _conv_glu(j, ca_ref, cb_ref, u_ref, seq=seq)
        if proj_cv_ref is not None:
            a = jnp.concatenate([a0_ref[...], a1_ref[...], a2_ref[...], proj_cv_ref[...]], axis=-1)
            ss = jnp.zeros((tm, 1), F32)
        zero = 0
        for i in range(n_chunks):
            if proj_cv_ref is not None:
                y = jnp.dot(a, w_ref[i + zero], preferred_element_type=F32)
                xn = x_ref[:, i * tn:(i + 1) * tn] + gate_ref[:, i * tn:(i + 1) * tn] * y
                if final:
                    ss = ss + jnp.sum(xn * xn, axis=-1, keepdims=True)
                o_ref[:, i * tn:(i + 1) * tn] = xn
            if conv_cv_ref is not None:
                tail = _conv_rows(i * CONV_ROWS, cw_ref, cbias_ref, lg_ref, lb_ref, u_ref, h_ref)
                zero = _runtime_zero(tail)
            if load_weight:
                n_rchunks = D_MODEL // wrows
                per_block = n_rchunks // n_chunks
                for r in range(i * per_block, (i + 1) * per_block):
                    weight_copy(r).wait()
                    for c in range(n_chunks):
                        w_ref[c, r * wrows:(r + 1) * wrows, :] = stage_ref[
                            r % n_slots, :, c * tn:(c + 1) * tn].astype(BF16)
                    if r + n_slots < n_rchunks:
                        weight_copy(r + n_slots).start()
        if conv_cv_ref is not None:
            _conv_pointwise(cg_ref, wpw_ref, h_ref, conv_cv_ref)
        if proj_cv_ref is not None and final:
            o_ref[...] = (o_ref[...] * lax.rsqrt(ss * (1.0 / D_MODEL) + EPS)) * fg_ref[...]

    odd = t % 2 == 1

    @pl.when(t == 0)
    def _():
        run(None, cv0_ref, load_weight=True)

    @pl.when(odd)
    def _():
        run(cv0_ref, cv1_ref)

    @pl.when(jnp.logical_and(jnp.logical_not(odd), jnp.logical_and(t > 0, t < n_tiles)))
    def _():
        run(cv1_ref, cv0_ref)

    @pl.when(t == n_tiles)
    def _():
        run(cv1_ref, None)


def _outproj(mixed, zp, conv_params, w_pw_bf16, w_out, x, mod, final_g, layer, final):
    batch, seq, _ = x.shape
    tm = OUTPROJ_TM
    per_batch = seq // tm
    n_tiles = batch * per_batch
    n_wchunks = tm // CONV_ROWS
    assert n_tiles % 2 == 0
    conv_tile = lambda t: jnp.minimum(t, n_tiles - 1)
    proj_tile = lambda t: jnp.maximum(t - 1, 0)
    a_spec = pl.BlockSpec((None, tm, D_GROUP),
                          lambda t: (proj_tile(t) // per_batch, proj_tile(t) % per_batch, 0))
    x_spec = pl.BlockSpec((None, tm, D_MODEL),
                          lambda t: (proj_tile(t) // per_batch, proj_tile(t) % per_batch, 0))
    seq_spec = lambda p: pl.BlockSpec((None, None, seq, D_GROUP),
                                      lambda t: (p, conv_tile(t) // per_batch, 0, 0))
    vec = pl.BlockSpec((None, 1, D_GROUP), lambda t: (layer, 0, 0))
    conv_w, conv_b, ln_g, ln_b = conv_params
    as_rows = lambda v: v.reshape(DEPTH, 1, D_GROUP)
    return pl.pallas_call(
        functools.partial(_outproj_kernel, final=final, n_tiles=n_tiles, per_batch=per_batch, seq=seq,
                          layer=layer),
        name="outproj",
        grid=(n_tiles + 1,),
        in_specs=[
            a_spec, a_spec, a_spec,
            pl.BlockSpec(memory_space=pl.ANY),
            x_spec,
            pl.BlockSpec((None, None, 1, D_MODEL), lambda t: (layer, proj_tile(t) // per_batch, 0, MOD_GATE)),
            pl.BlockSpec((1, D_MODEL), lambda t: (0, 0)),
            seq_spec(P_CA), seq_spec(P_CB),
            pl.BlockSpec((None, None, tm, D_GROUP),
                         lambda t: (P_CG, conv_tile(t) // per_batch, conv_tile(t) % per_batch, 0)),
            pl.BlockSpec((None, CONV_WIDTH, D_GROUP), lambda t: (layer, 0, 0)),
            vec, vec, vec,
            pl.BlockSpec((None, D_GROUP, D_GROUP), lambda t: (layer, 0, 0)),
        ],
        out_specs=x_spec,
        out_shape=jax.ShapeDtypeStruct((batch, seq, D_MODEL), F32),
        scratch_shapes=[
            pltpu.VMEM((tm + 2 * CONV_HALO, D_GROUP), F32),
            pltpu.VMEM((tm, D_GROUP), BF16),
            pltpu.VMEM((tm, D_GROUP), BF16), pltpu.VMEM((tm, D_GROUP), BF16),
            pltpu.VMEM((n_wchunks, D_MODEL, D_MODEL // n_wchunks), BF16),
            pltpu.VMEM((OUTPROJ_WSLOTS, OUTPROJ_WROWS, D_MODEL), F32),
            pltpu.SemaphoreType.DMA((OUTPROJ_WSLOTS,)),
        ],
        compiler_params=_params(("arbitrary",), OUTPROJ_VMEM),
    )(*mixed, w_out, x, mod, final_g.reshape(1, D_MODEL), zp, zp, zp, conv_w,
      as_rows(conv_b), as_rows(ln_g), as_rows(ln_b), w_pw_bf16)


FFT_BLOCK = 16


def _fft_constants(seq):
    rows = seq // GRID_W
    assert rows == GRID_W
    n = np.arange(GRID_W)
    ang1 = 2.0 * np.pi * ((n[:, None] * n[None, :]) % GRID_W) / GRID_W
    f1 = np.concatenate([np.cos(ang1), -np.sin(ang1)], axis=0)
    k1 = n[:, None, None]
    k2 = n[None, :, None]
    s2 = n[None, None, :]
    ang2 = 2.0 * np.pi * ((s2 * (k1 + GRID_W * k2)) % seq) / seq
    mr, mi = np.cos(ang2), -np.sin(ang2)
    m2 = np.concatenate([np.concatenate([mr, -mi], axis=2),
                         np.concatenate([mi, mr], axis=2)], axis=1)
    c = np.arange(FFT_GROUP_DIM)
    angc = 2.0 * np.pi * ((c[:, None] * c[None, :]) % FFT_GROUP_DIM) / FFT_GROUP_DIM
    fc = np.concatenate([np.cos(angc), np.sin(angc)], axis=0)
    return tuple(jnp.asarray(t, F32).astype(BF16) for t in (f1, m2, fc))


def _fft_kernel(u_ref, gate_ref, f1_ref, m2_ref, fc_ref, w_ref, o_ref, ut_ref, g_ref, gt_ref, y_ref, *, norm):
    grp = pl.program_id(1)
    n = GRID_W
    blk = FFT_BLOCK
    sub = SUBLANES_F32

    u3 = u_ref[...].astype(F32).reshape(n, n, FFT_GROUP_DIM)
    ut_ref[...] = jnp.swapaxes(u3, 0, 1).astype(BF16)

    def stage1(i, carry):
        for j in range(blk):
            s2 = i * blk + j
            g_ref[s2] = jnp.dot(f1_ref[...], ut_ref[s2], preferred_element_type=F32)
        return carry

    lax.fori_loop(0, n // blk, stage1, 0)

    for q in range(2 * n // sub):
        part = g_ref[:, q * sub:(q + 1) * sub, :]
        gt_ref[q * sub:(q + 1) * sub] = jnp.swapaxes(part, 0, 1).astype(BF16)

    def stage2(i, carry):
        lhs = []
        for j in range(blk):
            k1 = i * blk + j
            gk = jnp.concatenate([gt_ref[k1], gt_ref[n + k1]], axis=0)
            xk = jnp.dot(m2_ref[k1], gk, preferred_element_type=F32)
            lhs.append(jnp.concatenate([xk[:n], xk[n:]], axis=-1))
        lhs = jnp.concatenate(lhs, axis=0).astype(BF16)
        y = jnp.dot(lhs, fc_ref[...], preferred_element_type=F32) * norm
        y_ref[grp, pl.ds(pl.multiple_of(i * blk, blk), blk)] = y.reshape(blk, n, FFT_GROUP_DIM)
        return carry

    lax.fori_loop(0, n // blk, stage2, 0)

    @pl.when(grp == N_FFT_GROUPS - 1)
    def _():
        def stage3(i, carry):
            k2 = pl.ds(pl.multiple_of(i * sub, sub), sub)
            tiles = []
            for g in range(N_FFT_GROUPS):
                part = y_ref[g, :, k2, :]
                tiles.append(jnp.swapaxes(part, 0, 1).reshape(sub * n, FFT_GROUP_DIM))
            y = jnp.concatenate(tiles, axis=-1).astype(BF16)
            o = jnp.dot(y, w_ref[...], preferred_element_type=F32)
            rows = pl.ds(pl.multiple_of(i * sub * n, sub * n), sub * n)
            o_ref[rows, :] = (o * _silu(gate_ref[rows, :].astype(F32))).astype(BF16)
            return carry

        lax.fori_loop(0, n // sub, stage3, 0, unroll=2)


def _fourier(zp, w_fft_bf16, consts, layer):
    _, batch, seq, _ = zp.shape
    f1, m2, fc = consts
    n = GRID_W
    const = lambda shape: pl.BlockSpec(shape, lambda b, g: (0,) * len(shape))
    return pl.pallas_call(
        functools.partial(_fft_kernel, norm=1.0 / math.sqrt(seq * FFT_GROUP_DIM)),
        name="fourier",
        grid=(batch, N_FFT_GROUPS),
        in_specs=[
            pl.BlockSpec((None, None, seq, FFT_GROUP_DIM), lambda b, g: (P_FX, b, 0, g)),
            pl.BlockSpec((None, None, seq, D_GROUP), lambda b, g: (P_FG, b, 0, 0)),
            const((2 * n, n)), const((n, 2 * n, 2 * n)), const((2 * FFT_GROUP_DIM, FFT_GROUP_DIM)),
            pl.BlockSpec((None, D_GROUP, D_GROUP), lambda b, g: (layer, 0, 0)),
        ],
        out_specs=pl.BlockSpec((None, seq, D_GROUP), lambda b, g: (b, 0, 0)),
        out_shape=jax.ShapeDtypeStruct((batch, seq, D_GROUP), BF16),
        scratch_shapes=[
            pltpu.VMEM((n, n, FFT_GROUP_DIM), BF16),
            pltpu.VMEM((n, 2 * n, FFT_GROUP_DIM), F32),
            pltpu.VMEM((2 * n, n, FFT_GROUP_DIM), BF16),
            pltpu.VMEM((N_FFT_GROUPS, n, n, FFT_GROUP_DIM), F32),
        ],
        compiler_params=_params(("arbitrary", "arbitrary")),
    )(zp, zp, f1, m2, fc, w_fft_bf16)


NA_ROWS_PER_STEP = 16


def _na_bias_table(rel_bias):
    n_dc = 2 * NA_KW - 1
    col = np.arange(GRID_W)
    col_start = np.clip(col - NA_KW // 2, 0, GRID_W - NA_KW)
    rel_c = col[None, :] - col_start[:, None]
    col_in = (rel_c >= 0) & (rel_c < NA_KW)
    dc = np.clip(col[None, :] - col[:, None] + NA_KW - 1, 0, n_dc - 1)
    onehot = (dc[None] == np.arange(n_dc)[:, None, None]).astype(np.float32)
    zeros = np.zeros_like(onehot)
    onehot2 = np.concatenate([np.concatenate([onehot, zeros], axis=-1),
                              np.concatenate([zeros, onehot], axis=-1)], axis=0)
    rb = rel_bias.astype(F32)
    rb2 = jnp.concatenate([rb[:, :, :-1], rb[:, :, 1:]], axis=-1)
    tbl = jnp.einsum('lhdj,jqk->ldhqk', rb2, jnp.asarray(onehot2),
                     precision=lax.Precision.HIGHEST)
    col_in2 = np.concatenate([col_in, col_in], axis=-1)
    return jnp.where(col_in2[None, None, None], tbl, NEG_BIG)


def _head_select_mask():
    r = lax.broadcasted_iota(jnp.int32, (MXU_DIM, MXU_DIM), 0) // HEAD_DIM
    c = lax.broadcasted_iota(jnp.int32, (MXU_DIM, MXU_DIM), 1) // HEAD_DIM
    return r == c


def _stack_heads(t):
    n = t.shape[0]
    reps = jnp.concatenate([t] * HEADS_PER_TILE, axis=0)
    row_head = lax.broadcasted_iota(jnp.int32, reps.shape, 0) // n
    lane_head = lax.broadcasted_iota(jnp.int32, reps.shape, 1) // HEAD_DIM
    return jnp.where(row_head == lane_head, reps, jnp.zeros_like(reps))


def _unstack_heads(t4, n):
    lane_head = lax.broadcasted_iota(jnp.int32, (n, MXU_DIM), 1) // HEAD_DIM
    out = jnp.zeros((n, MXU_DIM), t4.dtype)
    for h in range(HEADS_PER_TILE):
        out = jnp.where(lane_head == h, t4[h * n:(h + 1) * n], out)
    return out


def _na_kernel(q_ref, k_ref, v_ref, g_ref, bias_ref, o_ref, *, rows):
    kh = min(NA_KH, rows)
    step = pl.program_id(1)

    def row_body(rr, carry):
        r = step * NA_ROWS_PER_STEP + rr
        r_start = jnp.clip(r - kh // 2, 0, rows - kh)
        variant = r - r_start
        q_off = pl.multiple_of(rr * GRID_W, GRID_W)
        k_off = pl.multiple_of(r_start * GRID_W, GRID_W)
        for cg in range(D_GROUP // MXU_DIM):
            lanes = slice(cg * MXU_DIM, (cg + 1) * MXU_DIM)
            q = q_ref[pl.ds(q_off, GRID_W), lanes] * (HEAD_DIM ** -0.5)
            kw = k_ref[pl.ds(k_off, kh * GRID_W), lanes]
            vw = v_ref[pl.ds(k_off, kh * GRID_W), lanes]
            q4 = _stack_heads(q)
            s = lax.dot_general(q4, kw, (((1,), (1,)), ((), ())), preferred_element_type=F32)
            heads = slice(cg * HEADS_PER_TILE, (cg + 1) * HEADS_PER_TILE)
            bias = jnp.concatenate(
                [bias_ref[NA_KH - 1 - variant + a, heads].reshape(HEADS_PER_TILE * GRID_W, 2 * GRID_W)
                 for a in range(0, kh, 2)], axis=-1)
            s = s + bias
            m = jnp.max(s, axis=-1, keepdims=True)
            e = jnp.exp(s - m)
            l = jnp.sum(e, axis=-1, keepdims=True)
            o4 = jnp.dot(e.astype(BF16), vw, preferred_element_type=F32) / l
            o = _unstack_heads(o4, GRID_W)
            gate = g_ref[pl.ds(q_off, GRID_W), lanes].astype(F32)
            o_ref[pl.ds(q_off, GRID_W), lanes] = (o * _silu(gate)).astype(BF16)
        return carry

    lax.fori_loop(0, NA_ROWS_PER_STEP, row_body, 0, unroll=True)


def _neighbourhood(zp, bias_tbl, layer):
    _, batch, seq, _ = zp.shape
    rows = seq // GRID_W
    tq = NA_ROWS_PER_STEP * GRID_W
    q_spec = lambda p: pl.BlockSpec((None, None, tq, D_GROUP), lambda b, i: (p, b, i, 0))
    kv_spec = lambda p: pl.BlockSpec((None, None, seq, D_GROUP), lambda b, i: (p, b, 0, 0))
    return pl.pallas_call(
        functools.partial(_na_kernel, rows=rows),
        name="natten",
        grid=(batch, rows // NA_ROWS_PER_STEP),
        in_specs=[
            q_spec(P_NQ), kv_spec(P_NK), kv_spec(P_NV), q_spec(P_NG),
            pl.BlockSpec((None,) + bias_tbl.shape[1:], lambda b, i: (layer, 0, 0, 0, 0),
                         pipeline_mode=pl.Buffered(1)),
        ],
        out_specs=pl.BlockSpec((None, tq, D_GROUP), lambda b, i: (b, i, 0)),
        out_shape=jax.ShapeDtypeStruct((batch, seq, D_GROUP), BF16),
        compiler_params=_params(("arbitrary", "arbitrary")),
    )(zp, zp, zp, zp, bias_tbl)


RET_CHUNK = MXU_DIM


def _log_sigmoid(t):
    return jnp.minimum(t, 0.0) - jnp.log1p(jnp.exp(-jnp.abs(t)))


def _rope_tables(seq):
    half = HEAD_DIM // 2
    inv = ROPE_BASE ** (-jnp.arange(half, dtype=F32) / half)
    ang = jnp.arange(seq, dtype=F32)[:, None] * inv[None, :]
    cos, sin = jnp.cos(ang), jnp.sin(ang)
    cos2 = jnp.concatenate([cos, cos], axis=-1)
    sin2 = jnp.concatenate([-sin, sin], axis=-1)
    return jnp.tile(cos2, (1, 2)), jnp.tile(sin2, (1, 2))


def _ret_kernel(lf_s_ref, lb_s_ref, q_ref, k_ref, v_ref, g_ref, cos_ref, sin_ref, lfl_ref, lbl_ref,
                o_ref, qr_ref, kr_ref, ob_ref, dmat_ref, dec_ref, state_ref, ones_ref, *, seq, layer):
    C = RET_CHUNK
    n_chunks = seq // C
    cg = pl.program_id(1)
    ones_bd = jnp.where(_head_select_mask(), 1.0, 0.0).astype(BF16)
    ones_ref[...] = jnp.concatenate([ones_bd, ones_bd], axis=0)

    ri = lax.broadcasted_iota(jnp.int32, (C, C), 0)
    ci = lax.broadcasted_iota(jnp.int32, (C, C), 1)
    diff = (ri - ci).astype(F32)
    for hh in range(HEADS_PER_TILE):
        h = layer * N_HEADS + cg * HEADS_PER_TILE + hh
        lf = _log_sigmoid(jnp.full((C, C), lf_s_ref[h], F32))
        lb = _log_sigmoid(jnp.full((C, C), lb_s_ref[h], F32))
        dmat_ref[hh] = jnp.where(diff >= 0, jnp.exp(lf * diff), jnp.exp(lb * (-diff)))
    lfl = _log_sigmoid(lfl_ref[...])
    lbl = _log_sigmoid(lbl_ref[...])
    idx = lax.broadcasted_iota(jnp.int32, (C, MXU_DIM), 0).astype(F32)
    dec_ref[0] = jnp.exp(lfl * (idx + 1.0))
    dec_ref[1] = jnp.exp(lfl * (C - 1.0 - idx))
    dec_ref[2] = jnp.exp(lbl * (C - idx))
    dec_ref[3] = jnp.exp(lbl * idx)
    cd_f = jnp.exp(lfl * float(C))
    cd_b = jnp.exp(lbl * float(C))

    def rope(t, rows):
        lane = lax.broadcasted_iota(jnp.int32, t.shape, 1)
        first_half = (lane % HEAD_DIM) < (HEAD_DIM // 2)
        cs = cos_ref[rows, :]
        sn = sin_ref[rows, :]
        cs = jnp.concatenate([cs, cs], axis=-1)
        sn = jnp.concatenate([sn, sn], axis=-1)
        swapped = jnp.where(first_half, pltpu.roll(t, MXU_DIM - HEAD_DIM // 2, 1),
                            pltpu.roll(t, HEAD_DIM // 2, 1))
        return t * cs + swapped * sn

    def kv_update(k, v, kd, cd):
        kv = lax.dot_general((k * kd).astype(BF16), v, (((0,), (0,)), ((), ())),
                             preferred_element_type=F32)
        state_ref[...] = cd * state_ref[...] + jnp.where(_head_select_mask(), kv, 0.0)

    state_ref[...] = jnp.zeros_like(state_ref)

    def bwd_body(i, carry):
        n = n_chunks - 1 - i
        rows = pl.ds(pl.multiple_of(n * C, C), C)
        q = rope(q_ref[rows, :].astype(F32), rows) * (HEAD_DIM ** -0.5)
        k = rope(k_ref[rows, :].astype(F32), rows)
        qr_ref[rows, :] = q.astype(BF16)
        kr_ref[rows, :] = k.astype(BF16)
        ob_ref[rows, :] = jnp.dot((q * dec_ref[2]).astype(BF16), state_ref[...].astype(BF16),
                                  preferred_element_type=F32)
        kv_update(k, v_ref[rows, :], dec_ref[3], cd_b)
        return carry

    lax.fori_loop(0, n_chunks, bwd_body, 0, unroll=8)

    state_ref[...] = jnp.zeros_like(state_ref)

    def fwd_body(n, carry):
        rows = pl.ds(pl.multiple_of(n * C, C), C)
        qb = qr_ref[rows, :]
        kb = kr_ref[rows, :]
        v = v_ref[rows, :]
        o_cross = jnp.dot((qb.astype(F32) * dec_ref[0]).astype(BF16), state_ref[...].astype(BF16),
                          preferred_element_type=F32)
        q4 = _stack_heads(qb)
        sc = lax.dot_general(q4, kb, (((1,), (1,)), ((), ())), preferred_element_type=F32)
        sc = sc * dmat_ref[...].reshape(HEADS_PER_TILE * C, C)
        o4 = jnp.dot(sc.astype(BF16), v, preferred_element_type=F32)
        o = _unstack_heads(o4, C) + o_cross + ob_ref[rows, :]
        sq = o * o
        hi = sq.astype(BF16)
        lo = (sq - hi.astype(F32)).astype(BF16)
        ss = jnp.dot(jnp.concatenate([hi, lo], axis=-1), ones_ref[...], preferred_element_type=F32)
        inv = lax.rsqrt(ss * (1.0 / HEAD_DIM) + EPS)
        gate = g_ref[rows, :].astype(F32)
        o_ref[rows, :] = (o * inv * _silu(gate)).astype(BF16)
        kv_update(kb.astype(F32), v, dec_ref[1], cd_f)
        return carry

    lax.fori_loop(0, n_chunks, fwd_body, 0, unroll=8)


def _retention(zp, logit_f, logit_b, rope_tbl, layer):
    _, batch, seq, _ = zp.shape
    cos_t, sin_t = rope_tbl
    n_cg = D_GROUP // MXU_DIM
    spec = lambda p: pl.BlockSpec((None, None, seq, MXU_DIM), lambda b, c, *_: (p, b, 0, c))
    tbl_spec = pl.BlockSpec((seq, 2 * HEAD_DIM), lambda b, c, *_: (0, 0))
    lane_spec = pl.BlockSpec((None, 1, MXU_DIM), lambda b, c, *_: (layer, 0, c))
    per_lane = lambda v: jnp.repeat(v.astype(F32), HEAD_DIM, axis=1)[:, None, :]
    return pl.pallas_call(
        functools.partial(_ret_kernel, seq=seq, layer=layer),
        name="retention",
        grid_spec=pltpu.PrefetchScalarGridSpec(
            num_scalar_prefetch=2,
            grid=(batch, n_cg),
            in_specs=[spec(P_RQ), spec(P_RK), spec(P_RV), spec(P_RG), tbl_spec, tbl_spec,
                      lane_spec, lane_spec],
            out_specs=pl.BlockSpec((None, seq, MXU_DIM), lambda b, c, *_: (b, 0, c)),
            scratch_shapes=[
                pltpu.VMEM((seq, MXU_DIM), BF16),
                pltpu.VMEM((seq, MXU_DIM), BF16),
                pltpu.VMEM((seq, MXU_DIM), F32),
                pltpu.VMEM((HEADS_PER_TILE, RET_CHUNK, RET_CHUNK), F32),
                pltpu.VMEM((4, RET_CHUNK, MXU_DIM), F32),
                pltpu.VMEM((MXU_DIM, MXU_DIM), F32),
                pltpu.VMEM((2 * MXU_DIM, MXU_DIM), BF16),
            ],
        ),
        out_shape=jax.ShapeDtypeStruct((batch, seq, D_GROUP), BF16),
        compiler_params=_params(("arbitrary", "arbitrary")),
    )(logit_f.astype(F32).reshape(-1), logit_b.astype(F32).reshape(-1), zp, zp, zp, zp, cos_t, sin_t,
      per_lane(logit_f), per_lane(logit_b))


CONV_HALO = SUBLANES_BF16
CONV_ROWS = 128


def _conv_glu(j, a_ref, b_ref, u_ref, *, seq):
    halo = CONV_HALO
    ts = u_ref.shape[0] - 2 * halo
    n_tiles = seq // ts

    def glu(start, size):
        rows = pl.ds(pl.multiple_of(start, halo), size)
        a = a_ref[rows, :].astype(F32)
        return a * _sigmoid(b_ref[rows, :].astype(F32))

    base = j * ts
    u_ref[halo:halo + ts, :] = glu(base, ts)
    lo = glu(jnp.maximum(base - halo, 0), halo)
    u_ref[0:halo, :] = jnp.where(j > 0, lo, 0.0)
    hi = glu(jnp.minimum(base + ts, seq - halo), halo)
    u_ref[halo + ts:, :] = jnp.where(j < n_tiles - 1, hi, 0.0)


def _conv_rows(s0, cw_ref, cb_ref, lg_ref, lb_ref, u_ref, h_ref):
    rb = CONV_ROWS
    first = CONV_HALO - CONV_HALF
    parts = []
    for lt in range(D_GROUP // LANES):
        lanes = slice(lt * LANES, (lt + 1) * LANES)
        y = None
        for r in range(SUBLANES_F32):
            acc = None
            for o in range(r, first + CONV_WIDTH, SUBLANES_F32):
                if o < first:
                    continue
                w = o - first
                lo_row = s0 + (o - r)
                term = u_ref[lo_row:lo_row + rb + SUBLANES_F32, lanes] * cw_ref[w:w + 1, lanes]
                acc = term if acc is None else acc + term
            shifted = acc[r:r + rb]
            y = shifted if y is None else y + shifted
        parts.append(y)
    y = jnp.concatenate(parts, axis=-1) + cb_ref[...]
    mu = jnp.mean(y, axis=-1, keepdims=True)
    yc = y - mu
    var = jnp.mean(yc * yc, axis=-1, keepdims=True)
    y = (yc * lax.rsqrt(var + EPS)) * lg_ref[...] + lb_ref[...]
    h_ref[s0:s0 + rb, :] = _silu(y).astype(BF16)
    return y[rb - SUBLANES_F32:, D_GROUP - LANES:]


def _runtime_zero(tile):
    bits = pltpu.bitcast(tile, jnp.uint32)
    bits = lax.shift_right_logical(lax.shift_right_logical(bits, jnp.uint32(16)), jnp.uint32(16))
    return bits[0, 0].astype(jnp.int32)


def _conv_pointwise(g_ref, w_ref, h_ref, o_ref):
    o = jnp.dot(h_ref[...], w_ref[...], preferred_element_type=F32)
    o_ref[...] = (o * _silu(g_ref[...].astype(F32))).astype(BF16)


def kernel(x, c, norm_g, w_ada, b_ada, w_in, w_fft, na_rel_bias, ret_logit_fwd, ret_logit_bwd,
           conv_w, conv_b, conv_ln_g, conv_ln_b, conv_w_pw, w_out, final_g):
    seq = x.shape[1]
    mod = _ada(c, w_ada, b_ada)
    fft_consts = _fft_constants(seq)
    rope_tbl = _rope_tables(seq)
    bias_tbl = _na_bias_table(na_rel_bias)
    w_fft_b, w_pw_b = w_fft.astype(BF16), conv_w_pw.astype(BF16)
    for l in range(DEPTH):
        zp = _inproj(x, norm_g, mod, w_in, l)
        o_fft = _fourier(zp, w_fft_b, fft_consts, l)
        o_na = _neighbourhood(zp, bias_tbl, l)
        o_ret = _retention(zp, ret_logit_fwd, ret_logit_bwd, rope_tbl, l)
        conv_params = (conv_w, conv_b, conv_ln_g, conv_ln_b)
        x = _outproj((o_fft, o_na, o_ret), zp, conv_params, w_pw_b, w_out, x, mod, final_g, l,
                     final=(l == DEPTH - 1))
    return x
```

```python
import functools
import math

import numpy as np
import jax
import jax.numpy as jnp
from jax import lax
from jax.experimental import pallas as pl
from jax.experimental.pallas import tpu as pltpu

F32 = jnp.float32
BF16 = jnp.bfloat16

D_MODEL = 2048
DEPTH = 2
GRID_W = 64
D_GROUP = 512
HEAD_DIM = 64
N_HEADS = D_GROUP // HEAD_DIM
N_FFT_GROUPS = 4
FFT_GROUP_DIM = D_GROUP // N_FFT_GROUPS
NA_KH = 8
NA_KW = 16
CONV_WIDTH = 31
CONV_HALF = CONV_WIDTH // 2
ROPE_BASE = 10000.0
EPS = 1e-6
N_PIECES = 13
(P_FX, P_FG, P_NQ, P_NK, P_NV, P_NG, P_RQ, P_RK, P_RV, P_RG, P_CA, P_CB, P_CG) = range(N_PIECES)

MXU_DIM = 256
HEADS_PER_TILE = MXU_DIM // HEAD_DIM
VMEM_LIMIT = 56 * 1024 * 1024
SUBLANES_BF16 = 16
SUBLANES_F32 = 8
LANES = 128

NEG_BIG = -1e30


def _params(sem, vmem=VMEM_LIMIT):
    return pltpu.CompilerParams(dimension_semantics=sem, vmem_limit_bytes=vmem)


def _sigmoid(t):
    return 0.5 * jnp.tanh(0.5 * t) + 0.5


def _silu(t):
    h = 0.5 * t
    return h + h * jnp.tanh(h)


ADA_TN = 768
ADA_ROWS = 8
(MOD_SHIFT, MOD_SCALE, MOD_GATE) = range(3)


def _ada_kernel(c_ref, w_ref, b_ref, o_ref):
    ca = _silu(c_ref[...]).astype(BF16)
    w = w_ref[...].astype(BF16)
    o_ref[...] = jnp.dot(ca, w, preferred_element_type=F32) + b_ref[...]


def _ada(c, w_ada, b_ada):
    batch = c.shape[0]
    c_pad = jnp.zeros((ADA_ROWS, D_MODEL), F32).at[:batch].set(c)
    n3 = 3 * D_MODEL
    out = pl.pallas_call(
        _ada_kernel,
        name="ada",
        grid=(DEPTH, n3 // ADA_TN),
        in_specs=[
            pl.BlockSpec((ADA_ROWS, D_MODEL), lambda l, j: (0, 0)),
            pl.BlockSpec((None, D_MODEL, ADA_TN), lambda l, j: (l, 0, j)),
            pl.BlockSpec((None, 1, ADA_TN), lambda l, j: (l, 0, j)),
        ],
        out_specs=pl.BlockSpec((None, ADA_ROWS, ADA_TN), lambda l, j: (l, 0, j)),
        out_shape=jax.ShapeDtypeStruct((DEPTH, ADA_ROWS, n3), F32),
        compiler_params=_params(("arbitrary", "arbitrary")),
    )(c_pad, w_ada, b_ada.reshape(DEPTH, 1, n3))
    return out.reshape(DEPTH, ADA_ROWS, 1, n3)


INPROJ_TM = 512
INPROJ_WROWS = SUBLANES_BF16
INPROJ_WSLOTS = 8
INPROJ_VMEM = 60 * 1024 * 1024


def _inproj_kernel(x_ref, g_ref, sc_ref, sh_ref, w_hbm, o_ref, h0_ref, h1_ref, w_ref, stage_ref, sem,
                   *, n_tiles, layer):
    t = pl.program_id(0)
    wrows = stage_ref.shape[1]
    n_rchunks = D_MODEL // wrows

    def weight_copy(r, slot):
        rows = pl.ds(pl.multiple_of(r * wrows, wrows), wrows)
        return pltpu.make_async_copy(w_hbm.at[layer, rows, :], stage_ref.at[slot], sem.at[slot])

    def load_weight():
        n_slots = stage_ref.shape[0]
        for slot in range(n_slots):
            weight_copy(slot, slot).start()

        def group(i, carry):
            for slot in range(n_slots):
                r = n_slots * i + slot
                weight_copy(r, slot).wait()
                rows = pl.ds(pl.multiple_of(r * wrows, wrows), wrows)
                w_ref[rows, :] = stage_ref[slot].astype(BF16)

                @pl.when(r + n_slots < n_rchunks)
                def _():
                    weight_copy(r + n_slots, slot).start()
            return carry

        lax.fori_loop(0, n_rchunks // n_slots, group, 0)

    def norm(h_ref):
        x = x_ref[...]
        ms = jnp.mean(x * x, axis=-1, keepdims=True)
        h = (x * lax.rsqrt(ms + EPS)) * g_ref[...]
        h = h * (1.0 + sc_ref[...]) + sh_ref[...]
        h_ref[...] = h.astype(BF16)

    def project(h_ref):
        h = h_ref[...]
        for p in range(N_PIECES):
            w = w_ref[:, p * D_GROUP:(p + 1) * D_GROUP]
            o_ref[p] = jnp.dot(h, w, preferred_element_type=F32).astype(BF16)

    odd = t % 2 == 1

    @pl.when(t == 0)
    def _():
        load_weight()
        norm(h0_ref)

    @pl.when(odd)
    def _():
        project(h0_ref)
        norm(h1_ref)

    @pl.when(jnp.logical_and(jnp.logical_not(odd), jnp.logical_and(t > 0, t < n_tiles)))
    def _():
        project(h1_ref)
        norm(h0_ref)

    @pl.when(t == n_tiles)
    def _():
        project(h1_ref)


def _inproj(x, norm_g, mod, w_in, layer):
    batch, seq, _ = x.shape
    tm = INPROJ_TM
    per_batch = seq // tm
    n_tiles = batch * per_batch
    assert n_tiles % 2 == 0 and (D_MODEL // INPROJ_WROWS) % INPROJ_WSLOTS == 0
    norm_tile = lambda t: jnp.minimum(t, n_tiles - 1)
    proj_tile = lambda t: jnp.maximum(t - 1, 0)
    return pl.pallas_call(
        functools.partial(_inproj_kernel, n_tiles=n_tiles, layer=layer),
        name="inproj",
        grid=(n_tiles + 1,),
        in_specs=[
            pl.BlockSpec((None, tm, D_MODEL),
                         lambda t: (norm_tile(t) // per_batch, norm_tile(t) % per_batch, 0)),
            pl.BlockSpec((None, 1, D_MODEL), lambda t: (layer, 0, 0)),
            pl.BlockSpec((None, None, 1, D_MODEL),
                         lambda t: (layer, norm_tile(t) // per_batch, 0, MOD_SCALE)),
            pl.BlockSpec((None, None, 1, D_MODEL),
                         lambda t: (layer, norm_tile(t) // per_batch, 0, MOD_SHIFT)),
            pl.BlockSpec(memory_space=pl.ANY),
        ],
        out_specs=pl.BlockSpec((N_PIECES, None, tm, D_GROUP),
                               lambda t: (0, proj_tile(t) // per_batch, proj_tile(t) % per_batch, 0)),
        out_shape=jax.ShapeDtypeStruct((N_PIECES, batch, seq, D_GROUP), BF16),
        scratch_shapes=[
            pltpu.VMEM((tm, D_MODEL), BF16), pltpu.VMEM((tm, D_MODEL), BF16),
            pltpu.VMEM((D_MODEL, N_PIECES * D_GROUP), BF16),
            pltpu.VMEM((INPROJ_WSLOTS, INPROJ_WROWS, N_PIECES * D_GROUP), F32),
            pltpu.SemaphoreType.DMA((INPROJ_WSLOTS,)),
        ],
        compiler_params=_params(("arbitrary",), INPROJ_VMEM),
    )(x, norm_g.reshape(DEPTH, 1, D_MODEL), mod, mod, w_in)


OUTPROJ_TM = 512
OUTPROJ_WROWS = 64
OUTPROJ_WSLOTS = 8
OUTPROJ_VMEM = 60 * 1024 * 1024


def _outproj_kernel(a0_ref, a1_ref, a2_ref, w_hbm, x_ref, gate_ref, fg_ref,
                    ca_ref, cb_ref, cg_ref, cw_ref, cbias_ref, lg_ref, lb_ref, wpw_ref,
                    o_ref, u_ref, h_ref, cv0_ref, cv1_ref, w_ref, stage_ref, sem,
                    *, final, n_tiles, per_batch, seq, layer):
    t = pl.program_id(0)
    j = jnp.minimum(t, n_tiles - 1) % per_batch

    tm = o_ref.shape[0]
    n_chunks = tm // CONV_ROWS
    tn = D_MODEL // n_chunks
    n_slots = stage_ref.shape[0]

    wrows = stage_ref.shape[1]

    def weight_copy(r):
        slot = r % n_slots
        return pltpu.make_async_copy(w_hbm.at[layer, pl.ds(r * wrows, wrows), :], stage_ref.at[slot],
                                     sem.at[slot])

    def run(proj_cv_ref, conv_cv_ref, load_weight=False):
        if load_weight:
            for c in range(n_slots):
                weight_copy(c).start()
        if conv_cv_ref is not None:
            _conv_glu(j, ca_ref, cb_ref, u_ref, seq=seq)
        if proj_cv_ref is not None:
            a = jnp.concatenate([a0_ref[...], a1_ref[...], a2_ref[...], proj_cv_ref[...]], axis=-1)
            ss = jnp.zeros((tm, 1), F32)
        zero = 0
        for i in range(n_chunks):
            if proj_cv_ref is not None:
                y = jnp.dot(a, w_ref[i + zero], preferred_element_type=F32)
                xn = x_ref[:, i * tn:(i + 1) * tn] + gate_ref[:, i * tn:(i + 1) * tn] * y
                if final:
                    ss = ss + jnp.sum(xn * xn, axis=-1, keepdims=True)
                o_ref[:, i * tn:(i + 1) * tn] = xn
            if conv_cv_ref is not None:
                tail = _conv_rows(i * CONV_ROWS, cw_ref, cbias_ref, lg_ref, lb_ref, u_ref, h_ref)
                zero = _runtime_zero(tail)
            if load_weight:
                n_rchunks = D_MODEL // wrows
                per_block = n_rchunks // n_chunks
                for r in range(i * per_block, (i + 1) * per_block):
                    weight_copy(r).wait()
                    for c in range(n_chunks):
                        w_ref[c, r * wrows:(r + 1) * wrows, :] = stage_ref[
                            r % n_slots, :, c * tn:(c + 1) * tn].astype(BF16)
                    if r + n_slots < n_rchunks:
                        weight_copy(r + n_slots).start()
        if conv_cv_ref is not None:
            _conv_pointwise(cg_ref, wpw_ref, h_ref, conv_cv_ref)
        if proj_cv_ref is not None and final:
            o_ref[...] = (o_ref[...] * lax.rsqrt(ss * (1.0 / D_MODEL) + EPS)) * fg_ref[...]

    odd = t % 2 == 1

    @pl.when(t == 0)
    def _():
        run(None, cv0_ref, load_weight=True)

    @pl.when(odd)
    def _():
        run(cv0_ref, cv1_ref)

    @pl.when(jnp.logical_and(jnp.logical_not(odd), jnp.logical_and(t > 0, t < n_tiles)))
    def _():
        run(cv1_ref, cv0_ref)

    @pl.when(t == n_tiles)
    def _():
        run(cv1_ref, None)


def _outproj(mixed, zp, conv_params, w_pw_bf16, w_out, x, mod, final_g, layer, final):
    batch, seq, _ = x.shape
    tm = OUTPROJ_TM
    per_batch = seq // tm
    n_tiles = batch * per_batch
    n_wchunks = tm // CONV_ROWS
    assert n_tiles % 2 == 0
    conv_tile = lambda t: jnp.minimum(t, n_tiles - 1)
    proj_tile = lambda t: jnp.maximum(t - 1, 0)
    a_spec = pl.BlockSpec((None, tm, D_GROUP),
                          lambda t: (proj_tile(t) // per_batch, proj_tile(t) % per_batch, 0))
    x_spec = pl.BlockSpec((None, tm, D_MODEL),
                          lambda t: (proj_tile(t) // per_batch, proj_tile(t) % per_batch, 0))
    seq_spec = lambda p: pl.BlockSpec((None, None, seq, D_GROUP),
                                      lambda t: (p, conv_tile(t) // per_batch, 0, 0))
    vec = pl.BlockSpec((None, 1, D_GROUP), lambda t: (layer, 0, 0))
    conv_w, conv_b, ln_g, ln_b = conv_params
    as_rows = lambda v: v.reshape(DEPTH, 1, D_GROUP)
    return pl.pallas_call(
        functools.partial(_outproj_kernel, final=final, n_tiles=n_tiles, per_batch=per_batch, seq=seq,
                          layer=layer),
        name="outproj",
        grid=(n_tiles + 1,),
        in_specs=[
            a_spec, a_spec, a_spec,
            pl.BlockSpec(memory_space=pl.ANY),
            x_spec,
            pl.BlockSpec((None, None, 1, D_MODEL), lambda t: (layer, proj_tile(t) // per_batch, 0, MOD_GATE)),
            pl.BlockSpec((1, D_MODEL), lambda t: (0, 0)),
            seq_spec(P_CA), seq_spec(P_CB),
            pl.BlockSpec((None, None, tm, D_GROUP),
                         lambda t: (P_CG, conv_tile(t) // per_batch, conv_tile(t) % per_batch, 0)),
            pl.BlockSpec((None, CONV_WIDTH, D_GROUP), lambda t: (layer, 0, 0)),
            vec, vec, vec,
            pl.BlockSpec((None, D_GROUP, D_GROUP), lambda t: (layer, 0, 0)),
        ],
        out_specs=x_spec,
        out_shape=jax.ShapeDtypeStruct((batch, seq, D_MODEL), F32),
        scratch_shapes=[
            pltpu.VMEM((tm + 2 * CONV_HALO, D_GROUP), F32),
            pltpu.VMEM((tm, D_GROUP), BF16),
            pltpu.VMEM((tm, D_GROUP), BF16), pltpu.VMEM((tm, D_GROUP), BF16),
            pltpu.VMEM((n_wchunks, D_MODEL, D_MODEL // n_wchunks), BF16),
            pltpu.VMEM((OUTPROJ_WSLOTS, OUTPROJ_WROWS, D_MODEL), F32),
            pltpu.SemaphoreType.DMA((OUTPROJ_WSLOTS,)),
        ],
        compiler_params=_params(("arbitrary",), OUTPROJ_VMEM),
    )(*mixed, w_out, x, mod, final_g.reshape(1, D_MODEL), zp, zp, zp, conv_w,
      as_rows(conv_b), as_rows(ln_g), as_rows(ln_b), w_pw_bf16)


FFT_BLOCK = 16
FFT_LANES = 2 * FFT_GROUP_DIM


def _fft_constants(seq):
    rows = seq // GRID_W
    assert rows == GRID_W
    n = np.arange(GRID_W)
    ang1 = 2.0 * np.pi * ((n[:, None] * n[None, :]) % GRID_W) / GRID_W
    f1 = np.concatenate([np.cos(ang1), -np.sin(ang1)], axis=0)
    k1 = n[:, None, None]
    k2 = n[None, :, None]
    s2 = n[None, None, :]
    ang2 = 2.0 * np.pi * ((s2 * (k1 + GRID_W * k2)) % seq) / seq
    mr, mi = np.cos(ang2), -np.sin(ang2)
    m2 = np.concatenate([np.concatenate([mr, -mi], axis=2),
                         np.concatenate([mi, mr], axis=2)], axis=1)
    c = np.arange(FFT_GROUP_DIM)
    angc = 2.0 * np.pi * ((c[:, None] * c[None, :]) % FFT_GROUP_DIM) / FFT_GROUP_DIM
    fc = np.concatenate([np.cos(angc), np.sin(angc)], axis=0)
    return tuple(jnp.asarray(t, F32).astype(BF16) for t in (f1, m2, fc))


def _fft_kernel(u_ref, gate_ref, f1_ref, m2_ref, fc_ref, w_ref, o_ref, ut_ref, g_ref, gt_ref, y_ref, *, norm):
    pair = pl.program_id(1)
    n = GRID_W
    blk = FFT_BLOCK
    sub = SUBLANES_F32
    lanes = ut_ref.shape[-1]
    groups = lanes // FFT_GROUP_DIM

    u3 = u_ref[...].astype(F32).reshape(n, n, lanes)
    ut_ref[...] = jnp.swapaxes(u3, 0, 1).astype(BF16)

    def stage1(i, carry):
        for j in range(blk):
            s2 = i * blk + j
            g_ref[s2] = jnp.dot(f1_ref[...], ut_ref[s2], preferred_element_type=F32)
        return carry

    lax.fori_loop(0, n // blk, stage1, 0)

    for q in range(2 * n // sub):
        part = g_ref[:, q * sub:(q + 1) * sub, :]
        gt_ref[q * sub:(q + 1) * sub] = jnp.swapaxes(part, 0, 1).astype(BF16)

    def stage2(i, carry):
        xs = []
        for j in range(blk):
            k1 = i * blk + j
            gk = jnp.concatenate([gt_ref[k1], gt_ref[n + k1]], axis=0)
            xs.append(jnp.dot(m2_ref[k1], gk, preferred_element_type=F32))
        for g in range(groups):
            sl = slice(g * FFT_GROUP_DIM, (g + 1) * FFT_GROUP_DIM)
            lhs = jnp.concatenate(
                [jnp.concatenate([xk[:n, sl], xk[n:, sl]], axis=-1) for xk in xs], axis=0)
            y = jnp.dot(lhs.astype(BF16), fc_ref[...], preferred_element_type=F32) * norm
            y_ref[pair * groups + g, pl.ds(pl.multiple_of(i * blk, blk), blk)] = (
                y.reshape(blk, n, FFT_GROUP_DIM))
        return carry

    lax.fori_loop(0, n // blk, stage2, 0)

    @pl.when(pair == pl.num_programs(1) - 1)
    def _():
        def stage3(i, carry):
            k2 = pl.ds(pl.multiple_of(i * sub, sub), sub)
            tiles = []
            for g in range(N_FFT_GROUPS):
                part = y_ref[g, :, k2, :]
                tiles.append(jnp.swapaxes(part, 0, 1).reshape(sub * n, FFT_GROUP_DIM))
            y = jnp.concatenate(tiles, axis=-1).astype(BF16)
            o = jnp.dot(y, w_ref[...], preferred_element_type=F32)
            rows = pl.ds(pl.multiple_of(i * sub * n, sub * n), sub * n)
            o_ref[rows, :] = (o * _silu(gate_ref[rows, :].astype(F32))).astype(BF16)
            return carry

        lax.fori_loop(0, n // sub, stage3, 0, unroll=2)


def _fourier(zp, w_fft_bf16, consts, layer):
    _, batch, seq, _ = zp.shape
    f1, m2, fc = consts
    n = GRID_W
    const = lambda shape: pl.BlockSpec(shape, lambda b, g: (0,) * len(shape))
    return pl.pallas_call(
        functools.partial(_fft_kernel, norm=1.0 / math.sqrt(seq * FFT_GROUP_DIM)),
        name="fourier",
        grid=(batch, D_GROUP // FFT_LANES),
        in_specs=[
            pl.BlockSpec((None, None, seq, FFT_LANES), lambda b, g: (P_FX, b, 0, g)),
            pl.BlockSpec((None, None, seq, D_GROUP), lambda b, g: (P_FG, b, 0, 0)),
            const((2 * n, n)), const((n, 2 * n, 2 * n)), const((2 * FFT_GROUP_DIM, FFT_GROUP_DIM)),
            pl.BlockSpec((None, D_GROUP, D_GROUP), lambda b, g: (layer, 0, 0)),
        ],
        out_specs=pl.BlockSpec((None, seq, D_GROUP), lambda b, g: (b, 0, 0)),
        out_shape=jax.ShapeDtypeStruct((batch, seq, D_GROUP), BF16),
        scratch_shapes=[
            pltpu.VMEM((n, n, FFT_LANES), BF16),
            pltpu.VMEM((n, 2 * n, FFT_LANES), F32),
            pltpu.VMEM((2 * n, n, FFT_LANES), BF16),
            pltpu.VMEM((N_FFT_GROUPS, n, n, FFT_GROUP_DIM), F32),
        ],
        compiler_params=_params(("arbitrary", "arbitrary")),
    )(zp, zp, f1, m2, fc, w_fft_bf16)


NA_ROWS_PER_STEP = 16


def _na_bias_table(rel_bias):
    n_dc = 2 * NA_KW - 1
    col = np.arange(GRID_W)
    col_start = np.clip(col - NA_KW // 2, 0, GRID_W - NA_KW)
    rel_c = col[None, :] - col_start[:, None]
    col_in = (rel_c >= 0) & (rel_c < NA_KW)
    dc = np.clip(col[None, :] - col[:, None] + NA_KW - 1, 0, n_dc - 1)
    onehot = (dc[None] == np.arange(n_dc)[:, None, None]).astype(np.float32)
    zeros = np.zeros_like(onehot)
    onehot2 = np.concatenate([np.concatenate([onehot, zeros], axis=-1),
                              np.concatenate([zeros, onehot], axis=-1)], axis=0)
    rb = rel_bias.astype(F32)
    rb2 = jnp.concatenate([rb[:, :, :-1], rb[:, :, 1:]], axis=-1)
    tbl = jnp.einsum('lhdj,jqk->ldhqk', rb2, jnp.asarray(onehot2),
                     precision=lax.Precision.HIGHEST)
    col_in2 = np.concatenate([col_in, col_in], axis=-1)
    return jnp.where(col_in2[None, None, None], tbl, NEG_BIG)


def _head_select_mask():
    r = lax.broadcasted_iota(jnp.int32, (MXU_DIM, MXU_DIM), 0) // HEAD_DIM
    c = lax.broadcasted_iota(jnp.int32, (MXU_DIM, MXU_DIM), 1) // HEAD_DIM
    return r == c


def _stack_heads(t):
    n = t.shape[0]
    reps = jnp.concatenate([t] * HEADS_PER_TILE, axis=0)
    row_head = lax.broadcasted_iota(jnp.int32, reps.shape, 0) // n
    lane_head = lax.broadcasted_iota(jnp.int32, reps.shape, 1) // HEAD_DIM
    return jnp.where(row_head == lane_head, reps, jnp.zeros_like(reps))


def _unstack_heads(t4, n):
    lane_head = lax.broadcasted_iota(jnp.int32, (n, MXU_DIM), 1) // HEAD_DIM
    out = jnp.zeros((n, MXU_DIM), t4.dtype)
    for h in range(HEADS_PER_TILE):
        out = jnp.where(lane_head == h, t4[h * n:(h + 1) * n], out)
    return out


def _na_kernel(q_ref, k_ref, v_ref, g_ref, bias_ref, o_ref, *, rows):
    kh = min(NA_KH, rows)
    step = pl.program_id(1)

    def row_body(rr, carry):
        r = step * NA_ROWS_PER_STEP + rr
        r_start = jnp.clip(r - kh // 2, 0, rows - kh)
        variant = r - r_start
        q_off = pl.multiple_of(rr * GRID_W, GRID_W)
        k_off = pl.multiple_of(r_start * GRID_W, GRID_W)
        for cg in range(D_GROUP // MXU_DIM):
            lanes = slice(cg * MXU_DIM, (cg + 1) * MXU_DIM)
            q = q_ref[pl.ds(q_off, GRID_W), lanes] * (HEAD_DIM ** -0.5)
            kw = k_ref[pl.ds(k_off, kh * GRID_W), lanes]
            vw = v_ref[pl.ds(k_off, kh * GRID_W), lanes]
            q4 = _stack_heads(q)
            s = lax.dot_general(q4, kw, (((1,), (1,)), ((), ())), preferred_element_type=F32)
            heads = slice(cg * HEADS_PER_TILE, (cg + 1) * HEADS_PER_TILE)
            bias = jnp.concatenate(
                [bias_ref[NA_KH - 1 - variant + a, heads].reshape(HEADS_PER_TILE * GRID_W, 2 * GRID_W)
                 for a in range(0, kh, 2)], axis=-1)
            s = s + bias
            m = jnp.max(s, axis=-1, keepdims=True)
            e = jnp.exp(s - m)
            l = jnp.sum(e, axis=-1, keepdims=True)
            o4 = jnp.dot(e.astype(BF16), vw, preferred_element_type=F32) / l
            o = _unstack_heads(o4, GRID_W)
            gate = g_ref[pl.ds(q_off, GRID_W), lanes].astype(F32)
            o_ref[pl.ds(q_off, GRID_W), lanes] = (o * _silu(gate)).astype(BF16)
        return carry

    lax.fori_loop(0, NA_ROWS_PER_STEP, row_body, 0, unroll=True)


def _neighbourhood(zp, bias_tbl, layer):
    _, batch, seq, _ = zp.shape
    rows = seq // GRID_W
    tq = NA_ROWS_PER_STEP * GRID_W
    q_spec = lambda p: pl.BlockSpec((None, None, tq, D_GROUP), lambda b, i: (p, b, i, 0))
    kv_spec = lambda p: pl.BlockSpec((None, None, seq, D_GROUP), lambda b, i: (p, b, 0, 0))
    return pl.pallas_call(
        functools.partial(_na_kernel, rows=rows),
        name="natten",
        grid=(batch, rows // NA_ROWS_PER_STEP),
        in_specs=[
            q_spec(P_NQ), kv_spec(P_NK), kv_spec(P_NV), q_spec(P_NG),
            pl.BlockSpec((None,) + bias_tbl.shape[1:], lambda b, i: (layer, 0, 0, 0, 0),
                         pipeline_mode=pl.Buffered(1)),
        ],
        out_specs=pl.BlockSpec((None, tq, D_GROUP), lambda b, i: (b, i, 0)),
        out_shape=jax.ShapeDtypeStruct((batch, seq, D_GROUP), BF16),
        compiler_params=_params(("arbitrary", "arbitrary")),
    )(zp, zp, zp, zp, bias_tbl)


RET_CHUNK = MXU_DIM


def _log_sigmoid(t):
    return jnp.minimum(t, 0.0) - jnp.log1p(jnp.exp(-jnp.abs(t)))


def _rope_tables(seq):
    half = HEAD_DIM // 2
    inv = ROPE_BASE ** (-jnp.arange(half, dtype=F32) / half)
    ang = jnp.arange(seq, dtype=F32)[:, None] * inv[None, :]
    cos, sin = jnp.cos(ang), jnp.sin(ang)
    cos2 = jnp.concatenate([cos, cos], axis=-1)
    sin2 = jnp.concatenate([-sin, sin], axis=-1)
    return jnp.tile(cos2, (1, 2)), jnp.tile(sin2, (1, 2))


def _ret_kernel(lf_s_ref, lb_s_ref, q_ref, k_ref, v_ref, g_ref, cos_ref, sin_ref, lfl_ref, lbl_ref,
                o_ref, qr_ref, kr_ref, ob_ref, dmat_ref, dec_ref, state_ref, ones_ref, *, seq, layer):
    C = RET_CHUNK
    n_chunks = seq // C
    cg = pl.program_id(1)
    ones_bd = jnp.where(_head_select_mask(), 1.0, 0.0).astype(BF16)
    ones_ref[...] = jnp.concatenate([ones_bd, ones_bd], axis=0)

    ri = lax.broadcasted_iota(jnp.int32, (C, C), 0)
    ci = lax.broadcasted_iota(jnp.int32, (C, C), 1)
    diff = (ri - ci).astype(F32)
    for hh in range(HEADS_PER_TILE):
        h = layer * N_HEADS + cg * HEADS_PER_TILE + hh
        lf = _log_sigmoid(jnp.full((C, C), lf_s_ref[h], F32))
        lb = _log_sigmoid(jnp.full((C, C), lb_s_ref[h], F32))
        dmat_ref[hh] = jnp.where(diff >= 0, jnp.exp(lf * diff), jnp.exp(lb * (-diff)))
    lfl = _log_sigmoid(lfl_ref[...])
    lbl = _log_sigmoid(lbl_ref[...])
    idx = lax.broadcasted_iota(jnp.int32, (C, MXU_DIM), 0).astype(F32)
    dec_ref[0] = jnp.exp(lfl * (idx + 1.0))
    dec_ref[1] = jnp.exp(lfl * (C - 1.0 - idx))
    dec_ref[2] = jnp.exp(lbl * (C - idx))
    dec_ref[3] = jnp.exp(lbl * idx)
    cd_f = jnp.exp(lfl * float(C))
    cd_b = jnp.exp(lbl * float(C))

    def rope(t, rows):
        lane = lax.broadcasted_iota(jnp.int32, t.shape, 1)
        first_half = (lane % HEAD_DIM) < (HEAD_DIM // 2)
        cs = cos_ref[rows, :]
        sn = sin_ref[rows, :]
        cs = jnp.concatenate([cs, cs], axis=-1)
        sn = jnp.concatenate([sn, sn], axis=-1)
        swapped = jnp.where(first_half, pltpu.roll(t, MXU_DIM - HEAD_DIM // 2, 1),
                            pltpu.roll(t, HEAD_DIM // 2, 1))
        return t * cs + swapped * sn

    def kv_update(k, v, kd, cd):
        kv = lax.dot_general((k * kd).astype(BF16), v, (((0,), (0,)), ((), ())),
                             preferred_element_type=F32)
        state_ref[...] = cd * state_ref[...] + jnp.where(_head_select_mask(), kv, 0.0)

    state_ref[...] = jnp.zeros_like(state_ref)

    def bwd_body(i, carry):
        n = n_chunks - 1 - i
        rows = pl.ds(pl.multiple_of(n * C, C), C)
        q = rope(q_ref[rows, :].astype(F32), rows) * (HEAD_DIM ** -0.5)
        k = rope(k_ref[rows, :].astype(F32), rows)
        qr_ref[rows, :] = q.astype(BF16)
        kr_ref[rows, :] = k.astype(BF16)
        ob_ref[rows, :] = jnp.dot((q * dec_ref[2]).astype(BF16), state_ref[...].astype(BF16),
                                  preferred_element_type=F32)
        kv_update(k, v_ref[rows, :], dec_ref[3], cd_b)
        return carry

    lax.fori_loop(0, n_chunks, bwd_body, 0, unroll=8)

    state_ref[...] = jnp.zeros_like(state_ref)

    def fwd_body(n, carry):
        rows = pl.ds(pl.multiple_of(n * C, C), C)
        qb = qr_ref[rows, :]
        kb = kr_ref[rows, :]
        v = v_ref[rows, :]
        o_cross = jnp.dot((qb.astype(F32) * dec_ref[0]).astype(BF16), state_ref[...].astype(BF16),
                          preferred_element_type=F32)
        q4 = _stack_heads(qb)
        sc = lax.dot_general(q4, kb, (((1,), (1,)), ((), ())), preferred_element_type=F32)
        sc = sc * dmat_ref[...].reshape(HEADS_PER_TILE * C, C)
        o4 = jnp.dot(sc.astype(BF16), v, preferred_element_type=F32)
        o = _unstack_heads(o4, C) + o_cross + ob_ref[rows, :]
        sq = o * o
        hi = sq.astype(BF16)
        lo = (sq - hi.astype(F32)).astype(BF16)
        ss = jnp.dot(jnp.concatenate([hi, lo], axis=-1), ones_ref[...], preferred_element_type=F32)
        inv = lax.rsqrt(ss * (1.0 / HEAD_DIM) + EPS)
        gate = g_ref[rows, :].astype(F32)
        o_ref[rows, :] = (o * inv * _silu(gate)).astype(BF16)
        kv_update(kb.astype(F32), v, dec_ref[1], cd_f)
        return carry

    lax.fori_loop(0, n_chunks, fwd_body, 0, unroll=8)


def _retention(zp, logit_f, logit_b, rope_tbl, layer):
    _, batch, seq, _ = zp.shape
    cos_t, sin_t = rope_tbl
    n_cg = D_GROUP // MXU_DIM
    spec = lambda p: pl.BlockSpec((None, None, seq, MXU_DIM), lambda b, c, *_: (p, b, 0, c))
    tbl_spec = pl.BlockSpec((seq, 2 * HEAD_DIM), lambda b, c, *_: (0, 0))
    lane_spec = pl.BlockSpec((None, 1, MXU_DIM), lambda b, c, *_: (layer, 0, c))
    per_lane = lambda v: jnp.repeat(v.astype(F32), HEAD_DIM, axis=1)[:, None, :]
    return pl.pallas_call(
        functools.partial(_ret_kernel, seq=seq, layer=layer),
        name="retention",
        grid_spec=pltpu.PrefetchScalarGridSpec(
            num_scalar_prefetch=2,
            grid=(batch, n_cg),
            in_specs=[spec(P_RQ), spec(P_RK), spec(P_RV), spec(P_RG), tbl_spec, tbl_spec,
                      lane_spec, lane_spec],
            out_specs=pl.BlockSpec((None, seq, MXU_DIM), lambda b, c, *_: (b, 0, c)),
            scratch_shapes=[
                pltpu.VMEM((seq, MXU_DIM), BF16),
                pltpu.VMEM((seq, MXU_DIM), BF16),
                pltpu.VMEM((seq, MXU_DIM), F32),
                pltpu.VMEM((HEADS_PER_TILE, RET_CHUNK, RET_CHUNK), F32),
                pltpu.VMEM((4, RET_CHUNK, MXU_DIM), F32),
                pltpu.VMEM((MXU_DIM, MXU_DIM), F32),
                pltpu.VMEM((2 * MXU_DIM, MXU_DIM), BF16),
            ],
        ),
        out_shape=jax.ShapeDtypeStruct((batch, seq, D_GROUP), BF16),
        compiler_params=_params(("arbitrary", "arbitrary")),
    )(logit_f.astype(F32).reshape(-1), logit_b.astype(F32).reshape(-1), zp, zp, zp, zp, cos_t, sin_t,
      per_lane(logit_f), per_lane(logit_b))


CONV_HALO = SUBLANES_BF16
CONV_ROWS = 128


def _conv_glu(j, a_ref, b_ref, u_ref, *, seq):
    halo = CONV_HALO
    ts = u_ref.shape[0] - 2 * halo
    n_tiles = seq // ts

    def glu(start, size):
        rows = pl.ds(pl.multiple_of(start, halo), size)
        a = a_ref[rows, :].astype(F32)
        return a * _sigmoid(b_ref[rows, :].astype(F32))

    base = j * ts
    u_ref[halo:halo + ts, :] = glu(base, ts)
    lo = glu(jnp.maximum(base - halo, 0), halo)
    u_ref[0:halo, :] = jnp.where(j > 0, lo, 0.0)
    hi = glu(jnp.minimum(base + ts, seq - halo), halo)
    u_ref[halo + ts:, :] = jnp.where(j < n_tiles - 1, hi, 0.0)


def _conv_rows(s0, cw_ref, cb_ref, lg_ref, lb_ref, u_ref, h_ref):
    rb = CONV_ROWS
    first = CONV_HALO - CONV_HALF
    parts = []
    for lt in range(D_GROUP // LANES):
        lanes = slice(lt * LANES, (lt + 1) * LANES)
        y = None
        for r in range(SUBLANES_F32):
            acc = None
            for o in range(r, first + CONV_WIDTH, SUBLANES_F32):
                if o < first:
                    continue
                w = o - first
                lo_row = s0 + (o - r)
                term = u_ref[lo_row:lo_row + rb + SUBLANES_F32, lanes] * cw_ref[w:w + 1, lanes]
                acc = term if acc is None else acc + term
            shifted = acc[r:r + rb]
            y = shifted if y is None else y + shifted
        parts.append(y)
    y = jnp.concatenate(parts, axis=-1) + cb_ref[...]
    mu = jnp.mean(y, axis=-1, keepdims=True)
    yc = y - mu
    var = jnp.mean(yc * yc, axis=-1, keepdims=True)
    y = (yc * lax.rsqrt(var + EPS)) * lg_ref[...] + lb_ref[...]
    h_ref[s0:s0 + rb, :] = _silu(y).astype(BF16)
    return y[rb - SUBLANES_F32:, D_GROUP - LANES:]


def _runtime_zero(tile):
    bits = pltpu.bitcast(tile, jnp.uint32)
    bits = lax.shift_right_logical(lax.shift_right_logical(bits, jnp.uint32(16)), jnp.uint32(16))
    return bits[0, 0].astype(jnp.int32)


def _conv_pointwise(g_ref, w_ref, h_ref, o_ref):
    o = jnp.dot(h_ref[...], w_ref[...], preferred_element_type=F32)
    o_ref[...] = (o * _silu(g_ref[...].astype(F32))).astype(BF16)


def kernel(x, c, norm_g, w_ada, b_ada, w_in, w_fft, na_rel_bias, ret_logit_fwd, ret_logit_bwd,
           conv_w, conv_b, conv_ln_g, conv_ln_b, conv_w_pw, w_out, final_g):
    seq = x.shape[1]
    mod = _ada(c, w_ada, b_ada)
    fft_consts = _fft_constants(seq)
    rope_tbl = _rope_tables(seq)
    bias_tbl = _na_bias_table(na_rel_bias)
    w_fft_b, w_pw_b = w_fft.astype(BF16), conv_w_pw.astype(BF16)
    for l in range(DEPTH):
        zp = _inproj(x, norm_g, mod, w_in, l)
        o_fft = _fourier(zp, w_fft_b, fft_consts, l)
        o_na = _neighbourhood(zp, bias_tbl, l)
        o_ret = _retention(zp, ret_logit_fwd, ret_logit_bwd, rope_tbl, l)
        conv_params = (conv_w, conv_b, conv_ln_g, conv_ln_b)
        x = _outproj((o_fft, o_na, o_ret), zp, conv_params, w_pw_b, w_out, x, mod, final_g, l,
                     final=(l == DEPTH - 1))
    return x
```

```python
import functools
import math

import numpy as np
import jax
import jax.numpy as jnp
from jax import lax
from jax.experimental import pallas as pl
from jax.experimental.pallas import tpu as pltpu

F32 = jnp.float32
BF16 = jnp.bfloat16

D_MODEL = 2048
DEPTH = 2
GRID_W = 64
D_GROUP = 512
HEAD_DIM = 64
N_HEADS = D_GROUP // HEAD_DIM
N_FFT_GROUPS = 4
FFT_GROUP_DIM = D_GROUP // N_FFT_GROUPS
NA_KH = 8
NA_KW = 16
CONV_WIDTH = 31
CONV_HALF = CONV_WIDTH // 2
ROPE_BASE = 10000.0
EPS = 1e-6
N_PIECES = 13
(P_FX, P_FG, P_NQ, P_NK, P_NV, P_NG, P_RQ, P_RK, P_RV, P_RG, P_CA, P_CB, P_CG) = range(N_PIECES)

MXU_DIM = 256
HEADS_PER_TILE = MXU_DIM // HEAD_DIM
VMEM_LIMIT = 56 * 1024 * 1024
SUBLANES_BF16 = 16
SUBLANES_F32 = 8
LANES = 128

NEG_BIG = -1e30


def _params(sem, vmem=VMEM_LIMIT):
    return pltpu.CompilerParams(dimension_semantics=sem, vmem_limit_bytes=vmem)


def _sigmoid(t):
    return 0.5 * jnp.tanh(0.5 * t) + 0.5


def _silu(t):
    h = 0.5 * t
    return h + h * jnp.tanh(h)


ADA_TN = 768
ADA_ROWS = 8
(MOD_SHIFT, MOD_SCALE, MOD_GATE) = range(3)


def _ada_kernel(c_ref, w_ref, b_ref, o_ref):
    c = c_ref[...]
    pad = jnp.zeros((ADA_ROWS - c.shape[0], D_MODEL), F32)
    ca = _silu(jnp.concatenate([c, pad], axis=0)).astype(BF16)
    w = w_ref[...].astype(BF16)
    bias = b_ref[pl.ds(pl.program_id(0), 1), :]
    o_ref[...] = jnp.dot(ca, w, preferred_element_type=F32) + bias


def _ada(c, w_ada, b_ada):
    batch = c.shape[0]
    n3 = 3 * D_MODEL
    return pl.pallas_call(
        _ada_kernel,
        name="ada",
        grid=(DEPTH, n3 // ADA_TN),
        in_specs=[
            pl.BlockSpec((batch, D_MODEL), lambda l, j: (0, 0)),
            pl.BlockSpec((None, D_MODEL, ADA_TN), lambda l, j: (l, 0, j)),
            pl.BlockSpec((DEPTH, ADA_TN), lambda l, j: (0, j)),
        ],
        out_specs=pl.BlockSpec((None, ADA_ROWS, ADA_TN), lambda l, j: (l, 0, j)),
        out_shape=jax.ShapeDtypeStruct((DEPTH, ADA_ROWS, n3), F32),
        compiler_params=_params(("arbitrary", "arbitrary")),
    )(c, w_ada, b_ada)


INPROJ_TM = 512
INPROJ_WROWS = SUBLANES_BF16
INPROJ_WSLOTS = 8
INPROJ_VMEM = 60 * 1024 * 1024


def _inproj_kernel(x_ref, g_ref, sc_ref, sh_ref, w_hbm, o_ref, h0_ref, h1_ref, w_ref, stage_ref, sem,
                   *, n_tiles, per_batch, layer):
    t = pl.program_id(0)
    wrows = stage_ref.shape[1]
    n_rchunks = D_MODEL // wrows

    def weight_copy(r, slot):
        rows = pl.ds(pl.multiple_of(r * wrows, wrows), wrows)
        return pltpu.make_async_copy(w_hbm.at[layer, rows, :], stage_ref.at[slot], sem.at[slot])

    def load_weight():
        n_slots = stage_ref.shape[0]
        for slot in range(n_slots):
            weight_copy(slot, slot).start()

        def group(i, carry):
            for slot in range(n_slots):
                r = n_slots * i + slot
                weight_copy(r, slot).wait()
                rows = pl.ds(pl.multiple_of(r * wrows, wrows), wrows)
                w_ref[rows, :] = stage_ref[slot].astype(BF16)

                @pl.when(r + n_slots < n_rchunks)
                def _():
                    weight_copy(r + n_slots, slot).start()
            return carry

        lax.fori_loop(0, n_rchunks // n_slots, group, 0)

    def norm(h_ref):
        x = x_ref[...]
        row = pl.ds(jnp.minimum(t, n_tiles - 1) // per_batch, 1)
        ms = jnp.mean(x * x, axis=-1, keepdims=True)
        h = (x * lax.rsqrt(ms + EPS)) * g_ref[layer:layer + 1, :]
        h = h * (1.0 + sc_ref[row, :]) + sh_ref[row, :]
        h_ref[...] = h.astype(BF16)

    def project(h_ref):
        h = h_ref[...]
        for p in range(N_PIECES):
            w = w_ref[:, p * D_GROUP:(p + 1) * D_GROUP]
            o_ref[p] = jnp.dot(h, w, preferred_element_type=F32).astype(BF16)

    odd = t % 2 == 1

    @pl.when(t == 0)
    def _():
        load_weight()
        norm(h0_ref)

    @pl.when(odd)
    def _():
        project(h0_ref)
        norm(h1_ref)

    @pl.when(jnp.logical_and(jnp.logical_not(odd), jnp.logical_and(t > 0, t < n_tiles)))
    def _():
        project(h1_ref)
        norm(h0_ref)

    @pl.when(t == n_tiles)
    def _():
        project(h1_ref)


def _inproj(x, norm_g, mod, w_in, layer):
    batch, seq, _ = x.shape
    tm = INPROJ_TM
    per_batch = seq // tm
    n_tiles = batch * per_batch
    assert n_tiles % 2 == 0 and (D_MODEL // INPROJ_WROWS) % INPROJ_WSLOTS == 0
    norm_tile = lambda t: jnp.minimum(t, n_tiles - 1)
    proj_tile = lambda t: jnp.maximum(t - 1, 0)
    return pl.pallas_call(
        functools.partial(_inproj_kernel, n_tiles=n_tiles, per_batch=per_batch, layer=layer),
        name="inproj",
        grid=(n_tiles + 1,),
        in_specs=[
            pl.BlockSpec((None, tm, D_MODEL),
                         lambda t: (norm_tile(t) // per_batch, norm_tile(t) % per_batch, 0)),
            pl.BlockSpec((DEPTH, D_MODEL), lambda t: (0, 0)),
            pl.BlockSpec((None, ADA_ROWS, D_MODEL), lambda t: (layer, 0, MOD_SCALE)),
            pl.BlockSpec((None, ADA_ROWS, D_MODEL), lambda t: (layer, 0, MOD_SHIFT)),
            pl.BlockSpec(memory_space=pl.ANY),
        ],
        out_specs=pl.BlockSpec((N_PIECES, None, tm, D_GROUP),
                               lambda t: (0, proj_tile(t) // per_batch, proj_tile(t) % per_batch, 0)),
        out_shape=jax.ShapeDtypeStruct((N_PIECES, batch, seq, D_GROUP), BF16),
        scratch_shapes=[
            pltpu.VMEM((tm, D_MODEL), BF16), pltpu.VMEM((tm, D_MODEL), BF16),
            pltpu.VMEM((D_MODEL, N_PIECES * D_GROUP), BF16),
            pltpu.VMEM((INPROJ_WSLOTS, INPROJ_WROWS, N_PIECES * D_GROUP), F32),
            pltpu.SemaphoreType.DMA((INPROJ_WSLOTS,)),
        ],
        compiler_params=_params(("arbitrary",), INPROJ_VMEM),
    )(x, norm_g, mod, mod, w_in)


OUTPROJ_TM = 512
OUTPROJ_WROWS = 64
OUTPROJ_WSLOTS = 8
OUTPROJ_VMEM = 60 * 1024 * 1024


def _outproj_kernel(a0_ref, a1_ref, a2_ref, w_hbm, x_ref, gate_ref, fg_ref,
                    ca_ref, cb_ref, cg_ref, cw_ref, cbias_ref, lg_ref, lb_ref, wpw_ref,
                    o_ref, u_ref, h_ref, cv0_ref, cv1_ref, w_ref, stage_ref, sem,
                    *, final, n_tiles, per_batch, seq, layer):
    t = pl.program_id(0)
    j = jnp.minimum(t, n_tiles - 1) % per_batch
    gate_row = pl.ds(jnp.maximum(t - 1, 0) // per_batch, 1)
    layer_row = pl.ds(layer, 1)

    tm = o_ref.shape[0]
    n_chunks = tm // CONV_ROWS
    tn = D_MODEL // n_chunks
    n_slots = stage_ref.shape[0]

    wrows = stage_ref.shape[1]

    def weight_copy(r):
        slot = r % n_slots
        return pltpu.make_async_copy(w_hbm.at[layer, pl.ds(r * wrows, wrows), :], stage_ref.at[slot],
                                     sem.at[slot])

    def run(proj_cv_ref, conv_cv_ref, load_weight=False):
        if load_weight:
            for c in range(n_slots):
                weight_copy(c).start()
        if conv_cv_ref is not None:
            _conv_glu(j, ca_ref, cb_ref, u_ref, seq=seq)
        if proj_cv_ref is not None:
            a = jnp.concatenate([a0_ref[...], a1_ref[...], a2_ref[...], proj_cv_ref[...]], axis=-1)
            ss = jnp.zeros((tm, LANES), F32)
        zero = 0
        for i in range(n_chunks):
            if proj_cv_ref is not None:
                y = jnp.dot(a, w_ref[i + zero], preferred_element_type=F32)
                xn = x_ref[:, i * tn:(i + 1) * tn] + gate_ref[gate_row, i * tn:(i + 1) * tn] * y
                if final:
                    sq = xn * xn
                    for lt in range(tn // LANES):
                        ss = ss + sq[:, lt * LANES:(lt + 1) * LANES]
                o_ref[:, i * tn:(i + 1) * tn] = xn
            if conv_cv_ref is not None:
                tail = _conv_rows(i * CONV_ROWS, cw_ref, cbias_ref.at[layer_row], lg_ref.at[layer_row],
                                  lb_ref.at[layer_row], u_ref, h_ref)
                zero = _runtime_zero(tail)
            if load_weight:
                n_rchunks = D_MODEL // wrows
                per_block = n_rchunks // n_chunks
                for r in range(i * per_block, (i + 1) * per_block):
                    weight_copy(r).wait()
                    for c in range(n_chunks):
                        w_ref[c, r * wrows:(r + 1) * wrows, :] = stage_ref[
                            r % n_slots, :, c * tn:(c + 1) * tn].astype(BF16)
                    if r + n_slots < n_rchunks:
                        weight_copy(r + n_slots).start()
        if conv_cv_ref is not None:
            _conv_pointwise(cg_ref, wpw_ref, h_ref, conv_cv_ref)
        if proj_cv_ref is not None and final:
            ms = jnp.sum(ss, axis=-1, keepdims=True) * (1.0 / D_MODEL)
            o_ref[...] = (o_ref[...] * lax.rsqrt(ms + EPS)) * fg_ref[...]

    odd = t % 2 == 1

    @pl.when(t == 0)
    def _():
        run(None, cv0_ref, load_weight=True)

    @pl.when(odd)
    def _():
        run(cv0_ref, cv1_ref)

    @pl.when(jnp.logical_and(jnp.logical_not(odd), jnp.logical_and(t > 0, t < n_tiles)))
    def _():
        run(cv1_ref, cv0_ref)

    @pl.when(t == n_tiles)
    def _():
        run(cv1_ref, None)


def _outproj(mixed, zp, conv_params, w_pw_bf16, w_out, x, mod, final_g, layer, final):
    batch, seq, _ = x.shape
    tm = OUTPROJ_TM
    per_batch = seq // tm
    n_tiles = batch * per_batch
    n_wchunks = tm // CONV_ROWS
    assert n_tiles % 2 == 0
    conv_tile = lambda t: jnp.minimum(t, n_tiles - 1)
    proj_tile = lambda t: jnp.maximum(t - 1, 0)
    a_spec = pl.BlockSpec((None, tm, D_GROUP),
                          lambda t: (proj_tile(t) // per_batch, proj_tile(t) % per_batch, 0))
    x_spec = pl.BlockSpec((None, tm, D_MODEL),
                          lambda t: (proj_tile(t) // per_batch, proj_tile(t) % per_batch, 0))
    seq_spec = lambda p: pl.BlockSpec((None, None, seq, D_GROUP),
                                      lambda t: (p, conv_tile(t) // per_batch, 0, 0))
    vec = pl.BlockSpec((DEPTH, D_GROUP), lambda t: (0, 0))
    conv_w, conv_b, ln_g, ln_b = conv_params
    return pl.pallas_call(
        functools.partial(_outproj_kernel, final=final, n_tiles=n_tiles, per_batch=per_batch, seq=seq,
                          layer=layer),
        name="outproj",
        grid=(n_tiles + 1,),
        in_specs=[
            a_spec, a_spec, a_spec,
            pl.BlockSpec(memory_space=pl.ANY),
            x_spec,
            pl.BlockSpec((None, ADA_ROWS, D_MODEL), lambda t: (layer, 0, MOD_GATE)),
            pl.BlockSpec((1, D_MODEL), lambda t: (0, 0)),
            seq_spec(P_CA), seq_spec(P_CB),
            pl.BlockSpec((None, None, tm, D_GROUP),
                         lambda t: (P_CG, conv_tile(t) // per_batch, conv_tile(t) % per_batch, 0)),
            pl.BlockSpec((None, CONV_WIDTH, D_GROUP), lambda t: (layer, 0, 0)),
            vec, vec, vec,
            pl.BlockSpec((None, D_GROUP, D_GROUP), lambda t: (layer, 0, 0)),
        ],
        out_specs=x_spec,
        out_shape=jax.ShapeDtypeStruct((batch, seq, D_MODEL), F32),
        scratch_shapes=[
            pltpu.VMEM((tm + 2 * CONV_HALO, D_GROUP), F32),
            pltpu.VMEM((tm, D_GROUP), BF16),
            pltpu.VMEM((tm, D_GROUP), BF16), pltpu.VMEM((tm, D_GROUP), BF16),
            pltpu.VMEM((n_wchunks, D_MODEL, D_MODEL // n_wchunks), BF16),
            pltpu.VMEM((OUTPROJ_WSLOTS, OUTPROJ_WROWS, D_MODEL), F32),
            pltpu.SemaphoreType.DMA((OUTPROJ_WSLOTS,)),
        ],
        compiler_params=_params(("arbitrary",), OUTPROJ_VMEM),
    )(*mixed, w_out, x, mod, final_g.reshape(1, D_MODEL), zp, zp, zp, conv_w,
      conv_b, ln_g, ln_b, w_pw_bf16)


FFT_BLOCK = 16
FFT_LANES = 2 * FFT_GROUP_DIM


def _fft_constants(seq):
    rows = seq // GRID_W
    assert rows == GRID_W
    n = np.arange(GRID_W)
    ang1 = 2.0 * np.pi * ((n[:, None] * n[None, :]) % GRID_W) / GRID_W
    f1 = np.concatenate([np.cos(ang1), -np.sin(ang1)], axis=0)
    k1 = n[:, None, None]
    k2 = n[None, :, None]
    s2 = n[None, None, :]
    ang2 = 2.0 * np.pi * ((s2 * (k1 + GRID_W * k2)) % seq) / seq
    mr, mi = np.cos(ang2), -np.sin(ang2)
    m2 = np.concatenate([np.concatenate([mr, -mi], axis=2),
                         np.concatenate([mi, mr], axis=2)], axis=1)
    c = np.arange(FFT_GROUP_DIM)
    angc = 2.0 * np.pi * ((c[:, None] * c[None, :]) % FFT_GROUP_DIM) / FFT_GROUP_DIM
    fc = np.concatenate([np.cos(angc), np.sin(angc)], axis=0)
    return tuple(jnp.asarray(t, F32).astype(BF16) for t in (f1, m2, fc))


def _fft_kernel(u_ref, gate_ref, f1_ref, m2_ref, fc_ref, w_ref, o_ref, ut_ref, g_ref, gt_ref, y_ref, *, norm):
    pair = pl.program_id(1)
    n = GRID_W
    blk = FFT_BLOCK
    sub = SUBLANES_F32
    lanes = ut_ref.shape[-1]
    groups = lanes // FFT_GROUP_DIM

    u3 = u_ref[...].astype(F32).reshape(n, n, lanes)
    ut_ref[...] = jnp.swapaxes(u3, 0, 1).astype(BF16)

    def stage1(i, carry):
        for j in range(blk):
            s2 = i * blk + j
            g_ref[s2] = jnp.dot(f1_ref[...], ut_ref[s2], preferred_element_type=F32)
        return carry

    lax.fori_loop(0, n // blk, stage1, 0)

    for q in range(2 * n // sub):
        part = g_ref[:, q * sub:(q + 1) * sub, :]
        gt_ref[q * sub:(q + 1) * sub] = jnp.swapaxes(part, 0, 1).astype(BF16)

    def stage2(i, carry):
        xs = []
        for j in range(blk):
            k1 = i * blk + j
            gk = jnp.concatenate([gt_ref[k1], gt_ref[n + k1]], axis=0)
            xs.append(jnp.dot(m2_ref[k1], gk, preferred_element_type=F32))
        for g in range(groups):
            sl = slice(g * FFT_GROUP_DIM, (g + 1) * FFT_GROUP_DIM)
            lhs = jnp.concatenate(
                [jnp.concatenate([xk[:n, sl], xk[n:, sl]], axis=-1) for xk in xs], axis=0)
            y = jnp.dot(lhs.astype(BF16), fc_ref[...], preferred_element_type=F32) * norm
            y_ref[pair * groups + g, pl.ds(pl.multiple_of(i * blk, blk), blk)] = (
                y.reshape(blk, n, FFT_GROUP_DIM))
        return carry

    lax.fori_loop(0, n // blk, stage2, 0)

    @pl.when(pair == pl.num_programs(1) - 1)
    def _():
        def stage3(i, carry):
            k2 = pl.ds(pl.multiple_of(i * sub, sub), sub)
            tiles = []
            for g in range(N_FFT_GROUPS):
                part = y_ref[g, :, k2, :]
                tiles.append(jnp.swapaxes(part, 0, 1).reshape(sub * n, FFT_GROUP_DIM))
            y = jnp.concatenate(tiles, axis=-1).astype(BF16)
            o = jnp.dot(y, w_ref[...], preferred_element_type=F32)
            rows = pl.ds(pl.multiple_of(i * sub * n, sub * n), sub * n)
            o_ref[rows, :] = (o * _silu(gate_ref[rows, :].astype(F32))).astype(BF16)
            return carry

        lax.fori_loop(0, n // sub, stage3, 0, unroll=2)


def _fourier(zp, w_fft_bf16, consts, layer):
    _, batch, seq, _ = zp.shape
    f1, m2, fc = consts
    n = GRID_W
    const = lambda shape: pl.BlockSpec(shape, lambda b, g: (0,) * len(shape))
    return pl.pallas_call(
        functools.partial(_fft_kernel, norm=1.0 / math.sqrt(seq * FFT_GROUP_DIM)),
        name="fourier",
        grid=(batch, D_GROUP // FFT_LANES),
        in_specs=[
            pl.BlockSpec((None, None, seq, FFT_LANES), lambda b, g: (P_FX, b, 0, g)),
            pl.BlockSpec((None, None, seq, D_GROUP), lambda b, g: (P_FG, b, 0, 0)),
            const((2 * n, n)), const((n, 2 * n, 2 * n)), const((2 * FFT_GROUP_DIM, FFT_GROUP_DIM)),
            pl.BlockSpec((None, D_GROUP, D_GROUP), lambda b, g: (layer, 0, 0)),
        ],
        out_specs=pl.BlockSpec((None, seq, D_GROUP), lambda b, g: (b, 0, 0)),
        out_shape=jax.ShapeDtypeStruct((batch, seq, D_GROUP), BF16),
        scratch_shapes=[
            pltpu.VMEM((n, n, FFT_LANES), BF16),
            pltpu.VMEM((n, 2 * n, FFT_LANES), F32),
            pltpu.VMEM((2 * n, n, FFT_LANES), BF16),
            pltpu.VMEM((N_FFT_GROUPS, n, n, FFT_GROUP_DIM), F32),
        ],
        compiler_params=_params(("arbitrary", "arbitrary")),
    )(zp, zp, f1, m2, fc, w_fft_bf16)


NA_ROWS_PER_STEP = 16


def _na_bias_table(rel_bias):
    n_dc = 2 * NA_KW - 1
    col = np.arange(GRID_W)
    col_start = np.clip(col - NA_KW // 2, 0, GRID_W - NA_KW)
    rel_c = col[None, :] - col_start[:, None]
    col_in = (rel_c >= 0) & (rel_c < NA_KW)
    dc = np.clip(col[None, :] - col[:, None] + NA_KW - 1, 0, n_dc - 1)
    onehot = (dc[None] == np.arange(n_dc)[:, None, None]).astype(np.float32)
    zeros = np.zeros_like(onehot)
    onehot2 = np.concatenate([np.concatenate([onehot, zeros], axis=-1),
                              np.concatenate([zeros, onehot], axis=-1)], axis=0)
    rb = rel_bias.astype(F32)
    rb2 = jnp.concatenate([rb[:, :, :-1], rb[:, :, 1:]], axis=-1)
    tbl = jnp.einsum('lhdj,jqk->ldhqk', rb2, jnp.asarray(onehot2),
                     precision=lax.Precision.HIGHEST)
    col_in2 = np.concatenate([col_in, col_in], axis=-1)
    return jnp.where(col_in2[None, None, None], tbl, NEG_BIG)


def _head_select_mask():
    r = lax.broadcasted_iota(jnp.int32, (MXU_DIM, MXU_DIM), 0) // HEAD_DIM
    c = lax.broadcasted_iota(jnp.int32, (MXU_DIM, MXU_DIM), 1) // HEAD_DIM
    return r == c


def _stack_heads(t):
    n = t.shape[0]
    reps = jnp.concatenate([t] * HEADS_PER_TILE, axis=0)
    row_head = lax.broadcasted_iota(jnp.int32, reps.shape, 0) // n
    lane_head = lax.broadcasted_iota(jnp.int32, reps.shape, 1) // HEAD_DIM
    return jnp.where(row_head == lane_head, reps, jnp.zeros_like(reps))


def _unstack_heads(t4, n):
    lane_head = lax.broadcasted_iota(jnp.int32, (n, MXU_DIM), 1) // HEAD_DIM
    out = jnp.zeros((n, MXU_DIM), t4.dtype)
    for h in range(HEADS_PER_TILE):
        out = jnp.where(lane_head == h, t4[h * n:(h + 1) * n], out)
    return out


def _na_kernel(q_ref, k_ref, v_ref, g_ref, bias_ref, o_ref, *, rows):
    kh = min(NA_KH, rows)
    step = pl.program_id(1)

    def row_body(rr, carry):
        r = step * NA_ROWS_PER_STEP + rr
        r_start = jnp.clip(r - kh // 2, 0, rows - kh)
        variant = r - r_start
        q_off = pl.multiple_of(rr * GRID_W, GRID_W)
        k_off = pl.multiple_of(r_start * GRID_W, GRID_W)
        for cg in range(D_GROUP // MXU_DIM):
            lanes = slice(cg * MXU_DIM, (cg + 1) * MXU_DIM)
            q = q_ref[pl.ds(q_off, GRID_W), lanes] * (HEAD_DIM ** -0.5)
            kw = k_ref[pl.ds(k_off, kh * GRID_W), lanes]
            vw = v_ref[pl.ds(k_off, kh * GRID_W), lanes]
            q4 = _stack_heads(q)
            s = lax.dot_general(q4, kw, (((1,), (1,)), ((), ())), preferred_element_type=F32)
            heads = slice(cg * HEADS_PER_TILE, (cg + 1) * HEADS_PER_TILE)
            bias = jnp.concatenate(
                [bias_ref[NA_KH - 1 - variant + a, heads].reshape(HEADS_PER_TILE * GRID_W, 2 * GRID_W)
                 for a in range(0, kh, 2)], axis=-1)
            s = s + bias
            m = jnp.max(s, axis=-1, keepdims=True)
            e = jnp.exp(s - m)
            l = jnp.sum(e, axis=-1, keepdims=True)
            o4 = jnp.dot(e.astype(BF16), vw, preferred_element_type=F32) / l
            o = _unstack_heads(o4, GRID_W)
            gate = g_ref[pl.ds(q_off, GRID_W), lanes].astype(F32)
            o_ref[pl.ds(q_off, GRID_W), lanes] = (o * _silu(gate)).astype(BF16)
        return carry

    lax.fori_loop(0, NA_ROWS_PER_STEP, row_body, 0, unroll=True)


def _neighbourhood(zp, bias_tbl, layer):
    _, batch, seq, _ = zp.shape
    rows = seq // GRID_W
    tq = NA_ROWS_PER_STEP * GRID_W
    q_spec = lambda p: pl.BlockSpec((None, None, tq, D_GROUP), lambda b, i: (p, b, i, 0))
    kv_spec = lambda p: pl.BlockSpec((None, None, seq, D_GROUP), lambda b, i: (p, b, 0, 0))
    return pl.pallas_call(
        functools.partial(_na_kernel, rows=rows),
        name="natten",
        grid=(batch, rows // NA_ROWS_PER_STEP),
        in_specs=[
            q_spec(P_NQ), kv_spec(P_NK), kv_spec(P_NV), q_spec(P_NG),
            pl.BlockSpec((None,) + bias_tbl.shape[1:], lambda b, i: (layer, 0, 0, 0, 0),
                         pipeline_mode=pl.Buffered(1)),
        ],
        out_specs=pl.BlockSpec((None, tq, D_GROUP), lambda b, i: (b, i, 0)),
        out_shape=jax.ShapeDtypeStruct((batch, seq, D_GROUP), BF16),
        compiler_params=_params(("arbitrary", "arbitrary")),
    )(zp, zp, zp, zp, bias_tbl)


RET_CHUNK = MXU_DIM


def _log_sigmoid(t):
    return jnp.minimum(t, 0.0) - jnp.log1p(jnp.exp(-jnp.abs(t)))


def _rope_tables(seq):
    half = HEAD_DIM // 2
    inv = ROPE_BASE ** (-jnp.arange(half, dtype=F32) / half)
    ang = jnp.arange(seq, dtype=F32)[:, None] * inv[None, :]
    cos, sin = jnp.cos(ang), jnp.sin(ang)
    cos2 = jnp.concatenate([cos, cos], axis=-1)
    sin2 = jnp.concatenate([-sin, sin], axis=-1)
    return jnp.tile(cos2, (1, 2)), jnp.tile(sin2, (1, 2))


def _ret_kernel(lf_s_ref, lb_s_ref, q_ref, k_ref, v_ref, g_ref, cos_ref, sin_ref,
                o_ref, qr_ref, kr_ref, ob_ref, dmat_ref, dec_ref, state_ref, ones_ref, *, seq, layer):
    C = RET_CHUNK
    n_chunks = seq // C
    cg = pl.program_id(1)
    ones_bd = jnp.where(_head_select_mask(), 1.0, 0.0).astype(BF16)
    ones_ref[...] = jnp.concatenate([ones_bd, ones_bd], axis=0)

    ri = lax.broadcasted_iota(jnp.int32, (C, C), 0)
    ci = lax.broadcasted_iota(jnp.int32, (C, C), 1)
    diff = (ri - ci).astype(F32)
    lane_head = lax.broadcasted_iota(jnp.int32, (1, MXU_DIM), 1) // HEAD_DIM
    lfl = jnp.zeros((1, MXU_DIM), F32)
    lbl = jnp.zeros((1, MXU_DIM), F32)
    for hh in range(HEADS_PER_TILE):
        h = cg * HEADS_PER_TILE + hh
        lf = _log_sigmoid(jnp.full((C, C), lf_s_ref[layer, h], F32))
        lb = _log_sigmoid(jnp.full((C, C), lb_s_ref[layer, h], F32))
        dmat_ref[hh] = jnp.where(diff >= 0, jnp.exp(lf * diff), jnp.exp(lb * (-diff)))
        lfl = jnp.where(lane_head == hh, lf_s_ref[layer, h], lfl)
        lbl = jnp.where(lane_head == hh, lb_s_ref[layer, h], lbl)
    lfl = _log_sigmoid(lfl)
    lbl = _log_sigmoid(lbl)
    idx = lax.broadcasted_iota(jnp.int32, (C, MXU_DIM), 0).astype(F32)
    dec_ref[0] = jnp.exp(lfl * (idx + 1.0))
    dec_ref[1] = jnp.exp(lfl * (C - 1.0 - idx))
    dec_ref[2] = jnp.exp(lbl * (C - idx))
    dec_ref[3] = jnp.exp(lbl * idx)
    cd_f = jnp.exp(lfl * float(C))
    cd_b = jnp.exp(lbl * float(C))

    def rope(t, rows):
        lane = lax.broadcasted_iota(jnp.int32, t.shape, 1)
        first_half = (lane % HEAD_DIM) < (HEAD_DIM // 2)
        cs = cos_ref[rows, :]
        sn = sin_ref[rows, :]
        cs = jnp.concatenate([cs, cs], axis=-1)
        sn = jnp.concatenate([sn, sn], axis=-1)
        swapped = jnp.where(first_half, pltpu.roll(t, MXU_DIM - HEAD_DIM // 2, 1),
                            pltpu.roll(t, HEAD_DIM // 2, 1))
        return t * cs + swapped * sn

    def kv_update(k, v, kd, cd):
        kv = lax.dot_general((k * kd).astype(BF16), v, (((0,), (0,)), ((), ())),
                             preferred_element_type=F32)
        state_ref[...] = cd * state_ref[...] + jnp.where(_head_select_mask(), kv, 0.0)

    state_ref[...] = jnp.zeros_like(state_ref)

    def bwd_body(i, carry):
        n = n_chunks - 1 - i
        rows = pl.ds(pl.multiple_of(n * C, C), C)
        q = rope(q_ref[rows, :].astype(F32), rows) * (HEAD_DIM ** -0.5)
        k = rope(k_ref[rows, :].astype(F32), rows)
        qr_ref[rows, :] = q.astype(BF16)
        kr_ref[rows, :] = k.astype(BF16)
        ob_ref[rows, :] = jnp.dot((q * dec_ref[2]).astype(BF16), state_ref[...].astype(BF16),
                                  preferred_element_type=F32)
        kv_update(k, v_ref[rows, :], dec_ref[3], cd_b)
        return carry

    lax.fori_loop(0, n_chunks, bwd_body, 0, unroll=8)

    state_ref[...] = jnp.zeros_like(state_ref)

    def fwd_body(n, carry):
        rows = pl.ds(pl.multiple_of(n * C, C), C)
        qb = qr_ref[rows, :]
        kb = kr_ref[rows, :]
        v = v_ref[rows, :]
        o_cross = jnp.dot((qb.astype(F32) * dec_ref[0]).astype(BF16), state_ref[...].astype(BF16),
                          preferred_element_type=F32)
        q4 = _stack_heads(qb)
        sc = lax.dot_general(q4, kb, (((1,), (1,)), ((), ())), preferred_element_type=F32)
        sc = sc * dmat_ref[...].reshape(HEADS_PER_TILE * C, C)
        o4 = jnp.dot(sc.astype(BF16), v, preferred_element_type=F32)
        o = _unstack_heads(o4, C) + o_cross + ob_ref[rows, :]
        sq = o * o
        hi = sq.astype(BF16)
        lo = (sq - hi.astype(F32)).astype(BF16)
        ss = jnp.dot(jnp.concatenate([hi, lo], axis=-1), ones_ref[...], preferred_element_type=F32)
        inv = lax.rsqrt(ss * (1.0 / HEAD_DIM) + EPS)
        gate = g_ref[rows, :].astype(F32)
        o_ref[rows, :] = (o * inv * _silu(gate)).astype(BF16)
        kv_update(kb.astype(F32), v, dec_ref[1], cd_f)
        return carry

    lax.fori_loop(0, n_chunks, fwd_body, 0, unroll=8)


def _retention(zp, logit_f, logit_b, rope_tbl, layer):
    _, batch, seq, _ = zp.shape
    cos_t, sin_t = rope_tbl
    n_cg = D_GROUP // MXU_DIM
    spec = lambda p: pl.BlockSpec((None, None, seq, MXU_DIM), lambda b, c, *_: (p, b, 0, c))
    tbl_spec = pl.BlockSpec((seq, 2 * HEAD_DIM), lambda b, c, *_: (0, 0))
    return pl.pallas_call(
        functools.partial(_ret_kernel, seq=seq, layer=layer),
        name="retention",
        grid_spec=pltpu.PrefetchScalarGridSpec(
            num_scalar_prefetch=2,
            grid=(batch, n_cg),
            in_specs=[spec(P_RQ), spec(P_RK), spec(P_RV), spec(P_RG), tbl_spec, tbl_spec],
            out_specs=pl.BlockSpec((None, seq, MXU_DIM), lambda b, c, *_: (b, 0, c)),
            scratch_shapes=[
                pltpu.VMEM((seq, MXU_DIM), BF16),
                pltpu.VMEM((seq, MXU_DIM), BF16),
                pltpu.VMEM((seq, MXU_DIM), F32),
                pltpu.VMEM((HEADS_PER_TILE, RET_CHUNK, RET_CHUNK), F32),
                pltpu.VMEM((4, RET_CHUNK, MXU_DIM), F32),
                pltpu.VMEM((MXU_DIM, MXU_DIM), F32),
                pltpu.VMEM((2 * MXU_DIM, MXU_DIM), BF16),
            ],
        ),
        out_shape=jax.ShapeDtypeStruct((batch, seq, D_GROUP), BF16),
        compiler_params=_params(("arbitrary", "arbitrary")),
    )(logit_f, logit_b, zp, zp, zp, zp, cos_t, sin_t)


CONV_HALO = SUBLANES_BF16
CONV_ROWS = 128


def _conv_glu(j, a_ref, b_ref, u_ref, *, seq):
    halo = CONV_HALO
    ts = u_ref.shape[0] - 2 * halo
    n_tiles = seq // ts

    def glu(start, size):
        rows = pl.ds(pl.multiple_of(start, halo), size)
        a = a_ref[rows, :].astype(F32)
        return a * _sigmoid(b_ref[rows, :].astype(F32))

    base = j * ts
    u_ref[halo:halo + ts, :] = glu(base, ts)
    lo = glu(jnp.maximum(base - halo, 0), halo)
    u_ref[0:halo, :] = jnp.where(j > 0, lo, 0.0)
    hi = glu(jnp.minimum(base + ts, seq - halo), halo)
    u_ref[halo + ts:, :] = jnp.where(j < n_tiles - 1, hi, 0.0)


def _conv_rows(s0, cw_ref, cb_ref, lg_ref, lb_ref, u_ref, h_ref):
    rb = CONV_ROWS
    first = CONV_HALO - CONV_HALF
    parts = []
    for lt in range(D_GROUP // LANES):
        lanes = slice(lt * LANES, (lt + 1) * LANES)
        y = None
        for r in range(SUBLANES_F32):
            acc = None
            for o in range(r, first + CONV_WIDTH, SUBLANES_F32):
                if o < first:
                    continue
                w = o - first
                lo_row = s0 + (o - r)
                term = u_ref[lo_row:lo_row + rb + SUBLANES_F32, lanes] * cw_ref[w:w + 1, lanes]
                acc = term if acc is None else acc + term
            shifted = acc[r:r + rb]
            y = shifted if y is None else y + shifted
        parts.append(y)
    y = jnp.concatenate(parts, axis=-1) + cb_ref[...]
    mu = jnp.mean(y, axis=-1, keepdims=True)
    yc = y - mu
    var = jnp.mean(yc * yc, axis=-1, keepdims=True)
    y = (yc * lax.rsqrt(var + EPS)) * lg_ref[...] + lb_ref[...]
    h_ref[s0:s0 + rb, :] = _silu(y).astype(BF16)
    return y[rb - SUBLANES_F32:, D_GROUP - LANES:]


def _runtime_zero(tile):
    bits = pltpu.bitcast(tile, jnp.uint32)
    bits = lax.shift_right_logical(lax.shift_right_logical(bits, jnp.uint32(16)), jnp.uint32(16))
    return bits[0, 0].astype(jnp.int32)


def _conv_pointwise(g_ref, w_ref, h_ref, o_ref):
    o = jnp.dot(h_ref[...], w_ref[...], preferred_element_type=F32)
    o_ref[...] = (o * _silu(g_ref[...].astype(F32))).astype(BF16)


def kernel(x, c, norm_g, w_ada, b_ada, w_in, w_fft, na_rel_bias, ret_logit_fwd, ret_logit_bwd,
           conv_w, conv_b, conv_ln_g, conv_ln_b, conv_w_pw, w_out, final_g):
    seq = x.shape[1]
    mod = _ada(c, w_ada, b_ada)
    fft_consts = _fft_constants(seq)
    rope_tbl = _rope_tables(seq)
    bias_tbl = _na_bias_table(na_rel_bias)
    w_fft_b, w_pw_b = w_fft.astype(BF16), conv_w_pw.astype(BF16)
    for l in range(DEPTH):
        zp = _inproj(x, norm_g, mod, w_in, l)
        o_fft = _fourier(zp, w_fft_b, fft_consts, l)
        o_na = _neighbourhood(zp, bias_tbl, l)
        o_ret = _retention(zp, ret_logit_fwd, ret_logit_bwd, rope_tbl, l)
        conv_params = (conv_w, conv_b, conv_ln_g, conv_ln_b)
        x = _outproj((o_fft, o_na, o_ret), zp, conv_params, w_pw_b, w_out, x, mod, final_g, l,
                     final=(l == DEPTH - 1))
    return x
```

```python
import functools
import math

import numpy as np
import jax
import jax.numpy as jnp
from jax import lax
from jax.experimental import pallas as pl
from jax.experimental.pallas import tpu as pltpu

F32 = jnp.float32
BF16 = jnp.bfloat16

D_MODEL = 2048
DEPTH = 2
GRID_W = 64
D_GROUP = 512
HEAD_DIM = 64
N_HEADS = D_GROUP // HEAD_DIM
N_FFT_GROUPS = 4
FFT_GROUP_DIM = D_GROUP // N_FFT_GROUPS
NA_KH = 8
NA_KW = 16
CONV_WIDTH = 31
CONV_HALF = CONV_WIDTH // 2
ROPE_BASE = 10000.0
EPS = 1e-6
N_PIECES = 13
(P_FX, P_FG, P_NQ, P_NK, P_NV, P_NG, P_RQ, P_RK, P_RV, P_RG, P_CA, P_CB, P_CG) = range(N_PIECES)

MXU_DIM = 256
HEADS_PER_TILE = MXU_DIM // HEAD_DIM
VMEM_LIMIT = 56 * 1024 * 1024
SUBLANES_BF16 = 16
SUBLANES_F32 = 8
LANES = 128

NEG_BIG = -1e30


def _params(sem, vmem=VMEM_LIMIT):
    return pltpu.CompilerParams(dimension_semantics=sem, vmem_limit_bytes=vmem)


def _sigmoid(t):
    return 0.5 * jnp.tanh(0.5 * t) + 0.5


def _silu(t):
    h = 0.5 * t
    return h + h * jnp.tanh(h)


ADA_TN = 768
ADA_ROWS = 8
(MOD_SHIFT, MOD_SCALE, MOD_GATE) = range(3)


def _ada_kernel(c_ref, w_ref, b_ref, o_ref):
    c = c_ref[...]
    pad = jnp.zeros((ADA_ROWS - c.shape[0], D_MODEL), F32)
    ca = _silu(jnp.concatenate([c, pad], axis=0)).astype(BF16)
    w = w_ref[...].astype(BF16)
    bias = b_ref[pl.ds(pl.program_id(0), 1), :]
    o_ref[...] = jnp.dot(ca, w, preferred_element_type=F32) + bias


def _ada(c, w_ada, b_ada):
    batch = c.shape[0]
    n3 = 3 * D_MODEL
    return pl.pallas_call(
        _ada_kernel,
        name="ada",
        grid=(DEPTH, n3 // ADA_TN),
        in_specs=[
            pl.BlockSpec((batch, D_MODEL), lambda l, j: (0, 0)),
            pl.BlockSpec((None, D_MODEL, ADA_TN), lambda l, j: (l, 0, j)),
            pl.BlockSpec((DEPTH, ADA_TN), lambda l, j: (0, j)),
        ],
        out_specs=pl.BlockSpec((None, ADA_ROWS, ADA_TN), lambda l, j: (l, 0, j)),
        out_shape=jax.ShapeDtypeStruct((DEPTH, ADA_ROWS, n3), F32),
        compiler_params=_params(("arbitrary", "arbitrary")),
    )(c, w_ada, b_ada)


INPROJ_TM = 512
INPROJ_WROWS = SUBLANES_BF16
INPROJ_WSLOTS = 8
INPROJ_VMEM = 60 * 1024 * 1024


def _inproj_kernel(x_ref, g_ref, sc_ref, sh_ref, w_hbm, o_ref, h0_ref, h1_ref, w_ref, stage_ref, sem,
                   *, n_tiles, per_batch, layer):
    t = pl.program_id(0)
    wrows = stage_ref.shape[1]
    n_rchunks = D_MODEL // wrows

    def weight_copy(r, slot):
        rows = pl.ds(pl.multiple_of(r * wrows, wrows), wrows)
        return pltpu.make_async_copy(w_hbm.at[layer, rows, :], stage_ref.at[slot], sem.at[slot])

    def load_weight():
        n_slots = stage_ref.shape[0]
        for slot in range(n_slots):
            weight_copy(slot, slot).start()

        def group(i, carry):
            for slot in range(n_slots):
                r = n_slots * i + slot
                weight_copy(r, slot).wait()
                rows = pl.ds(pl.multiple_of(r * wrows, wrows), wrows)
                w_ref[rows, :] = stage_ref[slot].astype(BF16)

                @pl.when(r + n_slots < n_rchunks)
                def _():
                    weight_copy(r + n_slots, slot).start()
            return carry

        lax.fori_loop(0, n_rchunks // n_slots, group, 0)

    def norm(h_ref):
        x = x_ref[...]
        row = pl.ds(jnp.minimum(t, n_tiles - 1) // per_batch, 1)
        ms = jnp.mean(x * x, axis=-1, keepdims=True)
        h = (x * lax.rsqrt(ms + EPS)) * g_ref[layer:layer + 1, :]
        h = h * (1.0 + sc_ref[row, :]) + sh_ref[row, :]
        h_ref[...] = h.astype(BF16)

    def project(h_ref):
        h = h_ref[...]
        for p in range(N_PIECES):
            w = w_ref[:, p * D_GROUP:(p + 1) * D_GROUP]
            o_ref[p] = jnp.dot(h, w, preferred_element_type=F32).astype(BF16)

    odd = t % 2 == 1

    @pl.when(t == 0)
    def _():
        load_weight()
        norm(h0_ref)

    @pl.when(odd)
    def _():
        project(h0_ref)
        norm(h1_ref)

    @pl.when(jnp.logical_and(jnp.logical_not(odd), jnp.logical_and(t > 0, t < n_tiles)))
    def _():
        project(h1_ref)
        norm(h0_ref)

    @pl.when(t == n_tiles)
    def _():
        project(h1_ref)


def _inproj(x, norm_g, mod, w_in, layer):
    batch, seq, _ = x.shape
    tm = INPROJ_TM
    per_batch = seq // tm
    n_tiles = batch * per_batch
    assert n_tiles % 2 == 0 and (D_MODEL // INPROJ_WROWS) % INPROJ_WSLOTS == 0
    norm_tile = lambda t: jnp.minimum(t, n_tiles - 1)
    proj_tile = lambda t: jnp.maximum(t - 1, 0)
    return pl.pallas_call(
        functools.partial(_inproj_kernel, n_tiles=n_tiles, per_batch=per_batch, layer=layer),
        name="inproj",
        grid=(n_tiles + 1,),
        in_specs=[
            pl.BlockSpec((None, tm, D_MODEL),
                         lambda t: (norm_tile(t) // per_batch, norm_tile(t) % per_batch, 0)),
            pl.BlockSpec((DEPTH, D_MODEL), lambda t: (0, 0)),
            pl.BlockSpec((None, ADA_ROWS, D_MODEL), lambda t: (layer, 0, MOD_SCALE)),
            pl.BlockSpec((None, ADA_ROWS, D_MODEL), lambda t: (layer, 0, MOD_SHIFT)),
            pl.BlockSpec(memory_space=pl.ANY),
        ],
        out_specs=pl.BlockSpec((N_PIECES, None, tm, D_GROUP),
                               lambda t: (0, proj_tile(t) // per_batch, proj_tile(t) % per_batch, 0)),
        out_shape=jax.ShapeDtypeStruct((N_PIECES, batch, seq, D_GROUP), BF16),
        scratch_shapes=[
            pltpu.VMEM((tm, D_MODEL), BF16), pltpu.VMEM((tm, D_MODEL), BF16),
            pltpu.VMEM((D_MODEL, N_PIECES * D_GROUP), BF16),
            pltpu.VMEM((INPROJ_WSLOTS, INPROJ_WROWS, N_PIECES * D_GROUP), F32),
            pltpu.SemaphoreType.DMA((INPROJ_WSLOTS,)),
        ],
        compiler_params=_params(("arbitrary",), INPROJ_VMEM),
    )(x, norm_g, mod, mod, w_in)


OUTPROJ_TM = 512
OUTPROJ_WROWS = 64
OUTPROJ_WSLOTS = 8
OUTPROJ_VMEM = 60 * 1024 * 1024


def _outproj_kernel(a0_ref, a1_ref, a2_ref, w_hbm, x_ref, gate_ref, fg_ref,
                    ca_ref, cb_ref, cg_ref, cw_ref, cbias_ref, lg_ref, lb_ref, wpw_ref,
                    o_ref, u_ref, h_ref, cv0_ref, cv1_ref, w_ref, stage_ref, sem,
                    *, final, n_tiles, per_batch, seq, layer):
    t = pl.program_id(0)
    j = jnp.minimum(t, n_tiles - 1) % per_batch
    gate_row = pl.ds(jnp.maximum(t - 1, 0) // per_batch, 1)
    layer_row = pl.ds(layer, 1)

    tm = o_ref.shape[0]
    n_chunks = tm // CONV_ROWS
    tn = D_MODEL // n_chunks
    n_slots = stage_ref.shape[0]

    wrows = stage_ref.shape[1]

    def weight_copy(r):
        slot = r % n_slots
        return pltpu.make_async_copy(w_hbm.at[layer, pl.ds(r * wrows, wrows), :], stage_ref.at[slot],
                                     sem.at[slot])

    def run(proj_cv_ref, conv_cv_ref, load_weight=False):
        if load_weight:
            for c in range(n_slots):
                weight_copy(c).start()
        if conv_cv_ref is not None:
            _conv_glu(j, ca_ref, cb_ref, u_ref, seq=seq)
        if proj_cv_ref is not None:
            a = jnp.concatenate([a0_ref[...], a1_ref[...], a2_ref[...], proj_cv_ref[...]], axis=-1)
            ss = jnp.zeros((tm, LANES), F32)
        zero = 0
        for i in range(n_chunks):
            if proj_cv_ref is not None:
                y = jnp.dot(a, w_ref[i + zero], preferred_element_type=F32)
                xn = x_ref[:, i * tn:(i + 1) * tn] + gate_ref[gate_row, i * tn:(i + 1) * tn] * y
                if final:
                    sq = xn * xn
                    for lt in range(tn // LANES):
                        ss = ss + sq[:, lt * LANES:(lt + 1) * LANES]
                o_ref[:, i * tn:(i + 1) * tn] = xn
            if conv_cv_ref is not None:
                tail = _conv_rows(i * CONV_ROWS, cw_ref, cbias_ref.at[layer_row], lg_ref.at[layer_row],
                                  lb_ref.at[layer_row], u_ref, h_ref)
                zero = _runtime_zero(tail)
            if load_weight:
                n_rchunks = D_MODEL // wrows
                per_block = n_rchunks // n_chunks
                for r in range(i * per_block, (i + 1) * per_block):
                    weight_copy(r).wait()
                    for c in range(n_chunks):
                        w_ref[c, r * wrows:(r + 1) * wrows, :] = stage_ref[
                            r % n_slots, :, c * tn:(c + 1) * tn].astype(BF16)
                    if r + n_slots < n_rchunks:
                        weight_copy(r + n_slots).start()
        if conv_cv_ref is not None:
            _conv_pointwise(cg_ref, wpw_ref, h_ref, conv_cv_ref)
        if proj_cv_ref is not None and final:
            ms = jnp.sum(ss, axis=-1, keepdims=True) * (1.0 / D_MODEL)
            o_ref[...] = (o_ref[...] * lax.rsqrt(ms + EPS)) * fg_ref[...]

    odd = t % 2 == 1

    @pl.when(t == 0)
    def _():
        run(None, cv0_ref, load_weight=True)

    @pl.when(odd)
    def _():
        run(cv0_ref, cv1_ref)

    @pl.when(jnp.logical_and(jnp.logical_not(odd), jnp.logical_and(t > 0, t < n_tiles)))
    def _():
        run(cv1_ref, cv0_ref)

    @pl.when(t == n_tiles)
    def _():
        run(cv1_ref, None)


def _outproj(mixed, zp, conv_params, w_pw_bf16, w_out, x, mod, final_g, layer, final):
    batch, seq, _ = x.shape
    tm = OUTPROJ_TM
    per_batch = seq // tm
    n_tiles = batch * per_batch
    n_wchunks = tm // CONV_ROWS
    assert n_tiles % 2 == 0
    conv_tile = lambda t: jnp.minimum(t, n_tiles - 1)
    proj_tile = lambda t: jnp.maximum(t - 1, 0)
    a_spec = pl.BlockSpec((None, tm, D_GROUP),
                          lambda t: (proj_tile(t) // per_batch, proj_tile(t) % per_batch, 0))
    x_spec = pl.BlockSpec((None, tm, D_MODEL),
                          lambda t: (proj_tile(t) // per_batch, proj_tile(t) % per_batch, 0))
    seq_spec = lambda p: pl.BlockSpec((None, None, seq, D_GROUP),
                                      lambda t: (p, conv_tile(t) // per_batch, 0, 0))
    vec = pl.BlockSpec((DEPTH, D_GROUP), lambda t: (0, 0))
    conv_w, conv_b, ln_g, ln_b = conv_params
    return pl.pallas_call(
        functools.partial(_outproj_kernel, final=final, n_tiles=n_tiles, per_batch=per_batch, seq=seq,
                          layer=layer),
        name="outproj",
        grid=(n_tiles + 1,),
        in_specs=[
            a_spec, a_spec, a_spec,
            pl.BlockSpec(memory_space=pl.ANY),
            x_spec,
            pl.BlockSpec((None, ADA_ROWS, D_MODEL), lambda t: (layer, 0, MOD_GATE)),
            pl.BlockSpec((1, D_MODEL), lambda t: (0, 0)),
            seq_spec(P_CA), seq_spec(P_CB),
            pl.BlockSpec((None, None, tm, D_GROUP),
                         lambda t: (P_CG, conv_tile(t) // per_batch, conv_tile(t) % per_batch, 0)),
            pl.BlockSpec((None, CONV_WIDTH, D_GROUP), lambda t: (layer, 0, 0)),
            vec, vec, vec,
            pl.BlockSpec((None, D_GROUP, D_GROUP), lambda t: (layer, 0, 0)),
        ],
        out_specs=x_spec,
        out_shape=jax.ShapeDtypeStruct((batch, seq, D_MODEL), F32),
        scratch_shapes=[
            pltpu.VMEM((tm + 2 * CONV_HALO, D_GROUP), F32),
            pltpu.VMEM((tm, D_GROUP), BF16),
            pltpu.VMEM((tm, D_GROUP), BF16), pltpu.VMEM((tm, D_GROUP), BF16),
            pltpu.VMEM((n_wchunks, D_MODEL, D_MODEL // n_wchunks), BF16),
            pltpu.VMEM((OUTPROJ_WSLOTS, OUTPROJ_WROWS, D_MODEL), F32),
            pltpu.SemaphoreType.DMA((OUTPROJ_WSLOTS,)),
        ],
        compiler_params=_params(("arbitrary",), OUTPROJ_VMEM),
    )(*mixed, w_out, x, mod, final_g.reshape(1, D_MODEL), zp, zp, zp, conv_w,
      conv_b, ln_g, ln_b, w_pw_bf16)


FFT_BLOCK = 32
FFT_LANES = 2 * FFT_GROUP_DIM


def _fft_constants(seq):
    rows = seq // GRID_W
    assert rows == GRID_W
    n = np.arange(GRID_W)
    ang1 = 2.0 * np.pi * ((n[:, None] * n[None, :]) % GRID_W) / GRID_W
    f1 = np.concatenate([np.cos(ang1), -np.sin(ang1)], axis=0)
    k1 = n[:, None, None]
    k2 = n[None, :, None]
    s2 = n[None, None, :]
    ang2 = 2.0 * np.pi * ((s2 * (k1 + GRID_W * k2)) % seq) / seq
    mr, mi = np.cos(ang2), -np.sin(ang2)
    m2 = np.concatenate([np.concatenate([mr, -mi], axis=2),
                         np.concatenate([mi, mr], axis=2)], axis=1)
    c = np.arange(FFT_GROUP_DIM)
    angc = 2.0 * np.pi * ((c[:, None] * c[None, :]) % FFT_GROUP_DIM) / FFT_GROUP_DIM
    fc = np.concatenate([np.cos(angc), np.sin(angc)], axis=0)
    return tuple(jnp.asarray(t, F32).astype(BF16) for t in (f1, m2, fc))


def _fft_kernel(u_ref, gate_ref, f1_ref, m2_ref, fc_ref, w_ref, o_ref, ut_ref, g_ref, gt_ref, y_ref, *, norm):
    pair = pl.program_id(1)
    n = GRID_W
    blk = FFT_BLOCK
    sub = SUBLANES_F32
    lanes = ut_ref.shape[-1]
    groups = lanes // FFT_GROUP_DIM

    u3 = u_ref[...].astype(F32).reshape(n, n, lanes)
    ut_ref[...] = jnp.swapaxes(u3, 0, 1).astype(BF16)

    def stage1(i, carry):
        for j in range(blk):
            s2 = i * blk + j
            g_ref[s2] = jnp.dot(f1_ref[...], ut_ref[s2], preferred_element_type=F32)
        return carry

    lax.fori_loop(0, n // blk, stage1, 0)

    for q in range(2 * n // sub):
        part = g_ref[:, q * sub:(q + 1) * sub, :]
        gt_ref[q * sub:(q + 1) * sub] = jnp.swapaxes(part, 0, 1).astype(BF16)

    def stage2(i, carry):
        xs = []
        for j in range(blk):
            k1 = i * blk + j
            gk = jnp.concatenate([gt_ref[k1], gt_ref[n + k1]], axis=0)
            xs.append(jnp.dot(m2_ref[k1], gk, preferred_element_type=F32))
        for g in range(groups):
            sl = slice(g * FFT_GROUP_DIM, (g + 1) * FFT_GROUP_DIM)
            lhs = jnp.concatenate(
                [jnp.concatenate([xk[:n, sl], xk[n:, sl]], axis=-1) for xk in xs], axis=0)
            y = jnp.dot(lhs.astype(BF16), fc_ref[...], preferred_element_type=F32) * norm
            y_ref[pair * groups + g, pl.ds(pl.multiple_of(i * blk, blk), blk)] = (
                y.reshape(blk, n, FFT_GROUP_DIM))
        return carry

    lax.fori_loop(0, n // blk, stage2, 0)

    @pl.when(pair == pl.num_programs(1) - 1)
    def _():
        def stage3(i, carry):
            k2 = pl.ds(pl.multiple_of(i * sub, sub), sub)
            tiles = []
            for g in range(N_FFT_GROUPS):
                part = y_ref[g, :, k2, :]
                tiles.append(jnp.swapaxes(part, 0, 1).reshape(sub * n, FFT_GROUP_DIM))
            y = jnp.concatenate(tiles, axis=-1).astype(BF16)
            o = jnp.dot(y, w_ref[...], preferred_element_type=F32)
            rows = pl.ds(pl.multiple_of(i * sub * n, sub * n), sub * n)
            o_ref[rows, :] = (o * _silu(gate_ref[rows, :].astype(F32))).astype(BF16)
            return carry

        lax.fori_loop(0, n // sub, stage3, 0, unroll=2)


def _fourier(zp, w_fft_bf16, consts, layer):
    _, batch, seq, _ = zp.shape
    f1, m2, fc = consts
    n = GRID_W
    const = lambda shape: pl.BlockSpec(shape, lambda b, g: (0,) * len(shape))
    return pl.pallas_call(
        functools.partial(_fft_kernel, norm=1.0 / math.sqrt(seq * FFT_GROUP_DIM)),
        name="fourier",
        grid=(batch, D_GROUP // FFT_LANES),
        in_specs=[
            pl.BlockSpec((None, None, seq, FFT_LANES), lambda b, g: (P_FX, b, 0, g)),
            pl.BlockSpec((None, None, seq, D_GROUP), lambda b, g: (P_FG, b, 0, 0)),
            const((2 * n, n)), const((n, 2 * n, 2 * n)), const((2 * FFT_GROUP_DIM, FFT_GROUP_DIM)),
            pl.BlockSpec((None, D_GROUP, D_GROUP), lambda b, g: (layer, 0, 0)),
        ],
        out_specs=pl.BlockSpec((None, seq, D_GROUP), lambda b, g: (b, 0, 0)),
        out_shape=jax.ShapeDtypeStruct((batch, seq, D_GROUP), BF16),
        scratch_shapes=[
            pltpu.VMEM((n, n, FFT_LANES), BF16),
            pltpu.VMEM((n, 2 * n, FFT_LANES), F32),
            pltpu.VMEM((2 * n, n, FFT_LANES), BF16),
            pltpu.VMEM((N_FFT_GROUPS, n, n, FFT_GROUP_DIM), F32),
        ],
        compiler_params=_params(("arbitrary", "arbitrary")),
    )(zp, zp, f1, m2, fc, w_fft_bf16)


NA_ROWS_PER_STEP = 32


def _na_bias_table(rel_bias):
    n_dc = 2 * NA_KW - 1
    col = np.arange(GRID_W)
    col_start = np.clip(col - NA_KW // 2, 0, GRID_W - NA_KW)
    rel_c = col[None, :] - col_start[:, None]
    col_in = (rel_c >= 0) & (rel_c < NA_KW)
    dc = np.clip(col[None, :] - col[:, None] + NA_KW - 1, 0, n_dc - 1)
    onehot = (dc[None] == np.arange(n_dc)[:, None, None]).astype(np.float32)
    zeros = np.zeros_like(onehot)
    onehot2 = np.concatenate([np.concatenate([onehot, zeros], axis=-1),
                              np.concatenate([zeros, onehot], axis=-1)], axis=0)
    rb = rel_bias.astype(F32)
    rb2 = jnp.concatenate([rb[:, :, :-1], rb[:, :, 1:]], axis=-1)
    tbl = jnp.einsum('lhdj,jqk->ldhqk', rb2, jnp.asarray(onehot2),
                     precision=lax.Precision.HIGHEST)
    col_in2 = np.concatenate([col_in, col_in], axis=-1)
    return jnp.where(col_in2[None, None, None], tbl, NEG_BIG)


def _head_select_mask():
    r = lax.broadcasted_iota(jnp.int32, (MXU_DIM, MXU_DIM), 0) // HEAD_DIM
    c = lax.broadcasted_iota(jnp.int32, (MXU_DIM, MXU_DIM), 1) // HEAD_DIM
    return r == c


def _stack_heads(t):
    n = t.shape[0]
    reps = jnp.concatenate([t] * HEADS_PER_TILE, axis=0)
    row_head = lax.broadcasted_iota(jnp.int32, reps.shape, 0) // n
    lane_head = lax.broadcasted_iota(jnp.int32, reps.shape, 1) // HEAD_DIM
    return jnp.where(row_head == lane_head, reps, jnp.zeros_like(reps))


def _unstack_heads(t4, n):
    lane_head = lax.broadcasted_iota(jnp.int32, (n, MXU_DIM), 1) // HEAD_DIM
    out = jnp.zeros((n, MXU_DIM), t4.dtype)
    for h in range(HEADS_PER_TILE):
        out = jnp.where(lane_head == h, t4[h * n:(h + 1) * n], out)
    return out


def _na_kernel(q_ref, k_ref, v_ref, g_ref, bias_ref, o_ref, *, rows):
    kh = min(NA_KH, rows)
    step = pl.program_id(1)

    def row_body(rr, carry):
        r = step * NA_ROWS_PER_STEP + rr
        r_start = jnp.clip(r - kh // 2, 0, rows - kh)
        variant = r - r_start
        q_off = pl.multiple_of(rr * GRID_W, GRID_W)
        k_off = pl.multiple_of(r_start * GRID_W, GRID_W)
        for cg in range(D_GROUP // MXU_DIM):
            lanes = slice(cg * MXU_DIM, (cg + 1) * MXU_DIM)
            q = q_ref[pl.ds(q_off, GRID_W), lanes] * (HEAD_DIM ** -0.5)
            kw = k_ref[pl.ds(k_off, kh * GRID_W), lanes]
            vw = v_ref[pl.ds(k_off, kh * GRID_W), lanes]
            q4 = _stack_heads(q)
            s = lax.dot_general(q4, kw, (((1,), (1,)), ((), ())), preferred_element_type=F32)
            heads = slice(cg * HEADS_PER_TILE, (cg + 1) * HEADS_PER_TILE)
            bias = jnp.concatenate(
                [bias_ref[NA_KH - 1 - variant + a, heads].reshape(HEADS_PER_TILE * GRID_W, 2 * GRID_W)
                 for a in range(0, kh, 2)], axis=-1)
            s = s + bias
            m = jnp.max(s, axis=-1, keepdims=True)
            e = jnp.exp(s - m)
            l = jnp.sum(e, axis=-1, keepdims=True)
            o4 = jnp.dot(e.astype(BF16), vw, preferred_element_type=F32) / l
            o = _unstack_heads(o4, GRID_W)
            gate = g_ref[pl.ds(q_off, GRID_W), lanes].astype(F32)
            o_ref[pl.ds(q_off, GRID_W), lanes] = (o * _silu(gate)).astype(BF16)
        return carry

    lax.fori_loop(0, NA_ROWS_PER_STEP, row_body, 0, unroll=True)


def _neighbourhood(zp, bias_tbl, layer):
    _, batch, seq, _ = zp.shape
    rows = seq // GRID_W
    tq = NA_ROWS_PER_STEP * GRID_W
    q_spec = lambda p: pl.BlockSpec((None, None, tq, D_GROUP), lambda b, i: (p, b, i, 0))
    kv_spec = lambda p: pl.BlockSpec((None, None, seq, D_GROUP), lambda b, i: (p, b, 0, 0))
    return pl.pallas_call(
        functools.partial(_na_kernel, rows=rows),
        name="natten",
        grid=(batch, rows // NA_ROWS_PER_STEP),
        in_specs=[
            q_spec(P_NQ), kv_spec(P_NK), kv_spec(P_NV), q_spec(P_NG),
            pl.BlockSpec((None,) + bias_tbl.shape[1:], lambda b, i: (layer, 0, 0, 0, 0),
                         pipeline_mode=pl.Buffered(1)),
        ],
        out_specs=pl.BlockSpec((None, tq, D_GROUP), lambda b, i: (b, i, 0)),
        out_shape=jax.ShapeDtypeStruct((batch, seq, D_GROUP), BF16),
        compiler_params=_params(("arbitrary", "arbitrary")),
    )(zp, zp, zp, zp, bias_tbl)


RET_CHUNK = MXU_DIM


def _log_sigmoid(t):
    return jnp.minimum(t, 0.0) - jnp.log1p(jnp.exp(-jnp.abs(t)))


def _rope_tables(seq):
    half = HEAD_DIM // 2
    inv = ROPE_BASE ** (-jnp.arange(half, dtype=F32) / half)
    ang = jnp.arange(seq, dtype=F32)[:, None] * inv[None, :]
    cos, sin = jnp.cos(ang), jnp.sin(ang)
    cos2 = jnp.concatenate([cos, cos], axis=-1)
    sin2 = jnp.concatenate([-sin, sin], axis=-1)
    return jnp.tile(cos2, (1, 2)), jnp.tile(sin2, (1, 2))


def _ret_kernel(lf_s_ref, lb_s_ref, q_ref, k_ref, v_ref, g_ref, cos_ref, sin_ref,
                o_ref, qr_ref, kr_ref, ob_ref, dmat_ref, dec_ref, state_ref, ones_ref, *, seq, layer):
    C = RET_CHUNK
    n_chunks = seq // C
    cg = pl.program_id(1)
    ones_bd = jnp.where(_head_select_mask(), 1.0, 0.0).astype(BF16)
    ones_ref[...] = jnp.concatenate([ones_bd, ones_bd], axis=0)

    ri = lax.broadcasted_iota(jnp.int32, (C, C), 0)
    ci = lax.broadcasted_iota(jnp.int32, (C, C), 1)
    diff = (ri - ci).astype(F32)
    lane_head = lax.broadcasted_iota(jnp.int32, (1, MXU_DIM), 1) // HEAD_DIM
    lfl = jnp.zeros((1, MXU_DIM), F32)
    lbl = jnp.zeros((1, MXU_DIM), F32)
    for hh in range(HEADS_PER_TILE):
        h = cg * HEADS_PER_TILE + hh
        lf = _log_sigmoid(jnp.full((C, C), lf_s_ref[layer, h], F32))
        lb = _log_sigmoid(jnp.full((C, C), lb_s_ref[layer, h], F32))
        dmat_ref[hh] = jnp.where(diff >= 0, jnp.exp(lf * diff), jnp.exp(lb * (-diff)))
        lfl = jnp.where(lane_head == hh, lf_s_ref[layer, h], lfl)
        lbl = jnp.where(lane_head == hh, lb_s_ref[layer, h], lbl)
    lfl = _log_sigmoid(lfl)
    lbl = _log_sigmoid(lbl)
    idx = lax.broadcasted_iota(jnp.int32, (C, MXU_DIM), 0).astype(F32)
    dec_ref[0] = jnp.exp(lfl * (idx + 1.0))
    dec_ref[1] = jnp.exp(lfl * (C - 1.0 - idx))
    dec_ref[2] = jnp.exp(lbl * (C - idx))
    dec_ref[3] = jnp.exp(lbl * idx)
    cd_f = jnp.exp(lfl * float(C))
    cd_b = jnp.exp(lbl * float(C))

    def rope(t, rows):
        lane = lax.broadcasted_iota(jnp.int32, t.shape, 1)
        first_half = (lane % HEAD_DIM) < (HEAD_DIM // 2)
        cs = cos_ref[rows, :]
        sn = sin_ref[rows, :]
        cs = jnp.concatenate([cs, cs], axis=-1)
        sn = jnp.concatenate([sn, sn], axis=-1)
        swapped = jnp.where(first_half, pltpu.roll(t, MXU_DIM - HEAD_DIM // 2, 1),
                            pltpu.roll(t, HEAD_DIM // 2, 1))
        return t * cs + swapped * sn

    def kv_update(k, v, kd, cd):
        kv = lax.dot_general((k * kd).astype(BF16), v, (((0,), (0,)), ((), ())),
                             preferred_element_type=F32)
        state_ref[...] = cd * state_ref[...] + jnp.where(_head_select_mask(), kv, 0.0)

    state_ref[...] = jnp.zeros_like(state_ref)

    def bwd_body(i, carry):
        n = n_chunks - 1 - i
        rows = pl.ds(pl.multiple_of(n * C, C), C)
        q = rope(q_ref[rows, :].astype(F32), rows) * (HEAD_DIM ** -0.5)
        k = rope(k_ref[rows, :].astype(F32), rows)
        qr_ref[rows, :] = q.astype(BF16)
        kr_ref[rows, :] = k.astype(BF16)
        ob_ref[rows, :] = jnp.dot((q * dec_ref[2]).astype(BF16), state_ref[...].astype(BF16),
                                  preferred_element_type=F32)
        kv_update(k, v_ref[rows, :], dec_ref[3], cd_b)
        return carry

    lax.fori_loop(0, n_chunks, bwd_body, 0, unroll=8)

    state_ref[...] = jnp.zeros_like(state_ref)

    def fwd_body(n, carry):
        rows = pl.ds(pl.multiple_of(n * C, C), C)
        qb = qr_ref[rows, :]
        kb = kr_ref[rows, :]
        v = v_ref[rows, :]
        o_cross = jnp.dot((qb.astype(F32) * dec_ref[0]).astype(BF16), state_ref[...].astype(BF16),
                          preferred_element_type=F32)
        q4 = _stack_heads(qb)
        sc = lax.dot_general(q4, kb, (((1,), (1,)), ((), ())), preferred_element_type=F32)
        sc = sc * dmat_ref[...].reshape(HEADS_PER_TILE * C, C)
        o4 = jnp.dot(sc.astype(BF16), v, preferred_element_type=F32)
        o = _unstack_heads(o4, C) + o_cross + ob_ref[rows, :]
        sq = o * o
        hi = sq.astype(BF16)
        lo = (sq - hi.astype(F32)).astype(BF16)
        ss = jnp.dot(jnp.concatenate([hi, lo], axis=-1), ones_ref[...], preferred_element_type=F32)
        inv = lax.rsqrt(ss * (1.0 / HEAD_DIM) + EPS)
        gate = g_ref[rows, :].astype(F32)
        o_ref[rows, :] = (o * inv * _silu(gate)).astype(BF16)
        kv_update(kb.astype(F32), v, dec_ref[1], cd_f)
        return carry

    lax.fori_loop(0, n_chunks, fwd_body, 0, unroll=8)


def _retention(zp, logit_f, logit_b, rope_tbl, layer):
    _, batch, seq, _ = zp.shape
    cos_t, sin_t = rope_tbl
    n_cg = D_GROUP // MXU_DIM
    spec = lambda p: pl.BlockSpec((None, None, seq, MXU_DIM), lambda b, c, *_: (p, b, 0, c))
    tbl_spec = pl.BlockSpec((seq, 2 * HEAD_DIM), lambda b, c, *_: (0, 0))
    return pl.pallas_call(
        functools.partial(_ret_kernel, seq=seq, layer=layer),
        name="retention",
        grid_spec=pltpu.PrefetchScalarGridSpec(
            num_scalar_prefetch=2,
            grid=(batch, n_cg),
            in_specs=[spec(P_RQ), spec(P_RK), spec(P_RV), spec(P_RG), tbl_spec, tbl_spec],
            out_specs=pl.BlockSpec((None, seq, MXU_DIM), lambda b, c, *_: (b, 0, c)),
            scratch_shapes=[
                pltpu.VMEM((seq, MXU_DIM), BF16),
                pltpu.VMEM((seq, MXU_DIM), BF16),
                pltpu.VMEM((seq, MXU_DIM), F32),
                pltpu.VMEM((HEADS_PER_TILE, RET_CHUNK, RET_CHUNK), F32),
                pltpu.VMEM((4, RET_CHUNK, MXU_DIM), F32),
                pltpu.VMEM((MXU_DIM, MXU_DIM), F32),
                pltpu.VMEM((2 * MXU_DIM, MXU_DIM), BF16),
            ],
        ),
        out_shape=jax.ShapeDtypeStruct((batch, seq, D_GROUP), BF16),
        compiler_params=_params(("arbitrary", "arbitrary")),
    )(logit_f, logit_b, zp, zp, zp, zp, cos_t, sin_t)


CONV_HALO = SUBLANES_BF16
CONV_ROWS = 128


def _conv_glu(j, a_ref, b_ref, u_ref, *, seq):
    halo = CONV_HALO
    ts = u_ref.shape[0] - 2 * halo
    n_tiles = seq // ts

    def glu(start, size):
        rows = pl.ds(pl.multiple_of(start, halo), size)
        a = a_ref[rows, :].astype(F32)
        return a * _sigmoid(b_ref[rows, :].astype(F32))

    base = j * ts
    u_ref[halo:halo + ts, :] = glu(base, ts)
    lo = glu(jnp.maximum(base - halo, 0), halo)
    u_ref[0:halo, :] = jnp.where(j > 0, lo, 0.0)
    hi = glu(jnp.minimum(base + ts, seq - halo), halo)
    u_ref[halo + ts:, :] = jnp.where(j < n_tiles - 1, hi, 0.0)


def _conv_rows(s0, cw_ref, cb_ref, lg_ref, lb_ref, u_ref, h_ref):
    rb = CONV_ROWS
    first = CONV_HALO - CONV_HALF
    parts = []
    for lt in range(D_GROUP // LANES):
        lanes = slice(lt * LANES, (lt + 1) * LANES)
        y = None
        for r in range(SUBLANES_F32):
            acc = None
            for o in range(r, first + CONV_WIDTH, SUBLANES_F32):
                if o < first:
                    continue
                w = o - first
                lo_row = s0 + (o - r)
                term = u_ref[lo_row:lo_row + rb + SUBLANES_F32, lanes] * cw_ref[w:w + 1, lanes]
                acc = term if acc is None else acc + term
            shifted = acc[r:r + rb]
            y = shifted if y is None else y + shifted
        parts.append(y)
    y = jnp.concatenate(parts, axis=-1) + cb_ref[...]
    mu = jnp.mean(y, axis=-1, keepdims=True)
    yc = y - mu
    var = jnp.mean(yc * yc, axis=-1, keepdims=True)
    y = (yc * lax.rsqrt(var + EPS)) * lg_ref[...] + lb_ref[...]
    h_ref[s0:s0 + rb, :] = _silu(y).astype(BF16)
    return y[rb - SUBLANES_F32:, D_GROUP - LANES:]


def _runtime_zero(tile):
    bits = pltpu.bitcast(tile, jnp.uint32)
    bits = lax.shift_right_logical(lax.shift_right_logical(bits, jnp.uint32(16)), jnp.uint32(16))
    return bits[0, 0].astype(jnp.int32)


def _conv_pointwise(g_ref, w_ref, h_ref, o_ref):
    o = jnp.dot(h_ref[...], w_ref[...], preferred_element_type=F32)
    o_ref[...] = (o * _silu(g_ref[...].astype(F32))).astype(BF16)


def kernel(x, c, norm_g, w_ada, b_ada, w_in, w_fft, na_rel_bias, ret_logit_fwd, ret_logit_bwd,
           conv_w, conv_b, conv_ln_g, conv_ln_b, conv_w_pw, w_out, final_g):
    seq = x.shape[1]
    mod = _ada(c, w_ada, b_ada)
    fft_consts = _fft_constants(seq)
    rope_tbl = _rope_tables(seq)
    bias_tbl = _na_bias_table(na_rel_bias)
    w_fft_b, w_pw_b = w_fft.astype(BF16), conv_w_pw.astype(BF16)
    for l in range(DEPTH):
        zp = _inproj(x, norm_g, mod, w_in, l)
        o_fft = _fourier(zp, w_fft_b, fft_consts, l)
        o_na = _neighbourhood(zp, bias_tbl, l)
        o_ret = _retention(zp, ret_logit_fwd, ret_logit_bwd, rope_tbl, l)
        conv_params = (conv_w, conv_b, conv_ln_g, conv_ln_b)
        x = _outproj((o_fft, o_na, o_ret), zp, conv_params, w_pw_b, w_out, x, mod, final_g, l,
                     final=(l == DEPTH - 1))
    return x
```

```python
import functools
import math

import numpy as np
import jax
import jax.numpy as jnp
from jax import lax
from jax.experimental import pallas as pl
from jax.experimental.pallas import tpu as pltpu

F32 = jnp.float32
BF16 = jnp.bfloat16

D_MODEL = 2048
DEPTH = 2
GRID_W = 64
D_GROUP = 512
HEAD_DIM = 64
N_HEADS = D_GROUP // HEAD_DIM
N_FFT_GROUPS = 4
FFT_GROUP_DIM = D_GROUP // N_FFT_GROUPS
NA_KH = 8
NA_KW = 16
CONV_WIDTH = 31
CONV_HALF = CONV_WIDTH // 2
ROPE_BASE = 10000.0
EPS = 1e-6
N_PIECES = 13
(P_FX, P_FG, P_NQ, P_NK, P_NV, P_NG, P_RQ, P_RK, P_RV, P_RG, P_CA, P_CB, P_CG) = range(N_PIECES)

MXU_DIM = 256
HEADS_PER_TILE = MXU_DIM // HEAD_DIM
VMEM_LIMIT = 56 * 1024 * 1024
SUBLANES_BF16 = 16
SUBLANES_F32 = 8
LANES = 128

NEG_BIG = -1e30


def _params(sem, vmem=VMEM_LIMIT):
    return pltpu.CompilerParams(dimension_semantics=sem, vmem_limit_bytes=vmem)


def _sigmoid(t):
    return 0.5 * jnp.tanh(0.5 * t) + 0.5


def _silu(t):
    h = 0.5 * t
    return h + h * jnp.tanh(h)


ADA_TN = 768
ADA_ROWS = 8
(MOD_SHIFT, MOD_SCALE, MOD_GATE) = range(3)


def _ada_kernel(c_ref, w_ref, b_ref, o_ref):
    c = c_ref[...]
    pad = jnp.zeros((ADA_ROWS - c.shape[0], D_MODEL), F32)
    ca = _silu(jnp.concatenate([c, pad], axis=0)).astype(BF16)
    w = w_ref[...].astype(BF16)
    bias = b_ref[pl.ds(pl.program_id(0), 1), :]
    o_ref[...] = jnp.dot(ca, w, preferred_element_type=F32) + bias


def _ada(c, w_ada, b_ada):
    batch = c.shape[0]
    n3 = 3 * D_MODEL
    return pl.pallas_call(
        _ada_kernel,
        name="ada",
        grid=(DEPTH, n3 // ADA_TN),
        in_specs=[
            pl.BlockSpec((batch, D_MODEL), lambda l, j: (0, 0)),
            pl.BlockSpec((None, D_MODEL, ADA_TN), lambda l, j: (l, 0, j)),
            pl.BlockSpec((DEPTH, ADA_TN), lambda l, j: (0, j)),
        ],
        out_specs=pl.BlockSpec((None, ADA_ROWS, ADA_TN), lambda l, j: (l, 0, j)),
        out_shape=jax.ShapeDtypeStruct((DEPTH, ADA_ROWS, n3), F32),
        compiler_params=_params(("arbitrary", "arbitrary")),
    )(c, w_ada, b_ada)


INPROJ_TM = 512
INPROJ_WROWS = SUBLANES_BF16
INPROJ_WSLOTS = 8
INPROJ_VMEM = 60 * 1024 * 1024


def _inproj_kernel(x_ref, g_ref, sc_ref, sh_ref, w_hbm, o_ref, h0_ref, h1_ref, w_ref, stage_ref, sem,
                   *, n_tiles, per_batch, layer):
    t = pl.program_id(0)
    wrows = stage_ref.shape[1]
    n_rchunks = D_MODEL // wrows

    def weight_copy(r, slot):
        rows = pl.ds(pl.multiple_of(r * wrows, wrows), wrows)
        return pltpu.make_async_copy(w_hbm.at[layer, rows, :], stage_ref.at[slot], sem.at[slot])

    def load_weight():
        n_slots = stage_ref.shape[0]
        for slot in range(n_slots):
            weight_copy(slot, slot).start()

        def group(i, carry):
            for slot in range(n_slots):
                r = n_slots * i + slot
                weight_copy(r, slot).wait()
                rows = pl.ds(pl.multiple_of(r * wrows, wrows), wrows)
                w_ref[rows, :] = stage_ref[slot].astype(BF16)

                @pl.when(r + n_slots < n_rchunks)
                def _():
                    weight_copy(r + n_slots, slot).start()
            return carry

        lax.fori_loop(0, n_rchunks // n_slots, group, 0)

    def norm(h_ref):
        x = x_ref[...]
        row = pl.ds(jnp.minimum(t, n_tiles - 1) // per_batch, 1)
        ms = jnp.mean(x * x, axis=-1, keepdims=True)
        gain = g_ref[layer:layer + 1, :] * (1.0 + sc_ref[row, :])
        h = (x * lax.rsqrt(ms + EPS)) * gain + sh_ref[row, :]
        h_ref[...] = h.astype(BF16)

    def project(h_ref):
        h = h_ref[...]
        for p in range(N_PIECES):
            w = w_ref[:, p * D_GROUP:(p + 1) * D_GROUP]
            o_ref[p] = jnp.dot(h, w, preferred_element_type=F32).astype(BF16)

    odd = t % 2 == 1

    @pl.when(t == 0)
    def _():
        load_weight()
        norm(h0_ref)

    @pl.when(odd)
    def _():
        project(h0_ref)
        norm(h1_ref)

    @pl.when(jnp.logical_and(jnp.logical_not(odd), jnp.logical_and(t > 0, t < n_tiles)))
    def _():
        project(h1_ref)
        norm(h0_ref)

    @pl.when(t == n_tiles)
    def _():
        project(h1_ref)


def _inproj(x, norm_g, mod, w_in, layer):
    batch, seq, _ = x.shape
    tm = INPROJ_TM
    per_batch = seq // tm
    n_tiles = batch * per_batch
    assert n_tiles % 2 == 0 and (D_MODEL // INPROJ_WROWS) % INPROJ_WSLOTS == 0
    norm_tile = lambda t: jnp.minimum(t, n_tiles - 1)
    proj_tile = lambda t: jnp.maximum(t - 1, 0)
    return pl.pallas_call(
        functools.partial(_inproj_kernel, n_tiles=n_tiles, per_batch=per_batch, layer=layer),
        name="inproj",
        grid=(n_tiles + 1,),
        in_specs=[
            pl.BlockSpec((None, tm, D_MODEL),
                         lambda t: (norm_tile(t) // per_batch, norm_tile(t) % per_batch, 0)),
            pl.BlockSpec((DEPTH, D_MODEL), lambda t: (0, 0)),
            pl.BlockSpec((None, ADA_ROWS, D_MODEL), lambda t: (layer, 0, MOD_SCALE)),
            pl.BlockSpec((None, ADA_ROWS, D_MODEL), lambda t: (layer, 0, MOD_SHIFT)),
            pl.BlockSpec(memory_space=pl.ANY),
        ],
        out_specs=pl.BlockSpec((N_PIECES, None, tm, D_GROUP),
                               lambda t: (0, proj_tile(t) // per_batch, proj_tile(t) % per_batch, 0)),
        out_shape=jax.ShapeDtypeStruct((N_PIECES, batch, seq, D_GROUP), BF16),
        scratch_shapes=[
            pltpu.VMEM((tm, D_MODEL), BF16), pltpu.VMEM((tm, D_MODEL), BF16),
            pltpu.VMEM((D_MODEL, N_PIECES * D_GROUP), BF16),
            pltpu.VMEM((INPROJ_WSLOTS, INPROJ_WROWS, N_PIECES * D_GROUP), F32),
            pltpu.SemaphoreType.DMA((INPROJ_WSLOTS,)),
        ],
        compiler_params=_params(("arbitrary",), INPROJ_VMEM),
    )(x, norm_g, mod, mod, w_in)


OUTPROJ_TM = 512
OUTPROJ_WROWS = 64
OUTPROJ_WSLOTS = 8
OUTPROJ_VMEM = 60 * 1024 * 1024


def _outproj_kernel(a0_ref, a1_ref, a2_ref, w_hbm, x_ref, gate_ref, fg_ref,
                    ca_ref, cb_ref, cg_ref, cw_ref, cbias_ref, lg_ref, lb_ref, wpw_ref,
                    o_ref, u_ref, h_ref, cv0_ref, cv1_ref, w_ref, stage_ref, sem,
                    *, final, n_tiles, per_batch, seq, layer):
    t = pl.program_id(0)
    j = jnp.minimum(t, n_tiles - 1) % per_batch
    gate_row = pl.ds(jnp.maximum(t - 1, 0) // per_batch, 1)
    layer_row = pl.ds(layer, 1)

    tm = o_ref.shape[0]
    n_chunks = tm // CONV_ROWS
    tn = D_MODEL // n_chunks
    n_slots = stage_ref.shape[0]

    wrows = stage_ref.shape[1]

    def weight_copy(r):
        slot = r % n_slots
        return pltpu.make_async_copy(w_hbm.at[layer, pl.ds(r * wrows, wrows), :], stage_ref.at[slot],
                                     sem.at[slot])

    def run(proj_cv_ref, conv_cv_ref, load_weight=False):
        if load_weight:
            for c in range(n_slots):
                weight_copy(c).start()
        if conv_cv_ref is not None:
            _conv_glu(j, ca_ref, cb_ref, u_ref, seq=seq)
        if proj_cv_ref is not None:
            a = jnp.concatenate([a0_ref[...], a1_ref[...], a2_ref[...], proj_cv_ref[...]], axis=-1)
            ss = jnp.zeros((tm, LANES), F32)
        zero = 0
        for i in range(n_chunks):
            if proj_cv_ref is not None:
                y = jnp.dot(a, w_ref[i + zero], preferred_element_type=F32)
                xn = x_ref[:, i * tn:(i + 1) * tn] + gate_ref[gate_row, i * tn:(i + 1) * tn] * y
                if final:
                    sq = xn * xn
                    for lt in range(tn // LANES):
                        ss = ss + sq[:, lt * LANES:(lt + 1) * LANES]
                o_ref[:, i * tn:(i + 1) * tn] = xn
            if conv_cv_ref is not None:
                tail = _conv_rows(i * CONV_ROWS, cw_ref, cbias_ref.at[layer_row], lg_ref.at[layer_row],
                                  lb_ref.at[layer_row], u_ref, h_ref)
                zero = _runtime_zero(tail)
            if load_weight:
                n_rchunks = D_MODEL // wrows
                per_block = n_rchunks // n_chunks
                for r in range(i * per_block, (i + 1) * per_block):
                    weight_copy(r).wait()
                    for c in range(n_chunks):
                        w_ref[c, r * wrows:(r + 1) * wrows, :] = stage_ref[
                            r % n_slots, :, c * tn:(c + 1) * tn].astype(BF16)
                    if r + n_slots < n_rchunks:
                        weight_copy(r + n_slots).start()
        if conv_cv_ref is not None:
            _conv_pointwise(cg_ref, wpw_ref, h_ref, conv_cv_ref)
        if proj_cv_ref is not None and final:
            ms = jnp.sum(ss, axis=-1, keepdims=True) * (1.0 / D_MODEL)
            o_ref[...] = (o_ref[...] * lax.rsqrt(ms + EPS)) * fg_ref[...]

    odd = t % 2 == 1

    @pl.when(t == 0)
    def _():
        run(None, cv0_ref, load_weight=True)

    @pl.when(odd)
    def _():
        run(cv0_ref, cv1_ref)

    @pl.when(jnp.logical_and(jnp.logical_not(odd), jnp.logical_and(t > 0, t < n_tiles)))
    def _():
        run(cv1_ref, cv0_ref)

    @pl.when(t == n_tiles)
    def _():
        run(cv1_ref, None)


def _outproj(mixed, zp, conv_params, w_pw_bf16, w_out, x, mod, final_g, layer, final):
    batch, seq, _ = x.shape
    tm = OUTPROJ_TM
    per_batch = seq // tm
    n_tiles = batch * per_batch
    n_wchunks = tm // CONV_ROWS
    assert n_tiles % 2 == 0
    conv_tile = lambda t: jnp.minimum(t, n_tiles - 1)
    proj_tile = lambda t: jnp.maximum(t - 1, 0)
    a_spec = pl.BlockSpec((None, tm, D_GROUP),
                          lambda t: (proj_tile(t) // per_batch, proj_tile(t) % per_batch, 0))
    x_spec = pl.BlockSpec((None, tm, D_MODEL),
                          lambda t: (proj_tile(t) // per_batch, proj_tile(t) % per_batch, 0))
    seq_spec = lambda p: pl.BlockSpec((None, None, seq, D_GROUP),
                                      lambda t: (p, conv_tile(t) // per_batch, 0, 0))
    vec = pl.BlockSpec((DEPTH, D_GROUP), lambda t: (0, 0))
    conv_w, conv_b, ln_g, ln_b = conv_params
    return pl.pallas_call(
        functools.partial(_outproj_kernel, final=final, n_tiles=n_tiles, per_batch=per_batch, seq=seq,
                          layer=layer),
        name="outproj",
        grid=(n_tiles + 1,),
        in_specs=[
            a_spec, a_spec, a_spec,
            pl.BlockSpec(memory_space=pl.ANY),
            x_spec,
            pl.BlockSpec((None, ADA_ROWS, D_MODEL), lambda t: (layer, 0, MOD_GATE)),
            pl.BlockSpec((1, D_MODEL), lambda t: (0, 0)),
            seq_spec(P_CA), seq_spec(P_CB),
            pl.BlockSpec((None, None, tm, D_GROUP),
                         lambda t: (P_CG, conv_tile(t) // per_batch, conv_tile(t) % per_batch, 0)),
            pl.BlockSpec((None, CONV_WIDTH, D_GROUP), lambda t: (layer, 0, 0)),
            vec, vec, vec,
            pl.BlockSpec((None, D_GROUP, D_GROUP), lambda t: (layer, 0, 0)),
        ],
        out_specs=x_spec,
        out_shape=jax.ShapeDtypeStruct((batch, seq, D_MODEL), F32),
        scratch_shapes=[
            pltpu.VMEM((tm + 2 * CONV_HALO, D_GROUP), F32),
            pltpu.VMEM((tm, D_GROUP), BF16),
            pltpu.VMEM((tm, D_GROUP), BF16), pltpu.VMEM((tm, D_GROUP), BF16),
            pltpu.VMEM((n_wchunks, D_MODEL, D_MODEL // n_wchunks), BF16),
            pltpu.VMEM((OUTPROJ_WSLOTS, OUTPROJ_WROWS, D_MODEL), F32),
            pltpu.SemaphoreType.DMA((OUTPROJ_WSLOTS,)),
        ],
        compiler_params=_params(("arbitrary",), OUTPROJ_VMEM),
    )(*mixed, w_out, x, mod, final_g.reshape(1, D_MODEL), zp, zp, zp, conv_w,
      conv_b, ln_g, ln_b, w_pw_bf16)


FFT_BLOCK = 32
FFT_LANES = 2 * FFT_GROUP_DIM


def _fft_constants(seq):
    rows = seq // GRID_W
    assert rows == GRID_W
    n = np.arange(GRID_W)
    ang1 = 2.0 * np.pi * ((n[:, None] * n[None, :]) % GRID_W) / GRID_W
    f1 = np.concatenate([np.cos(ang1), -np.sin(ang1)], axis=0)
    k1 = n[:, None, None]
    k2 = n[None, :, None]
    s2 = n[None, None, :]
    ang2 = 2.0 * np.pi * ((s2 * (k1 + GRID_W * k2)) % seq) / seq
    mr, mi = np.cos(ang2), -np.sin(ang2)
    m2 = np.concatenate([np.concatenate([mr, -mi], axis=2),
                         np.concatenate([mi, mr], axis=2)], axis=1)
    c = np.arange(FFT_GROUP_DIM)
    angc = 2.0 * np.pi * ((c[:, None] * c[None, :]) % FFT_GROUP_DIM) / FFT_GROUP_DIM
    fc = np.concatenate([np.cos(angc), np.sin(angc)], axis=0)
    return tuple(jnp.asarray(t, F32).astype(BF16) for t in (f1, m2, fc))


def _fft_kernel(u_ref, gate_ref, f1_ref, m2_ref, fc_ref, w_ref, o_ref, ut_ref, g_ref, gt_ref, y_ref, *, norm):
    pair = pl.program_id(1)
    n = GRID_W
    blk = FFT_BLOCK
    sub = SUBLANES_F32
    lanes = ut_ref.shape[-1]
    groups = lanes // FFT_GROUP_DIM

    u3 = u_ref[...].astype(F32).reshape(n, n, lanes)
    ut_ref[...] = jnp.swapaxes(u3, 0, 1).astype(BF16)

    def stage1(i, carry):
        for j in range(blk):
            s2 = i * blk + j
            g_ref[s2] = jnp.dot(f1_ref[...], ut_ref[s2], preferred_element_type=F32)
        return carry

    lax.fori_loop(0, n // blk, stage1, 0)

    for q in range(2 * n // sub):
        part = g_ref[:, q * sub:(q + 1) * sub, :]
        gt_ref[q * sub:(q + 1) * sub] = jnp.swapaxes(part, 0, 1).astype(BF16)

    def stage2(i, carry):
        xs = []
        for j in range(blk):
            k1 = i * blk + j
            gk = jnp.concatenate([gt_ref[k1], gt_ref[n + k1]], axis=0)
            xs.append(jnp.dot(m2_ref[k1], gk, preferred_element_type=F32))
        for g in range(groups):
            sl = slice(g * FFT_GROUP_DIM, (g + 1) * FFT_GROUP_DIM)
            lhs = jnp.concatenate(
                [jnp.concatenate([xk[:n, sl], xk[n:, sl]], axis=-1) for xk in xs], axis=0)
            y = jnp.dot(lhs.astype(BF16), fc_ref[...], preferred_element_type=F32) * norm
            y_ref[pair * groups + g, pl.ds(pl.multiple_of(i * blk, blk), blk)] = (
                y.reshape(blk, n, FFT_GROUP_DIM))
        return carry

    lax.fori_loop(0, n // blk, stage2, 0)

    @pl.when(pair == pl.num_programs(1) - 1)
    def _():
        def stage3(i, carry):
            k2 = pl.ds(pl.multiple_of(i * sub, sub), sub)
            tiles = []
            for g in range(N_FFT_GROUPS):
                part = y_ref[g, :, k2, :]
                tiles.append(jnp.swapaxes(part, 0, 1).reshape(sub * n, FFT_GROUP_DIM))
            y = jnp.concatenate(tiles, axis=-1).astype(BF16)
            o = jnp.dot(y, w_ref[...], preferred_element_type=F32)
            rows = pl.ds(pl.multiple_of(i * sub * n, sub * n), sub * n)
            o_ref[rows, :] = (o * _silu(gate_ref[rows, :].astype(F32))).astype(BF16)
            return carry

        lax.fori_loop(0, n // sub, stage3, 0, unroll=2)


def _fourier(zp, w_fft_bf16, consts, layer):
    _, batch, seq, _ = zp.shape
    f1, m2, fc = consts
    n = GRID_W
    const = lambda shape: pl.BlockSpec(shape, lambda b, g: (0,) * len(shape))
    return pl.pallas_call(
        functools.partial(_fft_kernel, norm=1.0 / math.sqrt(seq * FFT_GROUP_DIM)),
        name="fourier",
        grid=(batch, D_GROUP // FFT_LANES),
        in_specs=[
            pl.BlockSpec((None, None, seq, FFT_LANES), lambda b, g: (P_FX, b, 0, g)),
            pl.BlockSpec((None, None, seq, D_GROUP), lambda b, g: (P_FG, b, 0, 0)),
            const((2 * n, n)), const((n, 2 * n, 2 * n)), const((2 * FFT_GROUP_DIM, FFT_GROUP_DIM)),
            pl.BlockSpec((None, D_GROUP, D_GROUP), lambda b, g: (layer, 0, 0)),
        ],
        out_specs=pl.BlockSpec((None, seq, D_GROUP), lambda b, g: (b, 0, 0)),
        out_shape=jax.ShapeDtypeStruct((batch, seq, D_GROUP), BF16),
        scratch_shapes=[
            pltpu.VMEM((n, n, FFT_LANES), BF16),
            pltpu.VMEM((n, 2 * n, FFT_LANES), F32),
            pltpu.VMEM((2 * n, n, FFT_LANES), BF16),
            pltpu.VMEM((N_FFT_GROUPS, n, n, FFT_GROUP_DIM), F32),
        ],
        compiler_params=_params(("arbitrary", "arbitrary")),
    )(zp, zp, f1, m2, fc, w_fft_bf16)


NA_ROWS_PER_STEP = 16


def _na_bias_table(rel_bias):
    n_dc = 2 * NA_KW - 1
    col = np.arange(GRID_W)
    col_start = np.clip(col - NA_KW // 2, 0, GRID_W - NA_KW)
    rel_c = col[None, :] - col_start[:, None]
    col_in = (rel_c >= 0) & (rel_c < NA_KW)
    dc = np.clip(col[None, :] - col[:, None] + NA_KW - 1, 0, n_dc - 1)
    onehot = (dc[None] == np.arange(n_dc)[:, None, None]).astype(np.float32)
    zeros = np.zeros_like(onehot)
    onehot2 = np.concatenate([np.concatenate([onehot, zeros], axis=-1),
                              np.concatenate([zeros, onehot], axis=-1)], axis=0)
    rb = rel_bias.astype(F32)
    rb2 = jnp.concatenate([rb[:, :, :-1], rb[:, :, 1:]], axis=-1)
    tbl = jnp.einsum('lhdj,jqk->ldhqk', rb2, jnp.asarray(onehot2),
                     precision=lax.Precision.HIGHEST)
    col_in2 = np.concatenate([col_in, col_in], axis=-1)
    return jnp.where(col_in2[None, None, None], tbl, NEG_BIG)


def _head_select_mask():
    r = lax.broadcasted_iota(jnp.int32, (MXU_DIM, MXU_DIM), 0) // HEAD_DIM
    c = lax.broadcasted_iota(jnp.int32, (MXU_DIM, MXU_DIM), 1) // HEAD_DIM
    return r == c


def _stack_heads(t):
    n = t.shape[0]
    reps = jnp.concatenate([t] * HEADS_PER_TILE, axis=0)
    row_head = lax.broadcasted_iota(jnp.int32, reps.shape, 0) // n
    lane_head = lax.broadcasted_iota(jnp.int32, reps.shape, 1) // HEAD_DIM
    return jnp.where(row_head == lane_head, reps, jnp.zeros_like(reps))


def _unstack_heads(t4, n):
    lane_head = lax.broadcasted_iota(jnp.int32, (n, MXU_DIM), 1) // HEAD_DIM
    out = jnp.zeros((n, MXU_DIM), t4.dtype)
    for h in range(HEADS_PER_TILE):
        out = jnp.where(lane_head == h, t4[h * n:(h + 1) * n], out)
    return out


def _na_kernel(q_ref, k_ref, v_ref, g_ref, bias_ref, o_ref, *, rows):
    kh = min(NA_KH, rows)
    step = pl.program_id(1)

    def row_body(rr, carry):
        r = step * NA_ROWS_PER_STEP + rr
        r_start = jnp.clip(r - kh // 2, 0, rows - kh)
        variant = r - r_start
        q_off = pl.multiple_of(rr * GRID_W, GRID_W)
        k_off = pl.multiple_of(r_start * GRID_W, GRID_W)
        for cg in range(D_GROUP // MXU_DIM):
            lanes = slice(cg * MXU_DIM, (cg + 1) * MXU_DIM)
            q = q_ref[pl.ds(q_off, GRID_W), lanes] * (HEAD_DIM ** -0.5)
            kw = k_ref[pl.ds(k_off, kh * GRID_W), lanes]
            vw = v_ref[pl.ds(k_off, kh * GRID_W), lanes]
            q4 = _stack_heads(q)
            s = lax.dot_general(q4, kw, (((1,), (1,)), ((), ())), preferred_element_type=F32)
            heads = slice(cg * HEADS_PER_TILE, (cg + 1) * HEADS_PER_TILE)
            bias = jnp.concatenate(
                [bias_ref[NA_KH - 1 - variant + a, heads].reshape(HEADS_PER_TILE * GRID_W, 2 * GRID_W)
                 for a in range(0, kh, 2)], axis=-1)
            s = s + bias
            m = jnp.max(s, axis=-1, keepdims=True)
            e = jnp.exp(s - m)
            l = jnp.sum(e, axis=-1, keepdims=True)
            o4 = jnp.dot(e.astype(BF16), vw, preferred_element_type=F32) / l
            o = _unstack_heads(o4, GRID_W)
            gate = g_ref[pl.ds(q_off, GRID_W), lanes].astype(F32)
            o_ref[pl.ds(q_off, GRID_W), lanes] = (o * _silu(gate)).astype(BF16)
        return carry

    lax.fori_loop(0, NA_ROWS_PER_STEP, row_body, 0, unroll=True)


def _neighbourhood(zp, bias_tbl, layer):
    _, batch, seq, _ = zp.shape
    rows = seq // GRID_W
    tq = NA_ROWS_PER_STEP * GRID_W
    q_spec = lambda p: pl.BlockSpec((None, None, tq, D_GROUP), lambda b, i: (p, b, i, 0))
    kv_spec = lambda p: pl.BlockSpec((None, None, seq, D_GROUP), lambda b, i: (p, b, 0, 0))
    return pl.pallas_call(
        functools.partial(_na_kernel, rows=rows),
        name="natten",
        grid=(batch, rows // NA_ROWS_PER_STEP),
        in_specs=[
            q_spec(P_NQ), kv_spec(P_NK), kv_spec(P_NV), q_spec(P_NG),
            pl.BlockSpec((None,) + bias_tbl.shape[1:], lambda b, i: (layer, 0, 0, 0, 0),
                         pipeline_mode=pl.Buffered(1)),
        ],
        out_specs=pl.BlockSpec((None, tq, D_GROUP), lambda b, i: (b, i, 0)),
        out_shape=jax.ShapeDtypeStruct((batch, seq, D_GROUP), BF16),
        compiler_params=_params(("arbitrary", "arbitrary")),
    )(zp, zp, zp, zp, bias_tbl)


RET_CHUNK = MXU_DIM


def _log_sigmoid(t):
    return jnp.minimum(t, 0.0) - jnp.log1p(jnp.exp(-jnp.abs(t)))


def _rope_tables(seq):
    half = HEAD_DIM // 2
    inv = ROPE_BASE ** (-jnp.arange(half, dtype=F32) / half)
    ang = jnp.arange(seq, dtype=F32)[:, None] * inv[None, :]
    cos, sin = jnp.cos(ang), jnp.sin(ang)
    cos2 = jnp.concatenate([cos, cos], axis=-1)
    sin2 = jnp.concatenate([-sin, sin], axis=-1)
    return jnp.tile(cos2, (1, 2)), jnp.tile(sin2, (1, 2))


def _ret_kernel(lf_s_ref, lb_s_ref, q_ref, k_ref, v_ref, g_ref, cos_ref, sin_ref,
                o_ref, qr_ref, kr_ref, ob_ref, dmat_ref, dec_ref, state_ref, ones_ref, *, seq, layer):
    C = RET_CHUNK
    n_chunks = seq // C
    cg = pl.program_id(1)
    ones_bd = jnp.where(_head_select_mask(), 1.0, 0.0).astype(BF16)
    ones_ref[...] = jnp.concatenate([ones_bd, ones_bd], axis=0)

    ri = lax.broadcasted_iota(jnp.int32, (C, C), 0)
    ci = lax.broadcasted_iota(jnp.int32, (C, C), 1)
    diff = (ri - ci).astype(F32)
    lane_head = lax.broadcasted_iota(jnp.int32, (1, MXU_DIM), 1) // HEAD_DIM
    lfl = jnp.zeros((1, MXU_DIM), F32)
    lbl = jnp.zeros((1, MXU_DIM), F32)
    for hh in range(HEADS_PER_TILE):
        h = cg * HEADS_PER_TILE + hh
        lf = _log_sigmoid(jnp.full((C, C), lf_s_ref[layer, h], F32))
        lb = _log_sigmoid(jnp.full((C, C), lb_s_ref[layer, h], F32))
        dmat_ref[hh] = jnp.where(diff >= 0, jnp.exp(lf * diff), jnp.exp(lb * (-diff)))
        lfl = jnp.where(lane_head == hh, lf_s_ref[layer, h], lfl)
        lbl = jnp.where(lane_head == hh, lb_s_ref[layer, h], lbl)
    lfl = _log_sigmoid(lfl)
    lbl = _log_sigmoid(lbl)
    idx = lax.broadcasted_iota(jnp.int32, (C, MXU_DIM), 0).astype(F32)
    dec_ref[0] = jnp.exp(lfl * (idx + 1.0))
    dec_ref[1] = jnp.exp(lfl * (C - 1.0 - idx))
    dec_ref[2] = jnp.exp(lbl * (C - idx))
    dec_ref[3] = jnp.exp(lbl * idx)
    cd_f = jnp.exp(lfl * float(C))
    cd_b = jnp.exp(lbl * float(C))

    def rope(t, rows):
        lane = lax.broadcasted_iota(jnp.int32, t.shape, 1)
        first_half = (lane % HEAD_DIM) < (HEAD_DIM // 2)
        cs = cos_ref[rows, :]
        sn = sin_ref[rows, :]
        cs = jnp.concatenate([cs, cs], axis=-1)
        sn = jnp.concatenate([sn, sn], axis=-1)
        swapped = jnp.where(first_half, pltpu.roll(t, MXU_DIM - HEAD_DIM // 2, 1),
                            pltpu.roll(t, HEAD_DIM // 2, 1))
        return t * cs + swapped * sn

    def kv_update(k, v, kd, cd):
        kv = lax.dot_general((k * kd).astype(BF16), v, (((0,), (0,)), ((), ())),
                             preferred_element_type=F32)
        state_ref[...] = cd * state_ref[...] + jnp.where(_head_select_mask(), kv, 0.0)

    state_ref[...] = jnp.zeros_like(state_ref)

    def bwd_body(i, carry):
        n = n_chunks - 1 - i
        rows = pl.ds(pl.multiple_of(n * C, C), C)
        q = rope(q_ref[rows, :].astype(F32), rows) * (HEAD_DIM ** -0.5)
        k = rope(k_ref[rows, :].astype(F32), rows)
        qr_ref[rows, :] = q.astype(BF16)
        kr_ref[rows, :] = k.astype(BF16)
        ob_ref[rows, :] = jnp.dot((q * dec_ref[2]).astype(BF16), state_ref[...].astype(BF16),
                                  preferred_element_type=F32)
        kv_update(k, v_ref[rows, :], dec_ref[3], cd_b)
        return carry

    lax.fori_loop(0, n_chunks, bwd_body, 0, unroll=8)

    state_ref[...] = jnp.zeros_like(state_ref)

    def fwd_body(n, carry):
        rows = pl.ds(pl.multiple_of(n * C, C), C)
        qb = qr_ref[rows, :]
        kb = kr_ref[rows, :]
        v = v_ref[rows, :]
        o_cross = jnp.dot((qb.astype(F32) * dec_ref[0]).astype(BF16), state_ref[...].astype(BF16),
                          preferred_element_type=F32)
        q4 = _stack_heads(qb)
        sc = lax.dot_general(q4, kb, (((1,), (1,)), ((), ())), preferred_element_type=F32)
        sc = sc * dmat_ref[...].reshape(HEADS_PER_TILE * C, C)
        o4 = jnp.dot(sc.astype(BF16), v, preferred_element_type=F32)
        o = _unstack_heads(o4, C) + o_cross + ob_ref[rows, :]
        sq = o * o
        hi = sq.astype(BF16)
        lo = (sq - hi.astype(F32)).astype(BF16)
        ss = jnp.dot(jnp.concatenate([hi, lo], axis=-1), ones_ref[...], preferred_element_type=F32)
        inv = lax.rsqrt(ss * (1.0 / HEAD_DIM) + EPS)
        gate = g_ref[rows, :].astype(F32)
        o_ref[rows, :] = (o * inv * _silu(gate)).astype(BF16)
        kv_update(kb.astype(F32), v, dec_ref[1], cd_f)
        return carry

    lax.fori_loop(0, n_chunks, fwd_body, 0, unroll=8)


def _retention(zp, logit_f, logit_b, rope_tbl, layer):
    _, batch, seq, _ = zp.shape
    cos_t, sin_t = rope_tbl
    n_cg = D_GROUP // MXU_DIM
    spec = lambda p: pl.BlockSpec((None, None, seq, MXU_DIM), lambda b, c, *_: (p, b, 0, c))
    tbl_spec = pl.BlockSpec((seq, 2 * HEAD_DIM), lambda b, c, *_: (0, 0))
    return pl.pallas_call(
        functools.partial(_ret_kernel, seq=seq, layer=layer),
        name="retention",
        grid_spec=pltpu.PrefetchScalarGridSpec(
            num_scalar_prefetch=2,
            grid=(batch, n_cg),
            in_specs=[spec(P_RQ), spec(P_RK), spec(P_RV), spec(P_RG), tbl_spec, tbl_spec],
            out_specs=pl.BlockSpec((None, seq, MXU_DIM), lambda b, c, *_: (b, 0, c)),
            scratch_shapes=[
                pltpu.VMEM((seq, MXU_DIM), BF16),
                pltpu.VMEM((seq, MXU_DIM), BF16),
                pltpu.VMEM((seq, MXU_DIM), F32),
                pltpu.VMEM((HEADS_PER_TILE, RET_CHUNK, RET_CHUNK), F32),
                pltpu.VMEM((4, RET_CHUNK, MXU_DIM), F32),
                pltpu.VMEM((MXU_DIM, MXU_DIM), F32),
                pltpu.VMEM((2 * MXU_DIM, MXU_DIM), BF16),
            ],
        ),
        out_shape=jax.ShapeDtypeStruct((batch, seq, D_GROUP), BF16),
        compiler_params=_params(("arbitrary", "arbitrary")),
    )(logit_f, logit_b, zp, zp, zp, zp, cos_t, sin_t)


CONV_HALO = SUBLANES_BF16
CONV_ROWS = 128


def _conv_glu(j, a_ref, b_ref, u_ref, *, seq):
    halo = CONV_HALO
    ts = u_ref.shape[0] - 2 * halo
    n_tiles = seq // ts

    def glu(start, size):
        rows = pl.ds(pl.multiple_of(start, halo), size)
        a = a_ref[rows, :].astype(F32)
        return a * _sigmoid(b_ref[rows, :].astype(F32))

    base = j * ts
    u_ref[halo:halo + ts, :] = glu(base, ts)
    lo = glu(jnp.maximum(base - halo, 0), halo)
    u_ref[0:halo, :] = jnp.where(j > 0, lo, 0.0)
    hi = glu(jnp.minimum(base + ts, seq - halo), halo)
    u_ref[halo + ts:, :] = jnp.where(j < n_tiles - 1, hi, 0.0)


def _conv_rows(s0, cw_ref, cb_ref, lg_ref, lb_ref, u_ref, h_ref):
    rb = CONV_ROWS
    first = CONV_HALO - CONV_HALF
    parts = []
    for lt in range(D_GROUP // LANES):
        lanes = slice(lt * LANES, (lt + 1) * LANES)
        y = None
        for r in range(SUBLANES_F32):
            acc = None
            for o in range(r, first + CONV_WIDTH, SUBLANES_F32):
                if o < first:
                    continue
                w = o - first
                lo_row = s0 + (o - r)
                term = u_ref[lo_row:lo_row + rb + SUBLANES_F32, lanes] * cw_ref[w:w + 1, lanes]
                acc = term if acc is None else acc + term
            shifted = acc[r:r + rb]
            y = shifted if y is None else y + shifted
        parts.append(y)
    y = jnp.concatenate(parts, axis=-1) + cb_ref[...]
    mu = jnp.mean(y, axis=-1, keepdims=True)
    yc = y - mu
    var = jnp.mean(yc * yc, axis=-1, keepdims=True)
    y = (yc * lax.rsqrt(var + EPS)) * lg_ref[...] + lb_ref[...]
    h_ref[s0:s0 + rb, :] = _silu(y).astype(BF16)
    return y[rb - SUBLANES_F32:, D_GROUP - LANES:]


def _runtime_zero(tile):
    bits = pltpu.bitcast(tile, jnp.uint32)
    bits = lax.shift_right_logical(lax.shift_right_logical(bits, jnp.uint32(16)), jnp.uint32(16))
    return bits[0, 0].astype(jnp.int32)


def _conv_pointwise(g_ref, w_ref, h_ref, o_ref):
    o = jnp.dot(h_ref[...], w_ref[...], preferred_element_type=F32)
    o_ref[...] = (o * _silu(g_ref[...].astype(F32))).astype(BF16)


def kernel(x, c, norm_g, w_ada, b_ada, w_in, w_fft, na_rel_bias, ret_logit_fwd, ret_logit_bwd,
           conv_w, conv_b, conv_ln_g, conv_ln_b, conv_w_pw, w_out, final_g):
    seq = x.shape[1]
    mod = _ada(c, w_ada, b_ada)
    fft_consts = _fft_constants(seq)
    rope_tbl = _rope_tables(seq)
    bias_tbl = _na_bias_table(na_rel_bias)
    w_fft_b, w_pw_b = w_fft.astype(BF16), conv_w_pw.astype(BF16)
    for l in range(DEPTH):
        zp = _inproj(x, norm_g, mod, w_in, l)
        o_fft = _fourier(zp, w_fft_b, fft_consts, l)
        o_na = _neighbourhood(zp, bias_tbl, l)
        o_ret = _retention(zp, ret_logit_fwd, ret_logit_bwd, rope_tbl, l)
        conv_params = (conv_w, conv_b, conv_ln_g, conv_ln_b)
        x = _outproj((o_fft, o_na, o_ret), zp, conv_params, w_pw_b, w_out, x, mod, final_g, l,
                     final=(l == DEPTH - 1))
    return x
```

```python
import functools
import math

import numpy as np
import jax
import jax.numpy as jnp
from jax import lax
from jax.experimental import pallas as pl
from jax.experimental.pallas import tpu as pltpu

F32 = jnp.float32
BF16 = jnp.bfloat16

D_MODEL = 2048
DEPTH = 2
GRID_W = 64
D_GROUP = 512
HEAD_DIM = 64
N_HEADS = D_GROUP // HEAD_DIM
N_FFT_GROUPS = 4
FFT_GROUP_DIM = D_GROUP // N_FFT_GROUPS
NA_KH = 8
NA_KW = 16
CONV_WIDTH = 31
CONV_HALF = CONV_WIDTH // 2
ROPE_BASE = 10000.0
EPS = 1e-6
N_PIECES = 13
(P_FX, P_FG, P_NQ, P_NK, P_NV, P_NG, P_RQ, P_RK, P_RV, P_RG, P_CA, P_CB, P_CG) = range(N_PIECES)

MXU_DIM = 256
HEADS_PER_TILE = MXU_DIM // HEAD_DIM
VMEM_LIMIT = 56 * 1024 * 1024
SUBLANES_BF16 = 16
SUBLANES_F32 = 8
LANES = 128

NEG_BIG = -1e30


def _params(sem, vmem=VMEM_LIMIT):
    return pltpu.CompilerParams(dimension_semantics=sem, vmem_limit_bytes=vmem)


def _sigmoid(t):
    return 0.5 * jnp.tanh(0.5 * t) + 0.5


def _silu(t):
    h = 0.5 * t
    return h + h * jnp.tanh(h)


ADA_TN = 768
ADA_ROWS = 8
(MOD_SHIFT, MOD_SCALE, MOD_GATE) = range(3)


def _ada_kernel(c_ref, w_ref, b_ref, o_ref):
    c = c_ref[...]
    pad = jnp.zeros((ADA_ROWS - c.shape[0], D_MODEL), F32)
    ca = _silu(jnp.concatenate([c, pad], axis=0)).astype(BF16)
    w = w_ref[...].astype(BF16)
    bias = b_ref[pl.ds(pl.program_id(0), 1), :]
    o_ref[...] = jnp.dot(ca, w, preferred_element_type=F32) + bias


def _ada(c, w_ada, b_ada):
    batch = c.shape[0]
    n3 = 3 * D_MODEL
    return pl.pallas_call(
        _ada_kernel,
        name="ada",
        grid=(DEPTH, n3 // ADA_TN),
        in_specs=[
            pl.BlockSpec((batch, D_MODEL), lambda l, j: (0, 0)),
            pl.BlockSpec((None, D_MODEL, ADA_TN), lambda l, j: (l, 0, j)),
            pl.BlockSpec((DEPTH, ADA_TN), lambda l, j: (0, j)),
        ],
        out_specs=pl.BlockSpec((None, ADA_ROWS, ADA_TN), lambda l, j: (l, 0, j)),
        out_shape=jax.ShapeDtypeStruct((DEPTH, ADA_ROWS, n3), F32),
        compiler_params=_params(("arbitrary", "arbitrary")),
    )(c, w_ada, b_ada)


INPROJ_TM = 512
INPROJ_NORM_SLICES = 8
INPROJ_WROWS = SUBLANES_BF16
INPROJ_WSLOTS = 8
INPROJ_VMEM = 60 * 1024 * 1024


def _inproj_kernel(x_ref, g_ref, sc_ref, sh_ref, w_hbm, o_ref, h0_ref, h1_ref, w_ref, stage_ref, sem,
                   *, n_tiles, per_batch, layer):
    t = pl.program_id(0)
    wrows = stage_ref.shape[1]
    n_rchunks = D_MODEL // wrows

    def weight_copy(r, slot):
        rows = pl.ds(pl.multiple_of(r * wrows, wrows), wrows)
        return pltpu.make_async_copy(w_hbm.at[layer, rows, :], stage_ref.at[slot], sem.at[slot])

    def load_weight():
        n_slots = stage_ref.shape[0]
        for slot in range(n_slots):
            weight_copy(slot, slot).start()

        def group(i, carry):
            for slot in range(n_slots):
                r = n_slots * i + slot
                weight_copy(r, slot).wait()
                rows = pl.ds(pl.multiple_of(r * wrows, wrows), wrows)
                w_ref[rows, :] = stage_ref[slot].astype(BF16)

                @pl.when(r + n_slots < n_rchunks)
                def _():
                    weight_copy(r + n_slots, slot).start()
            return carry

        lax.fori_loop(0, n_rchunks // n_slots, group, 0)

    tm = x_ref.shape[0]
    n_slices = INPROJ_NORM_SLICES
    rs = tm // n_slices

    def run(proj_ref, norm_ref):
        if proj_ref is not None:
            h = proj_ref[...]
        if norm_ref is not None:
            row = pl.ds(jnp.minimum(t, n_tiles - 1) // per_batch, 1)
            gain = g_ref[layer:layer + 1, :] * (1.0 + sc_ref[row, :])
            shift = sh_ref[row, :]
        zero = 0
        for p in range(N_PIECES):
            if proj_ref is not None:
                cols = pl.ds(pl.multiple_of(p * D_GROUP + zero, D_GROUP), D_GROUP)
                o_ref[p] = jnp.dot(h, w_ref[:, cols], preferred_element_type=F32).astype(BF16)
            if norm_ref is not None and p < n_slices:
                x = x_ref[p * rs:(p + 1) * rs, :]
                ms = jnp.mean(x * x, axis=-1, keepdims=True)
                hn = (x * lax.rsqrt(ms + EPS)) * gain + shift
                norm_ref[p * rs:(p + 1) * rs, :] = hn.astype(BF16)
                zero = _runtime_zero(hn[rs - SUBLANES_F32:, D_MODEL - LANES:])

    odd = t % 2 == 1

    @pl.when(t == 0)
    def _():
        load_weight()
        run(None, h0_ref)

    @pl.when(odd)
    def _():
        run(h0_ref, h1_ref)

    @pl.when(jnp.logical_and(jnp.logical_not(odd), jnp.logical_and(t > 0, t < n_tiles)))
    def _():
        run(h1_ref, h0_ref)

    @pl.when(t == n_tiles)
    def _():
        run(h1_ref, None)


def _inproj(x, norm_g, mod, w_in, layer):
    batch, seq, _ = x.shape
    tm = INPROJ_TM
    per_batch = seq // tm
    n_tiles = batch * per_batch
    assert n_tiles % 2 == 0 and (D_MODEL // INPROJ_WROWS) % INPROJ_WSLOTS == 0
    norm_tile = lambda t: jnp.minimum(t, n_tiles - 1)
    proj_tile = lambda t: jnp.maximum(t - 1, 0)
    return pl.pallas_call(
        functools.partial(_inproj_kernel, n_tiles=n_tiles, per_batch=per_batch, layer=layer),
        name="inproj",
        grid=(n_tiles + 1,),
        in_specs=[
            pl.BlockSpec((None, tm, D_MODEL),
                         lambda t: (norm_tile(t) // per_batch, norm_tile(t) % per_batch, 0)),
            pl.BlockSpec((DEPTH, D_MODEL), lambda t: (0, 0)),
            pl.BlockSpec((None, ADA_ROWS, D_MODEL), lambda t: (layer, 0, MOD_SCALE)),
            pl.BlockSpec((None, ADA_ROWS, D_MODEL), lambda t: (layer, 0, MOD_SHIFT)),
            pl.BlockSpec(memory_space=pl.ANY),
        ],
        out_specs=pl.BlockSpec((N_PIECES, None, tm, D_GROUP),
                               lambda t: (0, proj_tile(t) // per_batch, proj_tile(t) % per_batch, 0)),
        out_shape=jax.ShapeDtypeStruct((N_PIECES, batch, seq, D_GROUP), BF16),
        scratch_shapes=[
            pltpu.VMEM((tm, D_MODEL), BF16), pltpu.VMEM((tm, D_MODEL), BF16),
            pltpu.VMEM((D_MODEL, N_PIECES * D_GROUP), BF16),
            pltpu.VMEM((INPROJ_WSLOTS, INPROJ_WROWS, N_PIECES * D_GROUP), F32),
            pltpu.SemaphoreType.DMA((INPROJ_WSLOTS,)),
        ],
        compiler_params=_params(("arbitrary",), INPROJ_VMEM),
    )(x, norm_g, mod, mod, w_in)


OUTPROJ_TM = 512
OUTPROJ_WROWS = 64
OUTPROJ_WSLOTS = 8
OUTPROJ_VMEM = 60 * 1024 * 1024


def _outproj_kernel(a0_ref, a1_ref, a2_ref, w_hbm, x_ref, gate_ref, fg_ref,
                    ca_ref, cb_ref, cg_ref, cw_ref, cbias_ref, lg_ref, lb_ref, wpw_ref,
                    o_ref, u_ref, h_ref, cv0_ref, cv1_ref, w_ref, stage_ref, sem,
                    *, final, n_tiles, per_batch, seq, layer):
    t = pl.program_id(0)
    j = jnp.minimum(t, n_tiles - 1) % per_batch
    gate_row = pl.ds(jnp.maximum(t - 1, 0) // per_batch, 1)
    layer_row = pl.ds(layer, 1)

    tm = o_ref.shape[0]
    n_chunks = tm // CONV_ROWS
    tn = D_MODEL // n_chunks
    n_slots = stage_ref.shape[0]

    wrows = stage_ref.shape[1]

    def weight_copy(r):
        slot = r % n_slots
        return pltpu.make_async_copy(w_hbm.at[layer, pl.ds(r * wrows, wrows), :], stage_ref.at[slot],
                                     sem.at[slot])

    def run(proj_cv_ref, conv_cv_ref, load_weight=False):
        if load_weight:
            for c in range(n_slots):
                weight_copy(c).start()
        if conv_cv_ref is not None:
            _conv_glu(j, ca_ref, cb_ref, u_ref, seq=seq)
        if proj_cv_ref is not None:
            a = jnp.concatenate([a0_ref[...], a1_ref[...], a2_ref[...], proj_cv_ref[...]], axis=-1)
            ss = jnp.zeros((tm, LANES), F32)
        zero = 0
        for i in range(n_chunks):
            if proj_cv_ref is not None:
                y = jnp.dot(a, w_ref[i + zero], preferred_element_type=F32)
                xn = x_ref[:, i * tn:(i + 1) * tn] + gate_ref[gate_row, i * tn:(i + 1) * tn] * y
                if final:
                    sq = xn * xn
                    for lt in range(tn // LANES):
                        ss = ss + sq[:, lt * LANES:(lt + 1) * LANES]
                o_ref[:, i * tn:(i + 1) * tn] = xn
            if conv_cv_ref is not None:
                tail = _conv_rows(i * CONV_ROWS, cw_ref, cbias_ref.at[layer_row], lg_ref.at[layer_row],
                                  lb_ref.at[layer_row], u_ref, h_ref)
                zero = _runtime_zero(tail)
            if load_weight:
                n_rchunks = D_MODEL // wrows
                per_block = n_rchunks // n_chunks
                for r in range(i * per_block, (i + 1) * per_block):
                    weight_copy(r).wait()
                    for c in range(n_chunks):
                        w_ref[c, r * wrows:(r + 1) * wrows, :] = stage_ref[
                            r % n_slots, :, c * tn:(c + 1) * tn].astype(BF16)
                    if r + n_slots < n_rchunks:
                        weight_copy(r + n_slots).start()
        if conv_cv_ref is not None:
            _conv_pointwise(cg_ref, wpw_ref, h_ref, conv_cv_ref)
        if proj_cv_ref is not None and final:
            ms = jnp.sum(ss, axis=-1, keepdims=True) * (1.0 / D_MODEL)
            o_ref[...] = (o_ref[...] * lax.rsqrt(ms + EPS)) * fg_ref[...]

    odd = t % 2 == 1

    @pl.when(t == 0)
    def _():
        run(None, cv0_ref, load_weight=True)

    @pl.when(odd)
    def _():
        run(cv0_ref, cv1_ref)

    @pl.when(jnp.logical_and(jnp.logical_not(odd), jnp.logical_and(t > 0, t < n_tiles)))
    def _():
        run(cv1_ref, cv0_ref)

    @pl.when(t == n_tiles)
    def _():
        run(cv1_ref, None)


def _outproj(mixed, zp, conv_params, w_pw_bf16, w_out, x, mod, final_g, layer, final):
    batch, seq, _ = x.shape
    tm = OUTPROJ_TM
    per_batch = seq // tm
    n_tiles = batch * per_batch
    n_wchunks = tm // CONV_ROWS
    assert n_tiles % 2 == 0
    conv_tile = lambda t: jnp.minimum(t, n_tiles - 1)
    proj_tile = lambda t: jnp.maximum(t - 1, 0)
    a_spec = pl.BlockSpec((None, tm, D_GROUP),
                          lambda t: (proj_tile(t) // per_batch, proj_tile(t) % per_batch, 0))
    x_spec = pl.BlockSpec((None, tm, D_MODEL),
                          lambda t: (proj_tile(t) // per_batch, proj_tile(t) % per_batch, 0))
    seq_spec = lambda p: pl.BlockSpec((None, None, seq, D_GROUP),
                                      lambda t: (p, conv_tile(t) // per_batch, 0, 0))
    vec = pl.BlockSpec((DEPTH, D_GROUP), lambda t: (0, 0))
    conv_w, conv_b, ln_g, ln_b = conv_params
    return pl.pallas_call(
        functools.partial(_outproj_kernel, final=final, n_tiles=n_tiles, per_batch=per_batch, seq=seq,
                          layer=layer),
        name="outproj",
        grid=(n_tiles + 1,),
        in_specs=[
            a_spec, a_spec, a_spec,
            pl.BlockSpec(memory_space=pl.ANY),
            x_spec,
            pl.BlockSpec((None, ADA_ROWS, D_MODEL), lambda t: (layer, 0, MOD_GATE)),
            pl.BlockSpec((1, D_MODEL), lambda t: (0, 0)),
            seq_spec(P_CA), seq_spec(P_CB),
            pl.BlockSpec((None, None, tm, D_GROUP),
                         lambda t: (P_CG, conv_tile(t) // per_batch, conv_tile(t) % per_batch, 0)),
            pl.BlockSpec((None, CONV_WIDTH, D_GROUP), lambda t: (layer, 0, 0)),
            vec, vec, vec,
            pl.BlockSpec((None, D_GROUP, D_GROUP), lambda t: (layer, 0, 0)),
        ],
        out_specs=x_spec,
        out_shape=jax.ShapeDtypeStruct((batch, seq, D_MODEL), F32),
        scratch_shapes=[
            pltpu.VMEM((tm + 2 * CONV_HALO, D_GROUP), F32),
            pltpu.VMEM((tm, D_GROUP), BF16),
            pltpu.VMEM((tm, D_GROUP), BF16), pltpu.VMEM((tm, D_GROUP), BF16),
            pltpu.VMEM((n_wchunks, D_MODEL, D_MODEL // n_wchunks), BF16),
            pltpu.VMEM((OUTPROJ_WSLOTS, OUTPROJ_WROWS, D_MODEL), F32),
            pltpu.SemaphoreType.DMA((OUTPROJ_WSLOTS,)),
        ],
        compiler_params=_params(("arbitrary",), OUTPROJ_VMEM),
    )(*mixed, w_out, x, mod, final_g.reshape(1, D_MODEL), zp, zp, zp, conv_w,
      conv_b, ln_g, ln_b, w_pw_bf16)


FFT_BLOCK = 32
FFT_LANES = 2 * FFT_GROUP_DIM


def _fft_constants(seq):
    rows = seq // GRID_W
    assert rows == GRID_W
    n = np.arange(GRID_W)
    ang1 = 2.0 * np.pi * ((n[:, None] * n[None, :]) % GRID_W) / GRID_W
    f1 = np.concatenate([np.cos(ang1), -np.sin(ang1)], axis=0)
    k1 = n[:, None, None]
    k2 = n[None, :, None]
    s2 = n[None, None, :]
    ang2 = 2.0 * np.pi * ((s2 * (k1 + GRID_W * k2)) % seq) / seq
    mr, mi = np.cos(ang2), -np.sin(ang2)
    m2 = np.concatenate([np.concatenate([mr, -mi], axis=2),
                         np.concatenate([mi, mr], axis=2)], axis=1)
    c = np.arange(FFT_GROUP_DIM)
    angc = 2.0 * np.pi * ((c[:, None] * c[None, :]) % FFT_GROUP_DIM) / FFT_GROUP_DIM
    fc = np.concatenate([np.cos(angc), np.sin(angc)], axis=0)
    return tuple(jnp.asarray(t, F32).astype(BF16) for t in (f1, m2, fc))


def _fft_kernel(u_ref, gate_ref, f1_ref, m2_ref, fc_ref, w_ref, o_ref, ut_ref, g_ref, gt_ref, y_ref, *, norm):
    pair = pl.program_id(1)
    n = GRID_W
    blk = FFT_BLOCK
    sub = SUBLANES_F32
    lanes = ut_ref.shape[-1]
    groups = lanes // FFT_GROUP_DIM

    u3 = u_ref[...].astype(F32).reshape(n, n, lanes)
    ut_ref[...] = jnp.swapaxes(u3, 0, 1).astype(BF16)

    def stage1(i, carry):
        for j in range(blk):
            s2 = i * blk + j
            g_ref[s2] = jnp.dot(f1_ref[...], ut_ref[s2], preferred_element_type=F32)
        return carry

    lax.fori_loop(0, n // blk, stage1, 0)

    for q in range(2 * n // sub):
        part = g_ref[:, q * sub:(q + 1) * sub, :]
        gt_ref[q * sub:(q + 1) * sub] = jnp.swapaxes(part, 0, 1).astype(BF16)

    def stage2(i, carry):
        xs = []
        for j in range(blk):
            k1 = i * blk + j
            gk = jnp.concatenate([gt_ref[k1], gt_ref[n + k1]], axis=0)
            xs.append(jnp.dot(m2_ref[k1], gk, preferred_element_type=F32))
        for g in range(groups):
            sl = slice(g * FFT_GROUP_DIM, (g + 1) * FFT_GROUP_DIM)
            lhs = jnp.concatenate(
                [jnp.concatenate([xk[:n, sl], xk[n:, sl]], axis=-1) for xk in xs], axis=0)
            y = jnp.dot(lhs.astype(BF16), fc_ref[...], preferred_element_type=F32) * norm
            y_ref[pair * groups + g, pl.ds(pl.multiple_of(i * blk, blk), blk)] = (
                y.reshape(blk, n, FFT_GROUP_DIM))
        return carry

    lax.fori_loop(0, n // blk, stage2, 0)

    @pl.when(pair == pl.num_programs(1) - 1)
    def _():
        def stage3(i, carry):
            k2 = pl.ds(pl.multiple_of(i * sub, sub), sub)
            tiles = []
            for g in range(N_FFT_GROUPS):
                part = y_ref[g, :, k2, :]
                tiles.append(jnp.swapaxes(part, 0, 1).reshape(sub * n, FFT_GROUP_DIM))
            y = jnp.concatenate(tiles, axis=-1).astype(BF16)
            o = jnp.dot(y, w_ref[...], preferred_element_type=F32)
            rows = pl.ds(pl.multiple_of(i * sub * n, sub * n), sub * n)
            o_ref[rows, :] = (o * _silu(gate_ref[rows, :].astype(F32))).astype(BF16)
            return carry

        lax.fori_loop(0, n // sub, stage3, 0, unroll=2)


def _fourier(zp, w_fft_bf16, consts, layer):
    _, batch, seq, _ = zp.shape
    f1, m2, fc = consts
    n = GRID_W
    const = lambda shape: pl.BlockSpec(shape, lambda b, g: (0,) * len(shape))
    return pl.pallas_call(
        functools.partial(_fft_kernel, norm=1.0 / math.sqrt(seq * FFT_GROUP_DIM)),
        name="fourier",
        grid=(batch, D_GROUP // FFT_LANES),
        in_specs=[
            pl.BlockSpec((None, None, seq, FFT_LANES), lambda b, g: (P_FX, b, 0, g)),
            pl.BlockSpec((None, None, seq, D_GROUP), lambda b, g: (P_FG, b, 0, 0)),
            const((2 * n, n)), const((n, 2 * n, 2 * n)), const((2 * FFT_GROUP_DIM, FFT_GROUP_DIM)),
            pl.BlockSpec((None, D_GROUP, D_GROUP), lambda b, g: (layer, 0, 0)),
        ],
        out_specs=pl.BlockSpec((None, seq, D_GROUP), lambda b, g: (b, 0, 0)),
        out_shape=jax.ShapeDtypeStruct((batch, seq, D_GROUP), BF16),
        scratch_shapes=[
            pltpu.VMEM((n, n, FFT_LANES), BF16),
            pltpu.VMEM((n, 2 * n, FFT_LANES), F32),
            pltpu.VMEM((2 * n, n, FFT_LANES), BF16),
            pltpu.VMEM((N_FFT_GROUPS, n, n, FFT_GROUP_DIM), F32),
        ],
        compiler_params=_params(("arbitrary", "arbitrary")),
    )(zp, zp, f1, m2, fc, w_fft_bf16)


NA_ROWS_PER_STEP = 16


def _na_bias_table(rel_bias):
    n_dc = 2 * NA_KW - 1
    col = np.arange(GRID_W)
    col_start = np.clip(col - NA_KW // 2, 0, GRID_W - NA_KW)
    rel_c = col[None, :] - col_start[:, None]
    col_in = (rel_c >= 0) & (rel_c < NA_KW)
    dc = np.clip(col[None, :] - col[:, None] + NA_KW - 1, 0, n_dc - 1)
    onehot = (dc[None] == np.arange(n_dc)[:, None, None]).astype(np.float32)
    zeros = np.zeros_like(onehot)
    onehot2 = np.concatenate([np.concatenate([onehot, zeros], axis=-1),
                              np.concatenate([zeros, onehot], axis=-1)], axis=0)
    rb = rel_bias.astype(F32)
    rb2 = jnp.concatenate([rb[:, :, :-1], rb[:, :, 1:]], axis=-1)
    tbl = jnp.einsum('lhdj,jqk->ldhqk', rb2, jnp.asarray(onehot2),
                     precision=lax.Precision.HIGHEST)
    col_in2 = np.concatenate([col_in, col_in], axis=-1)
    return jnp.where(col_in2[None, None, None], tbl, NEG_BIG)


def _head_select_mask():
    r = lax.broadcasted_iota(jnp.int32, (MXU_DIM, MXU_DIM), 0) // HEAD_DIM
    c = lax.broadcasted_iota(jnp.int32, (MXU_DIM, MXU_DIM), 1) // HEAD_DIM
    return r == c


def _stack_heads(t):
    n = t.shape[0]
    reps = jnp.concatenate([t] * HEADS_PER_TILE, axis=0)
    row_head = lax.broadcasted_iota(jnp.int32, reps.shape, 0) // n
    lane_head = lax.broadcasted_iota(jnp.int32, reps.shape, 1) // HEAD_DIM
    return jnp.where(row_head == lane_head, reps, jnp.zeros_like(reps))


def _unstack_heads(t4, n):
    lane_head = lax.broadcasted_iota(jnp.int32, (n, MXU_DIM), 1) // HEAD_DIM
    out = jnp.zeros((n, MXU_DIM), t4.dtype)
    for h in range(HEADS_PER_TILE):
        out = jnp.where(lane_head == h, t4[h * n:(h + 1) * n], out)
    return out


def _na_kernel(q_ref, k_ref, v_ref, g_ref, bias_ref, o_ref, *, rows):
    kh = min(NA_KH, rows)
    step = pl.program_id(1)

    def row_body(rr, carry):
        r = step * NA_ROWS_PER_STEP + rr
        r_start = jnp.clip(r - kh // 2, 0, rows - kh)
        variant = r - r_start
        q_off = pl.multiple_of(rr * GRID_W, GRID_W)
        k_off = pl.multiple_of(r_start * GRID_W, GRID_W)
        for cg in range(D_GROUP // MXU_DIM):
            lanes = slice(cg * MXU_DIM, (cg + 1) * MXU_DIM)
            q = q_ref[pl.ds(q_off, GRID_W), lanes] * (HEAD_DIM ** -0.5)
            kw = k_ref[pl.ds(k_off, kh * GRID_W), lanes]
            vw = v_ref[pl.ds(k_off, kh * GRID_W), lanes]
            q4 = _stack_heads(q)
            s = lax.dot_general(q4, kw, (((1,), (1,)), ((), ())), preferred_element_type=F32)
            heads = slice(cg * HEADS_PER_TILE, (cg + 1) * HEADS_PER_TILE)
            bias = jnp.concatenate(
                [bias_ref[NA_KH - 1 - variant + a, heads].reshape(HEADS_PER_TILE * GRID_W, 2 * GRID_W)
                 for a in range(0, kh, 2)], axis=-1)
            s = s + bias
            m = jnp.max(s, axis=-1, keepdims=True)
            e = jnp.exp(s - m)
            l = jnp.sum(e, axis=-1, keepdims=True)
            o4 = jnp.dot(e.astype(BF16), vw, preferred_element_type=F32) / l
            o = _unstack_heads(o4, GRID_W)
            gate = g_ref[pl.ds(q_off, GRID_W), lanes].astype(F32)
            o_ref[pl.ds(q_off, GRID_W), lanes] = (o * _silu(gate)).astype(BF16)
        return carry

    lax.fori_loop(0, NA_ROWS_PER_STEP, row_body, 0, unroll=True)


def _neighbourhood(zp, bias_tbl, layer):
    _, batch, seq, _ = zp.shape
    rows = seq // GRID_W
    tq = NA_ROWS_PER_STEP * GRID_W
    q_spec = lambda p: pl.BlockSpec((None, None, tq, D_GROUP), lambda b, i: (p, b, i, 0))
    kv_spec = lambda p: pl.BlockSpec((None, None, seq, D_GROUP), lambda b, i: (p, b, 0, 0))
    return pl.pallas_call(
        functools.partial(_na_kernel, rows=rows),
        name="natten",
        grid=(batch, rows // NA_ROWS_PER_STEP),
        in_specs=[
            q_spec(P_NQ), kv_spec(P_NK), kv_spec(P_NV), q_spec(P_NG),
            pl.BlockSpec((None,) + bias_tbl.shape[1:], lambda b, i: (layer, 0, 0, 0, 0),
                         pipeline_mode=pl.Buffered(1)),
        ],
        out_specs=pl.BlockSpec((None, tq, D_GROUP), lambda b, i: (b, i, 0)),
        out_shape=jax.ShapeDtypeStruct((batch, seq, D_GROUP), BF16),
        compiler_params=_params(("arbitrary", "arbitrary")),
    )(zp, zp, zp, zp, bias_tbl)


RET_CHUNK = MXU_DIM


def _log_sigmoid(t):
    return jnp.minimum(t, 0.0) - jnp.log1p(jnp.exp(-jnp.abs(t)))


def _rope_tables(seq):
    half = HEAD_DIM // 2
    inv = ROPE_BASE ** (-jnp.arange(half, dtype=F32) / half)
    ang = jnp.arange(seq, dtype=F32)[:, None] * inv[None, :]
    cos, sin = jnp.cos(ang), jnp.sin(ang)
    cos2 = jnp.concatenate([cos, cos], axis=-1)
    sin2 = jnp.concatenate([-sin, sin], axis=-1)
    return jnp.tile(cos2, (1, 2)), jnp.tile(sin2, (1, 2))


def _ret_kernel(lf_s_ref, lb_s_ref, q_ref, k_ref, v_ref, g_ref, cos_ref, sin_ref,
                o_ref, qr_ref, kr_ref, ob_ref, dmat_ref, dec_ref, state_ref, ones_ref, *, seq, layer):
    C = RET_CHUNK
    n_chunks = seq // C
    cg = pl.program_id(1)
    ones_bd = jnp.where(_head_select_mask(), 1.0, 0.0).astype(BF16)
    ones_ref[...] = jnp.concatenate([ones_bd, ones_bd], axis=0)

    ri = lax.broadcasted_iota(jnp.int32, (C, C), 0)
    ci = lax.broadcasted_iota(jnp.int32, (C, C), 1)
    diff = (ri - ci).astype(F32)
    lane_head = lax.broadcasted_iota(jnp.int32, (1, MXU_DIM), 1) // HEAD_DIM
    lfl = jnp.zeros((1, MXU_DIM), F32)
    lbl = jnp.zeros((1, MXU_DIM), F32)
    for hh in range(HEADS_PER_TILE):
        h = cg * HEADS_PER_TILE + hh
        lf = _log_sigmoid(jnp.full((C, C), lf_s_ref[layer, h], F32))
        lb = _log_sigmoid(jnp.full((C, C), lb_s_ref[layer, h], F32))
        dmat_ref[hh] = jnp.where(diff >= 0, jnp.exp(lf * diff), jnp.exp(lb * (-diff)))
        lfl = jnp.where(lane_head == hh, lf_s_ref[layer, h], lfl)
        lbl = jnp.where(lane_head == hh, lb_s_ref[layer, h], lbl)
    lfl = _log_sigmoid(lfl)
    lbl = _log_sigmoid(lbl)
    idx = lax.broadcasted_iota(jnp.int32, (C, MXU_DIM), 0).astype(F32)
    dec_ref[0] = jnp.exp(lfl * (idx + 1.0))
    dec_ref[1] = jnp.exp(lfl * (C - 1.0 - idx))
    dec_ref[2] = jnp.exp(lbl * (C - idx))
    dec_ref[3] = jnp.exp(lbl * idx)
    cd_f = jnp.exp(lfl * float(C))
    cd_b = jnp.exp(lbl * float(C))

    def rope(t, rows):
        lane = lax.broadcasted_iota(jnp.int32, t.shape, 1)
        first_half = (lane % HEAD_DIM) < (HEAD_DIM // 2)
        cs = cos_ref[rows, :]
        sn = sin_ref[rows, :]
        cs = jnp.concatenate([cs, cs], axis=-1)
        sn = jnp.concatenate([sn, sn], axis=-1)
        swapped = jnp.where(first_half, pltpu.roll(t, MXU_DIM - HEAD_DIM // 2, 1),
                            pltpu.roll(t, HEAD_DIM // 2, 1))
        return t * cs + swapped * sn

    def kv_update(k, v, kd, cd):
        kv = lax.dot_general((k * kd).astype(BF16), v, (((0,), (0,)), ((), ())),
                             preferred_element_type=F32)
        state_ref[...] = cd * state_ref[...] + jnp.where(_head_select_mask(), kv, 0.0)

    state_ref[...] = jnp.zeros_like(state_ref)

    def bwd_body(i, carry):
        n = n_chunks - 1 - i
        rows = pl.ds(pl.multiple_of(n * C, C), C)
        q = rope(q_ref[rows, :].astype(F32), rows) * (HEAD_DIM ** -0.5)
        k = rope(k_ref[rows, :].astype(F32), rows)
        qr_ref[rows, :] = q.astype(BF16)
        kr_ref[rows, :] = k.astype(BF16)
        ob_ref[rows, :] = jnp.dot((q * dec_ref[2]).astype(BF16), state_ref[...].astype(BF16),
                                  preferred_element_type=F32)
        kv_update(k, v_ref[rows, :], dec_ref[3], cd_b)
        return carry

    lax.fori_loop(0, n_chunks, bwd_body, 0, unroll=8)

    state_ref[...] = jnp.zeros_like(state_ref)

    def fwd_body(n, carry):
        rows = pl.ds(pl.multiple_of(n * C, C), C)
        qb = qr_ref[rows, :]
        kb = kr_ref[rows, :]
        v = v_ref[rows, :]
        o_cross = jnp.dot((qb.astype(F32) * dec_ref[0]).astype(BF16), state_ref[...].astype(BF16),
                          preferred_element_type=F32)
        q4 = _stack_heads(qb)
        sc = lax.dot_general(q4, kb, (((1,), (1,)), ((), ())), preferred_element_type=F32)
        sc = sc * dmat_ref[...].reshape(HEADS_PER_TILE * C, C)
        o4 = jnp.dot(sc.astype(BF16), v, preferred_element_type=F32)
        o = _unstack_heads(o4, C) + o_cross + ob_ref[rows, :]
        sq = o * o
        hi = sq.astype(BF16)
        lo = (sq - hi.astype(F32)).astype(BF16)
        ss = jnp.dot(jnp.concatenate([hi, lo], axis=-1), ones_ref[...], preferred_element_type=F32)
        inv = lax.rsqrt(ss * (1.0 / HEAD_DIM) + EPS)
        gate = g_ref[rows, :].astype(F32)
        o_ref[rows, :] = (o * inv * _silu(gate)).astype(BF16)
        kv_update(kb.astype(F32), v, dec_ref[1], cd_f)
        return carry

    lax.fori_loop(0, n_chunks, fwd_body, 0, unroll=8)


def _retention(zp, logit_f, logit_b, rope_tbl, layer):
    _, batch, seq, _ = zp.shape
    cos_t, sin_t = rope_tbl
    n_cg = D_GROUP // MXU_DIM
    spec = lambda p: pl.BlockSpec((None, None, seq, MXU_DIM), lambda b, c, *_: (p, b, 0, c))
    tbl_spec = pl.BlockSpec((seq, 2 * HEAD_DIM), lambda b, c, *_: (0, 0))
    return pl.pallas_call(
        functools.partial(_ret_kernel, seq=seq, layer=layer),
        name="retention",
        grid_spec=pltpu.PrefetchScalarGridSpec(
            num_scalar_prefetch=2,
            grid=(batch, n_cg),
            in_specs=[spec(P_RQ), spec(P_RK), spec(P_RV), spec(P_RG), tbl_spec, tbl_spec],
            out_specs=pl.BlockSpec((None, seq, MXU_DIM), lambda b, c, *_: (b, 0, c)),
            scratch_shapes=[
                pltpu.VMEM((seq, MXU_DIM), BF16),
                pltpu.VMEM((seq, MXU_DIM), BF16),
                pltpu.VMEM((seq, MXU_DIM), F32),
                pltpu.VMEM((HEADS_PER_TILE, RET_CHUNK, RET_CHUNK), F32),
                pltpu.VMEM((4, RET_CHUNK, MXU_DIM), F32),
                pltpu.VMEM((MXU_DIM, MXU_DIM), F32),
                pltpu.VMEM((2 * MXU_DIM, MXU_DIM), BF16),
            ],
        ),
        out_shape=jax.ShapeDtypeStruct((batch, seq, D_GROUP), BF16),
        compiler_params=_params(("arbitrary", "arbitrary")),
    )(logit_f, logit_b, zp, zp, zp, zp, cos_t, sin_t)


CONV_HALO = SUBLANES_BF16
CONV_ROWS = 128


def _conv_glu(j, a_ref, b_ref, u_ref, *, seq):
    halo = CONV_HALO
    ts = u_ref.shape[0] - 2 * halo
    n_tiles = seq // ts

    def glu(start, size):
        rows = pl.ds(pl.multiple_of(start, halo), size)
        a = a_ref[rows, :].astype(F32)
        return a * _sigmoid(b_ref[rows, :].astype(F32))

    base = j * ts
    u_ref[halo:halo + ts, :] = glu(base, ts)
    lo = glu(jnp.maximum(base - halo, 0), halo)
    u_ref[0:halo, :] = jnp.where(j > 0, lo, 0.0)
    hi = glu(jnp.minimum(base + ts, seq - halo), halo)
    u_ref[halo + ts:, :] = jnp.where(j < n_tiles - 1, hi, 0.0)


def _conv_rows(s0, cw_ref, cb_ref, lg_ref, lb_ref, u_ref, h_ref):
    rb = CONV_ROWS
    first = CONV_HALO - CONV_HALF
    parts = []
    for lt in range(D_GROUP // LANES):
        lanes = slice(lt * LANES, (lt + 1) * LANES)
        y = None
        for r in range(SUBLANES_F32):
            acc = None
            for o in range(r, first + CONV_WIDTH, SUBLANES_F32):
                if o < first:
                    continue
                w = o - first
                lo_row = s0 + (o - r)
                term = u_ref[lo_row:lo_row + rb + SUBLANES_F32, lanes] * cw_ref[w:w + 1, lanes]
                acc = term if acc is None else acc + term
            shifted = acc[r:r + rb]
            y = shifted if y is None else y + shifted
        parts.append(y)
    y = jnp.concatenate(parts, axis=-1) + cb_ref[...]
    mu = jnp.mean(y, axis=-1, keepdims=True)
    yc = y - mu
    var = jnp.mean(yc * yc, axis=-1, keepdims=True)
    y = (yc * lax.rsqrt(var + EPS)) * lg_ref[...] + lb_ref[...]
    h_ref[s0:s0 + rb, :] = _silu(y).astype(BF16)
    return y[rb - SUBLANES_F32:, D_GROUP - LANES:]


def _runtime_zero(tile):
    bits = pltpu.bitcast(tile, jnp.uint32)
    bits = lax.shift_right_logical(lax.shift_right_logical(bits, jnp.uint32(16)), jnp.uint32(16))
    return bits[0, 0].astype(jnp.int32)


def _conv_pointwise(g_ref, w_ref, h_ref, o_ref):
    o = jnp.dot(h_ref[...], w_ref[...], preferred_element_type=F32)
    o_ref[...] = (o * _silu(g_ref[...].astype(F32))).astype(BF16)


def kernel(x, c, norm_g, w_ada, b_ada, w_in, w_fft, na_rel_bias, ret_logit_fwd, ret_logit_bwd,
           conv_w, conv_b, conv_ln_g, conv_ln_b, conv_w_pw, w_out, final_g):
    seq = x.shape[1]
    mod = _ada(c, w_ada, b_ada)
    fft_consts = _fft_constants(seq)
    rope_tbl = _rope_tables(seq)
    bias_tbl = _na_bias_table(na_rel_bias)
    w_fft_b, w_pw_b = w_fft.astype(BF16), conv_w_pw.astype(BF16)
    for l in range(DEPTH):
        zp = _inproj(x, norm_g, mod, w_in, l)
        o_fft = _fourier(zp, w_fft_b, fft_consts, l)
        o_na = _neighbourhood(zp, bias_tbl, l)
        o_ret = _retention(zp, ret_logit_fwd, ret_logit_bwd, rope_tbl, l)
        conv_params = (conv_w, conv_b, conv_ln_g, conv_ln_b)
        x = _outproj((o_fft, o_na, o_ret), zp, conv_params, w_pw_b, w_out, x, mod, final_g, l,
                     final=(l == DEPTH - 1))
    return x
```

```python
import functools
import math

import numpy as np
import jax
import jax.numpy as jnp
from jax import lax
from jax.experimental import pallas as pl
from jax.experimental.pallas import tpu as pltpu

F32 = jnp.float32
BF16 = jnp.bfloat16

D_MODEL = 2048
DEPTH = 2
GRID_W = 64
D_GROUP = 512
HEAD_DIM = 64
N_FFT_GROUPS = 4
FFT_GROUP_DIM = D_GROUP // N_FFT_GROUPS
NA_KH = 8
NA_KW = 16
CONV_WIDTH = 31
CONV_HALF = CONV_WIDTH // 2
ROPE_BASE = 10000.0
EPS = 1e-6
N_PIECES = 13
(P_FX, P_FG, P_NQ, P_NK, P_NV, P_NG, P_RQ, P_RK, P_RV, P_RG, P_CA, P_CB, P_CG) = range(N_PIECES)

MXU_DIM = 256
HEADS_PER_TILE = MXU_DIM // HEAD_DIM
VMEM_LIMIT = 56 * 1024 * 1024
SUBLANES_BF16 = 16
SUBLANES_F32 = 8
LANES = 128

NEG_BIG = -1e30


def _params(sem, vmem=VMEM_LIMIT):
    return pltpu.CompilerParams(dimension_semantics=sem, vmem_limit_bytes=vmem)


def _sigmoid(t):
    return 0.5 * jnp.tanh(0.5 * t) + 0.5


def _silu(t):
    h = 0.5 * t
    return h + h * jnp.tanh(h)


ADA_TN = 768
ADA_ROWS = 8
(MOD_SHIFT, MOD_SCALE, MOD_GATE) = range(3)


def _ada_kernel(c_ref, w_ref, b_ref, o_ref):
    c = c_ref[...]
    pad = jnp.zeros((ADA_ROWS - c.shape[0], D_MODEL), F32)
    ca = _silu(jnp.concatenate([c, pad], axis=0)).astype(BF16)
    w = w_ref[...].astype(BF16)
    bias = b_ref[pl.ds(pl.program_id(0), 1), :]
    o_ref[...] = jnp.dot(ca, w, preferred_element_type=F32) + bias


def _ada(c, w_ada, b_ada):
    batch = c.shape[0]
    n3 = 3 * D_MODEL
    return pl.pallas_call(
        _ada_kernel,
        name="ada",
        grid=(DEPTH, n3 // ADA_TN),
        in_specs=[
            pl.BlockSpec((batch, D_MODEL), lambda l, j: (0, 0)),
            pl.BlockSpec((None, D_MODEL, ADA_TN), lambda l, j: (l, 0, j)),
            pl.BlockSpec((DEPTH, ADA_TN), lambda l, j: (0, j)),
        ],
        out_specs=pl.BlockSpec((None, ADA_ROWS, ADA_TN), lambda l, j: (l, 0, j)),
        out_shape=jax.ShapeDtypeStruct((DEPTH, ADA_ROWS, n3), F32),
        compiler_params=_params(("arbitrary", "arbitrary")),
    )(c, w_ada, b_ada)


INPROJ_TM = 512
INPROJ_WROWS = SUBLANES_BF16
INPROJ_WSLOTS = 8
INPROJ_VMEM = 60 * 1024 * 1024


def _inproj_kernel(x_ref, g_ref, sc_ref, sh_ref, w_hbm, o_ref, h0_ref, h1_ref, w_ref, stage_ref, sem,
                   *, n_tiles, per_batch, layer):
    t = pl.program_id(0)
    wrows = stage_ref.shape[1]
    n_rchunks = D_MODEL // wrows

    def weight_copy(r, slot):
        rows = pl.ds(pl.multiple_of(r * wrows, wrows), wrows)
        return pltpu.make_async_copy(w_hbm.at[layer, rows, :], stage_ref.at[slot], sem.at[slot])

    def load_weight():
        n_slots = stage_ref.shape[0]
        for slot in range(n_slots):
            weight_copy(slot, slot).start()

        def group(i, carry):
            for slot in range(n_slots):
                r = n_slots * i + slot
                weight_copy(r, slot).wait()
                rows = pl.ds(pl.multiple_of(r * wrows, wrows), wrows)
                w_ref[rows, :] = stage_ref[slot].astype(BF16)

                @pl.when(r + n_slots < n_rchunks)
                def _():
                    weight_copy(r + n_slots, slot).start()
            return carry

        lax.fori_loop(0, n_rchunks // n_slots, group, 0)

    def norm(h_ref):
        x = x_ref[...]
        row = pl.ds(jnp.minimum(t, n_tiles - 1) // per_batch, 1)
        ms = jnp.mean(x * x, axis=-1, keepdims=True)
        gain = g_ref[layer:layer + 1, :] * (1.0 + sc_ref[row, :])
        h = (x * lax.rsqrt(ms + EPS)) * gain + sh_ref[row, :]
        h_ref[...] = h.astype(BF16)

    def project(h_ref):
        h = h_ref[...]
        for p in range(N_PIECES):
            w = w_ref[:, p * D_GROUP:(p + 1) * D_GROUP]
            o_ref[p] = jnp.dot(h, w, preferred_element_type=F32).astype(BF16)

    odd = t % 2 == 1

    @pl.when(t == 0)
    def _():
        load_weight()
        norm(h0_ref)

    @pl.when(odd)
    def _():
        project(h0_ref)
        norm(h1_ref)

    @pl.when(jnp.logical_and(jnp.logical_not(odd), jnp.logical_and(t > 0, t < n_tiles)))
    def _():
        project(h1_ref)
        norm(h0_ref)

    @pl.when(t == n_tiles)
    def _():
        project(h1_ref)


def _inproj(x, norm_g, mod, w_in, layer):
    batch, seq, _ = x.shape
    tm = INPROJ_TM
    per_batch = seq // tm
    n_tiles = batch * per_batch
    assert n_tiles % 2 == 0 and (D_MODEL // INPROJ_WROWS) % INPROJ_WSLOTS == 0
    norm_tile = lambda t: jnp.minimum(t, n_tiles - 1)
    proj_tile = lambda t: jnp.maximum(t - 1, 0)
    return pl.pallas_call(
        functools.partial(_inproj_kernel, n_tiles=n_tiles, per_batch=per_batch, layer=layer),
        name="inproj",
        grid=(n_tiles + 1,),
        in_specs=[
            pl.BlockSpec((None, tm, D_MODEL),
                         lambda t: (norm_tile(t) // per_batch, norm_tile(t) % per_batch, 0)),
            pl.BlockSpec((DEPTH, D_MODEL), lambda t: (0, 0)),
            pl.BlockSpec((None, ADA_ROWS, D_MODEL), lambda t: (layer, 0, MOD_SCALE)),
            pl.BlockSpec((None, ADA_ROWS, D_MODEL), lambda t: (layer, 0, MOD_SHIFT)),
            pl.BlockSpec(memory_space=pl.ANY),
        ],
        out_specs=pl.BlockSpec((N_PIECES, None, tm, D_GROUP),
                               lambda t: (0, proj_tile(t) // per_batch, proj_tile(t) % per_batch, 0)),
        out_shape=jax.ShapeDtypeStruct((N_PIECES, batch, seq, D_GROUP), BF16),
        scratch_shapes=[
            pltpu.VMEM((tm, D_MODEL), BF16), pltpu.VMEM((tm, D_MODEL), BF16),
            pltpu.VMEM((D_MODEL, N_PIECES * D_GROUP), BF16),
            pltpu.VMEM((INPROJ_WSLOTS, INPROJ_WROWS, N_PIECES * D_GROUP), F32),
            pltpu.SemaphoreType.DMA((INPROJ_WSLOTS,)),
        ],
        compiler_params=_params(("arbitrary",), INPROJ_VMEM),
    )(x, norm_g, mod, mod, w_in)


OUTPROJ_TM = 512
OUTPROJ_WROWS = 64
OUTPROJ_WSLOTS = 8
OUTPROJ_VMEM = 60 * 1024 * 1024


def _outproj_kernel(a0_ref, a1_ref, a2_ref, w_hbm, x_ref, gate_ref, fg_ref,
                    ca_ref, cb_ref, cg_ref, cw_ref, cbias_ref, lg_ref, lb_ref, wpw_ref,
                    o_ref, u_ref, h_ref, cv0_ref, cv1_ref, w_ref, stage_ref, sem,
                    *, final, n_tiles, per_batch, seq, layer):
    t = pl.program_id(0)
    j = jnp.minimum(t, n_tiles - 1) % per_batch
    gate_row = pl.ds(jnp.maximum(t - 1, 0) // per_batch, 1)
    layer_row = pl.ds(layer, 1)

    tm = o_ref.shape[0]
    n_chunks = tm // CONV_ROWS
    tn = D_MODEL // n_chunks
    n_slots = stage_ref.shape[0]

    wrows = stage_ref.shape[1]

    def weight_copy(r):
        slot = r % n_slots
        return pltpu.make_async_copy(w_hbm.at[layer, pl.ds(r * wrows, wrows), :], stage_ref.at[slot],
                                     sem.at[slot])

    def run(proj_cv_ref, conv_cv_ref, load_weight=False):
        if load_weight:
            for c in range(n_slots):
                weight_copy(c).start()
        if conv_cv_ref is not None:
            _conv_glu(j, ca_ref, cb_ref, u_ref, seq=seq)
        if proj_cv_ref is not None:
            a = jnp.concatenate([a0_ref[...], a1_ref[...], a2_ref[...], proj_cv_ref[...]], axis=-1)
            ss = jnp.zeros((tm, LANES), F32)
        zero = 0
        for i in range(n_chunks):
            if proj_cv_ref is not None:
                y = jnp.dot(a, w_ref[i + zero], preferred_element_type=F32)
                xn = x_ref[:, i * tn:(i + 1) * tn] + gate_ref[gate_row, i * tn:(i + 1) * tn] * y
                if final:
                    sq = xn * xn
                    for lt in range(tn // LANES):
                        ss = ss + sq[:, lt * LANES:(lt + 1) * LANES]
                o_ref[:, i * tn:(i + 1) * tn] = xn
            if conv_cv_ref is not None:
                tail = _conv_rows(i * CONV_ROWS, cw_ref, cbias_ref.at[layer_row], lg_ref.at[layer_row],
                                  lb_ref.at[layer_row], u_ref, h_ref)
                zero = _runtime_zero(tail)
            if load_weight:
                n_rchunks = D_MODEL // wrows
                per_block = n_rchunks // n_chunks
                for r in range(i * per_block, (i + 1) * per_block):
                    weight_copy(r).wait()
                    for c in range(n_chunks):
                        w_ref[c, r * wrows:(r + 1) * wrows, :] = stage_ref[
                            r % n_slots, :, c * tn:(c + 1) * tn].astype(BF16)
                    if r + n_slots < n_rchunks:
                        weight_copy(r + n_slots).start()
        if conv_cv_ref is not None:
            _conv_pointwise(cg_ref, wpw_ref, h_ref, conv_cv_ref)
        if proj_cv_ref is not None and final:
            ms = jnp.sum(ss, axis=-1, keepdims=True) * (1.0 / D_MODEL)
            o_ref[...] = (o_ref[...] * lax.rsqrt(ms + EPS)) * fg_ref[...]

    odd = t % 2 == 1

    @pl.when(t == 0)
    def _():
        run(None, cv0_ref, load_weight=True)

    @pl.when(odd)
    def _():
        run(cv0_ref, cv1_ref)

    @pl.when(jnp.logical_and(jnp.logical_not(odd), jnp.logical_and(t > 0, t < n_tiles)))
    def _():
        run(cv1_ref, cv0_ref)

    @pl.when(t == n_tiles)
    def _():
        run(cv1_ref, None)


def _outproj(mixed, zp, conv_params, w_pw_bf16, w_out, x, mod, final_g, layer, final):
    batch, seq, _ = x.shape
    tm = OUTPROJ_TM
    per_batch = seq // tm
    n_tiles = batch * per_batch
    n_wchunks = tm // CONV_ROWS
    assert n_tiles % 2 == 0
    conv_tile = lambda t: jnp.minimum(t, n_tiles - 1)
    proj_tile = lambda t: jnp.maximum(t - 1, 0)
    a_spec = pl.BlockSpec((None, tm, D_GROUP),
                          lambda t: (proj_tile(t) // per_batch, proj_tile(t) % per_batch, 0))
    x_spec = pl.BlockSpec((None, tm, D_MODEL),
                          lambda t: (proj_tile(t) // per_batch, proj_tile(t) % per_batch, 0))
    seq_spec = lambda p: pl.BlockSpec((None, None, seq, D_GROUP),
                                      lambda t: (p, conv_tile(t) // per_batch, 0, 0))
    vec = pl.BlockSpec((DEPTH, D_GROUP), lambda t: (0, 0))
    conv_w, conv_b, ln_g, ln_b = conv_params
    return pl.pallas_call(
        functools.partial(_outproj_kernel, final=final, n_tiles=n_tiles, per_batch=per_batch, seq=seq,
                          layer=layer),
        name="outproj",
        grid=(n_tiles + 1,),
        in_specs=[
            a_spec, a_spec, a_spec,
            pl.BlockSpec(memory_space=pl.ANY),
            x_spec,
            pl.BlockSpec((None, ADA_ROWS, D_MODEL), lambda t: (layer, 0, MOD_GATE)),
            pl.BlockSpec((1, D_MODEL), lambda t: (0, 0)),
            seq_spec(P_CA), seq_spec(P_CB),
            pl.BlockSpec((None, None, tm, D_GROUP),
                         lambda t: (P_CG, conv_tile(t) // per_batch, conv_tile(t) % per_batch, 0)),
            pl.BlockSpec((None, CONV_WIDTH, D_GROUP), lambda t: (layer, 0, 0)),
            vec, vec, vec,
            pl.BlockSpec((None, D_GROUP, D_GROUP), lambda t: (layer, 0, 0)),
        ],
        out_specs=x_spec,
        out_shape=jax.ShapeDtypeStruct((batch, seq, D_MODEL), F32),
        scratch_shapes=[
            pltpu.VMEM((tm + 2 * CONV_HALO, D_GROUP), F32),
            pltpu.VMEM((tm, D_GROUP), BF16),
            pltpu.VMEM((tm, D_GROUP), BF16), pltpu.VMEM((tm, D_GROUP), BF16),
            pltpu.VMEM((n_wchunks, D_MODEL, D_MODEL // n_wchunks), BF16),
            pltpu.VMEM((OUTPROJ_WSLOTS, OUTPROJ_WROWS, D_MODEL), F32),
            pltpu.SemaphoreType.DMA((OUTPROJ_WSLOTS,)),
        ],
        compiler_params=_params(("arbitrary",), OUTPROJ_VMEM),
    )(*mixed, w_out, x, mod, final_g.reshape(1, D_MODEL), zp, zp, zp, conv_w,
      conv_b, ln_g, ln_b, w_pw_bf16)


FFT_BLOCK = 32
FFT_LANES = 2 * FFT_GROUP_DIM


def _fft_constants(seq):
    rows = seq // GRID_W
    assert rows == GRID_W
    n = np.arange(GRID_W)
    ang1 = 2.0 * np.pi * ((n[:, None] * n[None, :]) % GRID_W) / GRID_W
    f1 = np.concatenate([np.cos(ang1), -np.sin(ang1)], axis=0)
    k1 = n[:, None, None]
    k2 = n[None, :, None]
    s2 = n[None, None, :]
    ang2 = 2.0 * np.pi * ((s2 * (k1 + GRID_W * k2)) % seq) / seq
    mr, mi = np.cos(ang2), -np.sin(ang2)
    m2 = np.concatenate([np.concatenate([mr, -mi], axis=2),
                         np.concatenate([mi, mr], axis=2)], axis=1)
    c = np.arange(FFT_GROUP_DIM)
    angc = 2.0 * np.pi * ((c[:, None] * c[None, :]) % FFT_GROUP_DIM) / FFT_GROUP_DIM
    fc = np.concatenate([np.cos(angc), np.sin(angc)], axis=0)
    return tuple(jnp.asarray(t, F32).astype(BF16) for t in (f1, m2, fc))


def _fft_kernel(u_ref, gate_ref, f1_ref, m2_ref, fc_ref, w_ref, o_ref, ut_ref, g_ref, gt_ref, y_ref, *, norm):
    pair = pl.program_id(1)
    n = GRID_W
    blk = FFT_BLOCK
    sub = SUBLANES_F32
    lanes = ut_ref.shape[-1]
    groups = lanes // FFT_GROUP_DIM

    u3 = u_ref[...].astype(F32).reshape(n, n, lanes)
    ut_ref[...] = jnp.swapaxes(u3, 0, 1).astype(BF16)

    def stage1(i, carry):
        for j in range(blk):
            s2 = i * blk + j
            g_ref[s2] = jnp.dot(f1_ref[...], ut_ref[s2], preferred_element_type=F32)
        return carry

    lax.fori_loop(0, n // blk, stage1, 0)

    for q in range(2 * n // sub):
        part = g_ref[:, q * sub:(q + 1) * sub, :]
        gt_ref[q * sub:(q + 1) * sub] = jnp.swapaxes(part, 0, 1).astype(BF16)

    def stage2(i, carry):
        xs = []
        for j in range(blk):
            k1 = i * blk + j
            gk = jnp.concatenate([gt_ref[k1], gt_ref[n + k1]], axis=0)
            xs.append(jnp.dot(m2_ref[k1], gk, preferred_element_type=F32))
        for g in range(groups):
            sl = slice(g * FFT_GROUP_DIM, (g + 1) * FFT_GROUP_DIM)
            lhs = jnp.concatenate(
                [jnp.concatenate([xk[:n, sl], xk[n:, sl]], axis=-1) for xk in xs], axis=0)
            y = jnp.dot(lhs.astype(BF16), fc_ref[...], preferred_element_type=F32) * norm
            y_ref[pair * groups + g, pl.ds(pl.multiple_of(i * blk, blk), blk)] = (
                y.reshape(blk, n, FFT_GROUP_DIM))
        return carry

    lax.fori_loop(0, n // blk, stage2, 0)

    @pl.when(pair == pl.num_programs(1) - 1)
    def _():
        def stage3(i, carry):
            k2 = pl.ds(pl.multiple_of(i * sub, sub), sub)
            tiles = []
            for g in range(N_FFT_GROUPS):
                part = y_ref[g, :, k2, :]
                tiles.append(jnp.swapaxes(part, 0, 1).reshape(sub * n, FFT_GROUP_DIM))
            y = jnp.concatenate(tiles, axis=-1).astype(BF16)
            o = jnp.dot(y, w_ref[...], preferred_element_type=F32)
            rows = pl.ds(pl.multiple_of(i * sub * n, sub * n), sub * n)
            o_ref[rows, :] = (o * _silu(gate_ref[rows, :].astype(F32))).astype(BF16)
            return carry

        lax.fori_loop(0, n // sub, stage3, 0, unroll=2)


def _fourier(zp, w_fft_bf16, consts, layer):
    _, batch, seq, _ = zp.shape
    f1, m2, fc = consts
    n = GRID_W
    const = lambda shape: pl.BlockSpec(shape, lambda b, g: (0,) * len(shape))
    return pl.pallas_call(
        functools.partial(_fft_kernel, norm=1.0 / math.sqrt(seq * FFT_GROUP_DIM)),
        name="fourier",
        grid=(batch, D_GROUP // FFT_LANES),
        in_specs=[
            pl.BlockSpec((None, None, seq, FFT_LANES), lambda b, g: (P_FX, b, 0, g)),
            pl.BlockSpec((None, None, seq, D_GROUP), lambda b, g: (P_FG, b, 0, 0)),
            const((2 * n, n)), const((n, 2 * n, 2 * n)), const((2 * FFT_GROUP_DIM, FFT_GROUP_DIM)),
            pl.BlockSpec((None, D_GROUP, D_GROUP), lambda b, g: (layer, 0, 0)),
        ],
        out_specs=pl.BlockSpec((None, seq, D_GROUP), lambda b, g: (b, 0, 0)),
        out_shape=jax.ShapeDtypeStruct((batch, seq, D_GROUP), BF16),
        scratch_shapes=[
            pltpu.VMEM((n, n, FFT_LANES), BF16),
            pltpu.VMEM((n, 2 * n, FFT_LANES), F32),
            pltpu.VMEM((2 * n, n, FFT_LANES), BF16),
            pltpu.VMEM((N_FFT_GROUPS, n, n, FFT_GROUP_DIM), F32),
        ],
        compiler_params=_params(("arbitrary", "arbitrary")),
    )(zp, zp, f1, m2, fc, w_fft_bf16)


NA_ROWS_PER_STEP = 16


def _na_bias_table(rel_bias):
    n_dc = 2 * NA_KW - 1
    col = np.arange(GRID_W)
    col_start = np.clip(col - NA_KW // 2, 0, GRID_W - NA_KW)
    rel_c = col[None, :] - col_start[:, None]
    col_in = (rel_c >= 0) & (rel_c < NA_KW)
    dc = np.clip(col[None, :] - col[:, None] + NA_KW - 1, 0, n_dc - 1)
    onehot = (dc[None] == np.arange(n_dc)[:, None, None]).astype(np.float32)
    zeros = np.zeros_like(onehot)
    onehot2 = np.concatenate([np.concatenate([onehot, zeros], axis=-1),
                              np.concatenate([zeros, onehot], axis=-1)], axis=0)
    rb = rel_bias.astype(F32)
    rb2 = jnp.concatenate([rb[:, :, :-1], rb[:, :, 1:]], axis=-1)
    tbl = jnp.einsum('lhdj,jqk->ldhqk', rb2, jnp.asarray(onehot2),
                     precision=lax.Precision.HIGHEST)
    col_in2 = np.concatenate([col_in, col_in], axis=-1)
    return jnp.where(col_in2[None, None, None], tbl, NEG_BIG)


def _head_select_mask():
    r = lax.broadcasted_iota(jnp.int32, (MXU_DIM, MXU_DIM), 0) // HEAD_DIM
    c = lax.broadcasted_iota(jnp.int32, (MXU_DIM, MXU_DIM), 1) // HEAD_DIM
    return r == c


def _stack_heads(t):
    n = t.shape[0]
    reps = jnp.concatenate([t] * HEADS_PER_TILE, axis=0)
    row_head = lax.broadcasted_iota(jnp.int32, reps.shape, 0) // n
    lane_head = lax.broadcasted_iota(jnp.int32, reps.shape, 1) // HEAD_DIM
    return jnp.where(row_head == lane_head, reps, jnp.zeros_like(reps))


def _unstack_heads(t4, n):
    lane_head = lax.broadcasted_iota(jnp.int32, (n, MXU_DIM), 1) // HEAD_DIM
    out = jnp.zeros((n, MXU_DIM), t4.dtype)
    for h in range(HEADS_PER_TILE):
        out = jnp.where(lane_head == h, t4[h * n:(h + 1) * n], out)
    return out


def _na_kernel(q_ref, k_ref, v_ref, g_ref, bias_ref, o_ref, *, rows):
    kh = min(NA_KH, rows)
    step = pl.program_id(1)

    def row_body(rr, carry):
        r = step * NA_ROWS_PER_STEP + rr
        r_start = jnp.clip(r - kh // 2, 0, rows - kh)
        variant = r - r_start
        q_off = pl.multiple_of(rr * GRID_W, GRID_W)
        k_off = pl.multiple_of(r_start * GRID_W, GRID_W)
        for cg in range(D_GROUP // MXU_DIM):
            lanes = slice(cg * MXU_DIM, (cg + 1) * MXU_DIM)
            q = q_ref[pl.ds(q_off, GRID_W), lanes] * (HEAD_DIM ** -0.5)
            kw = k_ref[pl.ds(k_off, kh * GRID_W), lanes]
            vw = v_ref[pl.ds(k_off, kh * GRID_W), lanes]
            q4 = _stack_heads(q)
            s = lax.dot_general(q4, kw, (((1,), (1,)), ((), ())), preferred_element_type=F32)
            heads = slice(cg * HEADS_PER_TILE, (cg + 1) * HEADS_PER_TILE)
            bias = jnp.concatenate(
                [bias_ref[NA_KH - 1 - variant + a, heads].reshape(HEADS_PER_TILE * GRID_W, 2 * GRID_W)
                 for a in range(0, kh, 2)], axis=-1)
            s = s + bias
            m = jnp.max(s, axis=-1, keepdims=True)
            e = jnp.exp(s - m)
            l = jnp.sum(e, axis=-1, keepdims=True)
            o4 = jnp.dot(e.astype(BF16), vw, preferred_element_type=F32) / l
            o = _unstack_heads(o4, GRID_W)
            gate = g_ref[pl.ds(q_off, GRID_W), lanes].astype(F32)
            o_ref[pl.ds(q_off, GRID_W), lanes] = (o * _silu(gate)).astype(BF16)
        return carry

    lax.fori_loop(0, NA_ROWS_PER_STEP, row_body, 0, unroll=True)


def _neighbourhood(zp, bias_tbl, layer):
    _, batch, seq, _ = zp.shape
    rows = seq // GRID_W
    tq = NA_ROWS_PER_STEP * GRID_W
    q_spec = lambda p: pl.BlockSpec((None, None, tq, D_GROUP), lambda b, i: (p, b, i, 0))
    kv_spec = lambda p: pl.BlockSpec((None, None, seq, D_GROUP), lambda b, i: (p, b, 0, 0))
    return pl.pallas_call(
        functools.partial(_na_kernel, rows=rows),
        name="natten",
        grid=(batch, rows // NA_ROWS_PER_STEP),
        in_specs=[
            q_spec(P_NQ), kv_spec(P_NK), kv_spec(P_NV), q_spec(P_NG),
            pl.BlockSpec((None,) + bias_tbl.shape[1:], lambda b, i: (layer, 0, 0, 0, 0),
                         pipeline_mode=pl.Buffered(1)),
        ],
        out_specs=pl.BlockSpec((None, tq, D_GROUP), lambda b, i: (b, i, 0)),
        out_shape=jax.ShapeDtypeStruct((batch, seq, D_GROUP), BF16),
        compiler_params=_params(("arbitrary", "arbitrary")),
    )(zp, zp, zp, zp, bias_tbl)


RET_CHUNK = MXU_DIM


def _log_sigmoid(t):
    return jnp.minimum(t, 0.0) - jnp.log1p(jnp.exp(-jnp.abs(t)))


def _rope_tables(seq):
    half = HEAD_DIM // 2
    inv = ROPE_BASE ** (-jnp.arange(half, dtype=F32) / half)
    ang = jnp.arange(seq, dtype=F32)[:, None] * inv[None, :]
    cos, sin = jnp.cos(ang), jnp.sin(ang)
    cos2 = jnp.concatenate([cos, cos], axis=-1)
    sin2 = jnp.concatenate([-sin, sin], axis=-1)
    return jnp.tile(cos2, (1, 2)), jnp.tile(sin2, (1, 2))


def _ret_kernel(lf_s_ref, lb_s_ref, q_ref, k_ref, v_ref, g_ref, cos_ref, sin_ref,
                o_ref, qr_ref, kr_ref, ob_ref, dmat_ref, dec_ref, state_ref, ones_ref, *, seq, layer):
    C = RET_CHUNK
    n_chunks = seq // C
    cg = pl.program_id(1)
    ones_bd = jnp.where(_head_select_mask(), 1.0, 0.0).astype(BF16)
    ones_ref[...] = jnp.concatenate([ones_bd, ones_bd], axis=0)

    ri = lax.broadcasted_iota(jnp.int32, (C, C), 0)
    ci = lax.broadcasted_iota(jnp.int32, (C, C), 1)
    diff = (ri - ci).astype(F32)
    lane_head = lax.broadcasted_iota(jnp.int32, (1, MXU_DIM), 1) // HEAD_DIM
    lfl = jnp.zeros((1, MXU_DIM), F32)
    lbl = jnp.zeros((1, MXU_DIM), F32)
    for hh in range(HEADS_PER_TILE):
        h = cg * HEADS_PER_TILE + hh
        lf = _log_sigmoid(jnp.full((C, C), lf_s_ref[layer, h], F32))
        lb = _log_sigmoid(jnp.full((C, C), lb_s_ref[layer, h], F32))
        dmat_ref[hh] = jnp.where(diff >= 0, jnp.exp(lf * diff), jnp.exp(lb * (-diff)))
        lfl = jnp.where(lane_head == hh, lf_s_ref[layer, h], lfl)
        lbl = jnp.where(lane_head == hh, lb_s_ref[layer, h], lbl)
    lfl = _log_sigmoid(lfl)
    lbl = _log_sigmoid(lbl)
    idx = lax.broadcasted_iota(jnp.int32, (C, MXU_DIM), 0).astype(F32)
    dec_ref[0] = jnp.exp(lfl * (idx + 1.0))
    dec_ref[1] = jnp.exp(lfl * (C - 1.0 - idx))
    dec_ref[2] = jnp.exp(lbl * (C - idx))
    dec_ref[3] = jnp.exp(lbl * idx)
    cd_f = jnp.exp(lfl * float(C))
    cd_b = jnp.exp(lbl * float(C))

    def rope(t, rows):
        lane = lax.broadcasted_iota(jnp.int32, t.shape, 1)
        first_half = (lane % HEAD_DIM) < (HEAD_DIM // 2)
        cs = cos_ref[rows, :]
        sn = sin_ref[rows, :]
        cs = jnp.concatenate([cs, cs], axis=-1)
        sn = jnp.concatenate([sn, sn], axis=-1)
        swapped = jnp.where(first_half, pltpu.roll(t, MXU_DIM - HEAD_DIM // 2, 1),
                            pltpu.roll(t, HEAD_DIM // 2, 1))
        return t * cs + swapped * sn

    def kv_update(k, v, kd, cd):
        kv = lax.dot_general((k * kd).astype(BF16), v, (((0,), (0,)), ((), ())),
                             preferred_element_type=F32)
        state_ref[...] = cd * state_ref[...] + jnp.where(_head_select_mask(), kv, 0.0)

    state_ref[...] = jnp.zeros_like(state_ref)

    def bwd_body(i, carry):
        n = n_chunks - 1 - i
        rows = pl.ds(pl.multiple_of(n * C, C), C)
        q = rope(q_ref[rows, :].astype(F32), rows) * (HEAD_DIM ** -0.5)
        k = rope(k_ref[rows, :].astype(F32), rows)
        qr_ref[rows, :] = q.astype(BF16)
        kr_ref[rows, :] = k.astype(BF16)
        ob_ref[rows, :] = jnp.dot((q * dec_ref[2]).astype(BF16), state_ref[...].astype(BF16),
                                  preferred_element_type=F32)
        kv_update(k, v_ref[rows, :], dec_ref[3], cd_b)
        return carry

    lax.fori_loop(0, n_chunks, bwd_body, 0, unroll=8)

    state_ref[...] = jnp.zeros_like(state_ref)

    def fwd_body(n, carry):
        rows = pl.ds(pl.multiple_of(n * C, C), C)
        qb = qr_ref[rows, :]
        kb = kr_ref[rows, :]
        v = v_ref[rows, :]
        o_cross = jnp.dot((qb.astype(F32) * dec_ref[0]).astype(BF16), state_ref[...].astype(BF16),
                          preferred_element_type=F32)
        q4 = _stack_heads(qb)
        sc = lax.dot_general(q4, kb, (((1,), (1,)), ((), ())), preferred_element_type=F32)
        sc = sc * dmat_ref[...].reshape(HEADS_PER_TILE * C, C)
        o4 = jnp.dot(sc.astype(BF16), v, preferred_element_type=F32)
        o = _unstack_heads(o4, C) + o_cross + ob_ref[rows, :]
        sq = o * o
        hi = sq.astype(BF16)
        lo = (sq - hi.astype(F32)).astype(BF16)
        ss = jnp.dot(jnp.concatenate([hi, lo], axis=-1), ones_ref[...], preferred_element_type=F32)
        inv = lax.rsqrt(ss * (1.0 / HEAD_DIM) + EPS)
        gate = g_ref[rows, :].astype(F32)
        o_ref[rows, :] = (o * inv * _silu(gate)).astype(BF16)
        kv_update(kb.astype(F32), v, dec_ref[1], cd_f)
        return carry

    lax.fori_loop(0, n_chunks, fwd_body, 0, unroll=8)


def _retention(zp, logit_f, logit_b, rope_tbl, layer):
    _, batch, seq, _ = zp.shape
    cos_t, sin_t = rope_tbl
    n_cg = D_GROUP // MXU_DIM
    spec = lambda p: pl.BlockSpec((None, None, seq, MXU_DIM), lambda b, c, *_: (p, b, 0, c))
    tbl_spec = pl.BlockSpec((seq, 2 * HEAD_DIM), lambda b, c, *_: (0, 0))
    return pl.pallas_call(
        functools.partial(_ret_kernel, seq=seq, layer=layer),
        name="retention",
        grid_spec=pltpu.PrefetchScalarGridSpec(
            num_scalar_prefetch=2,
            grid=(batch, n_cg),
            in_specs=[spec(P_RQ), spec(P_RK), spec(P_RV), spec(P_RG), tbl_spec, tbl_spec],
            out_specs=pl.BlockSpec((None, seq, MXU_DIM), lambda b, c, *_: (b, 0, c)),
            scratch_shapes=[
                pltpu.VMEM((seq, MXU_DIM), BF16),
                pltpu.VMEM((seq, MXU_DIM), BF16),
                pltpu.VMEM((seq, MXU_DIM), F32),
                pltpu.VMEM((HEADS_PER_TILE, RET_CHUNK, RET_CHUNK), F32),
                pltpu.VMEM((4, RET_CHUNK, MXU_DIM), F32),
                pltpu.VMEM((MXU_DIM, MXU_DIM), F32),
                pltpu.VMEM((2 * MXU_DIM, MXU_DIM), BF16),
            ],
        ),
        out_shape=jax.ShapeDtypeStruct((batch, seq, D_GROUP), BF16),
        compiler_params=_params(("arbitrary", "arbitrary")),
    )(logit_f, logit_b, zp, zp, zp, zp, cos_t, sin_t)


CONV_HALO = SUBLANES_BF16
CONV_ROWS = 128


def _conv_glu(j, a_ref, b_ref, u_ref, *, seq):
    halo = CONV_HALO
    ts = u_ref.shape[0] - 2 * halo
    n_tiles = seq // ts

    def glu(start, size):
        rows = pl.ds(pl.multiple_of(start, halo), size)
        a = a_ref[rows, :].astype(F32)
        return a * _sigmoid(b_ref[rows, :].astype(F32))

    base = j * ts
    u_ref[halo:halo + ts, :] = glu(base, ts)
    lo = glu(jnp.maximum(base - halo, 0), halo)
    u_ref[0:halo, :] = jnp.where(j > 0, lo, 0.0)
    hi = glu(jnp.minimum(base + ts, seq - halo), halo)
    u_ref[halo + ts:, :] = jnp.where(j < n_tiles - 1, hi, 0.0)


def _conv_rows(s0, cw_ref, cb_ref, lg_ref, lb_ref, u_ref, h_ref):
    rb = CONV_ROWS
    first = CONV_HALO - CONV_HALF
    parts = []
    for lt in range(D_GROUP // LANES):
        lanes = slice(lt * LANES, (lt + 1) * LANES)
        y = None
        for r in range(SUBLANES_F32):
            acc = None
            for o in range(r, first + CONV_WIDTH, SUBLANES_F32):
                if o < first:
                    continue
                w = o - first
                lo_row = s0 + (o - r)
                term = u_ref[lo_row:lo_row + rb + SUBLANES_F32, lanes] * cw_ref[w:w + 1, lanes]
                acc = term if acc is None else acc + term
            shifted = acc[r:r + rb]
            y = shifted if y is None else y + shifted
        parts.append(y)
    y = jnp.concatenate(parts, axis=-1) + cb_ref[...]
    mu = jnp.mean(y, axis=-1, keepdims=True)
    yc = y - mu
    var = jnp.mean(yc * yc, axis=-1, keepdims=True)
    y = (yc * lax.rsqrt(var + EPS)) * lg_ref[...] + lb_ref[...]
    h_ref[s0:s0 + rb, :] = _silu(y).astype(BF16)
    return y[rb - SUBLANES_F32:, D_GROUP - LANES:]


def _runtime_zero(tile):
    bits = pltpu.bitcast(tile, jnp.uint32)
    bits = lax.shift_right_logical(lax.shift_right_logical(bits, jnp.uint32(16)), jnp.uint32(16))
    return bits[0, 0].astype(jnp.int32)


def _conv_pointwise(g_ref, w_ref, h_ref, o_ref):
    o = jnp.dot(h_ref[...], w_ref[...], preferred_element_type=F32)
    o_ref[...] = (o * _silu(g_ref[...].astype(F32))).astype(BF16)


def kernel(x, c, norm_g, w_ada, b_ada, w_in, w_fft, na_rel_bias, ret_logit_fwd, ret_logit_bwd,
           conv_w, conv_b, conv_ln_g, conv_ln_b, conv_w_pw, w_out, final_g):
    seq = x.shape[1]
    mod = _ada(c, w_ada, b_ada)
    fft_consts = _fft_constants(seq)
    rope_tbl = _rope_tables(seq)
    bias_tbl = _na_bias_table(na_rel_bias)
    w_fft_b, w_pw_b = w_fft.astype(BF16), conv_w_pw.astype(BF16)
    for l in range(DEPTH):
        zp = _inproj(x, norm_g, mod, w_in, l)
        o_fft = _fourier(zp, w_fft_b, fft_consts, l)
        o_na = _neighbourhood(zp, bias_tbl, l)
        o_ret = _retention(zp, ret_logit_fwd, ret_logit_bwd, rope_tbl, l)
        conv_params = (conv_w, conv_b, conv_ln_g, conv_ln_b)
        x = _outproj((o_fft, o_na, o_ret), zp, conv_params, w_pw_b, w_out, x, mod, final_g, l,
                     final=(l == DEPTH - 1))
    return x
```

```python
import functools
import math

import numpy as np
import jax
import jax.numpy as jnp
from jax import lax
from jax.experimental import pallas as pl
from jax.experimental.pallas import tpu as pltpu

F32 = jnp.float32
BF16 = jnp.bfloat16

D_MODEL = 2048
DEPTH = 2
GRID_W = 64
D_GROUP = 512
HEAD_DIM = 64
N_HEADS = D_GROUP // HEAD_DIM
N_FFT_GROUPS = 4
FFT_GROUP_DIM = D_GROUP // N_FFT_GROUPS
NA_KH = 8
NA_KW = 16
CONV_WIDTH = 31
CONV_HALF = CONV_WIDTH // 2
ROPE_BASE = 10000.0
EPS = 1e-6
N_PIECES = 13
(P_FX, P_FG, P_NQ, P_NK, P_NV, P_NG, P_RQ, P_RK, P_RV, P_RG, P_CA, P_CB, P_CG) = range(N_PIECES)

MXU_DIM = 256
HEADS_PER_TILE = MXU_DIM // HEAD_DIM
VMEM_LIMIT = 56 * 1024 * 1024
SUBLANES_BF16 = 16
SUBLANES_F32 = 8
LANES = 128

NEG_BIG = -1e30


def _params(sem, vmem=VMEM_LIMIT):
    return pltpu.CompilerParams(dimension_semantics=sem, vmem_limit_bytes=vmem)


def _sigmoid(t):
    return 0.5 * jnp.tanh(0.5 * t) + 0.5


def _silu(t):
    h = 0.5 * t
    return h + h * jnp.tanh(h)


ADA_TN = 768
ADA_ROWS = 8
(MOD_SHIFT, MOD_SCALE, MOD_GATE) = range(3)


def _ada_kernel(c_ref, w_ref, b_ref, o_ref):
    c = c_ref[...]
    pad = jnp.zeros((ADA_ROWS - c.shape[0], D_MODEL), F32)
    ca = _silu(jnp.concatenate([c, pad], axis=0)).astype(BF16)
    w = w_ref[...].astype(BF16)
    bias = b_ref[pl.ds(pl.program_id(0), 1), :]
    o_ref[...] = jnp.dot(ca, w, preferred_element_type=F32) + bias


def _ada(c, w_ada, b_ada):
    batch = c.shape[0]
    n3 = 3 * D_MODEL
    return pl.pallas_call(
        _ada_kernel,
        name="ada",
        grid=(DEPTH, n3 // ADA_TN),
        in_specs=[
            pl.BlockSpec((batch, D_MODEL), lambda l, j: (0, 0)),
            pl.BlockSpec((None, D_MODEL, ADA_TN), lambda l, j: (l, 0, j)),
            pl.BlockSpec((DEPTH, ADA_TN), lambda l, j: (0, j)),
        ],
        out_specs=pl.BlockSpec((None, ADA_ROWS, ADA_TN), lambda l, j: (l, 0, j)),
        out_shape=jax.ShapeDtypeStruct((DEPTH, ADA_ROWS, n3), F32),
        compiler_params=_params(("arbitrary", "arbitrary")),
    )(c, w_ada, b_ada)


INPROJ_TM = 512
INPROJ_WROWS = SUBLANES_BF16
INPROJ_WSLOTS = 8
INPROJ_VMEM = 60 * 1024 * 1024


def _inproj_kernel(x_ref, g_ref, sc_ref, sh_ref, w_hbm, o_ref, h0_ref, h1_ref, w_ref, stage_ref, sem,
                   *, n_tiles, per_batch, layer):
    t = pl.program_id(0)
    wrows = stage_ref.shape[1]
    n_rchunks = D_MODEL // wrows

    def weight_copy(r, slot):
        rows = pl.ds(pl.multiple_of(r * wrows, wrows), wrows)
        return pltpu.make_async_copy(w_hbm.at[layer, rows, :], stage_ref.at[slot], sem.at[slot])

    def load_weight():
        n_slots = stage_ref.shape[0]
        for slot in range(n_slots):
            weight_copy(slot, slot).start(priority=slot % 2)

        def group(i, carry):
            for slot in range(n_slots):
                r = n_slots * i + slot
                weight_copy(r, slot).wait()
                rows = pl.ds(pl.multiple_of(r * wrows, wrows), wrows)
                w_ref[rows, :] = stage_ref[slot].astype(BF16)

                @pl.when(r + n_slots < n_rchunks)
                def _():
                    weight_copy(r + n_slots, slot).start(priority=slot % 2)
            return carry

        lax.fori_loop(0, n_rchunks // n_slots, group, 0)

    def norm(h_ref):
        x = x_ref[...]
        row = pl.ds(jnp.minimum(t, n_tiles - 1) // per_batch, 1)
        ms = jnp.mean(x * x, axis=-1, keepdims=True)
        gain = g_ref[layer:layer + 1, :] * (1.0 + sc_ref[row, :])
        h = (x * lax.rsqrt(ms + EPS)) * gain + sh_ref[row, :]
        h_ref[...] = h.astype(BF16)

    def project(h_ref):
        h = h_ref[...]
        for p in range(N_PIECES):
            w = w_ref[:, p * D_GROUP:(p + 1) * D_GROUP]
            o_ref[p] = jnp.dot(h, w, preferred_element_type=F32).astype(BF16)

    odd = t % 2 == 1

    @pl.when(t == 0)
    def _():
        load_weight()
        norm(h0_ref)

    @pl.when(odd)
    def _():
        project(h0_ref)
        norm(h1_ref)

    @pl.when(jnp.logical_and(jnp.logical_not(odd), jnp.logical_and(t > 0, t < n_tiles)))
    def _():
        project(h1_ref)
        norm(h0_ref)

    @pl.when(t == n_tiles)
    def _():
        project(h1_ref)


def _inproj(x, norm_g, mod, w_in, layer):
    batch, seq, _ = x.shape
    tm = INPROJ_TM
    per_batch = seq // tm
    n_tiles = batch * per_batch
    assert n_tiles % 2 == 0 and (D_MODEL // INPROJ_WROWS) % INPROJ_WSLOTS == 0
    norm_tile = lambda t: jnp.minimum(t, n_tiles - 1)
    proj_tile = lambda t: jnp.maximum(t - 1, 0)
    return pl.pallas_call(
        functools.partial(_inproj_kernel, n_tiles=n_tiles, per_batch=per_batch, layer=layer),
        name="inproj",
        grid=(n_tiles + 1,),
        in_specs=[
            pl.BlockSpec((None, tm, D_MODEL),
                         lambda t: (norm_tile(t) // per_batch, norm_tile(t) % per_batch, 0)),
            pl.BlockSpec((DEPTH, D_MODEL), lambda t: (0, 0)),
            pl.BlockSpec((None, ADA_ROWS, D_MODEL), lambda t: (layer, 0, MOD_SCALE)),
            pl.BlockSpec((None, ADA_ROWS, D_MODEL), lambda t: (layer, 0, MOD_SHIFT)),
            pl.BlockSpec(memory_space=pl.ANY),
        ],
        out_specs=pl.BlockSpec((N_PIECES, None, tm, D_GROUP),
                               lambda t: (0, proj_tile(t) // per_batch, proj_tile(t) % per_batch, 0)),
        out_shape=jax.ShapeDtypeStruct((N_PIECES, batch, seq, D_GROUP), BF16),
        scratch_shapes=[
            pltpu.VMEM((tm, D_MODEL), BF16), pltpu.VMEM((tm, D_MODEL), BF16),
            pltpu.VMEM((D_MODEL, N_PIECES * D_GROUP), BF16),
            pltpu.VMEM((INPROJ_WSLOTS, INPROJ_WROWS, N_PIECES * D_GROUP), F32),
            pltpu.SemaphoreType.DMA((INPROJ_WSLOTS,)),
        ],
        compiler_params=_params(("arbitrary",), INPROJ_VMEM),
    )(x, norm_g, mod, mod, w_in)


OUTPROJ_TM = 512
OUTPROJ_WROWS = 64
OUTPROJ_WSLOTS = 8
OUTPROJ_VMEM = 60 * 1024 * 1024


def _outproj_kernel(a0_ref, a1_ref, a2_ref, w_hbm, x_ref, gate_ref, fg_ref,
                    ca_ref, cb_ref, cg_ref, cw_ref, cbias_ref, lg_ref, lb_ref, wpw_ref,
                    o_ref, u_ref, h_ref, cv0_ref, cv1_ref, w_ref, stage_ref, sem,
                    *, final, n_tiles, per_batch, seq, layer):
    t = pl.program_id(0)
    j = jnp.minimum(t, n_tiles - 1) % per_batch
    gate_row = pl.ds(jnp.maximum(t - 1, 0) // per_batch, 1)
    layer_row = pl.ds(layer, 1)

    tm = o_ref.shape[0]
    n_chunks = tm // CONV_ROWS
    tn = D_MODEL // n_chunks
    n_slots = stage_ref.shape[0]

    wrows = stage_ref.shape[1]

    def weight_copy(r):
        slot = r % n_slots
        return pltpu.make_async_copy(w_hbm.at[layer, pl.ds(r * wrows, wrows), :], stage_ref.at[slot],
                                     sem.at[slot])

    def run(proj_cv_ref, conv_cv_ref, load_weight=False):
        if load_weight:
            for c in range(n_slots):
                weight_copy(c).start(priority=c % 2)
        if conv_cv_ref is not None:
            _conv_glu(j, ca_ref, cb_ref, u_ref, seq=seq)
        if proj_cv_ref is not None:
            a = jnp.concatenate([a0_ref[...], a1_ref[...], a2_ref[...], proj_cv_ref[...]], axis=-1)
            ss = jnp.zeros((tm, LANES), F32)
        zero = 0
        for i in range(n_chunks):
            if proj_cv_ref is not None:
                y = jnp.dot(a, w_ref[i + zero], preferred_element_type=F32)
                xn = x_ref[:, i * tn:(i + 1) * tn] + gate_ref[gate_row, i * tn:(i + 1) * tn] * y
                if final:
                    sq = xn * xn
                    for lt in range(tn // LANES):
                        ss = ss + sq[:, lt * LANES:(lt + 1) * LANES]
                o_ref[:, i * tn:(i + 1) * tn] = xn
            if conv_cv_ref is not None:
                tail = _conv_rows(i * CONV_ROWS, cw_ref, cbias_ref.at[layer_row], lg_ref.at[layer_row],
                                  lb_ref.at[layer_row], u_ref, h_ref)
                zero = _runtime_zero(tail)
            if load_weight:
                n_rchunks = D_MODEL // wrows
                per_block = n_rchunks // n_chunks
                for r in range(i * per_block, (i + 1) * per_block):
                    weight_copy(r).wait()
                    for c in range(n_chunks):
                        w_ref[c, r * wrows:(r + 1) * wrows, :] = stage_ref[
                            r % n_slots, :, c * tn:(c + 1) * tn].astype(BF16)
                    if r + n_slots < n_rchunks:
                        weight_copy(r + n_slots).start(priority=r % 2)
        if conv_cv_ref is not None:
            _conv_pointwise(cg_ref, wpw_ref, h_ref, conv_cv_ref)
        if proj_cv_ref is not None and final:
            ms = jnp.sum(ss, axis=-1, keepdims=True) * (1.0 / D_MODEL)
            o_ref[...] = (o_ref[...] * lax.rsqrt(ms + EPS)) * fg_ref[...]

    odd = t % 2 == 1

    @pl.when(t == 0)
    def _():
        run(None, cv0_ref, load_weight=True)

    @pl.when(odd)
    def _():
        run(cv0_ref, cv1_ref)

    @pl.when(jnp.logical_and(jnp.logical_not(odd), jnp.logical_and(t > 0, t < n_tiles)))
    def _():
        run(cv1_ref, cv0_ref)

    @pl.when(t == n_tiles)
    def _():
        run(cv1_ref, None)


def _outproj(mixed, zp, conv_params, w_pw_bf16, w_out, x, mod, final_g, layer, final):
    batch, seq, _ = x.shape
    tm = OUTPROJ_TM
    per_batch = seq // tm
    n_tiles = batch * per_batch
    n_wchunks = tm // CONV_ROWS
    assert n_tiles % 2 == 0
    conv_tile = lambda t: jnp.minimum(t, n_tiles - 1)
    proj_tile = lambda t: jnp.maximum(t - 1, 0)
    a_spec = pl.BlockSpec((None, tm, D_GROUP),
                          lambda t: (proj_tile(t) // per_batch, proj_tile(t) % per_batch, 0))
    x_spec = pl.BlockSpec((None, tm, D_MODEL),
                          lambda t: (proj_tile(t) // per_batch, proj_tile(t) % per_batch, 0))
    seq_spec = lambda p: pl.BlockSpec((None, None, seq, D_GROUP),
                                      lambda t: (p, conv_tile(t) // per_batch, 0, 0))
    vec = pl.BlockSpec((DEPTH, D_GROUP), lambda t: (0, 0))
    conv_w, conv_b, ln_g, ln_b = conv_params
    return pl.pallas_call(
        functools.partial(_outproj_kernel, final=final, n_tiles=n_tiles, per_batch=per_batch, seq=seq,
                          layer=layer),
        name="outproj",
        grid=(n_tiles + 1,),
        in_specs=[
            a_spec, a_spec, a_spec,
            pl.BlockSpec(memory_space=pl.ANY),
            x_spec,
            pl.BlockSpec((None, ADA_ROWS, D_MODEL), lambda t: (layer, 0, MOD_GATE)),
            pl.BlockSpec((1, D_MODEL), lambda t: (0, 0)),
            seq_spec(P_CA), seq_spec(P_CB),
            pl.BlockSpec((None, None, tm, D_GROUP),
                         lambda t: (P_CG, conv_tile(t) // per_batch, conv_tile(t) % per_batch, 0)),
            pl.BlockSpec((None, CONV_WIDTH, D_GROUP), lambda t: (layer, 0, 0)),
            vec, vec, vec,
            pl.BlockSpec((None, D_GROUP, D_GROUP), lambda t: (layer, 0, 0)),
        ],
        out_specs=x_spec,
        out_shape=jax.ShapeDtypeStruct((batch, seq, D_MODEL), F32),
        scratch_shapes=[
            pltpu.VMEM((tm + 2 * CONV_HALO, D_GROUP), F32),
            pltpu.VMEM((tm, D_GROUP), BF16),
            pltpu.VMEM((tm, D_GROUP), BF16), pltpu.VMEM((tm, D_GROUP), BF16),
            pltpu.VMEM((n_wchunks, D_MODEL, D_MODEL // n_wchunks), BF16),
            pltpu.VMEM((OUTPROJ_WSLOTS, OUTPROJ_WROWS, D_MODEL), F32),
            pltpu.SemaphoreType.DMA((OUTPROJ_WSLOTS,)),
        ],
        compiler_params=_params(("arbitrary",), OUTPROJ_VMEM),
    )(*mixed, w_out, x, mod, final_g.reshape(1, D_MODEL), zp, zp, zp, conv_w,
      conv_b, ln_g, ln_b, w_pw_bf16)


FFT_BLOCK = 32
FFT_LANES = 2 * FFT_GROUP_DIM


def _fft_constants(seq):
    rows = seq // GRID_W
    assert rows == GRID_W
    n = np.arange(GRID_W)
    ang1 = 2.0 * np.pi * ((n[:, None] * n[None, :]) % GRID_W) / GRID_W
    f1 = np.concatenate([np.cos(ang1), -np.sin(ang1)], axis=0)
    k1 = n[:, None, None]
    k2 = n[None, :, None]
    s2 = n[None, None, :]
    ang2 = 2.0 * np.pi * ((s2 * (k1 + GRID_W * k2)) % seq) / seq
    mr, mi = np.cos(ang2), -np.sin(ang2)
    m2 = np.concatenate([np.concatenate([mr, -mi], axis=2),
                         np.concatenate([mi, mr], axis=2)], axis=1)
    c = np.arange(FFT_GROUP_DIM)
    angc = 2.0 * np.pi * ((c[:, None] * c[None, :]) % FFT_GROUP_DIM) / FFT_GROUP_DIM
    fc = np.concatenate([np.cos(angc), np.sin(angc)], axis=0)
    return tuple(jnp.asarray(t, F32).astype(BF16) for t in (f1, m2, fc))


def _fft_kernel(u_ref, gate_ref, f1_ref, m2_ref, fc_ref, w_ref, o_ref, ut_ref, g_ref, gt_ref, y_ref, *, norm):
    pair = pl.program_id(1)
    n = GRID_W
    blk = FFT_BLOCK
    sub = SUBLANES_F32
    lanes = ut_ref.shape[-1]
    groups = lanes // FFT_GROUP_DIM

    u3 = u_ref[...].astype(F32).reshape(n, n, lanes)
    ut_ref[...] = jnp.swapaxes(u3, 0, 1).astype(BF16)

    def stage1(i, carry):
        for j in range(blk):
            s2 = i * blk + j
            g_ref[s2] = jnp.dot(f1_ref[...], ut_ref[s2], preferred_element_type=F32)
        return carry

    lax.fori_loop(0, n // blk, stage1, 0)

    for q in range(2 * n // sub):
        part = g_ref[:, q * sub:(q + 1) * sub, :]
        gt_ref[q * sub:(q + 1) * sub] = jnp.swapaxes(part, 0, 1).astype(BF16)

    def stage2(i, carry):
        xs = []
        for j in range(blk):
            k1 = i * blk + j
            gk = jnp.concatenate([gt_ref[k1], gt_ref[n + k1]], axis=0)
            xs.append(jnp.dot(m2_ref[k1], gk, preferred_element_type=F32))
        for g in range(groups):
            sl = slice(g * FFT_GROUP_DIM, (g + 1) * FFT_GROUP_DIM)
            lhs = jnp.concatenate(
                [jnp.concatenate([xk[:n, sl], xk[n:, sl]], axis=-1) for xk in xs], axis=0)
            y = jnp.dot(lhs.astype(BF16), fc_ref[...], preferred_element_type=F32) * norm
            y_ref[pair * groups + g, pl.ds(pl.multiple_of(i * blk, blk), blk)] = (
                y.reshape(blk, n, FFT_GROUP_DIM))
        return carry

    lax.fori_loop(0, n // blk, stage2, 0)

    @pl.when(pair == pl.num_programs(1) - 1)
    def _():
        def stage3(i, carry):
            k2 = pl.ds(pl.multiple_of(i * sub, sub), sub)
            tiles = []
            for g in range(N_FFT_GROUPS):
                part = y_ref[g, :, k2, :]
                tiles.append(jnp.swapaxes(part, 0, 1).reshape(sub * n, FFT_GROUP_DIM))
            y = jnp.concatenate(tiles, axis=-1).astype(BF16)
            o = jnp.dot(y, w_ref[...], preferred_element_type=F32)
            rows = pl.ds(pl.multiple_of(i * sub * n, sub * n), sub * n)
            o_ref[rows, :] = (o * _silu(gate_ref[rows, :].astype(F32))).astype(BF16)
            return carry

        lax.fori_loop(0, n // sub, stage3, 0, unroll=2)


def _fourier(zp, w_fft_bf16, consts, layer):
    _, batch, seq, _ = zp.shape
    f1, m2, fc = consts
    n = GRID_W
    const = lambda shape: pl.BlockSpec(shape, lambda b, g: (0,) * len(shape))
    return pl.pallas_call(
        functools.partial(_fft_kernel, norm=1.0 / math.sqrt(seq * FFT_GROUP_DIM)),
        name="fourier",
        grid=(batch, D_GROUP // FFT_LANES),
        in_specs=[
            pl.BlockSpec((None, None, seq, FFT_LANES), lambda b, g: (P_FX, b, 0, g)),
            pl.BlockSpec((None, None, seq, D_GROUP), lambda b, g: (P_FG, b, 0, 0)),
            const((2 * n, n)), const((n, 2 * n, 2 * n)), const((2 * FFT_GROUP_DIM, FFT_GROUP_DIM)),
            pl.BlockSpec((None, D_GROUP, D_GROUP), lambda b, g: (layer, 0, 0)),
        ],
        out_specs=pl.BlockSpec((None, seq, D_GROUP), lambda b, g: (b, 0, 0)),
        out_shape=jax.ShapeDtypeStruct((batch, seq, D_GROUP), BF16),
        scratch_shapes=[
            pltpu.VMEM((n, n, FFT_LANES), BF16),
            pltpu.VMEM((n, 2 * n, FFT_LANES), F32),
            pltpu.VMEM((2 * n, n, FFT_LANES), BF16),
            pltpu.VMEM((N_FFT_GROUPS, n, n, FFT_GROUP_DIM), F32),
        ],
        compiler_params=_params(("arbitrary", "arbitrary")),
    )(zp, zp, f1, m2, fc, w_fft_bf16)


NA_ROWS_PER_STEP = 16


def _na_bias_table(rel_bias):
    n_dc = 2 * NA_KW - 1
    col = np.arange(GRID_W)
    col_start = np.clip(col - NA_KW // 2, 0, GRID_W - NA_KW)
    rel_c = col[None, :] - col_start[:, None]
    col_in = (rel_c >= 0) & (rel_c < NA_KW)
    dc = np.clip(col[None, :] - col[:, None] + NA_KW - 1, 0, n_dc - 1)
    onehot = (dc[None] == np.arange(n_dc)[:, None, None]).astype(np.float32)
    zeros = np.zeros_like(onehot)
    onehot2 = np.concatenate([np.concatenate([onehot, zeros], axis=-1),
                              np.concatenate([zeros, onehot], axis=-1)], axis=0)
    rb = rel_bias.astype(F32)
    rb2 = jnp.concatenate([rb[:, :, :-1], rb[:, :, 1:]], axis=-1)
    tbl = jnp.einsum('lhdj,jqk->ldhqk', rb2, jnp.asarray(onehot2),
                     precision=lax.Precision.HIGHEST)
    col_in2 = np.concatenate([col_in, col_in], axis=-1)
    return jnp.where(col_in2[None, None, None], tbl, NEG_BIG)


def _head_select_mask():
    r = lax.broadcasted_iota(jnp.int32, (MXU_DIM, MXU_DIM), 0) // HEAD_DIM
    c = lax.broadcasted_iota(jnp.int32, (MXU_DIM, MXU_DIM), 1) // HEAD_DIM
    return r == c


def _stack_heads(t):
    n = t.shape[0]
    reps = jnp.concatenate([t] * HEADS_PER_TILE, axis=0)
    row_head = lax.broadcasted_iota(jnp.int32, reps.shape, 0) // n
    lane_head = lax.broadcasted_iota(jnp.int32, reps.shape, 1) // HEAD_DIM
    return jnp.where(row_head == lane_head, reps, jnp.zeros_like(reps))


def _unstack_heads(t4, n):
    lane_head = lax.broadcasted_iota(jnp.int32, (n, MXU_DIM), 1) // HEAD_DIM
    out = jnp.zeros((n, MXU_DIM), t4.dtype)
    for h in range(HEADS_PER_TILE):
        out = jnp.where(lane_head == h, t4[h * n:(h + 1) * n], out)
    return out


def _na_kernel(q_ref, k_ref, v_ref, g_ref, bias_ref, o_ref, *, rows):
    kh = min(NA_KH, rows)
    step = pl.program_id(1)

    def row_body(rr, carry):
        r = step * NA_ROWS_PER_STEP + rr
        r_start = jnp.clip(r - kh // 2, 0, rows - kh)
        variant = r - r_start
        q_off = pl.multiple_of(rr * GRID_W, GRID_W)
        k_off = pl.multiple_of(r_start * GRID_W, GRID_W)
        for cg in range(D_GROUP // MXU_DIM):
            lanes = slice(cg * MXU_DIM, (cg + 1) * MXU_DIM)
            q = q_ref[pl.ds(q_off, GRID_W), lanes] * (HEAD_DIM ** -0.5)
            kw = k_ref[pl.ds(k_off, kh * GRID_W), lanes]
            vw = v_ref[pl.ds(k_off, kh * GRID_W), lanes]
            q4 = _stack_heads(q)
            s = lax.dot_general(q4, kw, (((1,), (1,)), ((), ())), preferred_element_type=F32)
            heads = slice(cg * HEADS_PER_TILE, (cg + 1) * HEADS_PER_TILE)
            bias = jnp.concatenate(
                [bias_ref[NA_KH - 1 - variant + a, heads].reshape(HEADS_PER_TILE * GRID_W, 2 * GRID_W)
                 for a in range(0, kh, 2)], axis=-1)
            s = s + bias
            m = jnp.max(s, axis=-1, keepdims=True)
            e = jnp.exp(s - m)
            l = jnp.sum(e, axis=-1, keepdims=True)
            o4 = jnp.dot(e.astype(BF16), vw, preferred_element_type=F32) / l
            o = _unstack_heads(o4, GRID_W)
            gate = g_ref[pl.ds(q_off, GRID_W), lanes].astype(F32)
            o_ref[pl.ds(q_off, GRID_W), lanes] = (o * _silu(gate)).astype(BF16)
        return carry

    lax.fori_loop(0, NA_ROWS_PER_STEP, row_body, 0, unroll=True)


def _neighbourhood(zp, bias_tbl, layer):
    _, batch, seq, _ = zp.shape
    rows = seq // GRID_W
    tq = NA_ROWS_PER_STEP * GRID_W
    q_spec = lambda p: pl.BlockSpec((None, None, tq, D_GROUP), lambda b, i: (p, b, i, 0))
    kv_spec = lambda p: pl.BlockSpec((None, None, seq, D_GROUP), lambda b, i: (p, b, 0, 0))
    return pl.pallas_call(
        functools.partial(_na_kernel, rows=rows),
        name="natten",
        grid=(batch, rows // NA_ROWS_PER_STEP),
        in_specs=[
            q_spec(P_NQ), kv_spec(P_NK), kv_spec(P_NV), q_spec(P_NG),
            pl.BlockSpec((None,) + bias_tbl.shape[1:], lambda b, i: (layer, 0, 0, 0, 0),
                         pipeline_mode=pl.Buffered(1)),
        ],
        out_specs=pl.BlockSpec((None, tq, D_GROUP), lambda b, i: (b, i, 0)),
        out_shape=jax.ShapeDtypeStruct((batch, seq, D_GROUP), BF16),
        compiler_params=_params(("arbitrary", "arbitrary")),
    )(zp, zp, zp, zp, bias_tbl)


RET_CHUNK = MXU_DIM


def _log_sigmoid(t):
    return jnp.minimum(t, 0.0) - jnp.log1p(jnp.exp(-jnp.abs(t)))


def _rope_tables(seq):
    half = HEAD_DIM // 2
    inv = ROPE_BASE ** (-jnp.arange(half, dtype=F32) / half)
    ang = jnp.arange(seq, dtype=F32)[:, None] * inv[None, :]
    cos, sin = jnp.cos(ang), jnp.sin(ang)
    cos2 = jnp.concatenate([cos, cos], axis=-1)
    sin2 = jnp.concatenate([-sin, sin], axis=-1)
    return jnp.tile(cos2, (1, 2)), jnp.tile(sin2, (1, 2))


def _ret_kernel(lf_s_ref, lb_s_ref, q_ref, k_ref, v_ref, g_ref, cos_ref, sin_ref,
                o_ref, qr_ref, kr_ref, ob_ref, dmat_ref, dec_ref, state_ref, ones_ref, *, seq, layer):
    C = RET_CHUNK
    n_chunks = seq // C
    cg = pl.program_id(1)
    ones_bd = jnp.where(_head_select_mask(), 1.0, 0.0).astype(BF16)
    ones_ref[...] = jnp.concatenate([ones_bd, ones_bd], axis=0)

    ri = lax.broadcasted_iota(jnp.int32, (C, C), 0)
    ci = lax.broadcasted_iota(jnp.int32, (C, C), 1)
    diff = (ri - ci).astype(F32)
    lane_head = lax.broadcasted_iota(jnp.int32, (1, MXU_DIM), 1) // HEAD_DIM
    lfl = jnp.zeros((1, MXU_DIM), F32)
    lbl = jnp.zeros((1, MXU_DIM), F32)
    for hh in range(HEADS_PER_TILE):
        h = cg * HEADS_PER_TILE + hh
        lf = _log_sigmoid(jnp.full((C, C), lf_s_ref[layer, h], F32))
        lb = _log_sigmoid(jnp.full((C, C), lb_s_ref[layer, h], F32))
        dmat_ref[hh] = jnp.where(diff >= 0, jnp.exp(lf * diff), jnp.exp(lb * (-diff)))
        lfl = jnp.where(lane_head == hh, lf_s_ref[layer, h], lfl)
        lbl = jnp.where(lane_head == hh, lb_s_ref[layer, h], lbl)
    lfl = _log_sigmoid(lfl)
    lbl = _log_sigmoid(lbl)
    idx = lax.broadcasted_iota(jnp.int32, (C, MXU_DIM), 0).astype(F32)
    dec_ref[0] = jnp.exp(lfl * (idx + 1.0))
    dec_ref[1] = jnp.exp(lfl * (C - 1.0 - idx))
    dec_ref[2] = jnp.exp(lbl * (C - idx))
    dec_ref[3] = jnp.exp(lbl * idx)
    cd_f = jnp.exp(lfl * float(C))
    cd_b = jnp.exp(lbl * float(C))

    def rope(t, rows):
        lane = lax.broadcasted_iota(jnp.int32, t.shape, 1)
        first_half = (lane % HEAD_DIM) < (HEAD_DIM // 2)
        cs = cos_ref[rows, :]
        sn = sin_ref[rows, :]
        cs = jnp.concatenate([cs, cs], axis=-1)
        sn = jnp.concatenate([sn, sn], axis=-1)
        swapped = jnp.where(first_half, pltpu.roll(t, MXU_DIM - HEAD_DIM // 2, 1),
                            pltpu.roll(t, HEAD_DIM // 2, 1))
        return t * cs + swapped * sn

    def kv_update(k, v, kd, cd):
        kv = lax.dot_general((k * kd).astype(BF16), v, (((0,), (0,)), ((), ())),
                             preferred_element_type=F32)
        state_ref[...] = cd * state_ref[...] + jnp.where(_head_select_mask(), kv, 0.0)

    state_ref[...] = jnp.zeros_like(state_ref)

    def bwd_body(i, carry):
        n = n_chunks - 1 - i
        rows = pl.ds(pl.multiple_of(n * C, C), C)
        q = rope(q_ref[rows, :].astype(F32), rows) * (HEAD_DIM ** -0.5)
        k = rope(k_ref[rows, :].astype(F32), rows)
        qr_ref[rows, :] = q.astype(BF16)
        kr_ref[rows, :] = k.astype(BF16)
        ob_ref[rows, :] = jnp.dot((q * dec_ref[2]).astype(BF16), state_ref[...].astype(BF16),
                                  preferred_element_type=F32)
        kv_update(k, v_ref[rows, :], dec_ref[3], cd_b)
        return carry

    lax.fori_loop(0, n_chunks, bwd_body, 0, unroll=8)

    state_ref[...] = jnp.zeros_like(state_ref)

    def fwd_body(n, carry):
        rows = pl.ds(pl.multiple_of(n * C, C), C)
        qb = qr_ref[rows, :]
        kb = kr_ref[rows, :]
        v = v_ref[rows, :]
        o_cross = jnp.dot((qb.astype(F32) * dec_ref[0]).astype(BF16), state_ref[...].astype(BF16),
                          preferred_element_type=F32)
        q4 = _stack_heads(qb)
        sc = lax.dot_general(q4, kb, (((1,), (1,)), ((), ())), preferred_element_type=F32)
        sc = sc * dmat_ref[...].reshape(HEADS_PER_TILE * C, C)
        o4 = jnp.dot(sc.astype(BF16), v, preferred_element_type=F32)
        o = _unstack_heads(o4, C) + o_cross + ob_ref[rows, :]
        sq = o * o
        hi = sq.astype(BF16)
        lo = (sq - hi.astype(F32)).astype(BF16)
        ss = jnp.dot(jnp.concatenate([hi, lo], axis=-1), ones_ref[...], preferred_element_type=F32)
        inv = lax.rsqrt(ss * (1.0 / HEAD_DIM) + EPS)
        gate = g_ref[rows, :].astype(F32)
        o_ref[rows, :] = (o * inv * _silu(gate)).astype(BF16)
        kv_update(kb.astype(F32), v, dec_ref[1], cd_f)
        return carry

    lax.fori_loop(0, n_chunks, fwd_body, 0, unroll=8)


def _retention(zp, logit_f, logit_b, rope_tbl, layer):
    _, batch, seq, _ = zp.shape
    cos_t, sin_t = rope_tbl
    n_cg = D_GROUP // MXU_DIM
    spec = lambda p: pl.BlockSpec((None, None, seq, MXU_DIM), lambda b, c, *_: (p, b, 0, c))
    tbl_spec = pl.BlockSpec((seq, 2 * HEAD_DIM), lambda b, c, *_: (0, 0))
    return pl.pallas_call(
        functools.partial(_ret_kernel, seq=seq, layer=layer),
        name="retention",
        grid_spec=pltpu.PrefetchScalarGridSpec(
            num_scalar_prefetch=2,
            grid=(batch, n_cg),
            in_specs=[spec(P_RQ), spec(P_RK), spec(P_RV), spec(P_RG), tbl_spec, tbl_spec],
            out_specs=pl.BlockSpec((None, seq, MXU_DIM), lambda b, c, *_: (b, 0, c)),
            scratch_shapes=[
                pltpu.VMEM((seq, MXU_DIM), BF16),
                pltpu.VMEM((seq, MXU_DIM), BF16),
                pltpu.VMEM((seq, MXU_DIM), F32),
                pltpu.VMEM((HEADS_PER_TILE, RET_CHUNK, RET_CHUNK), F32),
                pltpu.VMEM((4, RET_CHUNK, MXU_DIM), F32),
                pltpu.VMEM((MXU_DIM, MXU_DIM), F32),
                pltpu.VMEM((2 * MXU_DIM, MXU_DIM), BF16),
            ],
        ),
        out_shape=jax.ShapeDtypeStruct((batch, seq, D_GROUP), BF16),
        compiler_params=_params(("arbitrary", "arbitrary")),
    )(logit_f, logit_b, zp, zp, zp, zp, cos_t, sin_t)


CONV_HALO = SUBLANES_BF16
CONV_ROWS = 128


def _conv_glu(j, a_ref, b_ref, u_ref, *, seq):
    halo = CONV_HALO
    ts = u_ref.shape[0] - 2 * halo
    n_tiles = seq // ts

    def glu(start, size):
        rows = pl.ds(pl.multiple_of(start, halo), size)
        a = a_ref[rows, :].astype(F32)
        return a * _sigmoid(b_ref[rows, :].astype(F32))

    base = j * ts
    u_ref[halo:halo + ts, :] = glu(base, ts)
    lo = glu(jnp.maximum(base - halo, 0), halo)
    u_ref[0:halo, :] = jnp.where(j > 0, lo, 0.0)
    hi = glu(jnp.minimum(base + ts, seq - halo), halo)
    u_ref[halo + ts:, :] = jnp.where(j < n_tiles - 1, hi, 0.0)


def _conv_rows(s0, cw_ref, cb_ref, lg_ref, lb_ref, u_ref, h_ref):
    rb = CONV_ROWS
    first = CONV_HALO - CONV_HALF
    parts = []
    for lt in range(D_GROUP // LANES):
        lanes = slice(lt * LANES, (lt + 1) * LANES)
        y = None
        for r in range(SUBLANES_F32):
            acc = None
            for o in range(r, first + CONV_WIDTH, SUBLANES_F32):
                if o < first:
                    continue
                w = o - first
                lo_row = s0 + (o - r)
                term = u_ref[lo_row:lo_row + rb + SUBLANES_F32, lanes] * cw_ref[w:w + 1, lanes]
                acc = term if acc is None else acc + term
            shifted = acc[r:r + rb]
            y = shifted if y is None else y + shifted
        parts.append(y)
    y = jnp.concatenate(parts, axis=-1) + cb_ref[...]
    mu = jnp.mean(y, axis=-1, keepdims=True)
    yc = y - mu
    var = jnp.mean(yc * yc, axis=-1, keepdims=True)
    y = (yc * lax.rsqrt(var + EPS)) * lg_ref[...] + lb_ref[...]
    h_ref[s0:s0 + rb, :] = _silu(y).astype(BF16)
    return y[rb - SUBLANES_F32:, D_GROUP - LANES:]


def _runtime_zero(tile):
    bits = pltpu.bitcast(tile, jnp.uint32)
    bits = lax.shift_right_logical(lax.shift_right_logical(bits, jnp.uint32(16)), jnp.uint32(16))
    return bits[0, 0].astype(jnp.int32)


def _conv_pointwise(g_ref, w_ref, h_ref, o_ref):
    o = jnp.dot(h_ref[...], w_ref[...], preferred_element_type=F32)
    o_ref[...] = (o * _silu(g_ref[...].astype(F32))).astype(BF16)


def kernel(x, c, norm_g, w_ada, b_ada, w_in, w_fft, na_rel_bias, ret_logit_fwd, ret_logit_bwd,
           conv_w, conv_b, conv_ln_g, conv_ln_b, conv_w_pw, w_out, final_g):
    seq = x.shape[1]
    mod = _ada(c, w_ada, b_ada)
    fft_consts = _fft_constants(seq)
    rope_tbl = _rope_tables(seq)
    bias_tbl = _na_bias_table(na_rel_bias)
    w_fft_b, w_pw_b = w_fft.astype(BF16), conv_w_pw.astype(BF16)
    for l in range(DEPTH):
        zp = _inproj(x, norm_g, mod, w_in, l)
        o_fft = _fourier(zp, w_fft_b, fft_consts, l)
        o_na = _neighbourhood(zp, bias_tbl, l)
        o_ret = _retention(zp, ret_logit_fwd, ret_logit_bwd, rope_tbl, l)
        conv_params = (conv_w, conv_b, conv_ln_g, conv_ln_b)
        x = _outproj((o_fft, o_na, o_ret), zp, conv_params, w_pw_b, w_out, x, mod, final_g, l,
                     final=(l == DEPTH - 1))
    return x
```

```python
import functools
import math

import numpy as np
import jax
import jax.numpy as jnp
from jax import lax
from jax.experimental import pallas as pl
from jax.experimental.pallas import tpu as pltpu

F32 = jnp.float32
BF16 = jnp.bfloat16

D_MODEL = 2048
DEPTH = 2
GRID_W = 64
D_GROUP = 512
HEAD_DIM = 64
N_HEADS = D_GROUP // HEAD_DIM
N_FFT_GROUPS = 4
FFT_GROUP_DIM = D_GROUP // N_FFT_GROUPS
NA_KH = 8
NA_KW = 16
CONV_WIDTH = 31
CONV_HALF = CONV_WIDTH // 2
ROPE_BASE = 10000.0
EPS = 1e-6
N_PIECES = 13
(P_FX, P_FG, P_NQ, P_NK, P_NV, P_NG, P_RQ, P_RK, P_RV, P_RG, P_CA, P_CB, P_CG) = range(N_PIECES)

MXU_DIM = 256
HEADS_PER_TILE = MXU_DIM // HEAD_DIM
VMEM_LIMIT = 56 * 1024 * 1024
SUBLANES_BF16 = 16
SUBLANES_F32 = 8
LANES = 128

NEG_BIG = -1e30


def _params(sem, vmem=VMEM_LIMIT):
    return pltpu.CompilerParams(dimension_semantics=sem, vmem_limit_bytes=vmem)


def _sigmoid(t):
    return 0.5 * jnp.tanh(0.5 * t) + 0.5


def _silu(t):
    h = 0.5 * t
    return h + h * jnp.tanh(h)


ADA_TN = 768
ADA_ROWS = 8
(MOD_SHIFT, MOD_SCALE, MOD_GATE) = range(3)


def _ada_kernel(c_ref, w_ref, b_ref, o_ref):
    c = c_ref[...]
    pad = jnp.zeros((ADA_ROWS - c.shape[0], D_MODEL), F32)
    ca = _silu(jnp.concatenate([c, pad], axis=0)).astype(BF16)
    w = w_ref[...].astype(BF16)
    bias = b_ref[pl.ds(pl.program_id(0), 1), :]
    o_ref[...] = jnp.dot(ca, w, preferred_element_type=F32) + bias


def _ada(c, w_ada, b_ada):
    batch = c.shape[0]
    n3 = 3 * D_MODEL
    return pl.pallas_call(
        _ada_kernel,
        name="ada",
        grid=(DEPTH, n3 // ADA_TN),
        in_specs=[
            pl.BlockSpec((batch, D_MODEL), lambda l, j: (0, 0)),
            pl.BlockSpec((None, D_MODEL, ADA_TN), lambda l, j: (l, 0, j)),
            pl.BlockSpec((DEPTH, ADA_TN), lambda l, j: (0, j)),
        ],
        out_specs=pl.BlockSpec((None, ADA_ROWS, ADA_TN), lambda l, j: (l, 0, j)),
        out_shape=jax.ShapeDtypeStruct((DEPTH, ADA_ROWS, n3), F32),
        compiler_params=_params(("arbitrary", "arbitrary")),
    )(c, w_ada, b_ada)


INPROJ_TM = 512
INPROJ_WROWS = SUBLANES_BF16
INPROJ_WSLOTS = 8
INPROJ_VMEM = 60 * 1024 * 1024


def _inproj_kernel(x_ref, g_ref, sc_ref, sh_ref, w_hbm, o_ref, h0_ref, h1_ref, w_ref, stage_ref, sem,
                   *, n_tiles, per_batch, layer):
    t = pl.program_id(0)
    wrows = stage_ref.shape[1]
    n_rchunks = D_MODEL // wrows

    def weight_copy(r, slot):
        rows = pl.ds(pl.multiple_of(r * wrows, wrows), wrows)
        return pltpu.make_async_copy(w_hbm.at[layer, rows, :], stage_ref.at[slot], sem.at[slot])

    def load_weight(between):
        n_slots = stage_ref.shape[0]
        for slot in range(n_slots):
            weight_copy(slot, slot).start()
        between()

        def group(i, carry):
            for slot in range(n_slots):
                r = n_slots * i + slot
                weight_copy(r, slot).wait()
                rows = pl.ds(pl.multiple_of(r * wrows, wrows), wrows)
                w_ref[rows, :] = stage_ref[slot].astype(BF16)

                @pl.when(r + n_slots < n_rchunks)
                def _():
                    weight_copy(r + n_slots, slot).start()
            return carry

        lax.fori_loop(0, n_rchunks // n_slots, group, 0)

    def norm(h_ref):
        x = x_ref[...]
        row = pl.ds(jnp.minimum(t, n_tiles - 1) // per_batch, 1)
        ms = jnp.mean(x * x, axis=-1, keepdims=True)
        gain = g_ref[layer:layer + 1, :] * (1.0 + sc_ref[row, :])
        h = (x * lax.rsqrt(ms + EPS)) * gain + sh_ref[row, :]
        h_ref[...] = h.astype(BF16)

    def project(h_ref):
        h = h_ref[...]
        for p in range(N_PIECES):
            w = w_ref[:, p * D_GROUP:(p + 1) * D_GROUP]
            o_ref[p] = jnp.dot(h, w, preferred_element_type=F32).astype(BF16)

    odd = t % 2 == 1

    @pl.when(t == 0)
    def _():
        load_weight(between=lambda: norm(h0_ref))

    @pl.when(odd)
    def _():
        project(h0_ref)
        norm(h1_ref)

    @pl.when(jnp.logical_and(jnp.logical_not(odd), jnp.logical_and(t > 0, t < n_tiles)))
    def _():
        project(h1_ref)
        norm(h0_ref)

    @pl.when(t == n_tiles)
    def _():
        project(h1_ref)


def _inproj(x, norm_g, mod, w_in, layer):
    batch, seq, _ = x.shape
    tm = INPROJ_TM
    per_batch = seq // tm
    n_tiles = batch * per_batch
    assert n_tiles % 2 == 0 and (D_MODEL // INPROJ_WROWS) % INPROJ_WSLOTS == 0
    norm_tile = lambda t: jnp.minimum(t, n_tiles - 1)
    proj_tile = lambda t: jnp.maximum(t - 1, 0)
    return pl.pallas_call(
        functools.partial(_inproj_kernel, n_tiles=n_tiles, per_batch=per_batch, layer=layer),
        name="inproj",
        grid=(n_tiles + 1,),
        in_specs=[
            pl.BlockSpec((None, tm, D_MODEL),
                         lambda t: (norm_tile(t) // per_batch, norm_tile(t) % per_batch, 0)),
            pl.BlockSpec((DEPTH, D_MODEL), lambda t: (0, 0)),
            pl.BlockSpec((None, ADA_ROWS, D_MODEL), lambda t: (layer, 0, MOD_SCALE)),
            pl.BlockSpec((None, ADA_ROWS, D_MODEL), lambda t: (layer, 0, MOD_SHIFT)),
            pl.BlockSpec(memory_space=pl.ANY),
        ],
        out_specs=pl.BlockSpec((N_PIECES, None, tm, D_GROUP),
                               lambda t: (0, proj_tile(t) // per_batch, proj_tile(t) % per_batch, 0)),
        out_shape=jax.ShapeDtypeStruct((N_PIECES, batch, seq, D_GROUP), BF16),
        scratch_shapes=[
            pltpu.VMEM((tm, D_MODEL), BF16), pltpu.VMEM((tm, D_MODEL), BF16),
            pltpu.VMEM((D_MODEL, N_PIECES * D_GROUP), BF16),
            pltpu.VMEM((INPROJ_WSLOTS, INPROJ_WROWS, N_PIECES * D_GROUP), F32),
            pltpu.SemaphoreType.DMA((INPROJ_WSLOTS,)),
        ],
        compiler_params=_params(("arbitrary",), INPROJ_VMEM),
    )(x, norm_g, mod, mod, w_in)


OUTPROJ_TM = 512
OUTPROJ_WROWS = 64
OUTPROJ_WSLOTS = 8
OUTPROJ_VMEM = 60 * 1024 * 1024


def _outproj_kernel(a0_ref, a1_ref, a2_ref, w_hbm, x_ref, gate_ref, fg_ref,
                    ca_ref, cb_ref, cg_ref, cw_ref, cbias_ref, lg_ref, lb_ref, wpw_ref,
                    o_ref, u_ref, h_ref, cv0_ref, cv1_ref, w_ref, stage_ref, sem,
                    *, final, n_tiles, per_batch, seq, layer):
    t = pl.program_id(0)
    j = jnp.minimum(t, n_tiles - 1) % per_batch
    gate_row = pl.ds(jnp.maximum(t - 1, 0) // per_batch, 1)
    layer_row = pl.ds(layer, 1)

    tm = o_ref.shape[0]
    n_chunks = tm // CONV_ROWS
    tn = D_MODEL // n_chunks
    n_slots = stage_ref.shape[0]

    wrows = stage_ref.shape[1]

    def weight_copy(r):
        slot = r % n_slots
        return pltpu.make_async_copy(w_hbm.at[layer, pl.ds(r * wrows, wrows), :], stage_ref.at[slot],
                                     sem.at[slot])

    def run(proj_cv_ref, conv_cv_ref, load_weight=False):
        if load_weight:
            for c in range(n_slots):
                weight_copy(c).start()
        if conv_cv_ref is not None:
            _conv_glu(j, ca_ref, cb_ref, u_ref, seq=seq)
        if proj_cv_ref is not None:
            a = jnp.concatenate([a0_ref[...], a1_ref[...], a2_ref[...], proj_cv_ref[...]], axis=-1)
            ss = jnp.zeros((tm, LANES), F32)
        zero = 0
        for i in range(n_chunks):
            if proj_cv_ref is not None:
                y = jnp.dot(a, w_ref[i + zero], preferred_element_type=F32)
                xn = x_ref[:, i * tn:(i + 1) * tn] + gate_ref[gate_row, i * tn:(i + 1) * tn] * y
                if final:
                    sq = xn * xn
                    for lt in range(tn // LANES):
                        ss = ss + sq[:, lt * LANES:(lt + 1) * LANES]
                o_ref[:, i * tn:(i + 1) * tn] = xn
            if conv_cv_ref is not None:
                tail = _conv_rows(i * CONV_ROWS, cw_ref, cbias_ref.at[layer_row], lg_ref.at[layer_row],
                                  lb_ref.at[layer_row], u_ref, h_ref)
                zero = _runtime_zero(tail)
            if load_weight:
                n_rchunks = D_MODEL // wrows
                per_block = n_rchunks // n_chunks
                for r in range(i * per_block, (i + 1) * per_block):
                    weight_copy(r).wait()
                    for c in range(n_chunks):
                        w_ref[c, r * wrows:(r + 1) * wrows, :] = stage_ref[
                            r % n_slots, :, c * tn:(c + 1) * tn].astype(BF16)
                    if r + n_slots < n_rchunks:
                        weight_copy(r + n_slots).start()
        if conv_cv_ref is not None:
            _conv_pointwise(cg_ref, wpw_ref, h_ref, conv_cv_ref)
        if proj_cv_ref is not None and final:
            ms = jnp.sum(ss, axis=-1, keepdims=True) * (1.0 / D_MODEL)
            o_ref[...] = (o_ref[...] * lax.rsqrt(ms + EPS)) * fg_ref[...]

    odd = t % 2 == 1

    @pl.when(t == 0)
    def _():
        run(None, cv0_ref, load_weight=True)

    @pl.when(odd)
    def _():
        run(cv0_ref, cv1_ref)

    @pl.when(jnp.logical_and(jnp.logical_not(odd), jnp.logical_and(t > 0, t < n_tiles)))
    def _():
        run(cv1_ref, cv0_ref)

    @pl.when(t == n_tiles)
    def _():
        run(cv1_ref, None)


def _outproj(mixed, zp, conv_params, w_pw_bf16, w_out, x, mod, final_g, layer, final):
    batch, seq, _ = x.shape
    tm = OUTPROJ_TM
    per_batch = seq // tm
    n_tiles = batch * per_batch
    n_wchunks = tm // CONV_ROWS
    assert n_tiles % 2 == 0
    conv_tile = lambda t: jnp.minimum(t, n_tiles - 1)
    proj_tile = lambda t: jnp.maximum(t - 1, 0)
    a_spec = pl.BlockSpec((None, tm, D_GROUP),
                          lambda t: (proj_tile(t) // per_batch, proj_tile(t) % per_batch, 0))
    x_spec = pl.BlockSpec((None, tm, D_MODEL),
                          lambda t: (proj_tile(t) // per_batch, proj_tile(t) % per_batch, 0))
    seq_spec = lambda p: pl.BlockSpec((None, None, seq, D_GROUP),
                                      lambda t: (p, conv_tile(t) // per_batch, 0, 0))
    vec = pl.BlockSpec((DEPTH, D_GROUP), lambda t: (0, 0))
    conv_w, conv_b, ln_g, ln_b = conv_params
    return pl.pallas_call(
        functools.partial(_outproj_kernel, final=final, n_tiles=n_tiles, per_batch=per_batch, seq=seq,
                          layer=layer),
        name="outproj",
        grid=(n_tiles + 1,),
        in_specs=[
            a_spec, a_spec, a_spec,
            pl.BlockSpec(memory_space=pl.ANY),
            x_spec,
            pl.BlockSpec((None, ADA_ROWS, D_MODEL), lambda t: (layer, 0, MOD_GATE)),
            pl.BlockSpec((1, D_MODEL), lambda t: (0, 0)),
            seq_spec(P_CA), seq_spec(P_CB),
            pl.BlockSpec((None, None, tm, D_GROUP),
                         lambda t: (P_CG, conv_tile(t) // per_batch, conv_tile(t) % per_batch, 0)),
            pl.BlockSpec((None, CONV_WIDTH, D_GROUP), lambda t: (layer, 0, 0)),
            vec, vec, vec,
            pl.BlockSpec((None, D_GROUP, D_GROUP), lambda t: (layer, 0, 0)),
        ],
        out_specs=x_spec,
        out_shape=jax.ShapeDtypeStruct((batch, seq, D_MODEL), F32),
        scratch_shapes=[
            pltpu.VMEM((tm + 2 * CONV_HALO, D_GROUP), F32),
            pltpu.VMEM((tm, D_GROUP), BF16),
            pltpu.VMEM((tm, D_GROUP), BF16), pltpu.VMEM((tm, D_GROUP), BF16),
            pltpu.VMEM((n_wchunks, D_MODEL, D_MODEL // n_wchunks), BF16),
            pltpu.VMEM((OUTPROJ_WSLOTS, OUTPROJ_WROWS, D_MODEL), F32),
            pltpu.SemaphoreType.DMA((OUTPROJ_WSLOTS,)),
        ],
        compiler_params=_params(("arbitrary",), OUTPROJ_VMEM),
    )(*mixed, w_out, x, mod, final_g.reshape(1, D_MODEL), zp, zp, zp, conv_w,
      conv_b, ln_g, ln_b, w_pw_bf16)


FFT_BLOCK = 32
FFT_LANES = 2 * FFT_GROUP_DIM


def _fft_constants(seq):
    rows = seq // GRID_W
    assert rows == GRID_W
    n = np.arange(GRID_W)
    ang1 = 2.0 * np.pi * ((n[:, None] * n[None, :]) % GRID_W) / GRID_W
    f1 = np.concatenate([np.cos(ang1), -np.sin(ang1)], axis=0)
    k1 = n[:, None, None]
    k2 = n[None, :, None]
    s2 = n[None, None, :]
    ang2 = 2.0 * np.pi * ((s2 * (k1 + GRID_W * k2)) % seq) / seq
    mr, mi = np.cos(ang2), -np.sin(ang2)
    m2 = np.concatenate([np.concatenate([mr, -mi], axis=2),
                         np.concatenate([mi, mr], axis=2)], axis=1)
    c = np.arange(FFT_GROUP_DIM)
    angc = 2.0 * np.pi * ((c[:, None] * c[None, :]) % FFT_GROUP_DIM) / FFT_GROUP_DIM
    fc = np.concatenate([np.cos(angc), np.sin(angc)], axis=0)
    return tuple(jnp.asarray(t, F32).astype(BF16) for t in (f1, m2, fc))


def _fft_kernel(u_ref, gate_ref, f1_ref, m2_ref, fc_ref, w_ref, o_ref, ut_ref, g_ref, gt_ref, y_ref, *, norm):
    pair = pl.program_id(1)
    n = GRID_W
    blk = FFT_BLOCK
    sub = SUBLANES_F32
    lanes = ut_ref.shape[-1]
    groups = lanes // FFT_GROUP_DIM

    u3 = u_ref[...].astype(F32).reshape(n, n, lanes)
    ut_ref[...] = jnp.swapaxes(u3, 0, 1).astype(BF16)

    def stage1(i, carry):
        for j in range(blk):
            s2 = i * blk + j
            g_ref[s2] = jnp.dot(f1_ref[...], ut_ref[s2], preferred_element_type=F32)
        return carry

    lax.fori_loop(0, n // blk, stage1, 0)

    for q in range(2 * n // sub):
        part = g_ref[:, q * sub:(q + 1) * sub, :]
        gt_ref[q * sub:(q + 1) * sub] = jnp.swapaxes(part, 0, 1).astype(BF16)

    def stage2(i, carry):
        xs = []
        for j in range(blk):
            k1 = i * blk + j
            gk = jnp.concatenate([gt_ref[k1], gt_ref[n + k1]], axis=0)
            xs.append(jnp.dot(m2_ref[k1], gk, preferred_element_type=F32))
        for g in range(groups):
            sl = slice(g * FFT_GROUP_DIM, (g + 1) * FFT_GROUP_DIM)
            lhs = jnp.concatenate(
                [jnp.concatenate([xk[:n, sl], xk[n:, sl]], axis=-1) for xk in xs], axis=0)
            y = jnp.dot(lhs.astype(BF16), fc_ref[...], preferred_element_type=F32) * norm
            y_ref[pair * groups + g, pl.ds(pl.multiple_of(i * blk, blk), blk)] = (
                y.reshape(blk, n, FFT_GROUP_DIM))
        return carry

    lax.fori_loop(0, n // blk, stage2, 0)

    @pl.when(pair == pl.num_programs(1) - 1)
    def _():
        def stage3(i, carry):
            k2 = pl.ds(pl.multiple_of(i * sub, sub), sub)
            tiles = []
            for g in range(N_FFT_GROUPS):
                part = y_ref[g, :, k2, :]
                tiles.append(jnp.swapaxes(part, 0, 1).reshape(sub * n, FFT_GROUP_DIM))
            y = jnp.concatenate(tiles, axis=-1).astype(BF16)
            o = jnp.dot(y, w_ref[...], preferred_element_type=F32)
            rows = pl.ds(pl.multiple_of(i * sub * n, sub * n), sub * n)
            o_ref[rows, :] = (o * _silu(gate_ref[rows, :].astype(F32))).astype(BF16)
            return carry

        lax.fori_loop(0, n // sub, stage3, 0, unroll=2)


def _fourier(zp, w_fft_bf16, consts, layer):
    _, batch, seq, _ = zp.shape
    f1, m2, fc = consts
    n = GRID_W
    const = lambda shape: pl.BlockSpec(shape, lambda b, g: (0,) * len(shape))
    return pl.pallas_call(
        functools.partial(_fft_kernel, norm=1.0 / math.sqrt(seq * FFT_GROUP_DIM)),
        name="fourier",
        grid=(batch, D_GROUP // FFT_LANES),
        in_specs=[
            pl.BlockSpec((None, None, seq, FFT_LANES), lambda b, g: (P_FX, b, 0, g)),
            pl.BlockSpec((None, None, seq, D_GROUP), lambda b, g: (P_FG, b, 0, 0)),
            const((2 * n, n)), const((n, 2 * n, 2 * n)), const((2 * FFT_GROUP_DIM, FFT_GROUP_DIM)),
            pl.BlockSpec((None, D_GROUP, D_GROUP), lambda b, g: (layer, 0, 0)),
        ],
        out_specs=pl.BlockSpec((None, seq, D_GROUP), lambda b, g: (b, 0, 0)),
        out_shape=jax.ShapeDtypeStruct((batch, seq, D_GROUP), BF16),
        scratch_shapes=[
            pltpu.VMEM((n, n, FFT_LANES), BF16),
            pltpu.VMEM((n, 2 * n, FFT_LANES), F32),
            pltpu.VMEM((2 * n, n, FFT_LANES), BF16),
            pltpu.VMEM((N_FFT_GROUPS, n, n, FFT_GROUP_DIM), F32),
        ],
        compiler_params=_params(("arbitrary", "arbitrary")),
    )(zp, zp, f1, m2, fc, w_fft_bf16)


NA_ROWS_PER_STEP = 16


def _na_bias_table(rel_bias):
    n_dc = 2 * NA_KW - 1
    col = np.arange(GRID_W)
    col_start = np.clip(col - NA_KW // 2, 0, GRID_W - NA_KW)
    rel_c = col[None, :] - col_start[:, None]
    col_in = (rel_c >= 0) & (rel_c < NA_KW)
    dc = np.clip(col[None, :] - col[:, None] + NA_KW - 1, 0, n_dc - 1)
    onehot = (dc[None] == np.arange(n_dc)[:, None, None]).astype(np.float32)
    zeros = np.zeros_like(onehot)
    onehot2 = np.concatenate([np.concatenate([onehot, zeros], axis=-1),
                              np.concatenate([zeros, onehot], axis=-1)], axis=0)
    rb = rel_bias.astype(F32)
    rb2 = jnp.concatenate([rb[:, :, :-1], rb[:, :, 1:]], axis=-1)
    tbl = jnp.einsum('lhdj,jqk->ldhqk', rb2, jnp.asarray(onehot2),
                     precision=lax.Precision.HIGHEST)
    col_in2 = np.concatenate([col_in, col_in], axis=-1)
    return jnp.where(col_in2[None, None, None], tbl, NEG_BIG)


def _head_select_mask():
    r = lax.broadcasted_iota(jnp.int32, (MXU_DIM, MXU_DIM), 0) // HEAD_DIM
    c = lax.broadcasted_iota(jnp.int32, (MXU_DIM, MXU_DIM), 1) // HEAD_DIM
    return r == c


def _stack_heads(t):
    n = t.shape[0]
    reps = jnp.concatenate([t] * HEADS_PER_TILE, axis=0)
    row_head = lax.broadcasted_iota(jnp.int32, reps.shape, 0) // n
    lane_head = lax.broadcasted_iota(jnp.int32, reps.shape, 1) // HEAD_DIM
    return jnp.where(row_head == lane_head, reps, jnp.zeros_like(reps))


def _unstack_heads(t4, n):
    lane_head = lax.broadcasted_iota(jnp.int32, (n, MXU_DIM), 1) // HEAD_DIM
    out = jnp.zeros((n, MXU_DIM), t4.dtype)
    for h in range(HEADS_PER_TILE):
        out = jnp.where(lane_head == h, t4[h * n:(h + 1) * n], out)
    return out


def _na_kernel(q_ref, k_ref, v_ref, g_ref, bias_ref, o_ref, *, rows):
    kh = min(NA_KH, rows)
    step = pl.program_id(1)

    def row_body(rr, carry):
        r = step * NA_ROWS_PER_STEP + rr
        r_start = jnp.clip(r - kh // 2, 0, rows - kh)
        variant = r - r_start
        q_off = pl.multiple_of(rr * GRID_W, GRID_W)
        k_off = pl.multiple_of(r_start * GRID_W, GRID_W)
        for cg in range(D_GROUP // MXU_DIM):
            lanes = slice(cg * MXU_DIM, (cg + 1) * MXU_DIM)
            q = q_ref[pl.ds(q_off, GRID_W), lanes] * (HEAD_DIM ** -0.5)
            kw = k_ref[pl.ds(k_off, kh * GRID_W), lanes]
            vw = v_ref[pl.ds(k_off, kh * GRID_W), lanes]
            q4 = _stack_heads(q)
            s = lax.dot_general(q4, kw, (((1,), (1,)), ((), ())), preferred_element_type=F32)
            heads = slice(cg * HEADS_PER_TILE, (cg + 1) * HEADS_PER_TILE)
            bias = jnp.concatenate(
                [bias_ref[NA_KH - 1 - variant + a, heads].reshape(HEADS_PER_TILE * GRID_W, 2 * GRID_W)
                 for a in range(0, kh, 2)], axis=-1)
            s = s + bias
            m = jnp.max(s, axis=-1, keepdims=True)
            e = jnp.exp(s - m)
            l = jnp.sum(e, axis=-1, keepdims=True)
            o4 = jnp.dot(e.astype(BF16), vw, preferred_element_type=F32) / l
            o = _unstack_heads(o4, GRID_W)
            gate = g_ref[pl.ds(q_off, GRID_W), lanes].astype(F32)
            o_ref[pl.ds(q_off, GRID_W), lanes] = (o * _silu(gate)).astype(BF16)
        return carry

    lax.fori_loop(0, NA_ROWS_PER_STEP, row_body, 0, unroll=True)


def _neighbourhood(zp, bias_tbl, layer):
    _, batch, seq, _ = zp.shape
    rows = seq // GRID_W
    tq = NA_ROWS_PER_STEP * GRID_W
    q_spec = lambda p: pl.BlockSpec((None, None, tq, D_GROUP), lambda b, i: (p, b, i, 0))
    kv_spec = lambda p: pl.BlockSpec((None, None, seq, D_GROUP), lambda b, i: (p, b, 0, 0))
    return pl.pallas_call(
        functools.partial(_na_kernel, rows=rows),
        name="natten",
        grid=(batch, rows // NA_ROWS_PER_STEP),
        in_specs=[
            q_spec(P_NQ), kv_spec(P_NK), kv_spec(P_NV), q_spec(P_NG),
            pl.BlockSpec((None,) + bias_tbl.shape[1:], lambda b, i: (layer, 0, 0, 0, 0),
                         pipeline_mode=pl.Buffered(1)),
        ],
        out_specs=pl.BlockSpec((None, tq, D_GROUP), lambda b, i: (b, i, 0)),
        out_shape=jax.ShapeDtypeStruct((batch, seq, D_GROUP), BF16),
        compiler_params=_params(("arbitrary", "arbitrary")),
    )(zp, zp, zp, zp, bias_tbl)


RET_CHUNK = MXU_DIM


def _log_sigmoid(t):
    return jnp.minimum(t, 0.0) - jnp.log1p(jnp.exp(-jnp.abs(t)))


def _rope_tables(seq):
    half = HEAD_DIM // 2
    inv = ROPE_BASE ** (-jnp.arange(half, dtype=F32) / half)
    ang = jnp.arange(seq, dtype=F32)[:, None] * inv[None, :]
    cos, sin = jnp.cos(ang), jnp.sin(ang)
    cos2 = jnp.concatenate([cos, cos], axis=-1)
    sin2 = jnp.concatenate([-sin, sin], axis=-1)
    return jnp.tile(cos2, (1, 2)), jnp.tile(sin2, (1, 2))


def _ret_kernel(lf_s_ref, lb_s_ref, q_ref, k_ref, v_ref, g_ref, cos_ref, sin_ref,
                o_ref, qr_ref, kr_ref, ob_ref, dmat_ref, dec_ref, state_ref, ones_ref, *, seq, layer):
    C = RET_CHUNK
    n_chunks = seq // C
    cg = pl.program_id(1)
    ones_bd = jnp.where(_head_select_mask(), 1.0, 0.0).astype(BF16)
    ones_ref[...] = jnp.concatenate([ones_bd, ones_bd], axis=0)

    ri = lax.broadcasted_iota(jnp.int32, (C, C), 0)
    ci = lax.broadcasted_iota(jnp.int32, (C, C), 1)
    diff = (ri - ci).astype(F32)
    lane_head = lax.broadcasted_iota(jnp.int32, (1, MXU_DIM), 1) // HEAD_DIM
    lfl = jnp.zeros((1, MXU_DIM), F32)
    lbl = jnp.zeros((1, MXU_DIM), F32)
    for hh in range(HEADS_PER_TILE):
        h = cg * HEADS_PER_TILE + hh
        lf = _log_sigmoid(jnp.full((C, C), lf_s_ref[layer, h], F32))
        lb = _log_sigmoid(jnp.full((C, C), lb_s_ref[layer, h], F32))
        dmat_ref[hh] = jnp.where(diff >= 0, jnp.exp(lf * diff), jnp.exp(lb * (-diff)))
        lfl = jnp.where(lane_head == hh, lf_s_ref[layer, h], lfl)
        lbl = jnp.where(lane_head == hh, lb_s_ref[layer, h], lbl)
    lfl = _log_sigmoid(lfl)
    lbl = _log_sigmoid(lbl)
    idx = lax.broadcasted_iota(jnp.int32, (C, MXU_DIM), 0).astype(F32)
    dec_ref[0] = jnp.exp(lfl * (idx + 1.0))
    dec_ref[1] = jnp.exp(lfl * (C - 1.0 - idx))
    dec_ref[2] = jnp.exp(lbl * (C - idx))
    dec_ref[3] = jnp.exp(lbl * idx)
    cd_f = jnp.exp(lfl * float(C))
    cd_b = jnp.exp(lbl * float(C))

    def rope(t, rows):
        lane = lax.broadcasted_iota(jnp.int32, t.shape, 1)
        first_half = (lane % HEAD_DIM) < (HEAD_DIM // 2)
        cs = cos_ref[rows, :]
        sn = sin_ref[rows, :]
        cs = jnp.concatenate([cs, cs], axis=-1)
        sn = jnp.concatenate([sn, sn], axis=-1)
        swapped = jnp.where(first_half, pltpu.roll(t, MXU_DIM - HEAD_DIM // 2, 1),
                            pltpu.roll(t, HEAD_DIM // 2, 1))
        return t * cs + swapped * sn

    def kv_update(k, v, kd, cd):
        kv = lax.dot_general((k * kd).astype(BF16), v, (((0,), (0,)), ((), ())),
                             preferred_element_type=F32)
        state_ref[...] = cd * state_ref[...] + jnp.where(_head_select_mask(), kv, 0.0)

    state_ref[...] = jnp.zeros_like(state_ref)

    def bwd_body(i, carry):
        n = n_chunks - 1 - i
        rows = pl.ds(pl.multiple_of(n * C, C), C)
        q = rope(q_ref[rows, :].astype(F32), rows) * (HEAD_DIM ** -0.5)
        k = rope(k_ref[rows, :].astype(F32), rows)
        qr_ref[rows, :] = q.astype(BF16)
        kr_ref[rows, :] = k.astype(BF16)
        ob_ref[rows, :] = jnp.dot((q * dec_ref[2]).astype(BF16), state_ref[...].astype(BF16),
                                  preferred_element_type=F32)
        kv_update(k, v_ref[rows, :], dec_ref[3], cd_b)
        return carry

    lax.fori_loop(0, n_chunks, bwd_body, 0, unroll=8)

    state_ref[...] = jnp.zeros_like(state_ref)

    def fwd_body(n, carry):
        rows = pl.ds(pl.multiple_of(n * C, C), C)
        qb = qr_ref[rows, :]
        kb = kr_ref[rows, :]
        v = v_ref[rows, :]
        o_cross = jnp.dot((qb.astype(F32) * dec_ref[0]).astype(BF16), state_ref[...].astype(BF16),
                          preferred_element_type=F32)
        q4 = _stack_heads(qb)
        sc = lax.dot_general(q4, kb, (((1,), (1,)), ((), ())), preferred_element_type=F32)
        sc = sc * dmat_ref[...].reshape(HEADS_PER_TILE * C, C)
        o4 = jnp.dot(sc.astype(BF16), v, preferred_element_type=F32)
        o = _unstack_heads(o4, C) + o_cross + ob_ref[rows, :]
        sq = o * o
        hi = sq.astype(BF16)
        lo = (sq - hi.astype(F32)).astype(BF16)
        ss = jnp.dot(jnp.concatenate([hi, lo], axis=-1), ones_ref[...], preferred_element_type=F32)
        inv = lax.rsqrt(ss * (1.0 / HEAD_DIM) + EPS)
        gate = g_ref[rows, :].astype(F32)
        o_ref[rows, :] = (o * inv * _silu(gate)).astype(BF16)
        kv_update(kb.astype(F32), v, dec_ref[1], cd_f)
        return carry

    lax.fori_loop(0, n_chunks, fwd_body, 0, unroll=8)


def _retention(zp, logit_f, logit_b, rope_tbl, layer):
    _, batch, seq, _ = zp.shape
    cos_t, sin_t = rope_tbl
    n_cg = D_GROUP // MXU_DIM
    spec = lambda p: pl.BlockSpec((None, None, seq, MXU_DIM), lambda b, c, *_: (p, b, 0, c))
    tbl_spec = pl.BlockSpec((seq, 2 * HEAD_DIM), lambda b, c, *_: (0, 0))
    return pl.pallas_call(
        functools.partial(_ret_kernel, seq=seq, layer=layer),
        name="retention",
        grid_spec=pltpu.PrefetchScalarGridSpec(
            num_scalar_prefetch=2,
            grid=(batch, n_cg),
            in_specs=[spec(P_RQ), spec(P_RK), spec(P_RV), spec(P_RG), tbl_spec, tbl_spec],
            out_specs=pl.BlockSpec((None, seq, MXU_DIM), lambda b, c, *_: (b, 0, c)),
            scratch_shapes=[
                pltpu.VMEM((seq, MXU_DIM), BF16),
                pltpu.VMEM((seq, MXU_DIM), BF16),
                pltpu.VMEM((seq, MXU_DIM), F32),
                pltpu.VMEM((HEADS_PER_TILE, RET_CHUNK, RET_CHUNK), F32),
                pltpu.VMEM((4, RET_CHUNK, MXU_DIM), F32),
                pltpu.VMEM((MXU_DIM, MXU_DIM), F32),
                pltpu.VMEM((2 * MXU_DIM, MXU_DIM), BF16),
            ],
        ),
        out_shape=jax.ShapeDtypeStruct((batch, seq, D_GROUP), BF16),
        compiler_params=_params(("arbitrary", "arbitrary")),
    )(logit_f, logit_b, zp, zp, zp, zp, cos_t, sin_t)


CONV_HALO = SUBLANES_BF16
CONV_ROWS = 128


def _conv_glu(j, a_ref, b_ref, u_ref, *, seq):
    halo = CONV_HALO
    ts = u_ref.shape[0] - 2 * halo
    n_tiles = seq // ts

    def glu(start, size):
        rows = pl.ds(pl.multiple_of(start, halo), size)
        a = a_ref[rows, :].astype(F32)
        return a * _sigmoid(b_ref[rows, :].astype(F32))

    base = j * ts
    u_ref[halo:halo + ts, :] = glu(base, ts)
    lo = glu(jnp.maximum(base - halo, 0), halo)
    u_ref[0:halo, :] = jnp.where(j > 0, lo, 0.0)
    hi = glu(jnp.minimum(base + ts, seq - halo), halo)
    u_ref[halo + ts:, :] = jnp.where(j < n_tiles - 1, hi, 0.0)


def _conv_rows(s0, cw_ref, cb_ref, lg_ref, lb_ref, u_ref, h_ref):
    rb = CONV_ROWS
    first = CONV_HALO - CONV_HALF
    parts = []
    for lt in range(D_GROUP // LANES):
        lanes = slice(lt * LANES, (lt + 1) * LANES)
        y = None
        for r in range(SUBLANES_F32):
            acc = None
            for o in range(r, first + CONV_WIDTH, SUBLANES_F32):
                if o < first:
                    continue
                w = o - first
                lo_row = s0 + (o - r)
                term = u_ref[lo_row:lo_row + rb + SUBLANES_F32, lanes] * cw_ref[w:w + 1, lanes]
                acc = term if acc is None else acc + term
            shifted = acc[r:r + rb]
            y = shifted if y is None else y + shifted
        parts.append(y)
    y = jnp.concatenate(parts, axis=-1) + cb_ref[...]
    mu = jnp.mean(y, axis=-1, keepdims=True)
    yc = y - mu
    var = jnp.mean(yc * yc, axis=-1, keepdims=True)
    y = (yc * lax.rsqrt(var + EPS)) * lg_ref[...] + lb_ref[...]
    h_ref[s0:s0 + rb, :] = _silu(y).astype(BF16)
    return y[rb - SUBLANES_F32:, D_GROUP - LANES:]


def _runtime_zero(tile):
    bits = pltpu.bitcast(tile, jnp.uint32)
    bits = lax.shift_right_logical(lax.shift_right_logical(bits, jnp.uint32(16)), jnp.uint32(16))
    return bits[0, 0].astype(jnp.int32)


def _conv_pointwise(g_ref, w_ref, h_ref, o_ref):
    o = jnp.dot(h_ref[...], w_ref[...], preferred_element_type=F32)
    o_ref[...] = (o * _silu(g_ref[...].astype(F32))).astype(BF16)


def kernel(x, c, norm_g, w_ada, b_ada, w_in, w_fft, na_rel_bias, ret_logit_fwd, ret_logit_bwd,
           conv_w, conv_b, conv_ln_g, conv_ln_b, conv_w_pw, w_out, final_g):
    seq = x.shape[1]
    mod = _ada(c, w_ada, b_ada)
    fft_consts = _fft_constants(seq)
    rope_tbl = _rope_tables(seq)
    bias_tbl = _na_bias_table(na_rel_bias)
    w_fft_b, w_pw_b = w_fft.astype(BF16), conv_w_pw.astype(BF16)
    for l in range(DEPTH):
        zp = _inproj(x, norm_g, mod, w_in, l)
        o_fft = _fourier(zp, w_fft_b, fft_consts, l)
        o_na = _neighbourhood(zp, bias_tbl, l)
        o_ret = _retention(zp, ret_logit_fwd, ret_logit_bwd, rope_tbl, l)
        conv_params = (conv_w, conv_b, conv_ln_g, conv_ln_b)
        x = _outproj((o_fft, o_na, o_ret), zp, conv_params, w_pw_b, w_out, x, mod, final_g, l,
                     final=(l == DEPTH - 1))
    return x
```
